```python
import math
import jax
import jax.numpy as jnp
from jax import lax
import numpy as np

D_MODEL = 2048
BATCH = 4
SEQ = 2048
DEPTH = 2
DEC_BATCH = 128
DEC_SEQ = 1
PAST_LEN = 16384
PAGE_SIZE = 128

D_MIX = D_MODEL
D_GROUP = D_MIX // 4
RWKV_D = D_GROUP
RWKV_HD = 64
RWKV_H = RWKV_D // RWKV_HD
RWKV_R_DECAY = 96
RWKV_R_A = 96
RWKV_R_GATE = 256
RWKV_PROJ = 3 * RWKV_D + RWKV_R_DECAY + RWKV_R_A + RWKV_R_GATE
RWKV_SPLITS = (RWKV_D, 2 * RWKV_D, 3 * RWKV_D, 3 * RWKV_D + RWKV_R_DECAY, 3 * RWKV_D + RWKV_R_DECAY + RWKV_R_A)
RWKV_LN_EPS = 64e-5
S5_D = D_GROUP
S5_CH = 16
S5_G = S5_D // S5_CH
S5_N = 64
GDN_D = D_GROUP
GDN_HD = 128
GDN_H = GDN_D // GDN_HD
GDN_CONV = 4
GDN_CHUNK = 64
GDN_PROJ = 4 * GDN_D + 2 * GDN_H
LRU_D = D_MIX - RWKV_D - S5_D - GDN_D
LRU_BLOCKS = 8
LRU_BS = LRU_D // LRU_BLOCKS
LRU_CONV = 4
LRU_C = 8.0
LRU_PROJ = 2 * LRU_D
IN_SPLITS = (RWKV_PROJ, RWKV_PROJ + S5_D, RWKV_PROJ + S5_D + GDN_PROJ)
N_IN = RWKV_PROJ + S5_D + GDN_PROJ + LRU_PROJ
D_FF = 5632
FFN_CONV = 3
NORM_EPS = 1e-6
STATE_NAMES = ('rwkv_wkv', 'rwkv_shift', 's5_re', 's5_im', 'gdn', 'gdn_conv', 'lru_h', 'lru_conv', 'ffn_conv')
F32 = jnp.float32

kernel_name = 'hybrid_parallel_heads_decoder_step'


def rms_norm(x, g, eps=NORM_EPS):
    xf = x.astype(F32)
    return xf * lax.rsqrt(jnp.mean(xf * xf, axis=-1, keepdims=True) + eps) * g.astype(F32)


def l2_normalize(x, eps=1e-12):
    xf = x.astype(F32)
    return xf * lax.rsqrt(jnp.sum(xf * xf, axis=-1, keepdims=True) + eps)


def causal_dwconv(x, buf, w):
    L = x.shape[1]
    xp = jnp.concatenate([buf.astype(F32), x.astype(F32)], axis=1)
    y = lax.conv_general_dilated(xp, w.astype(F32)[:, None, :], window_strides=(1,), padding='VALID',
                                 dimension_numbers=('NWC', 'WIO', 'NWC'), feature_group_count=x.shape[-1])
    return y, xp[:, L:]


def real_linear_scan(a, b, h0):
    b = b.at[:, 0].add(a[:, 0] * h0)

    def comb(e1, e2):
        a1, b1 = e1
        a2, b2 = e2
        return a1 * a2, a2 * b1 + b2

    _, h = lax.associative_scan(comb, (a, b), axis=1)
    return h


def complex_linear_scan(a_re, a_im, b_re, b_im, h0_re, h0_im):
    b_re = b_re.at[:, 0].add(a_re * h0_re - a_im * h0_im)
    b_im = b_im.at[:, 0].add(a_re * h0_im + a_im * h0_re)
    a_re = jnp.broadcast_to(a_re, b_re.shape)
    a_im = jnp.broadcast_to(a_im, b_im.shape)

    def comb(e1, e2):
        ar1, ai1, br1, bi1 = e1
        ar2, ai2, br2, bi2 = e2
        return (ar2 * ar1 - ai2 * ai1, ar2 * ai1 + ai2 * ar1,
                ar2 * br1 - ai2 * bi1 + br2, ar2 * bi1 + ai2 * br1 + bi2)

    _, _, h_re, h_im = lax.associative_scan(comb, (a_re, a_im, b_re, b_im), axis=1)
    return h_re, h_im


def rwkv7_mixer(p_in, s0, shift0, p):
    bsz, L, _ = p_in.shape
    p_in = p_in.astype(F32)
    prev = jnp.concatenate([shift0[:, None].astype(F32), p_in[:, :-1]], axis=1)
    xm = p_in + p['rwkv_mu'] * (prev - p_in)
    r, k, v, xw, xa, xg = jnp.split(xm, RWKV_SPLITS, axis=-1)
    log_w = -jnp.exp(-jax.nn.softplus(-(p['rwkv_w0'] + jnp.tanh(xw) @ p['rwkv_w_up'])) - 0.5)
    a = jax.nn.sigmoid(p['rwkv_a0'] + xa @ p['rwkv_a_up'])
    g = jax.nn.sigmoid(xg) @ p['rwkv_g_up']

    def heads(t):
        return t.reshape(bsz, L, RWKV_H, RWKV_HD)

    r, k, v, a, w = heads(r), heads(k), heads(v), heads(a), jnp.exp(heads(log_w))
    kk = l2_normalize(k * p['rwkv_k_k'].reshape(RWKV_H, RWKV_HD))
    k = k * (1.0 + (a - 1.0) * p['rwkv_k_a'].reshape(RWKV_H, RWKV_HD))

    def step(S, inp):
        r_t, w_t, k_t, v_t, kk_t, a_t = inp
        S = (S * w_t[..., None]
             - (kk_t * a_t)[..., None] * jnp.einsum('bhk,bhkv->bhv', kk_t, S)[:, :, None, :]
             + k_t[..., None] * v_t[:, :, None, :])
        return S, jnp.einsum('bhk,bhkv->bhv', r_t, S)

    xs = tuple(jnp.swapaxes(t, 0, 1) for t in (r, w, k, v, kk, a))
    S, y = lax.scan(step, s0.astype(F32), xs)
    y = jnp.swapaxes(y, 0, 1)
    mean = jnp.mean(y, axis=-1, keepdims=True)
    var = jnp.mean(jnp.square(y - mean), axis=-1, keepdims=True)
    y = ((y - mean) * lax.rsqrt(var + RWKV_LN_EPS)).reshape(bsz, L, RWKV_D) * p['rwkv_ln_w'] + p['rwkv_ln_b']
    bonus = jnp.sum(r * k * p['rwkv_r_k'], axis=-1, keepdims=True) * v
    y = (y + bonus.reshape(bsz, L, RWKV_D)) * g
    return y, S, p_in[:, -1]


def s5_mixer(u, h0_re, h0_im, p):
    bsz, L, _ = u.shape
    u = u.astype(F32)
    ug = u.reshape(bsz, L, S5_G, S5_CH)
    lr = p['s5_lambda_re'].astype(F32)
    li = p['s5_lambda_im'].astype(F32)
    dt = jnp.exp(p['s5_log_dt'].astype(F32))[:, None]
    mag = jnp.exp(lr * dt)
    ab_re, ab_im = mag * jnp.cos(li * dt), mag * jnp.sin(li * dt)
    den = lr * lr + li * li
    nr = ab_re - 1.0
    f_re = (nr * lr + ab_im * li) / den
    f_im = (ab_im * lr - nr * li) / den
    b_re, b_im = p['s5_b_re'], p['s5_b_im']
    bb_re = f_re[..., None] * b_re - f_im[..., None] * b_im
    bb_im = f_re[..., None] * b_im + f_im[..., None] * b_re
    bu_re = jnp.einsum('gnc,blgc->blgn', bb_re, ug)
    bu_im = jnp.einsum('gnc,blgc->blgn', bb_im, ug)
    h_re, h_im = complex_linear_scan(ab_re, ab_im, bu_re, bu_im, h0_re.astype(F32), h0_im.astype(F32))
    y = (jnp.einsum('gcn,blgn->blgc', p['s5_c_re'], h_re)
         - jnp.einsum('gcn,blgn->blgc', p['s5_c_im'], h_im)).reshape(bsz, L, S5_D)
    y = y + p['s5_d'] * u
    z = jax.nn.gelu(y)
    return z * jax.nn.sigmoid(z @ p['s5_glu_w'] + p['s5_glu_b']), h_re[:, -1], h_im[:, -1]


def gated_delta_rule(q, k, v, g, beta, s0):
    bsz, L, H, K = q.shape
    V = v.shape[-1]
    C = GDN_CHUNK
    n = -(-L // C)
    pad = n * C - L

    def blocks(t):
        t = jnp.pad(t.astype(F32), [(0, 0), (0, pad)] + [(0, 0)] * (t.ndim - 2))
        t = t.reshape((bsz, n, C) + t.shape[2:])
        return jnp.moveaxis(t, 3, 1)

    q, k, v, g, beta = blocks(q), blocks(k), blocks(v), blocks(g), blocks(beta)
    q = q * (K ** -0.5)
    kb = k * beta[..., None]
    vb = v * beta[..., None]
    gc = jnp.cumsum(g, axis=-1)
    causal = jnp.tril(jnp.ones((C, C), dtype=bool))
    strict = jnp.tril(jnp.ones((C, C), dtype=bool), -1)
    diff = gc[..., :, None] - gc[..., None, :]
    decay = jnp.where(causal, jnp.exp(jnp.where(causal, diff, 0.0)), 0.0)
    lmat = jnp.where(strict, jnp.einsum('bhnck,bhndk->bhncd', kb, k) * decay, 0.0)
    eye = jnp.eye(C, dtype=F32)
    rhs = jnp.concatenate([vb, kb * jnp.exp(gc)[..., None]], axis=-1)
    sol = lax.linalg.triangular_solve(lmat + eye, rhs, left_side=True, lower=True, unit_diagonal=True)
    u, wk = sol[..., :V], sol[..., V:]
    attn = jnp.where(causal, jnp.einsum('bhnck,bhndk->bhncd', q, k) * decay, 0.0)
    q_dec = q * jnp.exp(gc)[..., None]
    k_tail = k * jnp.exp(gc[..., -1:] - gc)[..., None]
    g_tot = jnp.exp(gc[..., -1])
    xs = tuple(jnp.moveaxis(t, 2, 0) for t in (u, wk, attn, q_dec, k_tail, g_tot))

    def step(S, inp):
        u_i, wk_i, attn_i, qd_i, kt_i, gt_i = inp
        v_new = u_i - jnp.einsum('bhck,bhkv->bhcv', wk_i, S)
        o_i = jnp.einsum('bhck,bhkv->bhcv', qd_i, S) + jnp.einsum('bhcd,bhdv->bhcv', attn_i, v_new)
        S = S * gt_i[..., None, None] + jnp.einsum('bhck,bhcv->bhkv', kt_i, v_new)
        return S, o_i

    S, o = lax.scan(step, s0.astype(F32), xs)
    o = jnp.moveaxis(o, 0, 2).reshape(bsz, H, n * C, V)[:, :, :L]
    return jnp.swapaxes(o, 1, 2), S


def gdn_mixer(pc, s0, conv0, p):
    bsz, L, _ = pc.shape
    qkv, z, a_in, b_in = jnp.split(pc.astype(F32), [3 * GDN_D, 4 * GDN_D, 4 * GDN_D + GDN_H], axis=-1)
    qkv, conv_new = causal_dwconv(qkv, conv0, p['gdn_conv_w'])
    q, k, v = jnp.split(jax.nn.silu(qkv), 3, axis=-1)

    def heads(t):
        return t.reshape(bsz, L, GDN_H, GDN_HD)

    q, k, v = l2_normalize(heads(q), 1e-6), l2_normalize(heads(k), 1e-6), heads(v)
    g = -jnp.exp(p['gdn_a_log']) * jax.nn.softplus(a_in + p['gdn_dt_bias'])
    beta = jax.nn.sigmoid(b_in)
    o, s_new = gated_delta_rule(q, k, v, g, beta, s0)
    o = rms_norm(o, p['gdn_norm_w']) * jax.nn.silu(heads(z))
    return o.reshape(bsz, L, GDN_D), s_new, conv_new


def rglru_mixer(pd, h0, conv0, p):
    bsz, L, _ = pd.shape
    xr, gate = jnp.split(pd.astype(F32), 2, axis=-1)
    xc, conv_new = causal_dwconv(xr, conv0, p['lru_conv_w'])
    xc = xc + p['lru_conv_b']
    xb = xc.reshape(bsz, L, LRU_BLOCKS, LRU_BS)

    def blockdiag(w, b):
        return jnp.einsum('blhi,hij->blhj', xb, w).reshape(bsz, L, LRU_D) + b

    r = jax.nn.sigmoid(blockdiag(p['lru_wr'], p['lru_br']))
    i = jax.nn.sigmoid(blockdiag(p['lru_wi'], p['lru_bi']))
    log_a = -LRU_C * r * jax.nn.softplus(-p['lru_lambda'])
    a = jnp.exp(log_a)
    b = jnp.sqrt(-jnp.expm1(2.0 * log_a)) * (i * xc)
    h = real_linear_scan(a, b, h0.astype(F32))
    return h * jax.nn.gelu(gate), h[:, -1], conv_new


def conv_ffn(h, buf, p):
    up = h @ p['ffn_w_up']
    up, buf_new = causal_dwconv(up, buf, p['ffn_conv_w'])
    gate, val = jnp.split(up + p['ffn_conv_b'], 2, axis=-1)
    return (jax.nn.silu(gate) * val) @ p['ffn_w_down'], buf_new


def trunk_layer(x, c, st, p):
    mod = jax.nn.silu(c.astype(F32)) @ p['ada_w'] + p['ada_b']
    sh1, sc1, g1, sh2, sc2, g2 = (m[:, None, :] for m in jnp.split(mod, 6, axis=-1))
    h = rms_norm(x, p['norm1_g']) * (1.0 + sc1) + sh1
    proj = h @ p['w_in']
    pa, pb, pc, pd = jnp.split(proj, IN_SPLITS, axis=-1)
    ya, s_wkv, s_shift = rwkv7_mixer(pa, st['rwkv_wkv'], st['rwkv_shift'], p)
    yb, s_re, s_im = s5_mixer(pb, st['s5_re'], st['s5_im'], p)
    yc, s_gdn, s_gdn_conv = gdn_mixer(pc, st['gdn'], st['gdn_conv'], p)
    yd, s_lru, s_lru_conv = rglru_mixer(pd, st['lru_h'], st['lru_conv'], p)
    x = x + g1 * (jnp.concatenate([ya, yb, yc, yd], axis=-1) @ p['w_out'])
    h = rms_norm(x, p['norm2_g']) * (1.0 + sc2) + sh2
    f, s_ffn = conv_ffn(h, st['ffn_conv'], p)
    x = x + g2 * f
    new = {'rwkv_wkv': s_wkv, 'rwkv_shift': s_shift, 's5_re': s_re, 's5_im': s_im, 'gdn': s_gdn,
           'gdn_conv': s_gdn_conv, 'lru_h': s_lru, 'lru_conv': s_lru_conv, 'ffn_conv': s_ffn}
    return x, new


def setup_inputs(seed: int = 0) -> dict:
    key = jax.random.key(seed)
    keys = iter(jax.random.split(key, 96))

    def nrm(shape, scale):
        return scale * jax.random.normal(next(keys), shape, F32)

    def unif(shape, lo, hi):
        return jax.random.uniform(next(keys), shape, F32, lo, hi)

    L, D = DEPTH, D_MODEL
    s5_lam_im = jnp.pi * jnp.arange(S5_N, dtype=F32) + nrm((L, S5_G, S5_N), 0.01)
    gdn_dt = jnp.exp(unif((L, GDN_H), math.log(1e-3), math.log(1e-1)))
    lru_a = unif((L, LRU_D), 0.9, 0.999) ** (1.0 / LRU_C)
    return {
        'x_prompt': nrm((BATCH, SEQ, D), 1.0),
        'x_sample': nrm((DEC_BATCH, DEC_SEQ, D), 1.0),
        'c_prompt': nrm((BATCH, D), 1.0),
        'c_sample': nrm((DEC_BATCH, D), 1.0),
        'state_rwkv_wkv': nrm((L, DEC_BATCH, RWKV_H, RWKV_HD, RWKV_HD), 0.5),
        'state_rwkv_shift': nrm((L, DEC_BATCH, RWKV_PROJ), 1.0),
        'state_s5_re': nrm((L, DEC_BATCH, S5_G, S5_N), 0.3),
        'state_s5_im': nrm((L, DEC_BATCH, S5_G, S5_N), 0.3),
        'state_gdn': nrm((L, DEC_BATCH, GDN_H, GDN_HD, GDN_HD), 0.1),
        'state_gdn_conv': nrm((L, DEC_BATCH, GDN_CONV - 1, 3 * GDN_D), 1.0),
        'state_lru_h': nrm((L, DEC_BATCH, LRU_D), 0.5),
        'state_lru_conv': nrm((L, DEC_BATCH, LRU_CONV - 1, LRU_D), 1.0),
        'state_ffn_conv': nrm((L, DEC_BATCH, FFN_CONV - 1, 2 * D_FF), 1.0),
        'ada_w': nrm((L, D, 6 * D), 0.3 * D ** -0.5),
        'ada_b': nrm((L, 6 * D), 0.02),
        'norm1_g': 1.0 + nrm((L, D), 0.02),
        'norm2_g': 1.0 + nrm((L, D), 0.02),
        'final_g': 1.0 + nrm((D,), 0.02),
        'w_in': nrm((L, D, N_IN), D ** -0.5),
        'w_out': nrm((L, D_MIX, D), D_MIX ** -0.5),
        'rwkv_mu': unif((L, RWKV_PROJ), 0.0, 1.0),
        'rwkv_w0': unif((L, RWKV_D), -6.0, 1.0),
        'rwkv_w_up': nrm((L, RWKV_R_DECAY, RWKV_D), 0.5 * RWKV_R_DECAY ** -0.5),
        'rwkv_a0': nrm((L, RWKV_D), 0.1),
        'rwkv_a_up': nrm((L, RWKV_R_A, RWKV_D), 0.5 * RWKV_R_A ** -0.5),
        'rwkv_g_up': nrm((L, RWKV_R_GATE, RWKV_D), RWKV_R_GATE ** -0.5),
        'rwkv_k_k': 0.85 + nrm((L, RWKV_D), 0.02),
        'rwkv_k_a': 1.0 + nrm((L, RWKV_D), 0.02),
        'rwkv_r_k': nrm((L, RWKV_H, RWKV_HD), 0.1),
        'rwkv_ln_w': 1.0 + nrm((L, RWKV_D), 0.02),
        'rwkv_ln_b': nrm((L, RWKV_D), 0.02),
        's5_lambda_re': -0.5 + nrm((L, S5_G, S5_N), 0.01),
        's5_lambda_im': s5_lam_im,
        's5_log_dt': unif((L, S5_G), math.log(1e-3), math.log(1e-1)),
        's5_b_re': nrm((L, S5_G, S5_N, S5_CH), (2 * S5_CH) ** -0.5),
        's5_b_im': nrm((L, S5_G, S5_N, S5_CH), (2 * S5_CH) ** -0.5),
        's5_c_re': nrm((L, S5_G, S5_CH, S5_N), 0.5),
        's5_c_im': nrm((L, S5_G, S5_CH, S5_N), 0.5),
        's5_d': nrm((L, S5_D), 0.5),
        's5_glu_w': nrm((L, S5_D, S5_D), S5_D ** -0.5),
        's5_glu_b': nrm((L, S5_D), 0.02),
        'gdn_conv_w': nrm((L, GDN_CONV, 3 * GDN_D), 0.5),
        'gdn_a_log': jnp.log(unif((L, GDN_H), 1.0, 16.0)),
        'gdn_dt_bias': gdn_dt + jnp.log(-jnp.expm1(-gdn_dt)),
        'gdn_norm_w': 1.0 + nrm((L, GDN_HD), 0.02),
        'lru_conv_w': nrm((L, LRU_CONV, LRU_D), 0.5),
        'lru_conv_b': nrm((L, LRU_D), 0.02),
        'lru_wr': nrm((L, LRU_BLOCKS, LRU_BS, LRU_BS), LRU_BS ** -0.5),
        'lru_br': nrm((L, LRU_D), 0.02),
        'lru_wi': nrm((L, LRU_BLOCKS, LRU_BS, LRU_BS), LRU_BS ** -0.5),
        'lru_bi': nrm((L, LRU_D), 0.02),
        'lru_lambda': jnp.log(lru_a) - jnp.log1p(-lru_a),
        'ffn_w_up': nrm((L, D, 2 * D_FF), D ** -0.5),
        'ffn_conv_w': nrm((L, FFN_CONV, 2 * D_FF), 3 ** -0.5),
        'ffn_conv_b': nrm((L, 2 * D_FF), 0.02),
        'ffn_w_down': nrm((L, D_FF, D), D_FF ** -0.5),
    }


def reference(x_prompt, x_sample, c_prompt, c_sample,
              state_rwkv_wkv, state_rwkv_shift, state_s5_re, state_s5_im, state_gdn, state_gdn_conv,
              state_lru_h, state_lru_conv, state_ffn_conv,
              ada_w, ada_b, norm1_g, norm2_g, final_g, w_in, w_out,
              rwkv_mu, rwkv_w0, rwkv_w_up, rwkv_a0, rwkv_a_up, rwkv_g_up, rwkv_k_k, rwkv_k_a, rwkv_r_k,
              rwkv_ln_w, rwkv_ln_b,
              s5_lambda_re, s5_lambda_im, s5_log_dt, s5_b_re, s5_b_im, s5_c_re, s5_c_im, s5_d,
              s5_glu_w, s5_glu_b,
              gdn_conv_w, gdn_a_log, gdn_dt_bias, gdn_norm_w,
              lru_conv_w, lru_conv_b, lru_wr, lru_br, lru_wi, lru_bi, lru_lambda,
              ffn_w_up, ffn_conv_w, ffn_conv_b, ffn_w_down):
    weights = {
        'ada_w': ada_w, 'ada_b': ada_b, 'norm1_g': norm1_g, 'norm2_g': norm2_g, 'w_in': w_in, 'w_out': w_out,
        'rwkv_mu': rwkv_mu, 'rwkv_w0': rwkv_w0, 'rwkv_w_up': rwkv_w_up, 'rwkv_a0': rwkv_a0,
        'rwkv_a_up': rwkv_a_up, 'rwkv_g_up': rwkv_g_up, 'rwkv_k_k': rwkv_k_k, 'rwkv_k_a': rwkv_k_a,
        'rwkv_r_k': rwkv_r_k, 'rwkv_ln_w': rwkv_ln_w, 'rwkv_ln_b': rwkv_ln_b,
        's5_lambda_re': s5_lambda_re, 's5_lambda_im': s5_lambda_im, 's5_log_dt': s5_log_dt,
        's5_b_re': s5_b_re, 's5_b_im': s5_b_im, 's5_c_re': s5_c_re, 's5_c_im': s5_c_im, 's5_d': s5_d,
        's5_glu_w': s5_glu_w, 's5_glu_b': s5_glu_b,
        'gdn_conv_w': gdn_conv_w, 'gdn_a_log': gdn_a_log, 'gdn_dt_bias': gdn_dt_bias, 'gdn_norm_w': gdn_norm_w,
        'lru_conv_w': lru_conv_w, 'lru_conv_b': lru_conv_b, 'lru_wr': lru_wr, 'lru_br': lru_br,
        'lru_wi': lru_wi, 'lru_bi': lru_bi, 'lru_lambda': lru_lambda,
        'ffn_w_up': ffn_w_up, 'ffn_conv_w': ffn_conv_w, 'ffn_conv_b': ffn_conv_b, 'ffn_w_down': ffn_w_down,
    }
    cache = {'rwkv_wkv': state_rwkv_wkv, 'rwkv_shift': state_rwkv_shift, 's5_re': state_s5_re,
             's5_im': state_s5_im, 'gdn': state_gdn, 'gdn_conv': state_gdn_conv, 'lru_h': state_lru_h,
             'lru_conv': state_lru_conv, 'ffn_conv': state_ffn_conv}
    bp = x_prompt.shape[0]
    zero_state = {n: jnp.zeros((bp,) + cache[n].shape[2:], F32) for n in STATE_NAMES}
    xp, xs = x_prompt, x_sample
    new_p = {n: [] for n in STATE_NAMES}
    new_s = {n: [] for n in STATE_NAMES}
    for l in range(DEPTH):
        p = {name: arr[l] for name, arr in weights.items()}
        xp, sp = trunk_layer(xp, c_prompt, zero_state, p)
        xs, ss = trunk_layer(xs, c_sample, {n: cache[n][l] for n in STATE_NAMES}, p)
        for n in STATE_NAMES:
            new_p[n].append(sp[n])
            new_s[n].append(ss[n])
    y_prompt = rms_norm(xp, final_g)
    y_sample = rms_norm(xs, final_g)
    P = {n: jnp.stack(new_p[n], axis=0) for n in STATE_NAMES}
    S = {n: jnp.stack(new_s[n], axis=0) for n in STATE_NAMES}
    return (y_prompt, y_sample,
            P['rwkv_wkv'], P['rwkv_shift'], P['s5_re'], P['s5_im'], P['gdn'], P['gdn_conv'],
            P['lru_h'], P['lru_conv'], P['ffn_conv'],
            S['rwkv_wkv'], S['rwkv_shift'], S['s5_re'], S['s5_im'], S['gdn'], S['gdn_conv'],
            S['lru_h'], S['lru_conv'], S['ffn_conv'])
```

```python
import functools
import math

import jax
import jax.numpy as jnp
from jax import lax
from jax.experimental import pallas as pl
from jax.experimental.pallas import tpu as pltpu

F32 = jnp.float32
BF16 = jnp.bfloat16
HI = lax.Precision.HIGHEST

NORM_EPS = 1e-6
RWKV_LN_EPS = 64e-5
LRU_C = 8.0

D_MODEL = 2048
RWKV_D, RWKV_HD, RWKV_H = 512, 64, 8
RWKV_R_DECAY, RWKV_R_A, RWKV_R_GATE = 96, 96, 256
RWKV_PROJ = 3 * RWKV_D + RWKV_R_DECAY + RWKV_R_A + RWKV_R_GATE
S5_D, S5_CH, S5_G, S5_N = 512, 16, 32, 64
S5_STATE = S5_G * S5_N
GDN_D, GDN_HD, GDN_H = 512, 128, 4
GDN_PROJ = 4 * GDN_D + 2 * GDN_H
LRU_D, LRU_BLOCKS = 512, 8
D_FF = 5632

RW_W = 2048
S5_OFF = 2048
GDN_OFF, GDN_W = 2560, 2560
LRU_OFF, LRU_W = 5120, 1024
NP_IN = 6144

CHUNK = 64
SCAN_T = 128
VMEM_LIMIT = 56 * 1024 * 1024


def _cp(sem):
    return pltpu.CompilerParams(dimension_semantics=sem, vmem_limit_bytes=VMEM_LIMIT)


def _dot(a, b, prec=None):
    return jnp.dot(a, b, preferred_element_type=F32, precision=prec)


def _dot_nt(a, b, prec=None):
    return lax.dot_general(a, b, (((1,), (1,)), ((), ())), preferred_element_type=F32, precision=prec)


def _dot_tn(a, b, prec=None):
    return lax.dot_general(a, b, (((0,), (0,)), ((), ())), preferred_element_type=F32, precision=prec)


def _bdot(a, b):
    return _dot(a.astype(BF16), b.astype(BF16))


def _silu(x):
    return x * jax.nn.sigmoid(x)


def _gelu(x):
    return 0.5 * x * (1.0 + jnp.tanh(math.sqrt(2.0 / math.pi) * (x + 0.044715 * (x * x * x))))


def _seg_dot(x, seg):
    hi = x.astype(BF16)
    lo = (x - hi.astype(F32)).astype(BF16)
    n = x.shape[0]
    r = _dot(jnp.concatenate([hi, lo], axis=0), seg)
    return r[:n] + r[n:]


def _iota(shape, dim):
    return lax.broadcasted_iota(jnp.int32, shape, dim)


def _neumann_solve(n, rhs, steps):
    x = rhs
    p = n
    for i in range(steps):
        x = x + _bdot(p, x)
        if i + 1 < steps:
            p = _bdot(p, p)
    return x


def _split(x):
    hi = x.astype(BF16)
    return hi, (x - hi.astype(F32)).astype(BF16)


def _dot3(a, b):
    ah, al = _split(a)
    bh, bl = _split(b)
    return _dot(ah, bh) + (_dot(ah, bl) + _dot(al, bh))


def _neumann_solve3(n, rhs, steps):
    x = rhs
    p = n
    for i in range(steps):
        x = x + _dot3(p, x)
        if i + 1 < steps:
            p = _dot3(p, p)
    return x


def _row_to_col(row, eye):
    n = eye.shape[0]
    return jnp.sum(jnp.where(eye, jnp.broadcast_to(row, (n, n)), 0.0), axis=-1, keepdims=True)


def _ada_kernel(c_ref, w_ref, b_ref, o_ref):
    c = c_ref[...]
    o_ref[0] = _dot(_silu(c).astype(BF16), w_ref[0].astype(BF16)) + b_ref[0]


def ada_mod(c_all, ada_w, ada_b):
    depth, d, n = ada_w.shape
    r = c_all.shape[0]
    tn = 1024
    return pl.pallas_call(
        _ada_kernel,
        grid=(depth, n // tn),
        in_specs=[pl.BlockSpec((r, d), lambda l, j: (0, 0)),
                  pl.BlockSpec((1, d, tn), lambda l, j: (l, 0, j)),
                  pl.BlockSpec((1, 1, tn), lambda l, j: (l, 0, j))],
        out_specs=pl.BlockSpec((1, r, tn), lambda l, j: (l, 0, j)),
        out_shape=jax.ShapeDtypeStruct((depth, r, n), F32),
        compiler_params=_cp(("arbitrary", "arbitrary")),
        name="ada_mod",
    )(c_all, ada_w, ada_b.reshape(depth, 1, n))


def _norm_mod_kernel(x_ref, g_ref, sc_ref, sh_ref, o_ref):
    x = x_ref[0]
    h = x * lax.rsqrt(jnp.mean(x * x, axis=-1, keepdims=True) + NORM_EPS) * g_ref[...]
    o_ref[0] = (h * (1.0 + sc_ref[0]) + sh_ref[0]).astype(o_ref.dtype)


def _mod_spec(arr, tm):
    d = arr.shape[-1]
    if arr.shape[1] == 1:
        return pl.BlockSpec((1, 1, d), lambda g, i: (g, 0, 0))
    return pl.BlockSpec((1, tm, d), lambda g, i: (g, i, 0))


def norm_mod(x, gain, sc, sh, tm):
    g, r, d = x.shape
    return pl.pallas_call(
        _norm_mod_kernel,
        grid=(g, r // tm),
        in_specs=[pl.BlockSpec((1, tm, d), lambda g, i: (g, i, 0)),
                  pl.BlockSpec((1, d), lambda g, i: (0, 0)),
                  _mod_spec(sc, tm), _mod_spec(sh, tm)],
        out_specs=pl.BlockSpec((1, tm, d), lambda g, i: (g, i, 0)),
        out_shape=jax.ShapeDtypeStruct((g, r, d), BF16),
        compiler_params=_cp(("arbitrary", "arbitrary")),
        name="norm_mod",
    )(x, gain.reshape(1, d), sc, sh)


def _rms_kernel(x_ref, g_ref, o_ref):
    x = x_ref[0]
    o_ref[0] = x * lax.rsqrt(jnp.mean(x * x, axis=-1, keepdims=True) + NORM_EPS) * g_ref[...]


def final_norm(x, gain, tm):
    g, r, d = x.shape
    return pl.pallas_call(
        _rms_kernel,
        grid=(g, r // tm),
        in_specs=[pl.BlockSpec((1, tm, d), lambda g, i: (g, i, 0)),
                  pl.BlockSpec((1, d), lambda g, i: (0, 0))],
        out_specs=pl.BlockSpec((1, tm, d), lambda g, i: (g, i, 0)),
        out_shape=jax.ShapeDtypeStruct((g, r, d), F32),
        compiler_params=_cp(("arbitrary", "arbitrary")),
        name="final_norm",
    )(x, gain.reshape(1, d))


def _mm_kernel(a_ref, w_ref, o_ref):
    o_ref[0] = _dot(a_ref[0], w_ref[...])


def matmul_in(a, w, tm, tn):
    g, r, k = a.shape
    n = w.shape[1]
    return pl.pallas_call(
        _mm_kernel,
        grid=(g, r // tm, n // tn),
        in_specs=[pl.BlockSpec((1, tm, k), lambda g, i, j: (g, i, 0)),
                  pl.BlockSpec((k, tn), lambda g, i, j: (0, j))],
        out_specs=pl.BlockSpec((1, tm, tn), lambda g, i, j: (g, i, j)),
        out_shape=jax.ShapeDtypeStruct((g, r, n), F32),
        compiler_params=_cp(("arbitrary", "arbitrary", "arbitrary")),
        name="matmul_in",
    )(a, w)


def _mm_res_kernel(n_a, *refs):
    a_refs = refs[:n_a]
    w_refs = refs[n_a:2 * n_a]
    x_ref, g_ref, o_ref = refs[2 * n_a:]
    acc = _dot(a_refs[0][0], w_refs[0][...].astype(BF16))
    for a_ref, w_ref in zip(a_refs[1:], w_refs[1:]):
        acc = acc + _dot(a_ref[0], w_ref[...].astype(BF16))
    o_ref[0] = x_ref[0] + g_ref[0] * acc


def matmul_res(a_list, w, x, gate, tm, tn):
    g, r, n = x.shape
    n_a = len(a_list)
    k = a_list[0].shape[-1]
    a_specs = [pl.BlockSpec((1, tm, k), lambda g, i, j: (g, i, 0)) for _ in a_list]
    w_specs = [pl.BlockSpec((k, tn), functools.partial(lambda g, i, j, q: (q, j), q=q)) for q in range(n_a)]
    if gate.shape[1] == 1:
        g_spec = pl.BlockSpec((1, 1, tn), lambda g, i, j: (g, 0, j))
    else:
        g_spec = pl.BlockSpec((1, tm, tn), lambda g, i, j: (g, i, j))
    return pl.pallas_call(
        functools.partial(_mm_res_kernel, n_a),
        grid=(g, r // tm, n // tn),
        in_specs=a_specs + w_specs + [pl.BlockSpec((1, tm, tn), lambda g, i, j: (g, i, j)), g_spec],
        out_specs=pl.BlockSpec((1, tm, tn), lambda g, i, j: (g, i, j)),
        out_shape=jax.ShapeDtypeStruct((g, r, n), F32),
        compiler_params=_cp(("arbitrary", "arbitrary", "arbitrary")),
        name="matmul_res",
    )(*a_list, *([w] * n_a), x, gate)


def _ffn_up_seq_kernel(a_ref, wg_ref, wv_ref, cwg_ref, cwv_ref, cbg_ref, cbv_ref, act_ref, lg_ref, lv_ref):
    a = a_ref[0]
    rows = a.shape[0]
    rid = _iota((rows, 1), 0)

    def conv(up, cw_ref, cb_ref):
        cw = cw_ref[...]
        s1 = jnp.where(rid >= 1, pltpu.roll(up, 1, axis=0), 0.0)
        s2 = jnp.where(rid >= 2, pltpu.roll(up, 2, axis=0), 0.0)
        return cw[0:1] * s2 + cw[1:2] * s1 + cw[2:3] * up + cb_ref[...]

    up_g = _dot(a, wg_ref[...].astype(BF16))
    lg_ref[0] = up_g[rows - 2:rows]
    gate = conv(up_g, cwg_ref, cbg_ref)
    up_v = _dot(a, wv_ref[...].astype(BF16))
    lv_ref[0] = up_v[rows - 2:rows]
    val = conv(up_v, cwv_ref, cbv_ref)
    act_ref[0] = (_silu(gate) * val).astype(BF16)


def ffn_up_seq(h, w_up, conv_w, conv_b, tn):
    b, l, d = h.shape
    nt = D_FF // tn
    cb = conv_b.reshape(1, 2 * D_FF)
    return pl.pallas_call(
        _ffn_up_seq_kernel,
        grid=(b, nt),
        in_specs=[pl.BlockSpec((1, l, d), lambda b, j: (b, 0, 0)),
                  pl.BlockSpec((d, tn), lambda b, j: (0, j)),
                  pl.BlockSpec((d, tn), lambda b, j: (0, j + nt)),
                  pl.BlockSpec((3, tn), lambda b, j: (0, j)),
                  pl.BlockSpec((3, tn), lambda b, j: (0, j + nt)),
                  pl.BlockSpec((1, tn), lambda b, j: (0, j)),
                  pl.BlockSpec((1, tn), lambda b, j: (0, j + nt))],
        out_specs=[pl.BlockSpec((1, l, tn), lambda b, j: (b, 0, j)),
                   pl.BlockSpec((1, 2, tn), lambda b, j: (b, 0, j)),
                   pl.BlockSpec((1, 2, tn), lambda b, j: (b, 0, j))],
        out_shape=[jax.ShapeDtypeStruct((b, l, D_FF), BF16),
                   jax.ShapeDtypeStruct((b, 2, D_FF), F32),
                   jax.ShapeDtypeStruct((b, 2, D_FF), F32)],
        compiler_params=_cp(("arbitrary", "arbitrary")),
        name="ffn_up_seq",
    )(h, w_up, w_up, conv_w, conv_w, cb, cb)


def _ffn_up_step_kernel(a_ref, wg_ref, wv_ref, cwg_ref, cwv_ref, cbg_ref, cbv_ref,
                        b0g_ref, b1g_ref, b0v_ref, b1v_ref, act_ref, ug_ref, uv_ref):
    a = a_ref[...]
    up_g = _dot(a, wg_ref[...].astype(BF16))
    up_v = _dot(a, wv_ref[...].astype(BF16))
    ug_ref[...] = up_g
    uv_ref[...] = up_v
    cwg = cwg_ref[...]
    cwv = cwv_ref[...]
    gate = cwg[0:1] * b0g_ref[...] + cwg[1:2] * b1g_ref[...] + cwg[2:3] * up_g + cbg_ref[...]
    val = cwv[0:1] * b0v_ref[...] + cwv[1:2] * b1v_ref[...] + cwv[2:3] * up_v + cbv_ref[...]
    act_ref[...] = (_silu(gate) * val).astype(BF16)


def ffn_up_step(h, w_up, conv_w, conv_b, conv_state, tn):
    b, d = h.shape
    nt = D_FF // tn
    cb = conv_b.reshape(1, 2 * D_FF)
    st = conv_state.reshape(b, 4 * D_FF)
    return pl.pallas_call(
        _ffn_up_step_kernel,
        grid=(nt,),
        in_specs=[pl.BlockSpec((b, d), lambda j: (0, 0)),
                  pl.BlockSpec((d, tn), lambda j: (0, j)),
                  pl.BlockSpec((d, tn), lambda j: (0, j + nt)),
                  pl.BlockSpec((3, tn), lambda j: (0, j)),
                  pl.BlockSpec((3, tn), lambda j: (0, j + nt)),
                  pl.BlockSpec((1, tn), lambda j: (0, j)),
                  pl.BlockSpec((1, tn), lambda j: (0, j + nt)),
                  pl.BlockSpec((b, tn), lambda j: (0, j)),
                  pl.BlockSpec((b, tn), lambda j: (0, j + 2 * nt)),
                  pl.BlockSpec((b, tn), lambda j: (0, j + nt)),
                  pl.BlockSpec((b, tn), lambda j: (0, j + 3 * nt))],
        out_specs=[pl.BlockSpec((b, tn), lambda j: (0, j))] * 3,
        out_shape=[jax.ShapeDtypeStruct((b, D_FF), BF16),
                   jax.ShapeDtypeStruct((b, D_FF), F32),
                   jax.ShapeDtypeStruct((b, D_FF), F32)],
        compiler_params=_cp(("arbitrary",)),
        name="ffn_up_step",
    )(h, w_up, w_up, conv_w, conv_w, cb, cb, st, st, st, st)


def _rwkv_pre(p, prev, prm, seg):
    mu, w0, w_up, a0, a_up, g_up, k_k, k_a = prm
    xm = p + mu * (prev - p)
    r = xm[:, 0:512]
    k = xm[:, 512:1024]
    v = xm[:, 1024:1536]
    xw = xm[:, 1536:1664]
    xa = xm[:, 1664:1792]
    xg = xm[:, 1792:2048]
    log_w = -jnp.exp(-jax.nn.softplus(-(w0 + _bdot(jnp.tanh(xw), w_up))) - 0.5)
    a = jax.nn.sigmoid(a0 + _bdot(xa, a_up))
    g = _bdot(jax.nn.sigmoid(xg), g_up)
    kx = k * k_k
    kk = kx * lax.rsqrt(_seg_dot(kx * kx, seg) + 1e-12)
    k2 = k * (1.0 + (a - 1.0) * k_a)
    return r, k2, v, log_w, a, g, kk


def _rwkv_post(y, r, k2, v, g, r_k, ln_w, ln_b, seg):
    inv = 1.0 / RWKV_HD
    mean = _seg_dot(y, seg) * inv
    dlt = y - mean
    var = _seg_dot(dlt * dlt, seg) * inv
    yn = dlt * lax.rsqrt(var + RWKV_LN_EPS) * ln_w + ln_b
    bonus = _seg_dot(r * k2 * r_k, seg) * v
    return (yn + bonus) * g


def _rwkv_seq_kernel(p_ref, mu_ref, w0_ref, wup_ref, a0_ref, aup_ref, gup_ref, kk_ref, ka_ref, rk_ref,
                     lnw_ref, lnb_ref, seg_ref, y_ref, s_out_ref, s_ref, prev_ref, y_scr):
    c = CHUNK
    t = pl.program_id(1)

    @pl.when(t == 0)
    def _():
        s_ref[...] = jnp.zeros_like(s_ref)
        prev_ref[...] = jnp.zeros_like(prev_ref)

    p = p_ref[0]
    seg = seg_ref[...]
    rid = _iota((c, 1), 0)
    prev = jnp.where(rid == 0, prev_ref[...], pltpu.roll(p, 1, axis=0))
    prev_ref[...] = p[c - 1:c, :]
    prm = (mu_ref[...], w0_ref[...], wup_ref[...], a0_ref[...], aup_ref[...], gup_ref[...], kk_ref[...], ka_ref[...])
    r, k2, v, log_w, a, g, kk = _rwkv_pre(p, prev, prm, seg)

    ri = _iota((c, c), 0)
    ci = _iota((c, c), 1)
    tril = (ci <= ri).astype(F32)
    cum = _dot(tril, log_w, HI)
    e_in = jnp.exp(cum)
    e_out = jnp.exp(-cum)
    e_ex = jnp.exp(cum - log_w)
    tot = cum[c - 1:c, :]
    e_tail = jnp.exp(tot - cum)
    e_tot = jnp.exp(tot)
    alpha = -(a * kk)
    al_t = (alpha * e_out).astype(BF16)
    k_t = (k2 * e_out).astype(BF16)
    be_t = (kk * e_ex).astype(BF16)
    r_t = (r * e_in).astype(BF16)
    al_p = (alpha * e_tail).astype(BF16)
    k_p = (k2 * e_tail).astype(BF16)
    vb = v.astype(BF16)

    r2 = _iota((c, 2 * c), 0)
    c2 = _iota((c, 2 * c), 1)
    cm = jnp.where(c2 >= c, c2 - c, c2)
    strict2 = cm < r2
    incl2 = cm <= r2
    eye = _iota((RWKV_HD, RWKV_HD), 0) == _iota((RWKV_HD, RWKV_HD), 1)
    zeros_v = jnp.zeros((c, RWKV_HD), BF16)

    for h in range(RWKV_H):
        sl = slice(h * RWKV_HD, (h + 1) * RWKV_HD)
        s0 = s_ref[h]
        lhs = jnp.concatenate([be_t[:, sl], r_t[:, sl]], axis=0)
        rhs = jnp.concatenate([al_t[:, sl], k_t[:, sl]], axis=0)
        m1 = _dot_nt(lhs, rhs)
        m2 = _dot(lhs, s0.astype(BF16))
        l_full = jnp.where(strict2, m1[:c], 0.0)
        a_full = jnp.where(incl2, m1[c:], 0.0)
        v_h = vb[:, sl]
        rhs_u = m2[:c] + _dot(jnp.where(c2 >= c, l_full, 0.0).astype(BF16), jnp.concatenate([zeros_v, v_h], axis=0))
        u = _neumann_solve(l_full[:, :c], rhs_u, 6)
        uv = jnp.concatenate([u.astype(BF16), v_h], axis=0)
        y_scr[:, sl] = m2[c:] + _dot(a_full.astype(BF16), uv)
        scale = _row_to_col(e_tot[:, sl], eye)
        s_ref[h] = s0 * scale + _dot_tn(jnp.concatenate([al_p[:, sl], k_p[:, sl]], axis=0), uv)

    out = _rwkv_post(y_scr[...], r, k2, v, g, rk_ref[...], lnw_ref[...], lnb_ref[...], seg)
    y_ref[0] = out.astype(BF16)

    @pl.when(t == pl.num_programs(1) - 1)
    def _():
        s_out_ref[0] = s_ref[...]


def _const_spec(arr, nidx):
    zeros = (0,) * arr.ndim
    if nidx == 1:
        return pl.BlockSpec(arr.shape, lambda i: zeros)
    return pl.BlockSpec(arr.shape, lambda b, t: zeros)


def rwkv_seq(proj, prm, seg):
    b, l, _ = proj.shape
    c = CHUNK
    return pl.pallas_call(
        _rwkv_seq_kernel,
        grid=(b, l // c),
        in_specs=[pl.BlockSpec((1, c, RW_W), lambda b, t: (b, t, 0))] + [_const_spec(x, 2) for x in prm] + [_const_spec(seg, 2)],
        out_specs=[pl.BlockSpec((1, c, RWKV_D), lambda b, t: (b, t, 0)),
                   pl.BlockSpec((1, RWKV_H, RWKV_HD, RWKV_HD), lambda b, t: (b, 0, 0, 0))],
        out_shape=[jax.ShapeDtypeStruct((b, l, RWKV_D), BF16),
                   jax.ShapeDtypeStruct((b, RWKV_H, RWKV_HD, RWKV_HD), F32)],
        scratch_shapes=[pltpu.VMEM((RWKV_H, RWKV_HD, RWKV_HD), F32),
                        pltpu.VMEM((1, RW_W), F32),
                        pltpu.VMEM((c, RWKV_D), F32)],
        compiler_params=_cp(("arbitrary", "arbitrary")),
        name="rwkv_seq",
    )(proj, *prm, seg)


def _rwkv_step_kernel(nb, p_ref, prev_ref, s_in_ref, mu_ref, w0_ref, wup_ref, a0_ref, aup_ref, gup_ref, kk_ref,
                      ka_ref, rk_ref, lnw_ref, lnb_ref, seg_ref, y_ref, s_out_ref, y_scr):
    seg = seg_ref[...]
    prm = (mu_ref[...], w0_ref[...], wup_ref[...], a0_ref[...], aup_ref[...], gup_ref[...], kk_ref[...], ka_ref[...])
    r, k2, v, log_w, a, g, kk = _rwkv_pre(p_ref[...], prev_ref[...], prm, seg)
    w = jnp.exp(log_w)
    kka = kk * a
    eye = _iota((RWKV_HD, RWKV_HD), 0) == _iota((RWKV_HD, RWKV_HD), 1)
    for i in range(nb):
        for h in range(RWKV_H):
            sl = slice(h * RWKV_HD, (h + 1) * RWKV_HD)
            w_c = _row_to_col(w[i:i + 1, sl], eye)
            kka_c = _row_to_col(kka[i:i + 1, sl], eye)
            kk_c = _row_to_col(kk[i:i + 1, sl], eye)
            k_c = _row_to_col(k2[i:i + 1, sl], eye)
            r_c = _row_to_col(r[i:i + 1, sl], eye)
            s = s_in_ref[i, h]
            sa = jnp.sum(kk_c * s, axis=0, keepdims=True)
            s_new = s * w_c - kka_c * sa + k_c * v[i:i + 1, sl]
            s_out_ref[i, h] = s_new
            y_scr[i:i + 1, sl] = jnp.sum(r_c * s_new, axis=0, keepdims=True)
    out = _rwkv_post(y_scr[...], r, k2, v, g, rk_ref[...], lnw_ref[...], lnb_ref[...], seg)
    y_ref[...] = out.astype(BF16)


def rwkv_step(proj, shift, state, prm, seg, nb):
    b = proj.shape[0]
    return pl.pallas_call(
        functools.partial(_rwkv_step_kernel, nb),
        grid=(b // nb,),
        in_specs=[pl.BlockSpec((nb, RW_W), lambda i: (i, 0)),
                  pl.BlockSpec((nb, RW_W), lambda i: (i, 0)),
                  pl.BlockSpec((nb, RWKV_H, RWKV_HD, RWKV_HD), lambda i: (i, 0, 0, 0))]
                 + [_const_spec(x, 1) for x in prm] + [_const_spec(seg, 1)],
        out_specs=[pl.BlockSpec((nb, RWKV_D), lambda i: (i, 0)),
                   pl.BlockSpec((nb, RWKV_H, RWKV_HD, RWKV_HD), lambda i: (i, 0, 0, 0))],
        out_shape=[jax.ShapeDtypeStruct((b, RWKV_D), BF16),
                   jax.ShapeDtypeStruct(state.shape, F32)],
        scratch_shapes=[pltpu.VMEM((nb, RWKV_D), F32)],
        compiler_params=_cp(("arbitrary",)),
        name="rwkv_step",
    )(proj, shift, state, *prm, seg)


def _s5_out(u, h_re, h_im, cc_ref, d_ref, gw_ref, gb_ref):
    y = _dot(jnp.concatenate([h_re, h_im], axis=1).astype(BF16), cc_ref[...]) + d_ref[...] * u
    z = _gelu(y)
    return z * jax.nn.sigmoid(_bdot(z, gw_ref[...]) + gb_ref[...])


def _s5_seq_kernel(u_ref, bbr_ref, bbi_ref, cc_ref, abr_ref, abi_ref, d_ref, gw_ref, gb_ref,
                   y_ref, hr_ref, hi_ref, cr_ref, ci_ref, pr_ref, pi_ref):
    tc = u_ref.shape[1]
    t = pl.program_id(1)
    rid = _iota((tc, 1), 0)
    steps = [1 << i for i in range(tc.bit_length() - 1)]

    @pl.when(t == 0)
    def _():
        cr_ref[...] = jnp.zeros_like(cr_ref)
        ci_ref[...] = jnp.zeros_like(ci_ref)
        pr = jnp.broadcast_to(abr_ref[...], (tc, S5_STATE))
        pi = jnp.broadcast_to(abi_ref[...], (tc, S5_STATE))
        for dd in steps:
            sr = pltpu.roll(pr, dd, axis=0)
            si = pltpu.roll(pi, dd, axis=0)
            m = rid >= dd
            pr, pi = jnp.where(m, pr * sr - pi * si, pr), jnp.where(m, pr * si + pi * sr, pi)
        pr_ref[...] = pr
        pi_ref[...] = pi

    u = u_ref[0]
    ub = u.astype(BF16)
    h_re = _dot(ub, bbr_ref[...])
    h_im = _dot(ub, bbi_ref[...])
    for dd in steps:
        ar = pr_ref[dd - 1:dd, :]
        ai = pi_ref[dd - 1:dd, :]
        sr = pltpu.roll(h_re, dd, axis=0)
        si = pltpu.roll(h_im, dd, axis=0)
        m = rid >= dd
        h_re, h_im = h_re + jnp.where(m, ar * sr - ai * si, 0.0), h_im + jnp.where(m, ar * si + ai * sr, 0.0)
    pr = pr_ref[...]
    pi = pi_ref[...]
    c_re = cr_ref[...]
    c_im = ci_ref[...]
    h_re, h_im = h_re + pr * c_re - pi * c_im, h_im + pr * c_im + pi * c_re
    cr_ref[...] = h_re[tc - 1:tc]
    ci_ref[...] = h_im[tc - 1:tc]
    y_ref[0] = _s5_out(u, h_re, h_im, cc_ref, d_ref, gw_ref, gb_ref).astype(BF16)

    @pl.when(t == pl.num_programs(1) - 1)
    def _():
        hr_ref[0] = h_re[tc - 1:tc]
        hi_ref[0] = h_im[tc - 1:tc]


def s5_seq(proj, prm):
    b, l, _ = proj.shape
    tc = SCAN_T
    return pl.pallas_call(
        _s5_seq_kernel,
        grid=(b, l // tc),
        in_specs=[pl.BlockSpec((1, tc, S5_D), lambda b, t: (b, t, S5_OFF // S5_D))] + [_const_spec(x, 2) for x in prm],
        out_specs=[pl.BlockSpec((1, tc, S5_D), lambda b, t: (b, t, 0)),
                   pl.BlockSpec((1, 1, S5_STATE), lambda b, t: (b, 0, 0)),
                   pl.BlockSpec((1, 1, S5_STATE), lambda b, t: (b, 0, 0))],
        out_shape=[jax.ShapeDtypeStruct((b, l, S5_D), BF16),
                   jax.ShapeDtypeStruct((b, 1, S5_STATE), F32),
                   jax.ShapeDtypeStruct((b, 1, S5_STATE), F32)],
        scratch_shapes=[pltpu.VMEM((1, S5_STATE), F32), pltpu.VMEM((1, S5_STATE), F32),
                        pltpu.VMEM((tc, S5_STATE), F32), pltpu.VMEM((tc, S5_STATE), F32)],
        compiler_params=_cp(("arbitrary", "arbitrary")),
        name="s5_seq",
    )(proj, *prm)


def _s5_step_kernel(u_ref, h0r_ref, h0i_ref, bbr_ref, bbi_ref, cc_ref, abr_ref, abi_ref, d_ref, gw_ref, gb_ref,
                    y_ref, hr_ref, hi_ref):
    u = u_ref[...]
    ar = abr_ref[...]
    ai = abi_ref[...]
    h0r = h0r_ref[...]
    h0i = h0i_ref[...]
    h_re = ar * h0r - ai * h0i + _dot(u, bbr_ref[...], HI)
    h_im = ar * h0i + ai * h0r + _dot(u, bbi_ref[...], HI)
    hr_ref[...] = h_re
    hi_ref[...] = h_im
    y_ref[...] = _s5_out(u, h_re, h_im, cc_ref, d_ref, gw_ref, gb_ref).astype(BF16)


def s5_step(proj, h0_re, h0_im, prm):
    b = proj.shape[0]
    return pl.pallas_call(
        _s5_step_kernel,
        grid=(1,),
        in_specs=[pl.BlockSpec((b, S5_D), lambda i: (0, S5_OFF // S5_D)),
                  _const_spec(h0_re, 1), _const_spec(h0_im, 1)] + [_const_spec(x, 1) for x in prm],
        out_specs=[pl.BlockSpec((b, S5_D), lambda i: (0, 0)),
                   pl.BlockSpec((b, S5_STATE), lambda i: (0, 0)),
                   pl.BlockSpec((b, S5_STATE), lambda i: (0, 0))],
        out_shape=[jax.ShapeDtypeStruct((b, S5_D), BF16),
                   jax.ShapeDtypeStruct((b, S5_STATE), F32),
                   jax.ShapeDtypeStruct((b, S5_STATE), F32)],
        compiler_params=_cp(("arbitrary",)),
        name="s5_step",
    )(proj, h0_re, h0_im, *prm)


def _gdn_pre(qkv_c, z, ab, alog_ref, dtb_ref):
    act = _silu(qkv_c)
    qs, ks = [], []
    for h in range(GDN_H):
        qh = act[:, h * GDN_HD:(h + 1) * GDN_HD]
        kh = act[:, GDN_D + h * GDN_HD:GDN_D + (h + 1) * GDN_HD]
        qs.append(qh * lax.rsqrt(jnp.sum(qh * qh, axis=-1, keepdims=True) + 1e-6))
        ks.append(kh * lax.rsqrt(jnp.sum(kh * kh, axis=-1, keepdims=True) + 1e-6))
    v = act[:, 2 * GDN_D:3 * GDN_D]
    g = -jnp.exp(alog_ref[...]) * jax.nn.softplus(ab + dtb_ref[...])
    beta = jax.nn.sigmoid(ab)
    return qs, ks, v, g, beta


def _gdn_post(o, z_h, nw):
    on = o * lax.rsqrt(jnp.mean(o * o, axis=-1, keepdims=True) + NORM_EPS) * nw
    return on * _silu(z_h)


def _gdn_seq_kernel(p_ref, cw_ref, alog_ref, dtb_ref, nw_ref, y_ref, s_out_ref, s_ref, prev_ref):
    c = CHUNK
    t = pl.program_id(1)

    @pl.when(t == 0)
    def _():
        s_ref[...] = jnp.zeros_like(s_ref)
        prev_ref[...] = jnp.zeros_like(prev_ref)

    x = p_ref[0]
    qkv = x[:, 0:3 * GDN_D]
    z = x[:, 3 * GDN_D:4 * GDN_D]
    ab = x[:, 4 * GDN_D:4 * GDN_D + 128]
    ext = jnp.concatenate([prev_ref[...], qkv], axis=0)
    prev_ref[...] = qkv[c - 8:c]
    cw = cw_ref[...]
    conv = cw[3:4] * qkv
    for j in (1, 2, 3):
        conv = conv + cw[3 - j:4 - j] * pltpu.roll(ext, j, axis=0)[8:]
    qs, ks, v, g, beta = _gdn_pre(conv, z, ab, alog_ref, dtb_ref)

    hc = GDN_H * c
    heads = range(GDN_H)
    stack = lambda pieces: jnp.concatenate(pieces, axis=0)
    ri = _iota((c, c), 0)
    ci = _iota((c, c), 1)
    gc = _dot((ci <= ri).astype(F32), g, HI)
    lane = _iota((c, 128), 1)
    g_col = stack([gc[:, h:h + 1] for h in heads])
    b_col = stack([beta[:, GDN_H + h:GDN_H + h + 1] for h in heads])
    g_row = _dot_nt(jnp.ones((hc, 128), F32), stack([jnp.where(lane == h, gc, 0.0) for h in heads]), HI)
    rr = _iota((hc, hc), 0)
    cc = _iota((hc, hc), 1)
    shift = c.bit_length() - 1
    same = lax.shift_right_logical(rr, shift) == lax.shift_right_logical(cc, shift)
    causal = same & (cc <= rr)
    strict = same & (cc < rr)
    decay = jnp.where(causal, jnp.exp(jnp.where(causal, g_col - g_row, 0.0)), 0.0)
    q = stack(qs) * (GDN_HD ** -0.5)
    k = stack(ks)
    v_s = stack([v[:, h * GDN_HD:(h + 1) * GDN_HD] for h in heads])
    kb = k * b_col
    m1 = _dot_nt(stack([kb, q]).astype(BF16), k.astype(BF16))
    lmat = jnp.where(strict, m1[:hc] * decay, 0.0)
    attn = m1[hc:] * decay
    eg = jnp.exp(g_col)
    sol = _neumann_solve3(-lmat, jnp.concatenate([v_s * b_col, kb * eg], axis=1), shift)
    u = sol[:, :GDN_HD]
    wk = sol[:, GDN_HD:]
    qd = q * eg
    ws = [_dot(stack([wk[h * c:(h + 1) * c], qd[h * c:(h + 1) * c]]).astype(BF16), s_ref[h].astype(BF16))
          for h in heads]
    v_new = u - stack([w[:c] for w in ws])
    o = stack([w[c:] for w in ws]) + _bdot(attn, v_new)
    g_last = [gc[c - 1:c, h:h + 1] for h in heads]
    k_tail = k * jnp.exp(stack([jnp.broadcast_to(gl, (c, 1)) for gl in g_last]) - g_col)
    nw = nw_ref[...]
    for h in heads:
        rows = slice(h * c, (h + 1) * c)
        s_ref[h] = s_ref[h] * jnp.exp(g_last[h]) + _dot_tn(k_tail[rows].astype(BF16), v_new[rows].astype(BF16))
        z_h = z[:, h * GDN_HD:(h + 1) * GDN_HD]
        y_ref[0, :, h * GDN_HD:(h + 1) * GDN_HD] = _gdn_post(o[rows], z_h, nw).astype(BF16)

    @pl.when(t == pl.num_programs(1) - 1)
    def _():
        s_out_ref[0] = s_ref[...]


def gdn_seq(proj, prm):
    b, l, _ = proj.shape
    c = CHUNK
    return pl.pallas_call(
        _gdn_seq_kernel,
        grid=(b, l // c),
        in_specs=[pl.BlockSpec((1, c, GDN_W), lambda b, t: (b, t, GDN_OFF // GDN_W))] + [_const_spec(x, 2) for x in prm],
        out_specs=[pl.BlockSpec((1, c, GDN_D), lambda b, t: (b, t, 0)),
                   pl.BlockSpec((1, GDN_H, GDN_HD, GDN_HD), lambda b, t: (b, 0, 0, 0))],
        out_shape=[jax.ShapeDtypeStruct((b, l, GDN_D), BF16),
                   jax.ShapeDtypeStruct((b, GDN_H, GDN_HD, GDN_HD), F32)],
        scratch_shapes=[pltpu.VMEM((GDN_H, GDN_HD, GDN_HD), F32),
                        pltpu.VMEM((8, 3 * GDN_D), F32)],
        compiler_params=_cp(("arbitrary", "arbitrary")),
        name="gdn_seq",
    )(proj, *prm)


def _gdn_step_kernel(nb, p_ref, cs_ref, s_in_ref, cw_ref, alog_ref, dtb_ref, nw_ref, y_ref, s_out_ref):
    x = p_ref[...]
    qkv = x[:, 0:3 * GDN_D]
    z = x[:, 3 * GDN_D:4 * GDN_D]
    ab = x[:, 4 * GDN_D:4 * GDN_D + 128]
    cs = cs_ref[...]
    cw = cw_ref[...]
    w3 = 3 * GDN_D
    conv = cw[0:1] * cs[:, 0:w3] + cw[1:2] * cs[:, w3:2 * w3] + cw[2:3] * cs[:, 2 * w3:3 * w3] + cw[3:4] * qkv
    qs, ks, v, g, beta = _gdn_pre(conv, z, ab, alog_ref, dtb_ref)
    eye = _iota((GDN_HD, GDN_HD), 0) == _iota((GDN_HD, GDN_HD), 1)
    nw = nw_ref[...]
    for i in range(nb):
        for h in range(GDN_H):
            k_c = _row_to_col(ks[h][i:i + 1], eye)
            q_c = _row_to_col(qs[h][i:i + 1], eye)
            eg = jnp.exp(g[i:i + 1, h:h + 1])
            b_s = beta[i:i + 1, GDN_H + h:GDN_H + h + 1]
            s = s_in_ref[i, h]
            sk = jnp.sum(k_c * s, axis=0, keepdims=True)
            v_row = v[i:i + 1, h * GDN_HD:(h + 1) * GDN_HD]
            s_new = s * eg + (k_c * b_s) * (v_row - eg * sk)
            s_out_ref[i, h] = s_new
            o = jnp.sum(q_c * s_new, axis=0, keepdims=True) * (GDN_HD ** -0.5)
            z_h = z[i:i + 1, h * GDN_HD:(h + 1) * GDN_HD]
            y_ref[i:i + 1, h * GDN_HD:(h + 1) * GDN_HD] = _gdn_post(o, z_h, nw).astype(BF16)


def gdn_step(proj, conv_state, state, prm, nb):
    b = proj.shape[0]
    return pl.pallas_call(
        functools.partial(_gdn_step_kernel, nb),
        grid=(b // nb,),
        in_specs=[pl.BlockSpec((nb, GDN_W), lambda i: (i, GDN_OFF // GDN_W)),
                  pl.BlockSpec((nb, 9 * GDN_D), lambda i: (i, 0)),
                  pl.BlockSpec((nb, GDN_H, GDN_HD, GDN_HD), lambda i: (i, 0, 0, 0))] + [_const_spec(x, 1) for x in prm],
        out_specs=[pl.BlockSpec((nb, GDN_D), lambda i: (i, 0)),
                   pl.BlockSpec((nb, GDN_H, GDN_HD, GDN_HD), lambda i: (i, 0, 0, 0))],
        out_shape=[jax.ShapeDtypeStruct((b, GDN_D), BF16),
                   jax.ShapeDtypeStruct(state.shape, F32)],
        compiler_params=_cp(("arbitrary",)),
        name="gdn_step",
    )(proj, conv_state, state, *prm)


def _lru_gates(xc, gate, wr_ref, br_ref, wi_ref, bi_ref, lam_ref):
    r = jax.nn.sigmoid(_bdot(xc, wr_ref[...]) + br_ref[...])
    i = jax.nn.sigmoid(_bdot(xc, wi_ref[...]) + bi_ref[...])
    log_a = -LRU_C * r * jax.nn.softplus(-lam_ref[...])
    a = jnp.exp(log_a)
    one_minus_a2 = -jnp.tanh(log_a) * (a * a + 1.0)
    b = jnp.sqrt(one_minus_a2) * (i * xc)
    return a, b, _gelu(gate)


def _lru_seq_kernel(p_ref, cw_ref, cb_ref, wr_ref, br_ref, wi_ref, bi_ref, lam_ref, y_ref, h_out_ref,
                    prev_ref, carry_ref):
    tc = p_ref.shape[1]
    t = pl.program_id(1)

    @pl.when(t == 0)
    def _():
        prev_ref[...] = jnp.zeros_like(prev_ref)
        carry_ref[...] = jnp.zeros_like(carry_ref)

    x = p_ref[0]
    xr = x[:, :LRU_D]
    gate = x[:, LRU_D:]
    ext = jnp.concatenate([prev_ref[...], xr], axis=0)
    prev_ref[...] = xr[tc - 8:tc]
    cw = cw_ref[...]
    xc = cw[3:4] * xr + cb_ref[...]
    for j in (1, 2, 3):
        xc = xc + cw[3 - j:4 - j] * pltpu.roll(ext, j, axis=0)[8:]
    a, b, gg = _lru_gates(xc, gate, wr_ref, br_ref, wi_ref, bi_ref, lam_ref)
    rid = _iota((tc, 1), 0)
    dd = 1
    while dd < tc:
        m = rid >= dd
        sa = pltpu.roll(a, dd, axis=0)
        sb = pltpu.roll(b, dd, axis=0)
        a, b = jnp.where(m, a * sa, a), jnp.where(m, a * sb + b, b)
        dd *= 2
    h = b + a * carry_ref[...]
    carry_ref[...] = h[tc - 1:tc]
    y_ref[0] = (h * gg).astype(BF16)

    @pl.when(t == pl.num_programs(1) - 1)
    def _():
        h_out_ref[0] = h[tc - 1:tc]


def lru_seq(proj, prm):
    b, l, _ = proj.shape
    tc = SCAN_T
    return pl.pallas_call(
        _lru_seq_kernel,
        grid=(b, l // tc),
        in_specs=[pl.BlockSpec((1, tc, LRU_W), lambda b, t: (b, t, LRU_OFF // LRU_W))] + [_const_spec(x, 2) for x in prm],
        out_specs=[pl.BlockSpec((1, tc, LRU_D), lambda b, t: (b, t, 0)),
                   pl.BlockSpec((1, 1, LRU_D), lambda b, t: (b, 0, 0))],
        out_shape=[jax.ShapeDtypeStruct((b, l, LRU_D), BF16),
                   jax.ShapeDtypeStruct((b, 1, LRU_D), F32)],
        scratch_shapes=[pltpu.VMEM((8, LRU_D), F32), pltpu.VMEM((1, LRU_D), F32)],
        compiler_params=_cp(("arbitrary", "arbitrary")),
        name="lru_seq",
    )(proj, *prm)


def _lru_step_kernel(p_ref, cs_ref, h0_ref, cw_ref, cb_ref, wr_ref, br_ref, wi_ref, bi_ref, lam_ref, y_ref, h_ref):
    x = p_ref[...]
    xr = x[:, :LRU_D]
    gate = x[:, LRU_D:]
    cs = cs_ref[...]
    cw = cw_ref[...]
    xc = (cw[0:1] * cs[:, 0:LRU_D] + cw[1:2] * cs[:, LRU_D:2 * LRU_D] + cw[2:3] * cs[:, 2 * LRU_D:3 * LRU_D]
          + cw[3:4] * xr + cb_ref[...])
    a, b, gg = _lru_gates(xc, gate, wr_ref, br_ref, wi_ref, bi_ref, lam_ref)
    h = a * h0_ref[...] + b
    h_ref[...] = h
    y_ref[...] = (h * gg).astype(BF16)


def lru_step(proj, conv_state, h0, prm):
    b = proj.shape[0]
    return pl.pallas_call(
        _lru_step_kernel,
        grid=(1,),
        in_specs=[pl.BlockSpec((b, LRU_W), lambda i: (0, LRU_OFF // LRU_W)),
                  _const_spec(conv_state, 1), _const_spec(h0, 1)] + [_const_spec(x, 1) for x in prm],
        out_specs=[pl.BlockSpec((b, LRU_D), lambda i: (0, 0)),
                   pl.BlockSpec((b, LRU_D), lambda i: (0, 0))],
        out_shape=[jax.ShapeDtypeStruct((b, LRU_D), BF16),
                   jax.ShapeDtypeStruct((b, LRU_D), F32)],
        compiler_params=_cp(("arbitrary",)),
        name="lru_step",
    )(proj, conv_state, h0, *prm)


def _pad_cols(x, width):
    return jnp.pad(x, [(0, 0)] * (x.ndim - 1) + [(0, width - x.shape[-1])])


def _pack_rwkv_cols(x):
    return jnp.concatenate([x[..., :1536], _pad_cols(x[..., 1536:1632], 128), _pad_cols(x[..., 1632:1728], 128),
                            x[..., 1728:1984]], axis=-1)


def _unpack_rwkv_cols(x):
    return jnp.concatenate([x[..., :1536], x[..., 1536:1632], x[..., 1664:1760], x[..., 1792:2048]], axis=-1)


def _pack_w_in(w):
    o = RWKV_PROJ
    return jnp.concatenate([_pack_rwkv_cols(w[:, :o]), w[:, o:o + S5_D],
                            _pad_cols(w[:, o + S5_D:o + S5_D + GDN_PROJ], GDN_W), w[:, o + S5_D + GDN_PROJ:]],
                           axis=-1).astype(BF16)


def _block_diag(blocks):
    g, a, b = blocks.shape
    eye = jnp.eye(g, dtype=blocks.dtype)
    return (eye[:, None, :, None] * blocks[:, :, None, :]).reshape(g * a, g * b)


def _s5_params(lam_re, lam_im, log_dt, b_re, b_im, c_re, c_im, d, glu_w, glu_b):
    dt = jnp.exp(log_dt)[:, None]
    mag = jnp.exp(lam_re * dt)
    ab_re, ab_im = mag * jnp.cos(lam_im * dt), mag * jnp.sin(lam_im * dt)
    den = lam_re * lam_re + lam_im * lam_im
    nr = ab_re - 1.0
    f_re = (nr * lam_re + ab_im * lam_im) / den
    f_im = (ab_im * lam_re - nr * lam_im) / den
    bb_re = f_re[..., None] * b_re - f_im[..., None] * b_im
    bb_im = f_re[..., None] * b_im + f_im[..., None] * b_re
    bbr = _block_diag(jnp.swapaxes(bb_re, 1, 2))
    bbi = _block_diag(jnp.swapaxes(bb_im, 1, 2))
    cc = jnp.concatenate([_block_diag(jnp.swapaxes(c_re, 1, 2)), -_block_diag(jnp.swapaxes(c_im, 1, 2))],
                         axis=0).astype(BF16)
    tail = (cc, ab_re.reshape(1, S5_STATE), ab_im.reshape(1, S5_STATE), d.reshape(1, S5_D), glu_w,
            glu_b.reshape(1, S5_D))
    return (bbr.astype(BF16), bbi.astype(BF16)) + tail, (bbr, bbi) + tail


def _layer(xp, xs, mod, st, p, seg):
    bp, lp, d = xp.shape
    bs = xs.shape[1]
    sh1, sc1, g1, sh2, sc2, g2 = jnp.split(mod, 6, axis=-1)
    pm = lambda m: m[:bp, None, :]
    sm = lambda m: m[None, bp:bp + bs, :]

    w_in = _pack_w_in(p['w_in'])
    row = lambda v: v.reshape(1, -1)
    rw_prm = (row(_pack_rwkv_cols(p['rwkv_mu'])), row(p['rwkv_w0']),
              jnp.pad(p['rwkv_w_up'], ((0, 32), (0, 0))), row(p['rwkv_a0']),
              jnp.pad(p['rwkv_a_up'], ((0, 32), (0, 0))), p['rwkv_g_up'], row(p['rwkv_k_k']), row(p['rwkv_k_a']),
              row(p['rwkv_r_k']), row(p['rwkv_ln_w']), row(p['rwkv_ln_b']))
    s5_prm, s5_prm_f32 = _s5_params(p['s5_lambda_re'], p['s5_lambda_im'], p['s5_log_dt'], p['s5_b_re'], p['s5_b_im'],
                        p['s5_c_re'], p['s5_c_im'], p['s5_d'], p['s5_glu_w'], p['s5_glu_b'])
    gdn_prm = (p['gdn_conv_w'], _pad_cols(row(p['gdn_a_log']), 128), _pad_cols(row(p['gdn_dt_bias']), 128),
               row(p['gdn_norm_w']))
    lru_prm = (p['lru_conv_w'], row(p['lru_conv_b']), _block_diag(p['lru_wr']), row(p['lru_br']),
               _block_diag(p['lru_wi']), row(p['lru_bi']), row(p['lru_lambda']))

    h = norm_mod(xp, p['norm1_g'], pm(sc1), pm(sh1), 512)
    proj = matmul_in(h, w_in, 1024, 1024)
    ya, s_wkv = rwkv_seq(proj, rw_prm, seg)
    yb, s_re, s_im = s5_seq(proj, s5_prm)
    yc, s_gdn = gdn_seq(proj, gdn_prm)
    yd, s_lru = lru_seq(proj, lru_prm)
    xp = matmul_res([ya, yb, yc, yd], p['w_out'], xp, pm(g1), 1024, 512)
    h = norm_mod(xp, p['norm2_g'], pm(sc2), pm(sh2), 512)
    act, lg, lv = ffn_up_seq(h, p['ffn_w_up'], p['ffn_conv_w'], p['ffn_conv_b'], 256)
    xp = matmul_res([act], p['ffn_w_down'], xp, pm(g2), 1024, 256)
    o = RWKV_PROJ
    new_p = {
        'rwkv_wkv': s_wkv,
        'rwkv_shift': _unpack_rwkv_cols(proj[:, -1, :RW_W]),
        's5_re': s_re.reshape(bp, S5_G, S5_N), 's5_im': s_im.reshape(bp, S5_G, S5_N),
        'gdn': s_gdn,
        'gdn_conv': proj[:, -3:, GDN_OFF:GDN_OFF + 3 * GDN_D],
        'lru_h': s_lru.reshape(bp, LRU_D),
        'lru_conv': proj[:, -3:, LRU_OFF:LRU_OFF + LRU_D],
        'ffn_conv': jnp.concatenate([lg, lv], axis=-1),
    }

    h = norm_mod(xs, p['norm1_g'], sm(sc1), sm(sh1), bs)
    proj = matmul_in(h, w_in, bs, 1024)[0]
    ya, s_wkv = rwkv_step(proj, _pack_rwkv_cols(st['rwkv_shift']), st['rwkv_wkv'], rw_prm, seg, 16)
    yb, s_re, s_im = s5_step(proj, st['s5_re'].reshape(bs, S5_STATE), st['s5_im'].reshape(bs, S5_STATE), s5_prm_f32)
    yc, s_gdn = gdn_step(proj, st['gdn_conv'].reshape(bs, 9 * GDN_D), st['gdn'], gdn_prm, 16)
    yd, s_lru = lru_step(proj, st['lru_conv'].reshape(bs, 3 * LRU_D), st['lru_h'], lru_prm)
    xs = matmul_res([y[None] for y in (ya, yb, yc, yd)], p['w_out'], xs, sm(g1), bs, 512)
    h = norm_mod(xs, p['norm2_g'], sm(sc2), sm(sh2), bs)
    act, ug, uv = ffn_up_step(h[0], p['ffn_w_up'], p['ffn_conv_w'], p['ffn_conv_b'], st['ffn_conv'], 256)
    xs = matmul_res([act[None]], p['ffn_w_down'], xs, sm(g2), bs, 256)
    shift_rows = lambda buf, new: jnp.concatenate([buf[:, 1:], new[:, None, :]], axis=1)
    new_s = {
        'rwkv_wkv': s_wkv,
        'rwkv_shift': _unpack_rwkv_cols(proj[:, :RW_W]),
        's5_re': s_re.reshape(bs, S5_G, S5_N), 's5_im': s_im.reshape(bs, S5_G, S5_N),
        'gdn': s_gdn,
        'gdn_conv': shift_rows(st['gdn_conv'], proj[:, GDN_OFF:GDN_OFF + 3 * GDN_D]),
        'lru_h': s_lru,
        'lru_conv': shift_rows(st['lru_conv'], proj[:, LRU_OFF:LRU_OFF + LRU_D]),
        'ffn_conv': shift_rows(st['ffn_conv'], jnp.concatenate([ug, uv], axis=-1)),
    }
    return xp, xs, new_p, new_s


STATE_ORDER = ('rwkv_wkv', 'rwkv_shift', 's5_re', 's5_im', 'gdn', 'gdn_conv', 'lru_h', 'lru_conv', 'ffn_conv')


def kernel(x_prompt, x_sample, c_prompt, c_sample, state_rwkv_wkv, state_rwkv_shift, state_s5_re, state_s5_im, state_gdn, state_gdn_conv, state_lru_h, state_lru_conv, state_ffn_conv, ada_w, ada_b, norm1_g, norm2_g, final_g, w_in, w_out, rwkv_mu, rwkv_w0, rwkv_w_up, rwkv_a0, rwkv_a_up, rwkv_g_up, rwkv_k_k, rwkv_k_a, rwkv_r_k, rwkv_ln_w, rwkv_ln_b, s5_lambda_re, s5_lambda_im, s5_log_dt, s5_b_re, s5_b_im, s5_c_re, s5_c_im, s5_d, s5_glu_w, s5_glu_b, gdn_conv_w, gdn_a_log, gdn_dt_bias, gdn_norm_w, lru_conv_w, lru_conv_b, lru_wr, lru_br, lru_wi, lru_bi, lru_lambda, ffn_w_up, ffn_conv_w, ffn_conv_b, ffn_w_down):
    weights = {
        'norm1_g': norm1_g, 'norm2_g': norm2_g, 'w_in': w_in, 'w_out': w_out,
        'rwkv_mu': rwkv_mu, 'rwkv_w0': rwkv_w0, 'rwkv_w_up': rwkv_w_up, 'rwkv_a0': rwkv_a0,
        'rwkv_a_up': rwkv_a_up, 'rwkv_g_up': rwkv_g_up, 'rwkv_k_k': rwkv_k_k, 'rwkv_k_a': rwkv_k_a,
        'rwkv_r_k': rwkv_r_k, 'rwkv_ln_w': rwkv_ln_w, 'rwkv_ln_b': rwkv_ln_b,
        's5_lambda_re': s5_lambda_re, 's5_lambda_im': s5_lambda_im, 's5_log_dt': s5_log_dt,
        's5_b_re': s5_b_re, 's5_b_im': s5_b_im, 's5_c_re': s5_c_re, 's5_c_im': s5_c_im, 's5_d': s5_d,
        's5_glu_w': s5_glu_w, 's5_glu_b': s5_glu_b,
        'gdn_conv_w': gdn_conv_w, 'gdn_a_log': gdn_a_log, 'gdn_dt_bias': gdn_dt_bias, 'gdn_norm_w': gdn_norm_w,
        'lru_conv_w': lru_conv_w, 'lru_conv_b': lru_conv_b, 'lru_wr': lru_wr, 'lru_br': lru_br,
        'lru_wi': lru_wi, 'lru_bi': lru_bi, 'lru_lambda': lru_lambda,
        'ffn_w_up': ffn_w_up, 'ffn_conv_w': ffn_conv_w, 'ffn_conv_b': ffn_conv_b, 'ffn_w_down': ffn_w_down,
    }
    cache = {'rwkv_wkv': state_rwkv_wkv, 'rwkv_shift': state_rwkv_shift, 's5_re': state_s5_re,
             's5_im': state_s5_im, 'gdn': state_gdn, 'gdn_conv': state_gdn_conv, 'lru_h': state_lru_h,
             'lru_conv': state_lru_conv, 'ffn_conv': state_ffn_conv}
    depth = ada_w.shape[0]
    bp = x_prompt.shape[0]
    bs = x_sample.shape[0]
    rows = -(-(bp + bs) // 8) * 8
    c_all = jnp.pad(jnp.concatenate([c_prompt, c_sample], axis=0), ((0, rows - bp - bs), (0, 0)))
    mod = ada_mod(c_all, ada_w, ada_b)
    ids = jnp.arange(RWKV_D) // RWKV_HD
    seg = (ids[:, None] == ids[None, :]).astype(BF16)

    xp = x_prompt
    xs = jnp.swapaxes(x_sample, 0, 1)
    new_p = {n: [] for n in STATE_ORDER}
    new_s = {n: [] for n in STATE_ORDER}
    for l in range(depth):
        p = {name: arr[l] for name, arr in weights.items()}
        st = {n: cache[n][l] for n in STATE_ORDER}
        xp, xs, sp, ss = _layer(xp, xs, mod[l], st, p, seg)
        for n in STATE_ORDER:
            new_p[n].append(sp[n])
            new_s[n].append(ss[n])
    y_prompt = final_norm(xp, final_g, 512)
    y_sample = jnp.swapaxes(final_norm(xs, final_g, bs), 0, 1)
    outs_p = tuple(jnp.stack(new_p[n], axis=0) for n in STATE_ORDER)
    outs_s = tuple(jnp.stack(new_s[n], axis=0) for n in STATE_ORDER)
    return (y_prompt, y_sample) + outs_p + outs_s
```

```python
import functools
import math

import jax
import jax.numpy as jnp
from jax import lax
from jax.experimental import pallas as pl
from jax.experimental.pallas import tpu as pltpu

F32 = jnp.float32
BF16 = jnp.bfloat16
HI = lax.Precision.HIGHEST

NORM_EPS = 1e-6
RWKV_LN_EPS = 64e-5
LRU_C = 8.0

D_MODEL = 2048
RWKV_D, RWKV_HD, RWKV_H = 512, 64, 8
RWKV_R_DECAY, RWKV_R_A, RWKV_R_GATE = 96, 96, 256
RWKV_PROJ = 3 * RWKV_D + RWKV_R_DECAY + RWKV_R_A + RWKV_R_GATE
S5_D, S5_CH, S5_G, S5_N = 512, 16, 32, 64
S5_STATE = S5_G * S5_N
GDN_D, GDN_HD, GDN_H = 512, 128, 4
GDN_PROJ = 4 * GDN_D + 2 * GDN_H
LRU_D, LRU_BLOCKS = 512, 8
D_FF = 5632

RW_W = 2048
S5_OFF = 2048
GDN_OFF, GDN_W = 2560, 2560
LRU_OFF, LRU_W = 5120, 1024
NP_IN = 6144

CHUNK = 64
RWKV_GH = 4
assert CHUNK == RWKV_HD
SCAN_T = 128
VMEM_LIMIT = 56 * 1024 * 1024


def _cp(sem):
    return pltpu.CompilerParams(dimension_semantics=sem, vmem_limit_bytes=VMEM_LIMIT)


def _dot(a, b, prec=None):
    return jnp.dot(a, b, preferred_element_type=F32, precision=prec)


def _dot_nt(a, b, prec=None):
    return lax.dot_general(a, b, (((1,), (1,)), ((), ())), preferred_element_type=F32, precision=prec)


def _dot_tn(a, b, prec=None):
    return lax.dot_general(a, b, (((0,), (0,)), ((), ())), preferred_element_type=F32, precision=prec)


def _bdot(a, b):
    return _dot(a.astype(BF16), b.astype(BF16))


def _silu(x):
    return x * jax.nn.sigmoid(x)


def _gelu(x):
    return 0.5 * x * (1.0 + jnp.tanh(math.sqrt(2.0 / math.pi) * (x + 0.044715 * (x * x * x))))


def _seg_dot(x, seg):
    hi = x.astype(BF16)
    lo = (x - hi.astype(F32)).astype(BF16)
    n = x.shape[0]
    r = _dot(jnp.concatenate([hi, lo], axis=0), seg)
    return r[:n] + r[n:]


def _iota(shape, dim):
    return lax.broadcasted_iota(jnp.int32, shape, dim)


def _neumann_solve(n, rhs, steps):
    x = rhs
    p = n
    for i in range(steps):
        x = x + _bdot(p, x)
        if i + 1 < steps:
            p = _bdot(p, p)
    return x


def _split(x):
    hi = x.astype(BF16)
    return hi, (x - hi.astype(F32)).astype(BF16)


def _dot3(a, b):
    ah, al = _split(a)
    bh, bl = _split(b)
    return _dot(ah, bh) + (_dot(ah, bl) + _dot(al, bh))


def _neumann_solve3(n, rhs, steps):
    x = rhs
    p = n
    for i in range(steps):
        x = x + _dot3(p, x)
        if i + 1 < steps:
            p = _dot3(p, p)
    return x


def _row_to_col(row, eye):
    n = eye.shape[0]
    return jnp.sum(jnp.where(eye, jnp.broadcast_to(row, (n, n)), 0.0), axis=-1, keepdims=True)


def _ada_kernel(c_ref, w_ref, b_ref, o_ref):
    c = c_ref[...]
    o_ref[0] = _dot(_silu(c).astype(BF16), w_ref[0].astype(BF16)) + b_ref[0]


def ada_mod(c_all, ada_w, ada_b):
    depth, d, n = ada_w.shape
    r = c_all.shape[0]
    tn = 1024
    return pl.pallas_call(
        _ada_kernel,
        grid=(depth, n // tn),
        in_specs=[pl.BlockSpec((r, d), lambda l, j: (0, 0)),
                  pl.BlockSpec((1, d, tn), lambda l, j: (l, 0, j)),
                  pl.BlockSpec((1, 1, tn), lambda l, j: (l, 0, j))],
        out_specs=pl.BlockSpec((1, r, tn), lambda l, j: (l, 0, j)),
        out_shape=jax.ShapeDtypeStruct((depth, r, n), F32),
        compiler_params=_cp(("arbitrary", "arbitrary")),
        name="ada_mod",
    )(c_all, ada_w, ada_b.reshape(depth, 1, n))


def _norm_mod_kernel(x_ref, g_ref, sc_ref, sh_ref, o_ref):
    x = x_ref[0]
    h = x * lax.rsqrt(jnp.mean(x * x, axis=-1, keepdims=True) + NORM_EPS) * g_ref[...]
    o_ref[0] = (h * (1.0 + sc_ref[0]) + sh_ref[0]).astype(o_ref.dtype)


def _mod_spec(arr, tm):
    d = arr.shape[-1]
    if arr.shape[1] == 1:
        return pl.BlockSpec((1, 1, d), lambda g, i: (g, 0, 0))
    return pl.BlockSpec((1, tm, d), lambda g, i: (g, i, 0))


def norm_mod(x, gain, sc, sh, tm):
    g, r, d = x.shape
    return pl.pallas_call(
        _norm_mod_kernel,
        grid=(g, r // tm),
        in_specs=[pl.BlockSpec((1, tm, d), lambda g, i: (g, i, 0)),
                  pl.BlockSpec((1, d), lambda g, i: (0, 0)),
                  _mod_spec(sc, tm), _mod_spec(sh, tm)],
        out_specs=pl.BlockSpec((1, tm, d), lambda g, i: (g, i, 0)),
        out_shape=jax.ShapeDtypeStruct((g, r, d), BF16),
        compiler_params=_cp(("arbitrary", "arbitrary")),
        name="norm_mod",
    )(x, gain.reshape(1, d), sc, sh)


def _rms_kernel(x_ref, g_ref, o_ref):
    x = x_ref[0]
    o_ref[0] = x * lax.rsqrt(jnp.mean(x * x, axis=-1, keepdims=True) + NORM_EPS) * g_ref[...]


def final_norm(x, gain, tm):
    g, r, d = x.shape
    return pl.pallas_call(
        _rms_kernel,
        grid=(g, r // tm),
        in_specs=[pl.BlockSpec((1, tm, d), lambda g, i: (g, i, 0)),
                  pl.BlockSpec((1, d), lambda g, i: (0, 0))],
        out_specs=pl.BlockSpec((1, tm, d), lambda g, i: (g, i, 0)),
        out_shape=jax.ShapeDtypeStruct((g, r, d), F32),
        compiler_params=_cp(("arbitrary", "arbitrary")),
        name="final_norm",
    )(x, gain.reshape(1, d))


def _mm_kernel(a_ref, w_ref, o_ref):
    o_ref[0] = _dot(a_ref[0], w_ref[...])


def matmul_in(a, w, tm, tn):
    g, r, k = a.shape
    n = w.shape[1]
    return pl.pallas_call(
        _mm_kernel,
        grid=(g, r // tm, n // tn),
        in_specs=[pl.BlockSpec((1, tm, k), lambda g, i, j: (g, i, 0)),
                  pl.BlockSpec((k, tn), lambda g, i, j: (0, j))],
        out_specs=pl.BlockSpec((1, tm, tn), lambda g, i, j: (g, i, j)),
        out_shape=jax.ShapeDtypeStruct((g, r, n), F32),
        compiler_params=_cp(("arbitrary", "arbitrary", "arbitrary")),
        name="matmul_in",
    )(a, w)


def _mm_res_kernel(n_a, *refs):
    a_refs = refs[:n_a]
    w_refs = refs[n_a:2 * n_a]
    x_ref, g_ref, o_ref = refs[2 * n_a:]
    acc = _dot(a_refs[0][0], w_refs[0][...].astype(BF16))
    for a_ref, w_ref in zip(a_refs[1:], w_refs[1:]):
        acc = acc + _dot(a_ref[0], w_ref[...].astype(BF16))
    o_ref[0] = x_ref[0] + g_ref[0] * acc


def matmul_res(a_list, w, layer, x, gate, tm, tn):
    g, r, n = x.shape
    n_a = len(a_list)
    k = a_list[0].shape[-1]
    a_specs = [pl.BlockSpec((1, tm, k), lambda g, i, j: (g, i, 0)) for _ in a_list]
    w_specs = [pl.BlockSpec((None, k, tn), functools.partial(lambda g, i, j, q: (layer, q, j), q=q))
               for q in range(n_a)]
    if gate.shape[1] == 1:
        g_spec = pl.BlockSpec((1, 1, tn), lambda g, i, j: (g, 0, j))
    else:
        g_spec = pl.BlockSpec((1, tm, tn), lambda g, i, j: (g, i, j))
    return pl.pallas_call(
        functools.partial(_mm_res_kernel, n_a),
        grid=(g, r // tm, n // tn),
        in_specs=a_specs + w_specs + [pl.BlockSpec((1, tm, tn), lambda g, i, j: (g, i, j)), g_spec],
        out_specs=pl.BlockSpec((1, tm, tn), lambda g, i, j: (g, i, j)),
        out_shape=jax.ShapeDtypeStruct((g, r, n), F32),
        compiler_params=_cp(("arbitrary", "arbitrary", "arbitrary")),
        name="matmul_res",
    )(*a_list, *([w] * n_a), x, gate)


def _ffn_up_seq_kernel(a_ref, wg_ref, wv_ref, cwg_ref, cwv_ref, cbg_ref, cbv_ref, act_ref, lg_ref, lv_ref):
    a = a_ref[0]
    rows = a.shape[0]
    rid = _iota((rows, 1), 0)

    def conv(up, cw_ref, cb_ref):
        cw = cw_ref[...]
        s1 = jnp.where(rid >= 1, pltpu.roll(up, 1, axis=0), 0.0)
        s2 = jnp.where(rid >= 2, pltpu.roll(up, 2, axis=0), 0.0)
        return cw[0:1] * s2 + cw[1:2] * s1 + cw[2:3] * up + cb_ref[...]

    up_g = _dot(a, wg_ref[...].astype(BF16))
    lg_ref[0] = up_g[rows - 2:rows]
    gate = conv(up_g, cwg_ref, cbg_ref)
    up_v = _dot(a, wv_ref[...].astype(BF16))
    lv_ref[0] = up_v[rows - 2:rows]
    val = conv(up_v, cwv_ref, cbv_ref)
    act_ref[0] = (_silu(gate) * val).astype(BF16)


def ffn_up_seq(h, w_up, layer, conv_w, conv_b, tn):
    b, l, d = h.shape
    nt = D_FF // tn
    cb = conv_b.reshape(1, 2 * D_FF)
    return pl.pallas_call(
        _ffn_up_seq_kernel,
        grid=(b, nt),
        in_specs=[pl.BlockSpec((1, l, d), lambda b, j: (b, 0, 0)),
                  pl.BlockSpec((None, d, tn), lambda b, j: (layer, 0, j)),
                  pl.BlockSpec((None, d, tn), lambda b, j: (layer, 0, j + nt)),
                  pl.BlockSpec((3, tn), lambda b, j: (0, j)),
                  pl.BlockSpec((3, tn), lambda b, j: (0, j + nt)),
                  pl.BlockSpec((1, tn), lambda b, j: (0, j)),
                  pl.BlockSpec((1, tn), lambda b, j: (0, j + nt))],
        out_specs=[pl.BlockSpec((1, l, tn), lambda b, j: (b, 0, j)),
                   pl.BlockSpec((1, 2, tn), lambda b, j: (b, 0, j)),
                   pl.BlockSpec((1, 2, tn), lambda b, j: (b, 0, j))],
        out_shape=[jax.ShapeDtypeStruct((b, l, D_FF), BF16),
                   jax.ShapeDtypeStruct((b, 2, D_FF), F32),
                   jax.ShapeDtypeStruct((b, 2, D_FF), F32)],
        compiler_params=_cp(("arbitrary", "arbitrary")),
        name="ffn_up_seq",
    )(h, w_up, w_up, conv_w, conv_w, cb, cb)


def _ffn_up_step_kernel(a_ref, wg_ref, wv_ref, cwg_ref, cwv_ref, cbg_ref, cbv_ref,
                        b0g_ref, b1g_ref, b0v_ref, b1v_ref, act_ref, ug_ref, uv_ref):
    a = a_ref[...]
    up_g = _dot(a, wg_ref[...].astype(BF16))
    up_v = _dot(a, wv_ref[...].astype(BF16))
    ug_ref[...] = up_g
    uv_ref[...] = up_v
    cwg = cwg_ref[...]
    cwv = cwv_ref[...]
    gate = cwg[0:1] * b0g_ref[...] + cwg[1:2] * b1g_ref[...] + cwg[2:3] * up_g + cbg_ref[...]
    val = cwv[0:1] * b0v_ref[...] + cwv[1:2] * b1v_ref[...] + cwv[2:3] * up_v + cbv_ref[...]
    act_ref[...] = (_silu(gate) * val).astype(BF16)


def ffn_up_step(h, w_up, layer, conv_w, conv_b, conv_state, tn):
    b, d = h.shape
    nt = D_FF // tn
    cb = conv_b.reshape(1, 2 * D_FF)
    st = conv_state.reshape(conv_state.shape[0], b, 4 * D_FF)
    return pl.pallas_call(
        _ffn_up_step_kernel,
        grid=(nt,),
        in_specs=[pl.BlockSpec((b, d), lambda j: (0, 0)),
                  pl.BlockSpec((None, d, tn), lambda j: (layer, 0, j)),
                  pl.BlockSpec((None, d, tn), lambda j: (layer, 0, j + nt)),
                  pl.BlockSpec((3, tn), lambda j: (0, j)),
                  pl.BlockSpec((3, tn), lambda j: (0, j + nt)),
                  pl.BlockSpec((1, tn), lambda j: (0, j)),
                  pl.BlockSpec((1, tn), lambda j: (0, j + nt)),
                  pl.BlockSpec((None, b, tn), lambda j: (layer, 0, j)),
                  pl.BlockSpec((None, b, tn), lambda j: (layer, 0, j + 2 * nt)),
                  pl.BlockSpec((None, b, tn), lambda j: (layer, 0, j + nt)),
                  pl.BlockSpec((None, b, tn), lambda j: (layer, 0, j + 3 * nt))],
        out_specs=[pl.BlockSpec((b, tn), lambda j: (0, j))] * 3,
        out_shape=[jax.ShapeDtypeStruct((b, D_FF), BF16),
                   jax.ShapeDtypeStruct((b, D_FF), F32),
                   jax.ShapeDtypeStruct((b, D_FF), F32)],
        compiler_params=_cp(("arbitrary",)),
        name="ffn_up_step",
    )(h, w_up, w_up, conv_w, conv_w, cb, cb, st, st, st, st)


def _rwkv_pre(p, prev, prm, seg):
    mu, w0, w_up, a0, a_up, g_up, k_k, k_a = prm
    xm = p + mu * (prev - p)
    r = xm[:, 0:512]
    k = xm[:, 512:1024]
    v = xm[:, 1024:1536]
    xw = xm[:, 1536:1664]
    xa = xm[:, 1664:1792]
    xg = xm[:, 1792:2048]
    log_w = -jnp.exp(-jax.nn.softplus(-(w0 + _bdot(jnp.tanh(xw), w_up))) - 0.5)
    a = jax.nn.sigmoid(a0 + _bdot(xa, a_up))
    g = _bdot(jax.nn.sigmoid(xg), g_up)
    kx = k * k_k
    kk = kx * lax.rsqrt(_seg_dot(kx * kx, seg) + 1e-12)
    k2 = k * (1.0 + (a - 1.0) * k_a)
    return r, k2, v, log_w, a, g, kk


def _rwkv_post(y, r, k2, v, g, r_k, ln_w, ln_b, seg):
    inv = 1.0 / RWKV_HD
    mean = _seg_dot(y, seg) * inv
    dlt = y - mean
    var = _seg_dot(dlt * dlt, seg) * inv
    yn = dlt * lax.rsqrt(var + RWKV_LN_EPS) * ln_w + ln_b
    bonus = _seg_dot(r * k2 * r_k, seg) * v
    return (yn + bonus) * g


def _rwkv_seq_kernel(p_ref, mu_ref, w0_ref, wup_ref, a0_ref, aup_ref, gup_ref, kk_ref, ka_ref, rk_ref,
                     lnw_ref, lnb_ref, seg_ref, y_ref, s_out_ref, s_ref, prev_ref):
    c = CHUNK
    t = pl.program_id(1)

    @pl.when(t == 0)
    def _():
        s_ref[...] = jnp.zeros_like(s_ref)
        prev_ref[...] = jnp.zeros_like(prev_ref)

    p = p_ref[0]
    seg = seg_ref[...]
    rid = _iota((c, 1), 0)
    prev = jnp.where(rid == 0, prev_ref[...], pltpu.roll(p, 1, axis=0))
    prev_ref[...] = p[c - 1:c, :]
    prm = (mu_ref[...], w0_ref[...], wup_ref[...], a0_ref[...], aup_ref[...], gup_ref[...], kk_ref[...], ka_ref[...])
    r, k2, v, log_w, a, g, kk = _rwkv_pre(p, prev, prm, seg)

    ri = _iota((c, c), 0)
    ci = _iota((c, c), 1)
    tril = (ci <= ri).astype(F32)
    cum = _dot(tril, log_w, HI)
    e_in = jnp.exp(cum)
    e_out = jnp.exp(-cum)
    e_ex = jnp.exp(cum - log_w)
    tot = cum[c - 1:c, :]
    e_tail = jnp.exp(tot - cum)
    e_tot = jnp.exp(tot)
    alpha = -(a * kk)
    gw = RWKV_GH * RWKV_HD
    rr = _iota((gw, gw), 0)
    cc = _iota((gw, gw), 1)
    shift = RWKV_HD.bit_length() - 1
    own = lax.shift_right_logical(rr, shift) == lax.shift_right_logical(cc, shift)
    tri_strict = (cc & (c - 1)) < (rr & (c - 1))
    tri_incl = (cc & (c - 1)) <= (rr & (c - 1))
    eye = rr == cc

    def expand(x, gi):
        xg = x[:, gi * gw:(gi + 1) * gw].astype(BF16)
        return jnp.where(own, jnp.concatenate([xg] * RWKV_GH, axis=0), jnp.zeros((), BF16))

    ys = []
    for gi in range(RWKV_H // RWKV_GH):
        s0 = s_ref[gi]
        lhs = jnp.concatenate([expand(kk * e_ex, gi), expand(r * e_in, gi)], axis=0)
        rhs = jnp.concatenate([expand(alpha * e_out, gi), expand(k2 * e_out, gi)], axis=0)
        vv = expand(v, gi)
        m1 = _dot_nt(lhs, rhs)
        m2 = _dot(lhs, s0.astype(BF16))
        l_a = jnp.where(tri_strict, m1[:gw, :gw], 0.0)
        l_k = jnp.where(tri_strict, m1[:gw, gw:], 0.0)
        u = _neumann_solve(l_a, m2[:gw] + _dot(l_k.astype(BF16), vv), c.bit_length() - 1)
        uv = jnp.concatenate([u.astype(BF16), vv], axis=0)
        a_full = jnp.concatenate([jnp.where(tri_incl, m1[gw:, :gw], 0.0), jnp.where(tri_incl, m1[gw:, gw:], 0.0)],
                                 axis=1).astype(BF16)
        y_g = m2[gw:] + _dot(a_full, uv)
        ys.append(functools.reduce(lambda p_, q_: p_ + q_, [y_g[i * c:(i + 1) * c] for i in range(RWKV_GH)]))
        scale = _row_to_col(e_tot[:, gi * gw:(gi + 1) * gw], eye)
        tails = jnp.concatenate([expand(alpha * e_tail, gi), expand(k2 * e_tail, gi)], axis=0)
        s_ref[gi] = s0 * scale + _dot_tn(tails, uv)

    y = jnp.concatenate(ys, axis=1)
    out = _rwkv_post(y, r, k2, v, g, rk_ref[...], lnw_ref[...], lnb_ref[...], seg)
    y_ref[0] = out.astype(BF16)

    @pl.when(t == pl.num_programs(1) - 1)
    def _():
        for h in range(RWKV_H):
            gi, hh = divmod(h, RWKV_GH)
            s_out_ref[0, h] = s_ref[gi, hh * RWKV_HD:(hh + 1) * RWKV_HD, hh * RWKV_HD:(hh + 1) * RWKV_HD]


def _const_spec(arr, nidx):
    zeros = (0,) * arr.ndim
    if nidx == 1:
        return pl.BlockSpec(arr.shape, lambda i: zeros)
    return pl.BlockSpec(arr.shape, lambda b, t: zeros)


def rwkv_seq(proj, prm, seg):
    b, l, _ = proj.shape
    c = CHUNK
    return pl.pallas_call(
        _rwkv_seq_kernel,
        grid=(b, l // c),
        in_specs=[pl.BlockSpec((1, c, RW_W), lambda b, t: (b, t, 0))] + [_const_spec(x, 2) for x in prm] + [_const_spec(seg, 2)],
        out_specs=[pl.BlockSpec((1, c, RWKV_D), lambda b, t: (b, t, 0)),
                   pl.BlockSpec((1, RWKV_H, RWKV_HD, RWKV_HD), lambda b, t: (b, 0, 0, 0))],
        out_shape=[jax.ShapeDtypeStruct((b, l, RWKV_D), BF16),
                   jax.ShapeDtypeStruct((b, RWKV_H, RWKV_HD, RWKV_HD), F32)],
        scratch_shapes=[pltpu.VMEM((RWKV_H // RWKV_GH, RWKV_GH * RWKV_HD, RWKV_GH * RWKV_HD), F32),
                        pltpu.VMEM((1, RW_W), F32)],
        compiler_params=_cp(("arbitrary", "arbitrary")),
        name="rwkv_seq",
    )(proj, *prm, seg)


def _rwkv_step_kernel(p_ref, prev_ref, s_in_ref, mu_ref, w0_ref, wup_ref, a0_ref, aup_ref, gup_ref, kk_ref,
                      ka_ref, rk_ref, lnw_ref, lnb_ref, seg_ref, y_ref, s_out_ref, nat_scr, col_scr, vh_scr, yh_scr,
                      y_scr):
    h = pl.program_id(0)
    nb = p_ref.shape[0]

    @pl.when(h == 0)
    def _():
        prm = (mu_ref[...], w0_ref[...], wup_ref[...], a0_ref[...], aup_ref[...], gup_ref[...], kk_ref[...],
               ka_ref[...])
        r, k2, v, log_w, a, g, kk = _rwkv_pre(p_ref[...], prev_ref[...], prm, seg_ref[...])
        for i, x in enumerate((r, k2, v, g)):
            nat_scr[i] = x
        for i, x in enumerate((jnp.exp(log_w), kk * a, kk, k2, r)):
            col_scr[i] = x.T
        for hh in range(RWKV_H):
            vh_scr[hh] = v[:, hh * RWKV_HD:(hh + 1) * RWKV_HD]

    rows = pl.ds(pl.multiple_of(h * RWKV_HD, RWKV_HD), RWKV_HD)
    w_t, kka_t, kk_t, k_t, r_t = (col_scr[i, rows, :] for i in range(5))
    v_h = vh_scr[h]
    for i in range(nb):
        s = s_in_ref[i, 0]
        sa = jnp.sum(kk_t[:, i:i + 1] * s, axis=0, keepdims=True)
        s_new = s * w_t[:, i:i + 1] - kka_t[:, i:i + 1] * sa + k_t[:, i:i + 1] * v_h[i:i + 1]
        s_out_ref[i, 0] = s_new
        yh_scr[h, i:i + 1, :] = jnp.sum(r_t[:, i:i + 1] * s_new, axis=0, keepdims=True)

    @pl.when(h == RWKV_H - 1)
    def _():
        for hh in range(RWKV_H):
            y_scr[:, hh * RWKV_HD:(hh + 1) * RWKV_HD] = yh_scr[hh]
        out = _rwkv_post(y_scr[...], nat_scr[0], nat_scr[1], nat_scr[2], nat_scr[3], rk_ref[...], lnw_ref[...],
                         lnb_ref[...], seg_ref[...])
        y_ref[...] = out.astype(BF16)


def rwkv_step(proj, shift, state, layer, prm, seg):
    b = proj.shape[0]
    blk = (b, 1, RWKV_HD, RWKV_HD)
    return pl.pallas_call(
        _rwkv_step_kernel,
        grid=(RWKV_H,),
        in_specs=[pl.BlockSpec((b, RW_W), lambda h: (0, 0)),
                  pl.BlockSpec((b, RW_W), lambda h: (0, 0)),
                  pl.BlockSpec((None,) + blk, lambda h: (layer, 0, h, 0, 0))]
                 + [_const_spec(x, 1) for x in prm] + [_const_spec(seg, 1)],
        out_specs=[pl.BlockSpec((b, RWKV_D), lambda h: (0, 0)),
                   pl.BlockSpec(blk, lambda h: (0, h, 0, 0))],
        out_shape=[jax.ShapeDtypeStruct((b, RWKV_D), BF16),
                   jax.ShapeDtypeStruct(state.shape[1:], F32)],
        scratch_shapes=[pltpu.VMEM((4, b, RWKV_D), F32), pltpu.VMEM((5, RWKV_D, b), F32),
                        pltpu.VMEM((RWKV_H, b, RWKV_HD), F32), pltpu.VMEM((RWKV_H, b, RWKV_HD), F32),
                        pltpu.VMEM((b, RWKV_D), F32)],
        compiler_params=_cp(("arbitrary",)),
        name="rwkv_step",
    )(proj, shift, state, *prm, seg)


def _s5_out(u, h_re, h_im, cc_ref, d_ref, gw_ref, gb_ref):
    y = _dot(jnp.concatenate([h_re, h_im], axis=1).astype(BF16), cc_ref[...]) + d_ref[...] * u
    z = _gelu(y)
    return z * jax.nn.sigmoid(_bdot(z, gw_ref[...]) + gb_ref[...])


def _s5_seq_kernel(u_ref, bbr_ref, bbi_ref, cc_ref, abr_ref, abi_ref, d_ref, gw_ref, gb_ref,
                   y_ref, hr_ref, hi_ref, cr_ref, ci_ref, pr_ref, pi_ref):
    tc = u_ref.shape[1]
    gs = pr_ref.shape[0]
    t = pl.program_id(1)
    steps = [1 << i for i in range(gs.bit_length() - 1)]

    @pl.when(t == 0)
    def _():
        cr_ref[...] = jnp.zeros_like(cr_ref)
        ci_ref[...] = jnp.zeros_like(ci_ref)
        rid = _iota((gs, 1), 0)
        pr = jnp.broadcast_to(abr_ref[...], (gs, S5_STATE))
        pi = jnp.broadcast_to(abi_ref[...], (gs, S5_STATE))
        for dd in steps:
            sr = pltpu.roll(pr, dd, axis=0)
            si = pltpu.roll(pi, dd, axis=0)
            m = rid >= dd
            pr, pi = jnp.where(m, pr * sr - pi * si, pr), jnp.where(m, pr * si + pi * sr, pi)
        pr_ref[...] = pr
        pi_ref[...] = pi

    u = u_ref[0]
    ub = u.astype(BF16)
    h_re = _dot(ub, bbr_ref[...])
    h_im = _dot(ub, bbi_ref[...])
    rig = _iota((tc, 1), 0) & (gs - 1)
    for dd in steps:
        ar = pr_ref[dd - 1:dd, :]
        ai = pi_ref[dd - 1:dd, :]
        sr = pltpu.roll(h_re, dd, axis=0)
        si = pltpu.roll(h_im, dd, axis=0)
        m = rig >= dd
        h_re, h_im = h_re + jnp.where(m, ar * sr - ai * si, 0.0), h_im + jnp.where(m, ar * si + ai * sr, 0.0)
    pr = pr_ref[...]
    pi = pi_ref[...]
    c_re = cr_ref[...]
    c_im = ci_ref[...]
    g_re, g_im = [], []
    for j in range(tc // gs):
        b_re = h_re[j * gs:(j + 1) * gs]
        b_im = h_im[j * gs:(j + 1) * gs]
        b_re, b_im = b_re + pr * c_re - pi * c_im, b_im + pr * c_im + pi * c_re
        c_re = b_re[gs - 1:gs]
        c_im = b_im[gs - 1:gs]
        g_re.append(b_re)
        g_im.append(b_im)
    h_re = jnp.concatenate(g_re, axis=0)
    h_im = jnp.concatenate(g_im, axis=0)
    cr_ref[...] = c_re
    ci_ref[...] = c_im
    y_ref[0] = _s5_out(u, h_re, h_im, cc_ref, d_ref, gw_ref, gb_ref).astype(BF16)

    @pl.when(t == pl.num_programs(1) - 1)
    def _():
        hr_ref[0] = h_re[tc - 1:tc]
        hi_ref[0] = h_im[tc - 1:tc]


def s5_seq(proj, prm):
    b, l, _ = proj.shape
    tc = SCAN_T
    return pl.pallas_call(
        _s5_seq_kernel,
        grid=(b, l // tc),
        in_specs=[pl.BlockSpec((1, tc, S5_D), lambda b, t: (b, t, S5_OFF // S5_D))] + [_const_spec(x, 2) for x in prm],
        out_specs=[pl.BlockSpec((1, tc, S5_D), lambda b, t: (b, t, 0)),
                   pl.BlockSpec((1, 1, S5_STATE), lambda b, t: (b, 0, 0)),
                   pl.BlockSpec((1, 1, S5_STATE), lambda b, t: (b, 0, 0))],
        out_shape=[jax.ShapeDtypeStruct((b, l, S5_D), BF16),
                   jax.ShapeDtypeStruct((b, 1, S5_STATE), F32),
                   jax.ShapeDtypeStruct((b, 1, S5_STATE), F32)],
        scratch_shapes=[pltpu.VMEM((1, S5_STATE), F32), pltpu.VMEM((1, S5_STATE), F32),
                        pltpu.VMEM((8, S5_STATE), F32), pltpu.VMEM((8, S5_STATE), F32)],
        compiler_params=_cp(("arbitrary", "arbitrary")),
        name="s5_seq",
    )(proj, *prm)


def _s5_step_kernel(u_ref, h0r_ref, h0i_ref, bbr_ref, bbi_ref, cc_ref, abr_ref, abi_ref, d_ref, gw_ref, gb_ref,
                    y_ref, hr_ref, hi_ref):
    u = u_ref[...]
    ar = abr_ref[...]
    ai = abi_ref[...]
    h0r = h0r_ref[...]
    h0i = h0i_ref[...]
    h_re = ar * h0r - ai * h0i + _dot(u, bbr_ref[...], HI)
    h_im = ar * h0i + ai * h0r + _dot(u, bbi_ref[...], HI)
    hr_ref[...] = h_re
    hi_ref[...] = h_im
    y_ref[...] = _s5_out(u, h_re, h_im, cc_ref, d_ref, gw_ref, gb_ref).astype(BF16)


def s5_step(proj, h0_re, h0_im, prm):
    b = proj.shape[0]
    return pl.pallas_call(
        _s5_step_kernel,
        grid=(1,),
        in_specs=[pl.BlockSpec((b, S5_D), lambda i: (0, S5_OFF // S5_D)),
                  _const_spec(h0_re, 1), _const_spec(h0_im, 1)] + [_const_spec(x, 1) for x in prm],
        out_specs=[pl.BlockSpec((b, S5_D), lambda i: (0, 0)),
                   pl.BlockSpec((b, S5_STATE), lambda i: (0, 0)),
                   pl.BlockSpec((b, S5_STATE), lambda i: (0, 0))],
        out_shape=[jax.ShapeDtypeStruct((b, S5_D), BF16),
                   jax.ShapeDtypeStruct((b, S5_STATE), F32),
                   jax.ShapeDtypeStruct((b, S5_STATE), F32)],
        compiler_params=_cp(("arbitrary",)),
        name="s5_step",
    )(proj, h0_re, h0_im, *prm)


def _gdn_pre(qkv_c, z, ab, alog_ref, dtb_ref):
    act = _silu(qkv_c)
    qs, ks = [], []
    for h in range(GDN_H):
        qh = act[:, h * GDN_HD:(h + 1) * GDN_HD]
        kh = act[:, GDN_D + h * GDN_HD:GDN_D + (h + 1) * GDN_HD]
        qs.append(qh * lax.rsqrt(jnp.sum(qh * qh, axis=-1, keepdims=True) + 1e-6))
        ks.append(kh * lax.rsqrt(jnp.sum(kh * kh, axis=-1, keepdims=True) + 1e-6))
    v = act[:, 2 * GDN_D:3 * GDN_D]
    g = -jnp.exp(alog_ref[...]) * jax.nn.softplus(ab + dtb_ref[...])
    beta = jax.nn.sigmoid(ab)
    return qs, ks, v, g, beta


def _gdn_post(o, z_h, nw):
    on = o * lax.rsqrt(jnp.mean(o * o, axis=-1, keepdims=True) + NORM_EPS) * nw
    return on * _silu(z_h)


def _gdn_seq_kernel(p_ref, cw_ref, alog_ref, dtb_ref, nw_ref, y_ref, s_out_ref, s_ref, prev_ref):
    c = CHUNK
    t = pl.program_id(1)

    @pl.when(t == 0)
    def _():
        s_ref[...] = jnp.zeros_like(s_ref)
        prev_ref[...] = jnp.zeros_like(prev_ref)

    x = p_ref[0]
    qkv = x[:, 0:3 * GDN_D]
    z = x[:, 3 * GDN_D:4 * GDN_D]
    ab = x[:, 4 * GDN_D:4 * GDN_D + 128]
    ext = jnp.concatenate([prev_ref[...], qkv], axis=0)
    prev_ref[...] = qkv[c - 8:c]
    cw = cw_ref[...]
    conv = cw[3:4] * qkv
    for j in (1, 2, 3):
        conv = conv + cw[3 - j:4 - j] * pltpu.roll(ext, j, axis=0)[8:]
    qs, ks, v, g, beta = _gdn_pre(conv, z, ab, alog_ref, dtb_ref)

    hc = GDN_H * c
    heads = range(GDN_H)
    stack = lambda pieces: jnp.concatenate(pieces, axis=0)
    ri = _iota((c, c), 0)
    ci = _iota((c, c), 1)
    gc = _dot((ci <= ri).astype(F32), g, HI)
    lane = _iota((c, 128), 1)
    g_col = stack([gc[:, h:h + 1] for h in heads])
    b_col = stack([beta[:, GDN_H + h:GDN_H + h + 1] for h in heads])
    g_row = _dot_nt(jnp.ones((hc, 128), F32), stack([jnp.where(lane == h, gc, 0.0) for h in heads]), HI)
    rr = _iota((hc, hc), 0)
    cc = _iota((hc, hc), 1)
    shift = c.bit_length() - 1
    same = lax.shift_right_logical(rr, shift) == lax.shift_right_logical(cc, shift)
    causal = same & (cc <= rr)
    strict = same & (cc < rr)
    decay = jnp.where(causal, jnp.exp(jnp.where(causal, g_col - g_row, 0.0)), 0.0)
    q = stack(qs) * (GDN_HD ** -0.5)
    k = stack(ks)
    v_s = stack([v[:, h * GDN_HD:(h + 1) * GDN_HD] for h in heads])
    kb = k * b_col
    m1 = _dot_nt(stack([kb, q]).astype(BF16), k.astype(BF16))
    lmat = jnp.where(strict, m1[:hc] * decay, 0.0)
    attn = m1[hc:] * decay
    eg = jnp.exp(g_col)
    sol = _neumann_solve3(-lmat, jnp.concatenate([v_s * b_col, kb * eg], axis=1), shift)
    u = sol[:, :GDN_HD]
    wk = sol[:, GDN_HD:]
    qd = q * eg
    ws = [_dot(stack([wk[h * c:(h + 1) * c], qd[h * c:(h + 1) * c]]).astype(BF16), s_ref[h].astype(BF16))
          for h in heads]
    v_new = u - stack([w[:c] for w in ws])
    o = stack([w[c:] for w in ws]) + _bdot(attn, v_new)
    g_last = [gc[c - 1:c, h:h + 1] for h in heads]
    k_tail = k * jnp.exp(stack([jnp.broadcast_to(gl, (c, 1)) for gl in g_last]) - g_col)
    nw = nw_ref[...]
    for h in heads:
        rows = slice(h * c, (h + 1) * c)
        s_ref[h] = s_ref[h] * jnp.exp(g_last[h]) + _dot_tn(k_tail[rows].astype(BF16), v_new[rows].astype(BF16))
        z_h = z[:, h * GDN_HD:(h + 1) * GDN_HD]
        y_ref[0, :, h * GDN_HD:(h + 1) * GDN_HD] = _gdn_post(o[rows], z_h, nw).astype(BF16)

    @pl.when(t == pl.num_programs(1) - 1)
    def _():
        s_out_ref[0] = s_ref[...]


def gdn_seq(proj, prm):
    b, l, _ = proj.shape
    c = CHUNK
    return pl.pallas_call(
        _gdn_seq_kernel,
        grid=(b, l // c),
        in_specs=[pl.BlockSpec((1, c, GDN_W), lambda b, t: (b, t, GDN_OFF // GDN_W))] + [_const_spec(x, 2) for x in prm],
        out_specs=[pl.BlockSpec((1, c, GDN_D), lambda b, t: (b, t, 0)),
                   pl.BlockSpec((1, GDN_H, GDN_HD, GDN_HD), lambda b, t: (b, 0, 0, 0))],
        out_shape=[jax.ShapeDtypeStruct((b, l, GDN_D), BF16),
                   jax.ShapeDtypeStruct((b, GDN_H, GDN_HD, GDN_HD), F32)],
        scratch_shapes=[pltpu.VMEM((GDN_H, GDN_HD, GDN_HD), F32),
                        pltpu.VMEM((8, 3 * GDN_D), F32)],
        compiler_params=_cp(("arbitrary", "arbitrary")),
        name="gdn_seq",
    )(proj, *prm)


def _gdn_step_kernel(nb, p_ref, cs_ref, s_in_ref, cw_ref, alog_ref, dtb_ref, nw_ref, y_ref, s_out_ref):
    x = p_ref[...]
    qkv = x[:, 0:3 * GDN_D]
    z = x[:, 3 * GDN_D:4 * GDN_D]
    ab = x[:, 4 * GDN_D:4 * GDN_D + 128]
    cs = cs_ref[...]
    cw = cw_ref[...]
    w3 = 3 * GDN_D
    conv = cw[0:1] * cs[:, 0:w3] + cw[1:2] * cs[:, w3:2 * w3] + cw[2:3] * cs[:, 2 * w3:3 * w3] + cw[3:4] * qkv
    qs, ks, v, g, beta = _gdn_pre(conv, z, ab, alog_ref, dtb_ref)
    eye = _iota((GDN_HD, GDN_HD), 0) == _iota((GDN_HD, GDN_HD), 1)
    nw = nw_ref[...]
    for i in range(nb):
        for h in range(GDN_H):
            k_c = _row_to_col(ks[h][i:i + 1], eye)
            q_c = _row_to_col(qs[h][i:i + 1], eye)
            eg = jnp.exp(g[i:i + 1, h:h + 1])
            b_s = beta[i:i + 1, GDN_H + h:GDN_H + h + 1]
            s = s_in_ref[i, h]
            sk = jnp.sum(k_c * s, axis=0, keepdims=True)
            v_row = v[i:i + 1, h * GDN_HD:(h + 1) * GDN_HD]
            s_new = s * eg + (k_c * b_s) * (v_row - eg * sk)
            s_out_ref[i, h] = s_new
            o = jnp.sum(q_c * s_new, axis=0, keepdims=True) * (GDN_HD ** -0.5)
            z_h = z[i:i + 1, h * GDN_HD:(h + 1) * GDN_HD]
            y_ref[i:i + 1, h * GDN_HD:(h + 1) * GDN_HD] = _gdn_post(o, z_h, nw).astype(BF16)


def gdn_step(proj, conv_state, state, layer, prm, nb):
    b = proj.shape[0]
    blk = (nb, GDN_H, GDN_HD, GDN_HD)
    return pl.pallas_call(
        functools.partial(_gdn_step_kernel, nb),
        grid=(b // nb,),
        in_specs=[pl.BlockSpec((nb, GDN_W), lambda i: (i, GDN_OFF // GDN_W)),
                  pl.BlockSpec((nb, 9 * GDN_D), lambda i: (i, 0)),
                  pl.BlockSpec((None,) + blk, lambda i: (layer, i, 0, 0, 0))] + [_const_spec(x, 1) for x in prm],
        out_specs=[pl.BlockSpec((nb, GDN_D), lambda i: (i, 0)),
                   pl.BlockSpec(blk, lambda i: (i, 0, 0, 0))],
        out_shape=[jax.ShapeDtypeStruct((b, GDN_D), BF16),
                   jax.ShapeDtypeStruct(state.shape[1:], F32)],
        compiler_params=_cp(("arbitrary",)),
        name="gdn_step",
    )(proj, conv_state, state, *prm)


def _lru_gates(xc, gate, wr_ref, br_ref, wi_ref, bi_ref, lam_ref):
    r = jax.nn.sigmoid(_bdot(xc, wr_ref[...]) + br_ref[...])
    i = jax.nn.sigmoid(_bdot(xc, wi_ref[...]) + bi_ref[...])
    log_a = -LRU_C * r * jax.nn.softplus(-lam_ref[...])
    a = jnp.exp(log_a)
    one_minus_a2 = -jnp.tanh(log_a) * (a * a + 1.0)
    b = jnp.sqrt(one_minus_a2) * (i * xc)
    return a, b, _gelu(gate)


def _lru_seq_kernel(p_ref, cw_ref, cb_ref, wr_ref, br_ref, wi_ref, bi_ref, lam_ref, y_ref, h_out_ref,
                    prev_ref, carry_ref):
    tc = p_ref.shape[1]
    t = pl.program_id(1)

    @pl.when(t == 0)
    def _():
        prev_ref[...] = jnp.zeros_like(prev_ref)
        carry_ref[...] = jnp.zeros_like(carry_ref)

    x = p_ref[0]
    xr = x[:, :LRU_D]
    gate = x[:, LRU_D:]
    ext = jnp.concatenate([prev_ref[...], xr], axis=0)
    prev_ref[...] = xr[tc - 8:tc]
    cw = cw_ref[...]
    xc = cw[3:4] * xr + cb_ref[...]
    for j in (1, 2, 3):
        xc = xc + cw[3 - j:4 - j] * pltpu.roll(ext, j, axis=0)[8:]
    a, b, gg = _lru_gates(xc, gate, wr_ref, br_ref, wi_ref, bi_ref, lam_ref)
    rid = _iota((tc, 1), 0)
    dd = 1
    while dd < tc:
        m = rid >= dd
        sa = pltpu.roll(a, dd, axis=0)
        sb = pltpu.roll(b, dd, axis=0)
        a, b = jnp.where(m, a * sa, a), jnp.where(m, a * sb + b, b)
        dd *= 2
    h = b + a * carry_ref[...]
    carry_ref[...] = h[tc - 1:tc]
    y_ref[0] = (h * gg).astype(BF16)

    @pl.when(t == pl.num_programs(1) - 1)
    def _():
        h_out_ref[0] = h[tc - 1:tc]


def lru_seq(proj, prm):
    b, l, _ = proj.shape
    tc = SCAN_T
    return pl.pallas_call(
        _lru_seq_kernel,
        grid=(b, l // tc),
        in_specs=[pl.BlockSpec((1, tc, LRU_W), lambda b, t: (b, t, LRU_OFF // LRU_W))] + [_const_spec(x, 2) for x in prm],
        out_specs=[pl.BlockSpec((1, tc, LRU_D), lambda b, t: (b, t, 0)),
                   pl.BlockSpec((1, 1, LRU_D), lambda b, t: (b, 0, 0))],
        out_shape=[jax.ShapeDtypeStruct((b, l, LRU_D), BF16),
                   jax.ShapeDtypeStruct((b, 1, LRU_D), F32)],
        scratch_shapes=[pltpu.VMEM((8, LRU_D), F32), pltpu.VMEM((1, LRU_D), F32)],
        compiler_params=_cp(("arbitrary", "arbitrary")),
        name="lru_seq",
    )(proj, *prm)


def _lru_step_kernel(p_ref, cs_ref, h0_ref, cw_ref, cb_ref, wr_ref, br_ref, wi_ref, bi_ref, lam_ref, y_ref, h_ref):
    x = p_ref[...]
    xr = x[:, :LRU_D]
    gate = x[:, LRU_D:]
    cs = cs_ref[...]
    cw = cw_ref[...]
    xc = (cw[0:1] * cs[:, 0:LRU_D] + cw[1:2] * cs[:, LRU_D:2 * LRU_D] + cw[2:3] * cs[:, 2 * LRU_D:3 * LRU_D]
          + cw[3:4] * xr + cb_ref[...])
    a, b, gg = _lru_gates(xc, gate, wr_ref, br_ref, wi_ref, bi_ref, lam_ref)
    h = a * h0_ref[...] + b
    h_ref[...] = h
    y_ref[...] = (h * gg).astype(BF16)


def lru_step(proj, conv_state, h0, prm):
    b = proj.shape[0]
    return pl.pallas_call(
        _lru_step_kernel,
        grid=(1,),
        in_specs=[pl.BlockSpec((b, LRU_W), lambda i: (0, LRU_OFF // LRU_W)),
                  _const_spec(conv_state, 1), _const_spec(h0, 1)] + [_const_spec(x, 1) for x in prm],
        out_specs=[pl.BlockSpec((b, LRU_D), lambda i: (0, 0)),
                   pl.BlockSpec((b, LRU_D), lambda i: (0, 0))],
        out_shape=[jax.ShapeDtypeStruct((b, LRU_D), BF16),
                   jax.ShapeDtypeStruct((b, LRU_D), F32)],
        compiler_params=_cp(("arbitrary",)),
        name="lru_step",
    )(proj, conv_state, h0, *prm)


def _pad_cols(x, width):
    return jnp.pad(x, [(0, 0)] * (x.ndim - 1) + [(0, width - x.shape[-1])])


def _pack_rwkv_cols(x):
    return jnp.concatenate([x[..., :1536], _pad_cols(x[..., 1536:1632], 128), _pad_cols(x[..., 1632:1728], 128),
                            x[..., 1728:1984]], axis=-1)


def _unpack_rwkv_cols(x):
    return jnp.concatenate([x[..., :1536], x[..., 1536:1632], x[..., 1664:1760], x[..., 1792:2048]], axis=-1)


def _pack_w_in(w):
    o = RWKV_PROJ
    return jnp.concatenate([_pack_rwkv_cols(w[:, :o]), w[:, o:o + S5_D],
                            _pad_cols(w[:, o + S5_D:o + S5_D + GDN_PROJ], GDN_W), w[:, o + S5_D + GDN_PROJ:]],
                           axis=-1).astype(BF16)


def _block_diag(blocks):
    g, a, b = blocks.shape
    eye = jnp.eye(g, dtype=blocks.dtype)
    return (eye[:, None, :, None] * blocks[:, :, None, :]).reshape(g * a, g * b)


def _s5_params(lam_re, lam_im, log_dt, b_re, b_im, c_re, c_im, d, glu_w, glu_b):
    dt = jnp.exp(log_dt)[:, None]
    mag = jnp.exp(lam_re * dt)
    ab_re, ab_im = mag * jnp.cos(lam_im * dt), mag * jnp.sin(lam_im * dt)
    den = lam_re * lam_re + lam_im * lam_im
    nr = ab_re - 1.0
    f_re = (nr * lam_re + ab_im * lam_im) / den
    f_im = (ab_im * lam_re - nr * lam_im) / den
    bb_re = f_re[..., None] * b_re - f_im[..., None] * b_im
    bb_im = f_re[..., None] * b_im + f_im[..., None] * b_re
    bbr = _block_diag(jnp.swapaxes(bb_re, 1, 2))
    bbi = _block_diag(jnp.swapaxes(bb_im, 1, 2))
    cc = jnp.concatenate([_block_diag(jnp.swapaxes(c_re, 1, 2)), -_block_diag(jnp.swapaxes(c_im, 1, 2))],
                         axis=0).astype(BF16)
    tail = (cc, ab_re.reshape(1, S5_STATE), ab_im.reshape(1, S5_STATE), d.reshape(1, S5_D), glu_w,
            glu_b.reshape(1, S5_D))
    return (bbr.astype(BF16), bbi.astype(BF16)) + tail, (bbr, bbi) + tail


def _layer(xp, xs, mod, st, p, seg, layer, stacked):
    bp, lp, d = xp.shape
    bs = xs.shape[1]
    sh1, sc1, g1, sh2, sc2, g2 = jnp.split(mod, 6, axis=-1)
    pm = lambda m: m[:bp, None, :]
    sm = lambda m: m[None, bp:bp + bs, :]

    w_in = _pack_w_in(p['w_in'])
    row = lambda v: v.reshape(1, -1)
    rw_prm = (row(_pack_rwkv_cols(p['rwkv_mu'])), row(p['rwkv_w0']),
              jnp.pad(p['rwkv_w_up'], ((0, 32), (0, 0))), row(p['rwkv_a0']),
              jnp.pad(p['rwkv_a_up'], ((0, 32), (0, 0))), p['rwkv_g_up'], row(p['rwkv_k_k']), row(p['rwkv_k_a']),
              row(p['rwkv_r_k']), row(p['rwkv_ln_w']), row(p['rwkv_ln_b']))
    s5_prm, s5_prm_f32 = _s5_params(p['s5_lambda_re'], p['s5_lambda_im'], p['s5_log_dt'], p['s5_b_re'], p['s5_b_im'],
                        p['s5_c_re'], p['s5_c_im'], p['s5_d'], p['s5_glu_w'], p['s5_glu_b'])
    gdn_prm = (p['gdn_conv_w'], _pad_cols(row(p['gdn_a_log']), 128), _pad_cols(row(p['gdn_dt_bias']), 128),
               row(p['gdn_norm_w']))
    lru_prm = (p['lru_conv_w'], row(p['lru_conv_b']), _block_diag(p['lru_wr']), row(p['lru_br']),
               _block_diag(p['lru_wi']), row(p['lru_bi']), row(p['lru_lambda']))

    h = norm_mod(xp, p['norm1_g'], pm(sc1), pm(sh1), 512)
    proj = matmul_in(h, w_in, 1024, 1024)
    ya, s_wkv = rwkv_seq(proj, rw_prm, seg)
    yb, s_re, s_im = s5_seq(proj, s5_prm)
    yc, s_gdn = gdn_seq(proj, gdn_prm)
    yd, s_lru = lru_seq(proj, lru_prm)
    xp = matmul_res([ya, yb, yc, yd], stacked['w_out'], layer, xp, pm(g1), 1024, 512)
    h = norm_mod(xp, p['norm2_g'], pm(sc2), pm(sh2), 512)
    act, lg, lv = ffn_up_seq(h, stacked['ffn_w_up'], layer, p['ffn_conv_w'], p['ffn_conv_b'], 256)
    xp = matmul_res([act], stacked['ffn_w_down'], layer, xp, pm(g2), 1024, 256)
    o = RWKV_PROJ
    new_p = {
        'rwkv_wkv': s_wkv,
        'rwkv_shift': _unpack_rwkv_cols(proj[:, -1, :RW_W]),
        's5_re': s_re.reshape(bp, S5_G, S5_N), 's5_im': s_im.reshape(bp, S5_G, S5_N),
        'gdn': s_gdn,
        'gdn_conv': proj[:, -3:, GDN_OFF:GDN_OFF + 3 * GDN_D],
        'lru_h': s_lru.reshape(bp, LRU_D),
        'lru_conv': proj[:, -3:, LRU_OFF:LRU_OFF + LRU_D],
        'ffn_conv': jnp.concatenate([lg, lv], axis=-1),
    }

    h = norm_mod(xs, p['norm1_g'], sm(sc1), sm(sh1), bs)
    proj = matmul_in(h, w_in, bs, 1024)[0]
    ya, s_wkv = rwkv_step(proj, _pack_rwkv_cols(st['rwkv_shift']), stacked['rwkv_wkv'], layer, rw_prm, seg)
    yb, s_re, s_im = s5_step(proj, st['s5_re'].reshape(bs, S5_STATE), st['s5_im'].reshape(bs, S5_STATE), s5_prm_f32)
    yc, s_gdn = gdn_step(proj, st['gdn_conv'].reshape(bs, 9 * GDN_D), stacked['gdn'], layer, gdn_prm, 16)
    yd, s_lru = lru_step(proj, st['lru_conv'].reshape(bs, 3 * LRU_D), st['lru_h'], lru_prm)
    xs = matmul_res([y[None] for y in (ya, yb, yc, yd)], stacked['w_out'], layer, xs, sm(g1), bs, 512)
    h = norm_mod(xs, p['norm2_g'], sm(sc2), sm(sh2), bs)
    act, ug, uv = ffn_up_step(h[0], stacked['ffn_w_up'], layer, p['ffn_conv_w'], p['ffn_conv_b'],
                              stacked['ffn_conv'], 256)
    xs = matmul_res([act[None]], stacked['ffn_w_down'], layer, xs, sm(g2), bs, 256)
    shift_rows = lambda buf, new: jnp.concatenate([buf[:, 1:], new[:, None, :]], axis=1)
    new_s = {
        'rwkv_wkv': s_wkv,
        'rwkv_shift': _unpack_rwkv_cols(proj[:, :RW_W]),
        's5_re': s_re.reshape(bs, S5_G, S5_N), 's5_im': s_im.reshape(bs, S5_G, S5_N),
        'gdn': s_gdn,
        'gdn_conv': shift_rows(st['gdn_conv'], proj[:, GDN_OFF:GDN_OFF + 3 * GDN_D]),
        'lru_h': s_lru,
        'lru_conv': shift_rows(st['lru_conv'], proj[:, LRU_OFF:LRU_OFF + LRU_D]),
        'ffn_conv': shift_rows(st['ffn_conv'], jnp.concatenate([ug, uv], axis=-1)),
    }
    return xp, xs, new_p, new_s


STATE_ORDER = ('rwkv_wkv', 'rwkv_shift', 's5_re', 's5_im', 'gdn', 'gdn_conv', 'lru_h', 'lru_conv', 'ffn_conv')


def kernel(x_prompt, x_sample, c_prompt, c_sample, state_rwkv_wkv, state_rwkv_shift, state_s5_re, state_s5_im, state_gdn, state_gdn_conv, state_lru_h, state_lru_conv, state_ffn_conv, ada_w, ada_b, norm1_g, norm2_g, final_g, w_in, w_out, rwkv_mu, rwkv_w0, rwkv_w_up, rwkv_a0, rwkv_a_up, rwkv_g_up, rwkv_k_k, rwkv_k_a, rwkv_r_k, rwkv_ln_w, rwkv_ln_b, s5_lambda_re, s5_lambda_im, s5_log_dt, s5_b_re, s5_b_im, s5_c_re, s5_c_im, s5_d, s5_glu_w, s5_glu_b, gdn_conv_w, gdn_a_log, gdn_dt_bias, gdn_norm_w, lru_conv_w, lru_conv_b, lru_wr, lru_br, lru_wi, lru_bi, lru_lambda, ffn_w_up, ffn_conv_w, ffn_conv_b, ffn_w_down):
    weights = {
        'norm1_g': norm1_g, 'norm2_g': norm2_g, 'w_in': w_in,
        'rwkv_mu': rwkv_mu, 'rwkv_w0': rwkv_w0, 'rwkv_w_up': rwkv_w_up, 'rwkv_a0': rwkv_a0,
        'rwkv_a_up': rwkv_a_up, 'rwkv_g_up': rwkv_g_up, 'rwkv_k_k': rwkv_k_k, 'rwkv_k_a': rwkv_k_a,
        'rwkv_r_k': rwkv_r_k, 'rwkv_ln_w': rwkv_ln_w, 'rwkv_ln_b': rwkv_ln_b,
        's5_lambda_re': s5_lambda_re, 's5_lambda_im': s5_lambda_im, 's5_log_dt': s5_log_dt,
        's5_b_re': s5_b_re, 's5_b_im': s5_b_im, 's5_c_re': s5_c_re, 's5_c_im': s5_c_im, 's5_d': s5_d,
        's5_glu_w': s5_glu_w, 's5_glu_b': s5_glu_b,
        'gdn_conv_w': gdn_conv_w, 'gdn_a_log': gdn_a_log, 'gdn_dt_bias': gdn_dt_bias, 'gdn_norm_w': gdn_norm_w,
        'lru_conv_w': lru_conv_w, 'lru_conv_b': lru_conv_b, 'lru_wr': lru_wr, 'lru_br': lru_br,
        'lru_wi': lru_wi, 'lru_bi': lru_bi, 'lru_lambda': lru_lambda,
        'ffn_conv_w': ffn_conv_w, 'ffn_conv_b': ffn_conv_b,
    }
    stacked = {'w_out': w_out, 'ffn_w_up': ffn_w_up, 'ffn_w_down': ffn_w_down, 'rwkv_wkv': state_rwkv_wkv,
               'gdn': state_gdn, 'ffn_conv': state_ffn_conv}
    cache = {'rwkv_wkv': state_rwkv_wkv, 'rwkv_shift': state_rwkv_shift, 's5_re': state_s5_re,
             's5_im': state_s5_im, 'gdn': state_gdn, 'gdn_conv': state_gdn_conv, 'lru_h': state_lru_h,
             'lru_conv': state_lru_conv, 'ffn_conv': state_ffn_conv}
    depth = ada_w.shape[0]
    bp = x_prompt.shape[0]
    bs = x_sample.shape[0]
    rows = -(-(bp + bs) // 8) * 8
    c_all = jnp.pad(jnp.concatenate([c_prompt, c_sample], axis=0), ((0, rows - bp - bs), (0, 0)))
    mod = ada_mod(c_all, ada_w, ada_b)
    ids = jnp.arange(RWKV_D) // RWKV_HD
    seg = (ids[:, None] == ids[None, :]).astype(BF16)

    xp = x_prompt
    xs = jnp.swapaxes(x_sample, 0, 1)
    new_p = {n: [] for n in STATE_ORDER}
    new_s = {n: [] for n in STATE_ORDER}
    for l in range(depth):
        p = {name: arr[l] for name, arr in weights.items()}
        st = {n: cache[n][l] for n in STATE_ORDER}
        xp, xs, sp, ss = _layer(xp, xs, mod[l], st, p, seg, l, stacked)
        for n in STATE_ORDER:
            new_p[n].append(sp[n])
            new_s[n].append(ss[n])
    y_prompt = final_norm(xp, final_g, 512)
    y_sample = jnp.swapaxes(final_norm(xs, final_g, bs), 0, 1)
    outs_p = tuple(jnp.stack(new_p[n], axis=0) for n in STATE_ORDER)
    outs_s = tuple(jnp.stack(new_s[n], axis=0) for n in STATE_ORDER)
    return (y_prompt, y_sample) + outs_p + outs_s
```

```python
import functools
import math

import jax
import jax.numpy as jnp
from jax import lax
from jax.experimental import pallas as pl
from jax.experimental.pallas import tpu as pltpu

F32 = jnp.float32
BF16 = jnp.bfloat16
HI = lax.Precision.HIGHEST

NORM_EPS = 1e-6
RWKV_LN_EPS = 64e-5
LRU_C = 8.0

D_MODEL = 2048
RWKV_D, RWKV_HD, RWKV_H = 512, 64, 8
RWKV_R_DECAY, RWKV_R_A, RWKV_R_GATE = 96, 96, 256
RWKV_PROJ = 3 * RWKV_D + RWKV_R_DECAY + RWKV_R_A + RWKV_R_GATE
S5_D, S5_CH, S5_G, S5_N = 512, 16, 32, 64
S5_STATE = S5_G * S5_N
GDN_D, GDN_HD, GDN_H = 512, 128, 4
GDN_PROJ = 4 * GDN_D + 2 * GDN_H
LRU_D, LRU_BLOCKS = 512, 8
D_FF = 5632

RW_W = 2048
S5_OFF = 2048
GDN_OFF, GDN_W = 2560, 2560
LRU_OFF, LRU_W = 5120, 1024
NP_IN = 6144

CHUNK = 64
RWKV_GH = 4
assert CHUNK == RWKV_HD
SCAN_T = 128
VMEM_LIMIT = 56 * 1024 * 1024


def _cp(sem):
    return pltpu.CompilerParams(dimension_semantics=sem, vmem_limit_bytes=VMEM_LIMIT)


def _dot(a, b, prec=None):
    return jnp.dot(a, b, preferred_element_type=F32, precision=prec)


def _dot_nt(a, b, prec=None):
    return lax.dot_general(a, b, (((1,), (1,)), ((), ())), preferred_element_type=F32, precision=prec)


def _dot_tn(a, b, prec=None):
    return lax.dot_general(a, b, (((0,), (0,)), ((), ())), preferred_element_type=F32, precision=prec)


def _bdot(a, b):
    return _dot(a.astype(BF16), b.astype(BF16))


def _silu(x):
    return x * jax.nn.sigmoid(x)


def _gelu(x):
    return 0.5 * x * (1.0 + jnp.tanh(math.sqrt(2.0 / math.pi) * (x + 0.044715 * (x * x * x))))


def _seg_dot(x, seg):
    hi = x.astype(BF16)
    lo = (x - hi.astype(F32)).astype(BF16)
    n = x.shape[0]
    r = _dot(jnp.concatenate([hi, lo], axis=0), seg)
    return r[:n] + r[n:]


def _iota(shape, dim):
    return lax.broadcasted_iota(jnp.int32, shape, dim)


def _neumann_solve(n, rhs, steps):
    x = rhs
    p = n
    for i in range(steps):
        x = x + _bdot(p, x)
        if i + 1 < steps:
            p = _bdot(p, p)
    return x


def _split(x):
    hi = x.astype(BF16)
    return hi, (x - hi.astype(F32)).astype(BF16)


def _dot3(a, b):
    ah, al = _split(a)
    bh, bl = _split(b)
    return _dot(ah, bh) + (_dot(ah, bl) + _dot(al, bh))


def _neumann_solve3(n, rhs, steps):
    x = rhs
    p = n
    for i in range(steps):
        x = x + _dot3(p, x)
        if i + 1 < steps:
            p = _dot3(p, p)
    return x


def _row_to_col(row, eye):
    n = eye.shape[0]
    return jnp.sum(jnp.where(eye, jnp.broadcast_to(row, (n, n)), 0.0), axis=-1, keepdims=True)


def _ada_kernel(c_ref, w_ref, b_ref, o_ref):
    c = c_ref[...]
    o_ref[0] = _dot(_silu(c).astype(BF16), w_ref[0].astype(BF16)) + b_ref[0]


def ada_mod(c_all, ada_w, ada_b):
    depth, d, n = ada_w.shape
    r = c_all.shape[0]
    tn = 1024
    return pl.pallas_call(
        _ada_kernel,
        grid=(depth, n // tn),
        in_specs=[pl.BlockSpec((r, d), lambda l, j: (0, 0)),
                  pl.BlockSpec((1, d, tn), lambda l, j: (l, 0, j)),
                  pl.BlockSpec((1, 1, tn), lambda l, j: (l, 0, j))],
        out_specs=pl.BlockSpec((1, r, tn), lambda l, j: (l, 0, j)),
        out_shape=jax.ShapeDtypeStruct((depth, r, n), F32),
        compiler_params=_cp(("arbitrary", "arbitrary")),
        name="ada_mod",
    )(c_all, ada_w, ada_b.reshape(depth, 1, n))


def _norm_mod_kernel(x_ref, g_ref, sc_ref, sh_ref, o_ref):
    x = x_ref[0]
    h = x * lax.rsqrt(jnp.mean(x * x, axis=-1, keepdims=True) + NORM_EPS) * g_ref[...]
    o_ref[0] = (h * (1.0 + sc_ref[0]) + sh_ref[0]).astype(o_ref.dtype)


def _mod_spec(arr, tm):
    d = arr.shape[-1]
    if arr.shape[1] == 1:
        return pl.BlockSpec((1, 1, d), lambda g, i: (g, 0, 0))
    return pl.BlockSpec((1, tm, d), lambda g, i: (g, i, 0))


def norm_mod(x, gain, sc, sh, tm):
    g, r, d = x.shape
    return pl.pallas_call(
        _norm_mod_kernel,
        grid=(g, r // tm),
        in_specs=[pl.BlockSpec((1, tm, d), lambda g, i: (g, i, 0)),
                  pl.BlockSpec((1, d), lambda g, i: (0, 0)),
                  _mod_spec(sc, tm), _mod_spec(sh, tm)],
        out_specs=pl.BlockSpec((1, tm, d), lambda g, i: (g, i, 0)),
        out_shape=jax.ShapeDtypeStruct((g, r, d), BF16),
        compiler_params=_cp(("arbitrary", "arbitrary")),
        name="norm_mod",
    )(x, gain.reshape(1, d), sc, sh)


def _rms_kernel(x_ref, g_ref, o_ref):
    x = x_ref[0]
    o_ref[0] = x * lax.rsqrt(jnp.mean(x * x, axis=-1, keepdims=True) + NORM_EPS) * g_ref[...]


def final_norm(x, gain, tm):
    g, r, d = x.shape
    return pl.pallas_call(
        _rms_kernel,
        grid=(g, r // tm),
        in_specs=[pl.BlockSpec((1, tm, d), lambda g, i: (g, i, 0)),
                  pl.BlockSpec((1, d), lambda g, i: (0, 0))],
        out_specs=pl.BlockSpec((1, tm, d), lambda g, i: (g, i, 0)),
        out_shape=jax.ShapeDtypeStruct((g, r, d), F32),
        compiler_params=_cp(("arbitrary", "arbitrary")),
        name="final_norm",
    )(x, gain.reshape(1, d))


def _mm_kernel(a_ref, w_ref, o_ref):
    o_ref[0] = _dot(a_ref[0], w_ref[...])


def matmul_in(a, w, tm, tn):
    g, r, k = a.shape
    n = w.shape[1]
    return pl.pallas_call(
        _mm_kernel,
        grid=(g, r // tm, n // tn),
        in_specs=[pl.BlockSpec((1, tm, k), lambda g, i, j: (g, i, 0)),
                  pl.BlockSpec((k, tn), lambda g, i, j: (0, j))],
        out_specs=pl.BlockSpec((1, tm, tn), lambda g, i, j: (g, i, j)),
        out_shape=jax.ShapeDtypeStruct((g, r, n), F32),
        compiler_params=_cp(("arbitrary", "arbitrary", "arbitrary")),
        name="matmul_in",
    )(a, w)


def _mm_res_kernel(n_a, *refs):
    a_refs = refs[:n_a]
    w_refs = refs[n_a:2 * n_a]
    x_ref, g_ref, o_ref = refs[2 * n_a:]
    acc = _dot(a_refs[0][0], w_refs[0][...].astype(BF16))
    for a_ref, w_ref in zip(a_refs[1:], w_refs[1:]):
        acc = acc + _dot(a_ref[0], w_ref[...].astype(BF16))
    o_ref[0] = x_ref[0] + g_ref[0] * acc


def matmul_res(a_list, w, layer, x, gate, tm, tn):
    g, r, n = x.shape
    n_a = len(a_list)
    k = a_list[0].shape[-1]
    a_specs = [pl.BlockSpec((1, tm, k), lambda g, i, j: (g, i, 0)) for _ in a_list]
    w_specs = [pl.BlockSpec((None, k, tn), functools.partial(lambda g, i, j, q: (layer, q, j), q=q))
               for q in range(n_a)]
    if gate.shape[1] == 1:
        g_spec = pl.BlockSpec((1, 1, tn), lambda g, i, j: (g, 0, j))
    else:
        g_spec = pl.BlockSpec((1, tm, tn), lambda g, i, j: (g, i, j))
    return pl.pallas_call(
        functools.partial(_mm_res_kernel, n_a),
        grid=(g, r // tm, n // tn),
        in_specs=a_specs + w_specs + [pl.BlockSpec((1, tm, tn), lambda g, i, j: (g, i, j)), g_spec],
        out_specs=pl.BlockSpec((1, tm, tn), lambda g, i, j: (g, i, j)),
        out_shape=jax.ShapeDtypeStruct((g, r, n), F32),
        compiler_params=_cp(("arbitrary", "arbitrary", "arbitrary")),
        name="matmul_res",
    )(*a_list, *([w] * n_a), x, gate)


def _ffn_up_seq_kernel(a_ref, wg_ref, wv_ref, cwg_ref, cwv_ref, cbg_ref, cbv_ref, act_ref, lg_ref, lv_ref):
    a = a_ref[0]
    rows = a.shape[0]
    rid = _iota((rows, 1), 0)

    def conv(up, cw_ref, cb_ref):
        cw = cw_ref[...]
        s1 = jnp.where(rid >= 1, pltpu.roll(up, 1, axis=0), 0.0)
        s2 = jnp.where(rid >= 2, pltpu.roll(up, 2, axis=0), 0.0)
        return cw[0:1] * s2 + cw[1:2] * s1 + cw[2:3] * up + cb_ref[...]

    up_g = _dot(a, wg_ref[...].astype(BF16))
    lg_ref[0] = up_g[rows - 2:rows]
    gate = conv(up_g, cwg_ref, cbg_ref)
    up_v = _dot(a, wv_ref[...].astype(BF16))
    lv_ref[0] = up_v[rows - 2:rows]
    val = conv(up_v, cwv_ref, cbv_ref)
    act_ref[0] = (_silu(gate) * val).astype(BF16)


def ffn_up_seq(h, w_up, layer, conv_w, conv_b, tn):
    b, l, d = h.shape
    nt = D_FF // tn
    cb = conv_b.reshape(1, 2 * D_FF)
    return pl.pallas_call(
        _ffn_up_seq_kernel,
        grid=(b, nt),
        in_specs=[pl.BlockSpec((1, l, d), lambda b, j: (b, 0, 0)),
                  pl.BlockSpec((None, d, tn), lambda b, j: (layer, 0, j)),
                  pl.BlockSpec((None, d, tn), lambda b, j: (layer, 0, j + nt)),
                  pl.BlockSpec((3, tn), lambda b, j: (0, j)),
                  pl.BlockSpec((3, tn), lambda b, j: (0, j + nt)),
                  pl.BlockSpec((1, tn), lambda b, j: (0, j)),
                  pl.BlockSpec((1, tn), lambda b, j: (0, j + nt))],
        out_specs=[pl.BlockSpec((1, l, tn), lambda b, j: (b, 0, j)),
                   pl.BlockSpec((1, 2, tn), lambda b, j: (b, 0, j)),
                   pl.BlockSpec((1, 2, tn), lambda b, j: (b, 0, j))],
        out_shape=[jax.ShapeDtypeStruct((b, l, D_FF), BF16),
                   jax.ShapeDtypeStruct((b, 2, D_FF), F32),
                   jax.ShapeDtypeStruct((b, 2, D_FF), F32)],
        compiler_params=_cp(("arbitrary", "arbitrary")),
        name="ffn_up_seq",
    )(h, w_up, w_up, conv_w, conv_w, cb, cb)


def _ffn_up_step_kernel(a_ref, wg_ref, wv_ref, cwg_ref, cwv_ref, cbg_ref, cbv_ref,
                        stg_ref, stv_ref, act_ref, ug_ref, uv_ref):
    a = a_ref[...]
    up_g = _dot(a, wg_ref[...].astype(BF16))
    up_v = _dot(a, wv_ref[...].astype(BF16))
    ug_ref[...] = up_g
    uv_ref[...] = up_v
    cwg = cwg_ref[...]
    cwv = cwv_ref[...]
    gate = cwg[0:1] * stg_ref[:, 0, :] + cwg[1:2] * stg_ref[:, 1, :] + cwg[2:3] * up_g + cbg_ref[...]
    val = cwv[0:1] * stv_ref[:, 0, :] + cwv[1:2] * stv_ref[:, 1, :] + cwv[2:3] * up_v + cbv_ref[...]
    act_ref[...] = (_silu(gate) * val).astype(BF16)


def ffn_up_step(h, w_up, layer, conv_w, conv_b, conv_state, tn):
    b, d = h.shape
    nt = D_FF // tn
    cb = conv_b.reshape(1, 2 * D_FF)
    return pl.pallas_call(
        _ffn_up_step_kernel,
        grid=(nt,),
        in_specs=[pl.BlockSpec((b, d), lambda j: (0, 0)),
                  pl.BlockSpec((None, d, tn), lambda j: (layer, 0, j)),
                  pl.BlockSpec((None, d, tn), lambda j: (layer, 0, j + nt)),
                  pl.BlockSpec((3, tn), lambda j: (0, j)),
                  pl.BlockSpec((3, tn), lambda j: (0, j + nt)),
                  pl.BlockSpec((1, tn), lambda j: (0, j)),
                  pl.BlockSpec((1, tn), lambda j: (0, j + nt)),
                  pl.BlockSpec((None, b, 2, tn), lambda j: (layer, 0, 0, j)),
                  pl.BlockSpec((None, b, 2, tn), lambda j: (layer, 0, 0, j + nt))],
        out_specs=[pl.BlockSpec((b, tn), lambda j: (0, j))] * 3,
        out_shape=[jax.ShapeDtypeStruct((b, D_FF), BF16),
                   jax.ShapeDtypeStruct((b, D_FF), F32),
                   jax.ShapeDtypeStruct((b, D_FF), F32)],
        compiler_params=_cp(("arbitrary",)),
        name="ffn_up_step",
    )(h, w_up, w_up, conv_w, conv_w, cb, cb, conv_state, conv_state)


def _rwkv_pre(p, prev, prm, seg):
    mu, w0, w_up, a0, a_up, g_up, k_k, k_a = prm
    xm = p + mu * (prev - p)
    r = xm[:, 0:512]
    k = xm[:, 512:1024]
    v = xm[:, 1024:1536]
    xw = xm[:, 1536:1664]
    xa = xm[:, 1664:1792]
    xg = xm[:, 1792:2048]
    log_w = -jnp.exp(-jax.nn.softplus(-(w0 + _bdot(jnp.tanh(xw), w_up))) - 0.5)
    a = jax.nn.sigmoid(a0 + _bdot(xa, a_up))
    g = _bdot(jax.nn.sigmoid(xg), g_up)
    kx = k * k_k
    kk = kx * lax.rsqrt(_seg_dot(kx * kx, seg) + 1e-12)
    k2 = k * (1.0 + (a - 1.0) * k_a)
    return r, k2, v, log_w, a, g, kk


def _rwkv_post(y, r, k2, v, g, r_k, ln_w, ln_b, seg):
    inv = 1.0 / RWKV_HD
    mean = _seg_dot(y, seg) * inv
    dlt = y - mean
    var = _seg_dot(dlt * dlt, seg) * inv
    yn = dlt * lax.rsqrt(var + RWKV_LN_EPS) * ln_w + ln_b
    bonus = _seg_dot(r * k2 * r_k, seg) * v
    return (yn + bonus) * g


def _rwkv_seq_kernel(p_ref, mu_ref, w0_ref, wup_ref, a0_ref, aup_ref, gup_ref, kk_ref, ka_ref, rk_ref,
                     lnw_ref, lnb_ref, seg_ref, y_ref, s_out_ref, s_ref, prev_ref):
    c = CHUNK
    ns = p_ref.shape[0]
    n = ns * c
    t = pl.program_id(1)

    @pl.when(t == 0)
    def _():
        s_ref[...] = jnp.zeros_like(s_ref)
        prev_ref[...] = jnp.zeros_like(prev_ref)

    p = p_ref[...].reshape(n, RW_W)
    seg = seg_ref[...]
    seq_rows = lambda x: jnp.concatenate([jnp.broadcast_to(x[i:i + 1], (c, x.shape[-1])) for i in range(ns)], axis=0)
    tloc = _iota((n, 1), 0) & (c - 1)
    prev = jnp.where(tloc == 0, seq_rows(prev_ref[...]), pltpu.roll(p, 1, axis=0))
    last_rows = lambda x: jnp.concatenate([x[(i + 1) * c - 1:(i + 1) * c] for i in range(ns)], axis=0)
    prev_ref[...] = last_rows(p)
    prm = (mu_ref[...], w0_ref[...], wup_ref[...], a0_ref[...], aup_ref[...], gup_ref[...], kk_ref[...], ka_ref[...])
    r, k2, v, log_w, a, g, kk = _rwkv_pre(p, prev, prm, seg)

    ri = _iota((n, n), 0)
    ci = _iota((n, n), 1)
    cshift = c.bit_length() - 1
    tril = ((lax.shift_right_logical(ri, cshift) == lax.shift_right_logical(ci, cshift)) & (ci <= ri)).astype(F32)
    cum = _dot(tril, log_w, HI)
    e_in = jnp.exp(cum)
    e_out = jnp.exp(-cum)
    e_ex = jnp.exp(cum - log_w)
    tot = last_rows(cum)
    e_tail = jnp.exp(seq_rows(tot) - cum)
    e_tot = jnp.exp(tot)
    alpha = -(a * kk)
    gw = RWKV_GH * RWKV_HD
    rr = _iota((gw, gw), 0)
    cc = _iota((gw, gw), 1)
    shift = RWKV_HD.bit_length() - 1
    own = lax.shift_right_logical(rr, shift) == lax.shift_right_logical(cc, shift)
    tri_strict = (cc & (c - 1)) < (rr & (c - 1))
    tri_incl = (cc & (c - 1)) <= (rr & (c - 1))
    eye = rr == cc

    ng = RWKV_H // RWKV_GH
    be_t = (kk * e_ex).astype(BF16)
    r_t = (r * e_in).astype(BF16)
    al_t = (alpha * e_out).astype(BF16)
    k_t = (k2 * e_out).astype(BF16)
    al_p = (alpha * e_tail).astype(BF16)
    k_p = (k2 * e_tail).astype(BF16)
    vb = v.astype(BF16)

    def expand(x, i, gi):
        xg = x[i * c:(i + 1) * c, gi * gw:(gi + 1) * gw]
        return jnp.where(own, jnp.concatenate([xg] * RWKV_GH, axis=0), jnp.zeros((), BF16))

    y_rows = []
    for i in range(ns):
        ys = []
        for gi in range(ng):
            s0 = s_ref[i * ng + gi]
            lhs = jnp.concatenate([expand(be_t, i, gi), expand(r_t, i, gi)], axis=0)
            rhs = jnp.concatenate([expand(al_t, i, gi), expand(k_t, i, gi)], axis=0)
            vv = expand(vb, i, gi)
            m1 = _dot_nt(lhs, rhs)
            m2 = _dot(lhs, s0.astype(BF16))
            l_a = jnp.where(tri_strict, m1[:gw, :gw], 0.0)
            l_k = jnp.where(tri_strict, m1[:gw, gw:], 0.0)
            u = _neumann_solve(l_a, m2[:gw] + _dot(l_k.astype(BF16), vv), cshift)
            uv = jnp.concatenate([u.astype(BF16), vv], axis=0)
            a_full = jnp.concatenate([jnp.where(tri_incl, m1[gw:, :gw], 0.0),
                                      jnp.where(tri_incl, m1[gw:, gw:], 0.0)], axis=1).astype(BF16)
            y_g = m2[gw:] + _dot(a_full, uv)
            ys.append(functools.reduce(lambda p_, q_: p_ + q_, [y_g[j * c:(j + 1) * c] for j in range(RWKV_GH)]))
            scale = _row_to_col(e_tot[i:i + 1, gi * gw:(gi + 1) * gw], eye)
            tails = jnp.concatenate([expand(al_p, i, gi), expand(k_p, i, gi)], axis=0)
            s_ref[i * ng + gi] = s0 * scale + _dot_tn(tails, uv)
        y_rows.append(jnp.concatenate(ys, axis=1))

    y = jnp.concatenate(y_rows, axis=0)
    out = _rwkv_post(y, r, k2, v, g, rk_ref[...], lnw_ref[...], lnb_ref[...], seg)
    y_ref[...] = out.astype(BF16).reshape(ns, c, RWKV_D)

    @pl.when(t == pl.num_programs(1) - 1)
    def _():
        for i in range(ns):
            for h in range(RWKV_H):
                gi, hh = divmod(h, RWKV_GH)
                s_out_ref[i, h] = s_ref[i * ng + gi, hh * RWKV_HD:(hh + 1) * RWKV_HD, hh * RWKV_HD:(hh + 1) * RWKV_HD]


def _const_spec(arr, nidx):
    zeros = (0,) * arr.ndim
    if nidx == 1:
        return pl.BlockSpec(arr.shape, lambda i: zeros)
    return pl.BlockSpec(arr.shape, lambda b, t: zeros)


def rwkv_seq(proj, prm, seg, ns):
    b, l, _ = proj.shape
    c = CHUNK
    gw = RWKV_GH * RWKV_HD
    return pl.pallas_call(
        _rwkv_seq_kernel,
        grid=(b // ns, l // c),
        in_specs=[pl.BlockSpec((ns, c, RW_W), lambda b, t: (b, t, 0))] + [_const_spec(x, 2) for x in prm] + [_const_spec(seg, 2)],
        out_specs=[pl.BlockSpec((ns, c, RWKV_D), lambda b, t: (b, t, 0)),
                   pl.BlockSpec((ns, RWKV_H, RWKV_HD, RWKV_HD), lambda b, t: (b, 0, 0, 0))],
        out_shape=[jax.ShapeDtypeStruct((b, l, RWKV_D), BF16),
                   jax.ShapeDtypeStruct((b, RWKV_H, RWKV_HD, RWKV_HD), F32)],
        scratch_shapes=[pltpu.VMEM((ns * (RWKV_H // RWKV_GH), gw, gw), F32),
                        pltpu.VMEM((ns, RW_W), F32)],
        compiler_params=_cp(("arbitrary", "arbitrary")),
        name="rwkv_seq",
    )(proj, *prm, seg)


def _rwkv_step_kernel(p_ref, prev_ref, s_in_ref, mu_ref, w0_ref, wup_ref, a0_ref, aup_ref, gup_ref, kk_ref,
                      ka_ref, rk_ref, lnw_ref, lnb_ref, seg_ref, y_ref, s_out_ref, nat_scr, col_scr, yt_scr):
    h = pl.program_id(0)

    @pl.when(h == 0)
    def _():
        prm = (mu_ref[...], w0_ref[...], wup_ref[...], a0_ref[...], aup_ref[...], gup_ref[...], kk_ref[...],
               ka_ref[...])
        r, k2, v, log_w, a, g, kk = _rwkv_pre(p_ref[...], prev_ref[...], prm, seg_ref[...])
        for i, x in enumerate((r, k2, v, g)):
            nat_scr[i] = x
        for i, x in enumerate((jnp.exp(log_w), kk * a, kk, k2, r, v)):
            col_scr[i] = x.T

    rows = pl.ds(pl.multiple_of(h * RWKV_HD, RWKV_HD), RWKV_HD)
    w_t, kka_t, kk_t, k_t, r_t, v_t = (col_scr[i, rows, :] for i in range(6))
    nacc = 4
    acc = [kk_t[j:j + 1] * s_in_ref[j] for j in range(nacc)]
    for k in range(nacc, RWKV_HD):
        acc[k % nacc] = acc[k % nacc] + kk_t[k:k + 1] * s_in_ref[k]
    sa = (acc[0] + acc[1]) + (acc[2] + acc[3])
    acc = [None] * nacc
    for k in range(RWKV_HD):
        s_new = s_in_ref[k] * w_t[k:k + 1] - kka_t[k:k + 1] * sa + k_t[k:k + 1] * v_t
        s_out_ref[k] = s_new
        term = r_t[k:k + 1] * s_new
        acc[k % nacc] = term if acc[k % nacc] is None else acc[k % nacc] + term
    yt_scr[rows, :] = (acc[0] + acc[1]) + (acc[2] + acc[3])

    @pl.when(h == RWKV_H - 1)
    def _():
        out = _rwkv_post(yt_scr[...].T, nat_scr[0], nat_scr[1], nat_scr[2], nat_scr[3], rk_ref[...], lnw_ref[...],
                         lnb_ref[...], seg_ref[...])
        y_ref[...] = out.astype(BF16)


def rwkv_step(proj, shift, state_t, layer, prm, seg):
    b = proj.shape[0]
    blk = (None, RWKV_HD, RWKV_HD, b)
    return pl.pallas_call(
        _rwkv_step_kernel,
        grid=(RWKV_H,),
        in_specs=[pl.BlockSpec((b, RW_W), lambda h: (0, 0)),
                  pl.BlockSpec((b, RW_W), lambda h: (0, 0)),
                  pl.BlockSpec((None,) + blk, lambda h: (layer, h, 0, 0, 0))]
                 + [_const_spec(x, 1) for x in prm] + [_const_spec(seg, 1)],
        out_specs=[pl.BlockSpec((b, RWKV_D), lambda h: (0, 0)),
                   pl.BlockSpec(blk, lambda h: (h, 0, 0, 0))],
        out_shape=[jax.ShapeDtypeStruct((b, RWKV_D), BF16),
                   jax.ShapeDtypeStruct(state_t.shape[1:], F32)],
        scratch_shapes=[pltpu.VMEM((4, b, RWKV_D), F32), pltpu.VMEM((6, RWKV_D, b), F32),
                        pltpu.VMEM((RWKV_D, b), F32)],
        compiler_params=_cp(("arbitrary",)),
        name="rwkv_step",
    )(proj, shift, state_t, *prm, seg)


def _s5_out(u, h_re, h_im, cc_ref, d_ref, gw_ref, gb_ref):
    y = _dot(jnp.concatenate([h_re, h_im], axis=1).astype(BF16), cc_ref[...]) + d_ref[...] * u
    z = _gelu(y)
    return z * jax.nn.sigmoid(_bdot(z, gw_ref[...]) + gb_ref[...])


def _s5_seq_kernel(u_ref, bbr_ref, bbi_ref, cc_ref, abr_ref, abi_ref, d_ref, gw_ref, gb_ref,
                   y_ref, hr_ref, hi_ref, cr_ref, ci_ref, pr_ref, pi_ref):
    tc = u_ref.shape[1]
    gs = pr_ref.shape[0]
    t = pl.program_id(1)
    steps = [1 << i for i in range(gs.bit_length() - 1)]

    @pl.when(t == 0)
    def _():
        cr_ref[...] = jnp.zeros_like(cr_ref)
        ci_ref[...] = jnp.zeros_like(ci_ref)
        rid = _iota((gs, 1), 0)
        pr = jnp.broadcast_to(abr_ref[...], (gs, S5_STATE))
        pi = jnp.broadcast_to(abi_ref[...], (gs, S5_STATE))
        for dd in steps:
            sr = pltpu.roll(pr, dd, axis=0)
            si = pltpu.roll(pi, dd, axis=0)
            m = rid >= dd
            pr, pi = jnp.where(m, pr * sr - pi * si, pr), jnp.where(m, pr * si + pi * sr, pi)
        pr_ref[...] = pr
        pi_ref[...] = pi

    u = u_ref[0]
    ub = u.astype(BF16)
    h_re = _dot(ub, bbr_ref[...])
    h_im = _dot(ub, bbi_ref[...])
    rig = _iota((tc, 1), 0) & (gs - 1)
    for dd in steps:
        ar = pr_ref[dd - 1:dd, :]
        ai = pi_ref[dd - 1:dd, :]
        sr = pltpu.roll(h_re, dd, axis=0)
        si = pltpu.roll(h_im, dd, axis=0)
        m = rig >= dd
        h_re, h_im = h_re + jnp.where(m, ar * sr - ai * si, 0.0), h_im + jnp.where(m, ar * si + ai * sr, 0.0)
    pr = pr_ref[...]
    pi = pi_ref[...]
    c_re = cr_ref[...]
    c_im = ci_ref[...]
    g_re, g_im = [], []
    for j in range(tc // gs):
        b_re = h_re[j * gs:(j + 1) * gs]
        b_im = h_im[j * gs:(j + 1) * gs]
        b_re, b_im = b_re + pr * c_re - pi * c_im, b_im + pr * c_im + pi * c_re
        c_re = b_re[gs - 1:gs]
        c_im = b_im[gs - 1:gs]
        g_re.append(b_re)
        g_im.append(b_im)
    h_re = jnp.concatenate(g_re, axis=0)
    h_im = jnp.concatenate(g_im, axis=0)
    cr_ref[...] = c_re
    ci_ref[...] = c_im
    y_ref[0] = _s5_out(u, h_re, h_im, cc_ref, d_ref, gw_ref, gb_ref).astype(BF16)

    @pl.when(t == pl.num_programs(1) - 1)
    def _():
        hr_ref[0] = h_re[tc - 1:tc]
        hi_ref[0] = h_im[tc - 1:tc]


def s5_seq(proj, prm):
    b, l, _ = proj.shape
    tc = SCAN_T
    return pl.pallas_call(
        _s5_seq_kernel,
        grid=(b, l // tc),
        in_specs=[pl.BlockSpec((1, tc, S5_D), lambda b, t: (b, t, S5_OFF // S5_D))] + [_const_spec(x, 2) for x in prm],
        out_specs=[pl.BlockSpec((1, tc, S5_D), lambda b, t: (b, t, 0)),
                   pl.BlockSpec((1, 1, S5_STATE), lambda b, t: (b, 0, 0)),
                   pl.BlockSpec((1, 1, S5_STATE), lambda b, t: (b, 0, 0))],
        out_shape=[jax.ShapeDtypeStruct((b, l, S5_D), BF16),
                   jax.ShapeDtypeStruct((b, 1, S5_STATE), F32),
                   jax.ShapeDtypeStruct((b, 1, S5_STATE), F32)],
        scratch_shapes=[pltpu.VMEM((1, S5_STATE), F32), pltpu.VMEM((1, S5_STATE), F32),
                        pltpu.VMEM((8, S5_STATE), F32), pltpu.VMEM((8, S5_STATE), F32)],
        compiler_params=_cp(("arbitrary", "arbitrary")),
        name="s5_seq",
    )(proj, *prm)


def _s5_step_kernel(u_ref, h0r_ref, h0i_ref, bbr_ref, bbi_ref, cc_ref, abr_ref, abi_ref, d_ref, gw_ref, gb_ref,
                    y_ref, hr_ref, hi_ref):
    u = u_ref[...]
    ar = abr_ref[...]
    ai = abi_ref[...]
    h0r = h0r_ref[...]
    h0i = h0i_ref[...]
    h_re = ar * h0r - ai * h0i + _dot(u, bbr_ref[...], HI)
    h_im = ar * h0i + ai * h0r + _dot(u, bbi_ref[...], HI)
    hr_ref[...] = h_re
    hi_ref[...] = h_im
    y_ref[...] = _s5_out(u, h_re, h_im, cc_ref, d_ref, gw_ref, gb_ref).astype(BF16)


def s5_step(proj, h0_re, h0_im, prm):
    b = proj.shape[0]
    return pl.pallas_call(
        _s5_step_kernel,
        grid=(1,),
        in_specs=[pl.BlockSpec((b, S5_D), lambda i: (0, S5_OFF // S5_D)),
                  _const_spec(h0_re, 1), _const_spec(h0_im, 1)] + [_const_spec(x, 1) for x in prm],
        out_specs=[pl.BlockSpec((b, S5_D), lambda i: (0, 0)),
                   pl.BlockSpec((b, S5_STATE), lambda i: (0, 0)),
                   pl.BlockSpec((b, S5_STATE), lambda i: (0, 0))],
        out_shape=[jax.ShapeDtypeStruct((b, S5_D), BF16),
                   jax.ShapeDtypeStruct((b, S5_STATE), F32),
                   jax.ShapeDtypeStruct((b, S5_STATE), F32)],
        compiler_params=_cp(("arbitrary",)),
        name="s5_step",
    )(proj, h0_re, h0_im, *prm)


def _gdn_pre(qkv_c, z, ab, alog_ref, dtb_ref):
    act = _silu(qkv_c)
    qs, ks = [], []
    for h in range(GDN_H):
        qh = act[:, h * GDN_HD:(h + 1) * GDN_HD]
        kh = act[:, GDN_D + h * GDN_HD:GDN_D + (h + 1) * GDN_HD]
        qs.append(qh * lax.rsqrt(jnp.sum(qh * qh, axis=-1, keepdims=True) + 1e-6))
        ks.append(kh * lax.rsqrt(jnp.sum(kh * kh, axis=-1, keepdims=True) + 1e-6))
    v = act[:, 2 * GDN_D:3 * GDN_D]
    g = -jnp.exp(alog_ref[...]) * jax.nn.softplus(ab + dtb_ref[...])
    beta = jax.nn.sigmoid(ab)
    return qs, ks, v, g, beta


def _gdn_post(o, z_h, nw):
    on = o * lax.rsqrt(jnp.mean(o * o, axis=-1, keepdims=True) + NORM_EPS) * nw
    return on * _silu(z_h)


def _gdn_seq_kernel(p_ref, cw_ref, alog_ref, dtb_ref, nw_ref, y_ref, s_out_ref, s_ref, prev_ref):
    c = CHUNK
    ns = p_ref.shape[0]
    n = ns * c
    t = pl.program_id(1)

    @pl.when(t == 0)
    def _():
        s_ref[...] = jnp.zeros_like(s_ref)
        prev_ref[...] = jnp.zeros_like(prev_ref)

    cw = cw_ref[...]
    convs = []
    for i in range(ns):
        qkv = p_ref[i, :, 0:3 * GDN_D]
        ext = jnp.concatenate([prev_ref[i], qkv], axis=0)
        prev_ref[i] = qkv[c - 8:c]
        conv = cw[3:4] * qkv
        for j in (1, 2, 3):
            conv = conv + cw[3 - j:4 - j] * pltpu.roll(ext, j, axis=0)[8:]
        convs.append(conv)
    x = p_ref[...].reshape(n, GDN_W)
    z = x[:, 3 * GDN_D:4 * GDN_D]
    ab = x[:, 4 * GDN_D:4 * GDN_D + 128]
    qs, ks, v, g, beta = _gdn_pre(jnp.concatenate(convs, axis=0), z, ab, alog_ref, dtb_ref)

    shift = c.bit_length() - 1
    ri = _iota((n, n), 0)
    ci = _iota((n, n), 1)
    tril = ((lax.shift_right_logical(ri, shift) == lax.shift_right_logical(ci, shift)) & (ci <= ri)).astype(F32)
    gc_all = _dot(tril, g, HI)

    hc = GDN_H * c
    heads = range(GDN_H)
    stack = lambda pieces: jnp.concatenate(pieces, axis=0)
    lane = _iota((c, 128), 1)
    rr = _iota((hc, hc), 0)
    cc = _iota((hc, hc), 1)
    same = lax.shift_right_logical(rr, shift) == lax.shift_right_logical(cc, shift)
    causal = same & (cc <= rr)
    strict = same & (cc < rr)
    ones = jnp.ones((hc, 128), F32)
    nw = nw_ref[...]
    for i in range(ns):
        sq = slice(i * c, (i + 1) * c)
        gc = gc_all[sq]
        g_col = stack([gc[:, h:h + 1] for h in heads])
        b_col = stack([beta[sq, GDN_H + h:GDN_H + h + 1] for h in heads])
        g_row = _dot_nt(ones, stack([jnp.where(lane == h, gc, 0.0) for h in heads]), HI)
        decay = jnp.where(causal, jnp.exp(jnp.where(causal, g_col - g_row, 0.0)), 0.0)
        q = stack([qs[h][sq] for h in heads]) * (GDN_HD ** -0.5)
        k = stack([ks[h][sq] for h in heads])
        v_s = stack([v[sq, h * GDN_HD:(h + 1) * GDN_HD] for h in heads])
        kb = k * b_col
        m1 = _dot_nt(stack([kb, q]).astype(BF16), k.astype(BF16))
        lmat = jnp.where(strict, m1[:hc] * decay, 0.0)
        attn = m1[hc:] * decay
        eg = jnp.exp(g_col)
        sol = _neumann_solve3(-lmat, jnp.concatenate([v_s * b_col, kb * eg], axis=1), shift)
        u = sol[:, :GDN_HD]
        wk = sol[:, GDN_HD:]
        qd = q * eg
        ws = [_dot(stack([wk[h * c:(h + 1) * c], qd[h * c:(h + 1) * c]]).astype(BF16),
                   s_ref[i * GDN_H + h].astype(BF16)) for h in heads]
        v_new = u - stack([w[:c] for w in ws])
        o = stack([w[c:] for w in ws]) + _bdot(attn, v_new)
        g_last = [gc[c - 1:c, h:h + 1] for h in heads]
        k_tail = k * jnp.exp(stack([jnp.broadcast_to(gl, (c, 1)) for gl in g_last]) - g_col)
        for h in heads:
            rows = slice(h * c, (h + 1) * c)
            s_ref[i * GDN_H + h] = (s_ref[i * GDN_H + h] * jnp.exp(g_last[h])
                                    + _dot_tn(k_tail[rows].astype(BF16), v_new[rows].astype(BF16)))
            z_h = z[sq, h * GDN_HD:(h + 1) * GDN_HD]
            y_ref[i, :, h * GDN_HD:(h + 1) * GDN_HD] = _gdn_post(o[rows], z_h, nw).astype(BF16)

    @pl.when(t == pl.num_programs(1) - 1)
    def _():
        s_out_ref[...] = s_ref[...].reshape(s_out_ref.shape)


def gdn_seq(proj, prm, ns):
    b, l, _ = proj.shape
    c = CHUNK
    return pl.pallas_call(
        _gdn_seq_kernel,
        grid=(b // ns, l // c),
        in_specs=[pl.BlockSpec((ns, c, GDN_W), lambda b, t: (b, t, GDN_OFF // GDN_W))] + [_const_spec(x, 2) for x in prm],
        out_specs=[pl.BlockSpec((ns, c, GDN_D), lambda b, t: (b, t, 0)),
                   pl.BlockSpec((ns, GDN_H, GDN_HD, GDN_HD), lambda b, t: (b, 0, 0, 0))],
        out_shape=[jax.ShapeDtypeStruct((b, l, GDN_D), BF16),
                   jax.ShapeDtypeStruct((b, GDN_H, GDN_HD, GDN_HD), F32)],
        scratch_shapes=[pltpu.VMEM((ns * GDN_H, GDN_HD, GDN_HD), F32),
                        pltpu.VMEM((ns, 8, 3 * GDN_D), F32)],
        compiler_params=_cp(("arbitrary", "arbitrary")),
        name="gdn_seq",
    )(proj, *prm)


def _gdn_step_kernel(nb, p_ref, cs_ref, s_in_ref, cw_ref, alog_ref, dtb_ref, nw_ref, y_ref, s_out_ref):
    x = p_ref[...]
    qkv = x[:, 0:3 * GDN_D]
    z = x[:, 3 * GDN_D:4 * GDN_D]
    ab = x[:, 4 * GDN_D:4 * GDN_D + 128]
    cs = cs_ref[...]
    cw = cw_ref[...]
    w3 = 3 * GDN_D
    conv = cw[0:1] * cs[:, 0:w3] + cw[1:2] * cs[:, w3:2 * w3] + cw[2:3] * cs[:, 2 * w3:3 * w3] + cw[3:4] * qkv
    qs, ks, v, g, beta = _gdn_pre(conv, z, ab, alog_ref, dtb_ref)
    nw = nw_ref[...]
    pad = jnp.zeros((GDN_HD - nb, GDN_HD), F32)
    ks_t = [jnp.concatenate([ks[h], pad], axis=0).T for h in range(GDN_H)]
    qs_t = [jnp.concatenate([qs[h], pad], axis=0).T for h in range(GDN_H)]
    for i in range(nb):
        for h in range(GDN_H):
            k_c = ks_t[h][:, i:i + 1]
            q_c = qs_t[h][:, i:i + 1]
            eg = jnp.exp(g[i:i + 1, h:h + 1])
            b_s = beta[i:i + 1, GDN_H + h:GDN_H + h + 1]
            s = s_in_ref[i, h]
            sk = jnp.sum(k_c * s, axis=0, keepdims=True)
            v_row = v[i:i + 1, h * GDN_HD:(h + 1) * GDN_HD]
            s_new = s * eg + (k_c * b_s) * (v_row - eg * sk)
            s_out_ref[i, h] = s_new
            o = jnp.sum(q_c * s_new, axis=0, keepdims=True) * (GDN_HD ** -0.5)
            z_h = z[i:i + 1, h * GDN_HD:(h + 1) * GDN_HD]
            y_ref[i:i + 1, h * GDN_HD:(h + 1) * GDN_HD] = _gdn_post(o, z_h, nw).astype(BF16)


def gdn_step(proj, conv_state, state, layer, prm, nb):
    b = proj.shape[0]
    blk = (nb, GDN_H, GDN_HD, GDN_HD)
    return pl.pallas_call(
        functools.partial(_gdn_step_kernel, nb),
        grid=(b // nb,),
        in_specs=[pl.BlockSpec((nb, GDN_W), lambda i: (i, GDN_OFF // GDN_W)),
                  pl.BlockSpec((nb, 9 * GDN_D), lambda i: (i, 0)),
                  pl.BlockSpec((None,) + blk, lambda i: (layer, i, 0, 0, 0))] + [_const_spec(x, 1) for x in prm],
        out_specs=[pl.BlockSpec((nb, GDN_D), lambda i: (i, 0)),
                   pl.BlockSpec(blk, lambda i: (i, 0, 0, 0))],
        out_shape=[jax.ShapeDtypeStruct((b, GDN_D), BF16),
                   jax.ShapeDtypeStruct(state.shape[1:], F32)],
        compiler_params=_cp(("arbitrary",)),
        name="gdn_step",
    )(proj, conv_state, state, *prm)


def _lru_gates(xc, gate, wr_ref, br_ref, wi_ref, bi_ref, lam_ref):
    r = jax.nn.sigmoid(_bdot(xc, wr_ref[...]) + br_ref[...])
    i = jax.nn.sigmoid(_bdot(xc, wi_ref[...]) + bi_ref[...])
    log_a = -LRU_C * r * jax.nn.softplus(-lam_ref[...])
    a = jnp.exp(log_a)
    one_minus_a2 = -jnp.tanh(log_a) * (a * a + 1.0)
    b = jnp.sqrt(one_minus_a2) * (i * xc)
    return a, b, _gelu(gate)


def _lru_seq_kernel(p_ref, cw_ref, cb_ref, wr_ref, br_ref, wi_ref, bi_ref, lam_ref, y_ref, h_out_ref,
                    prev_ref, carry_ref):
    tc = p_ref.shape[1]
    t = pl.program_id(1)

    @pl.when(t == 0)
    def _():
        prev_ref[...] = jnp.zeros_like(prev_ref)
        carry_ref[...] = jnp.zeros_like(carry_ref)

    x = p_ref[0]
    xr = x[:, :LRU_D]
    gate = x[:, LRU_D:]
    ext = jnp.concatenate([prev_ref[...], xr], axis=0)
    prev_ref[...] = xr[tc - 8:tc]
    cw = cw_ref[...]
    xc = cw[3:4] * xr + cb_ref[...]
    for j in (1, 2, 3):
        xc = xc + cw[3 - j:4 - j] * pltpu.roll(ext, j, axis=0)[8:]
    a, b, gg = _lru_gates(xc, gate, wr_ref, br_ref, wi_ref, bi_ref, lam_ref)
    rid = _iota((tc, 1), 0)
    dd = 1
    while dd < tc:
        m = rid >= dd
        sa = pltpu.roll(a, dd, axis=0)
        sb = pltpu.roll(b, dd, axis=0)
        a, b = jnp.where(m, a * sa, a), jnp.where(m, a * sb + b, b)
        dd *= 2
    h = b + a * carry_ref[...]
    carry_ref[...] = h[tc - 1:tc]
    y_ref[0] = (h * gg).astype(BF16)

    @pl.when(t == pl.num_programs(1) - 1)
    def _():
        h_out_ref[0] = h[tc - 1:tc]


def lru_seq(proj, prm):
    b, l, _ = proj.shape
    tc = SCAN_T
    return pl.pallas_call(
        _lru_seq_kernel,
        grid=(b, l // tc),
        in_specs=[pl.BlockSpec((1, tc, LRU_W), lambda b, t: (b, t, LRU_OFF // LRU_W))] + [_const_spec(x, 2) for x in prm],
        out_specs=[pl.BlockSpec((1, tc, LRU_D), lambda b, t: (b, t, 0)),
                   pl.BlockSpec((1, 1, LRU_D), lambda b, t: (b, 0, 0))],
        out_shape=[jax.ShapeDtypeStruct((b, l, LRU_D), BF16),
                   jax.ShapeDtypeStruct((b, 1, LRU_D), F32)],
        scratch_shapes=[pltpu.VMEM((8, LRU_D), F32), pltpu.VMEM((1, LRU_D), F32)],
        compiler_params=_cp(("arbitrary", "arbitrary")),
        name="lru_seq",
    )(proj, *prm)


def _lru_step_kernel(p_ref, cs_ref, h0_ref, cw_ref, cb_ref, wr_ref, br_ref, wi_ref, bi_ref, lam_ref, y_ref, h_ref):
    x = p_ref[...]
    xr = x[:, :LRU_D]
    gate = x[:, LRU_D:]
    cs = cs_ref[...]
    cw = cw_ref[...]
    xc = (cw[0:1] * cs[:, 0:LRU_D] + cw[1:2] * cs[:, LRU_D:2 * LRU_D] + cw[2:3] * cs[:, 2 * LRU_D:3 * LRU_D]
          + cw[3:4] * xr + cb_ref[...])
    a, b, gg = _lru_gates(xc, gate, wr_ref, br_ref, wi_ref, bi_ref, lam_ref)
    h = a * h0_ref[...] + b
    h_ref[...] = h
    y_ref[...] = (h * gg).astype(BF16)


def lru_step(proj, conv_state, h0, prm):
    b = proj.shape[0]
    return pl.pallas_call(
        _lru_step_kernel,
        grid=(1,),
        in_specs=[pl.BlockSpec((b, LRU_W), lambda i: (0, LRU_OFF // LRU_W)),
                  _const_spec(conv_state, 1), _const_spec(h0, 1)] + [_const_spec(x, 1) for x in prm],
        out_specs=[pl.BlockSpec((b, LRU_D), lambda i: (0, 0)),
                   pl.BlockSpec((b, LRU_D), lambda i: (0, 0))],
        out_shape=[jax.ShapeDtypeStruct((b, LRU_D), BF16),
                   jax.ShapeDtypeStruct((b, LRU_D), F32)],
        compiler_params=_cp(("arbitrary",)),
        name="lru_step",
    )(proj, conv_state, h0, *prm)


def _pad_cols(x, width):
    return jnp.pad(x, [(0, 0)] * (x.ndim - 1) + [(0, width - x.shape[-1])])


def _pack_rwkv_cols(x):
    return jnp.concatenate([x[..., :1536], _pad_cols(x[..., 1536:1632], 128), _pad_cols(x[..., 1632:1728], 128),
                            x[..., 1728:1984]], axis=-1)


def _unpack_rwkv_cols(x):
    return jnp.concatenate([x[..., :1536], x[..., 1536:1632], x[..., 1664:1760], x[..., 1792:2048]], axis=-1)


def _pack_w_in(w):
    o = RWKV_PROJ
    return jnp.concatenate([_pack_rwkv_cols(w[:, :o]), w[:, o:o + S5_D],
                            _pad_cols(w[:, o + S5_D:o + S5_D + GDN_PROJ], GDN_W), w[:, o + S5_D + GDN_PROJ:]],
                           axis=-1).astype(BF16)


def _block_diag(blocks):
    g, a, b = blocks.shape
    eye = jnp.eye(g, dtype=blocks.dtype)
    return (eye[:, None, :, None] * blocks[:, :, None, :]).reshape(g * a, g * b)


def _s5_params(lam_re, lam_im, log_dt, b_re, b_im, c_re, c_im, d, glu_w, glu_b):
    dt = jnp.exp(log_dt)[:, None]
    mag = jnp.exp(lam_re * dt)
    ab_re, ab_im = mag * jnp.cos(lam_im * dt), mag * jnp.sin(lam_im * dt)
    den = lam_re * lam_re + lam_im * lam_im
    nr = ab_re - 1.0
    f_re = (nr * lam_re + ab_im * lam_im) / den
    f_im = (ab_im * lam_re - nr * lam_im) / den
    bb_re = f_re[..., None] * b_re - f_im[..., None] * b_im
    bb_im = f_re[..., None] * b_im + f_im[..., None] * b_re
    bbr = _block_diag(jnp.swapaxes(bb_re, 1, 2))
    bbi = _block_diag(jnp.swapaxes(bb_im, 1, 2))
    cc = jnp.concatenate([_block_diag(jnp.swapaxes(c_re, 1, 2)), -_block_diag(jnp.swapaxes(c_im, 1, 2))],
                         axis=0).astype(BF16)
    tail = (cc, ab_re.reshape(1, S5_STATE), ab_im.reshape(1, S5_STATE), d.reshape(1, S5_D), glu_w,
            glu_b.reshape(1, S5_D))
    return (bbr.astype(BF16), bbi.astype(BF16)) + tail, (bbr, bbi) + tail


def _layer(xp, xs, mod, st, p, seg, layer, stacked):
    bp, lp, d = xp.shape
    bs = xs.shape[1]
    sh1, sc1, g1, sh2, sc2, g2 = jnp.split(mod, 6, axis=-1)
    pm = lambda m: m[:bp, None, :]
    sm = lambda m: m[None, bp:bp + bs, :]

    w_in = _pack_w_in(p['w_in'])
    row = lambda v: v.reshape(1, -1)
    rw_prm = (row(_pack_rwkv_cols(p['rwkv_mu'])), row(p['rwkv_w0']),
              jnp.pad(p['rwkv_w_up'], ((0, 32), (0, 0))), row(p['rwkv_a0']),
              jnp.pad(p['rwkv_a_up'], ((0, 32), (0, 0))), p['rwkv_g_up'], row(p['rwkv_k_k']), row(p['rwkv_k_a']),
              row(p['rwkv_r_k']), row(p['rwkv_ln_w']), row(p['rwkv_ln_b']))
    s5_prm, s5_prm_f32 = _s5_params(p['s5_lambda_re'], p['s5_lambda_im'], p['s5_log_dt'], p['s5_b_re'], p['s5_b_im'],
                        p['s5_c_re'], p['s5_c_im'], p['s5_d'], p['s5_glu_w'], p['s5_glu_b'])
    gdn_prm = (p['gdn_conv_w'], _pad_cols(row(p['gdn_a_log']), 128), _pad_cols(row(p['gdn_dt_bias']), 128),
               row(p['gdn_norm_w']))
    lru_prm = (p['lru_conv_w'], row(p['lru_conv_b']), _block_diag(p['lru_wr']), row(p['lru_br']),
               _block_diag(p['lru_wi']), row(p['lru_bi']), row(p['lru_lambda']))

    h = norm_mod(xp, p['norm1_g'], pm(sc1), pm(sh1), 512)
    proj = matmul_in(h, w_in, 1024, 1024)
    ya, s_wkv = rwkv_seq(proj, rw_prm, seg, bp)
    yb, s_re, s_im = s5_seq(proj, s5_prm)
    yc, s_gdn = gdn_seq(proj, gdn_prm, bp)
    yd, s_lru = lru_seq(proj, lru_prm)
    xp = matmul_res([ya, yb, yc, yd], stacked['w_out'], layer, xp, pm(g1), lp, 512)
    h = norm_mod(xp, p['norm2_g'], pm(sc2), pm(sh2), 512)
    act, lg, lv = ffn_up_seq(h, stacked['ffn_w_up'], layer, p['ffn_conv_w'], p['ffn_conv_b'], 256)
    xp = matmul_res([act], stacked['ffn_w_down'], layer, xp, pm(g2), 1024, 256)
    o = RWKV_PROJ
    new_p = {
        'rwkv_wkv': s_wkv,
        'rwkv_shift': _unpack_rwkv_cols(proj[:, -1, :RW_W]),
        's5_re': s_re.reshape(bp, S5_G, S5_N), 's5_im': s_im.reshape(bp, S5_G, S5_N),
        'gdn': s_gdn,
        'gdn_conv': proj[:, -3:, GDN_OFF:GDN_OFF + 3 * GDN_D],
        'lru_h': s_lru.reshape(bp, LRU_D),
        'lru_conv': proj[:, -3:, LRU_OFF:LRU_OFF + LRU_D],
        'ffn_conv': jnp.concatenate([lg, lv], axis=-1),
    }

    h = norm_mod(xs, p['norm1_g'], sm(sc1), sm(sh1), bs)
    proj = matmul_in(h, w_in, bs, 1024)[0]
    ya, s_wkv = rwkv_step(proj, _pack_rwkv_cols(st['rwkv_shift']), stacked['rwkv_wkv'], layer, rw_prm, seg)
    yb, s_re, s_im = s5_step(proj, st['s5_re'].reshape(bs, S5_STATE), st['s5_im'].reshape(bs, S5_STATE), s5_prm_f32)
    yc, s_gdn = gdn_step(proj, st['gdn_conv'].reshape(bs, 9 * GDN_D), stacked['gdn'], layer, gdn_prm, 16)
    yd, s_lru = lru_step(proj, st['lru_conv'].reshape(bs, 3 * LRU_D), st['lru_h'], lru_prm)
    xs = matmul_res([y[None] for y in (ya, yb, yc, yd)], stacked['w_out'], layer, xs, sm(g1), bs, 512)
    h = norm_mod(xs, p['norm2_g'], sm(sc2), sm(sh2), bs)
    act, ug, uv = ffn_up_step(h[0], stacked['ffn_w_up'], layer, p['ffn_conv_w'], p['ffn_conv_b'],
                              stacked['ffn_conv'], 256)
    xs = matmul_res([act[None]], stacked['ffn_w_down'], layer, xs, sm(g2), bs, 256)
    shift_rows = lambda buf, new: jnp.concatenate([buf[:, 1:], new[:, None, :]], axis=1)
    new_s = {
        'rwkv_wkv': s_wkv,
        'rwkv_shift': _unpack_rwkv_cols(proj[:, :RW_W]),
        's5_re': s_re.reshape(bs, S5_G, S5_N), 's5_im': s_im.reshape(bs, S5_G, S5_N),
        'gdn': s_gdn,
        'gdn_conv': shift_rows(st['gdn_conv'], proj[:, GDN_OFF:GDN_OFF + 3 * GDN_D]),
        'lru_h': s_lru,
        'lru_conv': shift_rows(st['lru_conv'], proj[:, LRU_OFF:LRU_OFF + LRU_D]),
        'ffn_conv': shift_rows(st['ffn_conv'], jnp.concatenate([ug, uv], axis=-1)),
    }
    return xp, xs, new_p, new_s


STATE_ORDER = ('rwkv_wkv', 'rwkv_shift', 's5_re', 's5_im', 'gdn', 'gdn_conv', 'lru_h', 'lru_conv', 'ffn_conv')


def kernel(x_prompt, x_sample, c_prompt, c_sample, state_rwkv_wkv, state_rwkv_shift, state_s5_re, state_s5_im, state_gdn, state_gdn_conv, state_lru_h, state_lru_conv, state_ffn_conv, ada_w, ada_b, norm1_g, norm2_g, final_g, w_in, w_out, rwkv_mu, rwkv_w0, rwkv_w_up, rwkv_a0, rwkv_a_up, rwkv_g_up, rwkv_k_k, rwkv_k_a, rwkv_r_k, rwkv_ln_w, rwkv_ln_b, s5_lambda_re, s5_lambda_im, s5_log_dt, s5_b_re, s5_b_im, s5_c_re, s5_c_im, s5_d, s5_glu_w, s5_glu_b, gdn_conv_w, gdn_a_log, gdn_dt_bias, gdn_norm_w, lru_conv_w, lru_conv_b, lru_wr, lru_br, lru_wi, lru_bi, lru_lambda, ffn_w_up, ffn_conv_w, ffn_conv_b, ffn_w_down):
    weights = {
        'norm1_g': norm1_g, 'norm2_g': norm2_g, 'w_in': w_in,
        'rwkv_mu': rwkv_mu, 'rwkv_w0': rwkv_w0, 'rwkv_w_up': rwkv_w_up, 'rwkv_a0': rwkv_a0,
        'rwkv_a_up': rwkv_a_up, 'rwkv_g_up': rwkv_g_up, 'rwkv_k_k': rwkv_k_k, 'rwkv_k_a': rwkv_k_a,
        'rwkv_r_k': rwkv_r_k, 'rwkv_ln_w': rwkv_ln_w, 'rwkv_ln_b': rwkv_ln_b,
        's5_lambda_re': s5_lambda_re, 's5_lambda_im': s5_lambda_im, 's5_log_dt': s5_log_dt,
        's5_b_re': s5_b_re, 's5_b_im': s5_b_im, 's5_c_re': s5_c_re, 's5_c_im': s5_c_im, 's5_d': s5_d,
        's5_glu_w': s5_glu_w, 's5_glu_b': s5_glu_b,
        'gdn_conv_w': gdn_conv_w, 'gdn_a_log': gdn_a_log, 'gdn_dt_bias': gdn_dt_bias, 'gdn_norm_w': gdn_norm_w,
        'lru_conv_w': lru_conv_w, 'lru_conv_b': lru_conv_b, 'lru_wr': lru_wr, 'lru_br': lru_br,
        'lru_wi': lru_wi, 'lru_bi': lru_bi, 'lru_lambda': lru_lambda,
        'ffn_conv_w': ffn_conv_w, 'ffn_conv_b': ffn_conv_b,
    }
    stacked = {'w_out': w_out, 'ffn_w_up': ffn_w_up, 'ffn_w_down': ffn_w_down,
               'rwkv_wkv': jnp.transpose(state_rwkv_wkv, (0, 2, 3, 4, 1)),
               'gdn': state_gdn, 'ffn_conv': state_ffn_conv}
    cache = {'rwkv_wkv': state_rwkv_wkv, 'rwkv_shift': state_rwkv_shift, 's5_re': state_s5_re,
             's5_im': state_s5_im, 'gdn': state_gdn, 'gdn_conv': state_gdn_conv, 'lru_h': state_lru_h,
             'lru_conv': state_lru_conv, 'ffn_conv': state_ffn_conv}
    depth = ada_w.shape[0]
    bp = x_prompt.shape[0]
    bs = x_sample.shape[0]
    rows = -(-(bp + bs) // 8) * 8
    c_all = jnp.pad(jnp.concatenate([c_prompt, c_sample], axis=0), ((0, rows - bp - bs), (0, 0)))
    mod = ada_mod(c_all, ada_w, ada_b)
    ids = jnp.arange(RWKV_D) // RWKV_HD
    seg = (ids[:, None] == ids[None, :]).astype(BF16)

    xp = x_prompt
    xs = jnp.swapaxes(x_sample, 0, 1)
    new_p = {n: [] for n in STATE_ORDER}
    new_s = {n: [] for n in STATE_ORDER}
    for l in range(depth):
        p = {name: arr[l] for name, arr in weights.items()}
        st = {n: cache[n][l] for n in STATE_ORDER}
        xp, xs, sp, ss = _layer(xp, xs, mod[l], st, p, seg, l, stacked)
        for n in STATE_ORDER:
            new_p[n].append(sp[n])
            new_s[n].append(ss[n])
    y_prompt = final_norm(xp, final_g, 512)
    y_sample = jnp.swapaxes(final_norm(xs, final_g, bs), 0, 1)
    outs_p = tuple(jnp.stack(new_p[n], axis=0) for n in STATE_ORDER)
    stacked_s = {n: jnp.stack(new_s[n], axis=0) for n in STATE_ORDER}
    stacked_s['rwkv_wkv'] = jnp.transpose(stacked_s['rwkv_wkv'], (0, 4, 1, 2, 3))
    outs_s = tuple(stacked_s[n] for n in STATE_ORDER)
    return (y_prompt, y_sample) + outs_p + outs_s
```

```python
import functools
import math
from typing import NamedTuple

import jax
import jax.numpy as jnp
from jax import lax
from jax.experimental import pallas as pl
from jax.experimental.pallas import tpu as pltpu

F32 = jnp.float32
BF16 = jnp.bfloat16
HI = lax.Precision.HIGHEST

NORM_EPS = 1e-6
RWKV_LN_EPS = 64e-5
LRU_C = 8.0

D_MODEL = 2048
RWKV_D, RWKV_HD, RWKV_H = 512, 64, 8
RWKV_R_DECAY, RWKV_R_A, RWKV_R_GATE = 96, 96, 256
RWKV_PROJ = 3 * RWKV_D + RWKV_R_DECAY + RWKV_R_A + RWKV_R_GATE
S5_D, S5_CH, S5_G, S5_N = 512, 16, 32, 64
S5_STATE = S5_G * S5_N
GDN_D, GDN_HD, GDN_H = 512, 128, 4
GDN_PROJ = 4 * GDN_D + 2 * GDN_H
LRU_D, LRU_BLOCKS = 512, 8
D_FF = 5632

RW_W = 2048
S5_OFF = 2048
GDN_OFF, GDN_W = 2560, 2560
LRU_OFF, LRU_W = 5120, 1024
NP_IN = 6144

CHUNK = 64
RWKV_GH = 4
assert CHUNK == RWKV_HD
SCAN_T = 128
VMEM_LIMIT = 56 * 1024 * 1024


def _cp(sem):
    return pltpu.CompilerParams(dimension_semantics=sem, vmem_limit_bytes=VMEM_LIMIT)


def _dot(a, b, prec=None):
    return jnp.dot(a, b, preferred_element_type=F32, precision=prec)


def _dot_nt(a, b, prec=None):
    return lax.dot_general(a, b, (((1,), (1,)), ((), ())), preferred_element_type=F32, precision=prec)


def _dot_tn(a, b, prec=None):
    return lax.dot_general(a, b, (((0,), (0,)), ((), ())), preferred_element_type=F32, precision=prec)


def _bdot(a, b):
    return _dot(a.astype(BF16), b.astype(BF16))


def _silu(x):
    return x * jax.nn.sigmoid(x)


def _gelu(x):
    return 0.5 * x * (1.0 + jnp.tanh(math.sqrt(2.0 / math.pi) * (x + 0.044715 * (x * x * x))))


def _seg_dot(x, seg):
    hi = x.astype(BF16)
    lo = (x - hi.astype(F32)).astype(BF16)
    n = x.shape[0]
    r = _dot(jnp.concatenate([hi, lo], axis=0), seg)
    return r[:n] + r[n:]


def _iota(shape, dim):
    return lax.broadcasted_iota(jnp.int32, shape, dim)


class Layered(NamedTuple):
    arr: jax.Array
    layer: int


def _const_spec(x, nidx):
    if isinstance(x, Layered):
        shape = (None,) + x.arr.shape[1:]
        idx = (x.layer,) + (0,) * (x.arr.ndim - 1)
    else:
        shape = x.shape
        idx = (0,) * x.ndim
    if nidx == 1:
        return pl.BlockSpec(shape, lambda i: idx)
    return pl.BlockSpec(shape, lambda b, t: idx)


def _args(params):
    return [x.arr if isinstance(x, Layered) else x for x in params]


def _neumann_solve(n, rhs, steps):
    x = rhs
    p = n
    for i in range(steps):
        x = x + _bdot(p, x)
        if i + 1 < steps:
            p = _bdot(p, p)
    return x


def _split(x):
    hi = x.astype(BF16)
    return hi, (x - hi.astype(F32)).astype(BF16)


def _dot3(a, b):
    ah, al = _split(a)
    bh, bl = _split(b)
    return _dot(ah, bh) + (_dot(ah, bl) + _dot(al, bh))


def _neumann_solve3(n, rhs, steps):
    x = rhs
    p = n
    for i in range(steps):
        x = x + _dot3(p, x)
        if i + 1 < steps:
            p = _dot3(p, p)
    return x


def _row_to_col(row, eye):
    n = eye.shape[0]
    return jnp.sum(jnp.where(eye, jnp.broadcast_to(row, (n, n)), 0.0), axis=-1, keepdims=True)


def _ada_kernel(c_ref, w_ref, b_ref, o_ref):
    c = c_ref[...]
    o_ref[0] = _dot(_silu(c).astype(BF16), w_ref[0].astype(BF16)) + b_ref[0]


def ada_mod(c_all, ada_w, ada_b):
    depth, d, n = ada_w.shape
    r = c_all.shape[0]
    tn = 1024
    return pl.pallas_call(
        _ada_kernel,
        grid=(depth, n // tn),
        in_specs=[pl.BlockSpec((r, d), lambda l, j: (0, 0)),
                  pl.BlockSpec((1, d, tn), lambda l, j: (l, 0, j)),
                  pl.BlockSpec((1, 1, tn), lambda l, j: (l, 0, j))],
        out_specs=pl.BlockSpec((1, r, tn), lambda l, j: (l, 0, j)),
        out_shape=jax.ShapeDtypeStruct((depth, r, n), F32),
        compiler_params=_cp(("arbitrary", "arbitrary")),
        name="ada_mod",
    )(c_all, ada_w, ada_b.reshape(depth, 1, n))


def _norm_mod_kernel(x_ref, g_ref, sc_ref, sh_ref, o_ref):
    x = x_ref[0]
    h = x * lax.rsqrt(jnp.mean(x * x, axis=-1, keepdims=True) + NORM_EPS) * g_ref[...]
    o_ref[0] = (h * (1.0 + sc_ref[0]) + sh_ref[0]).astype(o_ref.dtype)


def _mod_spec(arr, tm):
    d = arr.shape[-1]
    if arr.shape[1] == 1:
        return pl.BlockSpec((1, 1, d), lambda g, i: (g, 0, 0))
    return pl.BlockSpec((1, tm, d), lambda g, i: (g, i, 0))


def norm_mod(x, gain, sc, sh, tm):
    g, r, d = x.shape
    return pl.pallas_call(
        _norm_mod_kernel,
        grid=(g, r // tm),
        in_specs=[pl.BlockSpec((1, tm, d), lambda g, i: (g, i, 0)),
                  _const_spec(gain, 2),
                  _mod_spec(sc, tm), _mod_spec(sh, tm)],
        out_specs=pl.BlockSpec((1, tm, d), lambda g, i: (g, i, 0)),
        out_shape=jax.ShapeDtypeStruct((g, r, d), BF16),
        compiler_params=_cp(("arbitrary", "arbitrary")),
        name="norm_mod",
    )(x, *_args([gain]), sc, sh)


def _rms_kernel(x_ref, g_ref, o_ref):
    x = x_ref[0]
    o_ref[0] = x * lax.rsqrt(jnp.mean(x * x, axis=-1, keepdims=True) + NORM_EPS) * g_ref[...]


def final_norm(x, gain, tm):
    g, r, d = x.shape
    return pl.pallas_call(
        _rms_kernel,
        grid=(g, r // tm),
        in_specs=[pl.BlockSpec((1, tm, d), lambda g, i: (g, i, 0)),
                  pl.BlockSpec((1, d), lambda g, i: (0, 0))],
        out_specs=pl.BlockSpec((1, tm, d), lambda g, i: (g, i, 0)),
        out_shape=jax.ShapeDtypeStruct((g, r, d), F32),
        compiler_params=_cp(("arbitrary", "arbitrary")),
        name="final_norm",
    )(x, gain.reshape(1, d))


def _mm_kernel(a_ref, w_ref, o_ref):
    o_ref[0] = _dot(a_ref[0], w_ref[...])


def matmul_in(a, w, layer, tm, tn):
    g, r, k = a.shape
    n = w.shape[2]
    return pl.pallas_call(
        _mm_kernel,
        grid=(g, r // tm, n // tn),
        in_specs=[pl.BlockSpec((1, tm, k), lambda g, i, j: (g, i, 0)),
                  pl.BlockSpec((None, k, tn), lambda g, i, j: (layer, 0, j))],
        out_specs=pl.BlockSpec((1, tm, tn), lambda g, i, j: (g, i, j)),
        out_shape=jax.ShapeDtypeStruct((g, r, n), F32),
        compiler_params=_cp(("arbitrary", "arbitrary", "arbitrary")),
        name="matmul_in",
    )(a, w)


def _mm_res_kernel(n_a, *refs):
    a_refs = refs[:n_a]
    w_refs = refs[n_a:2 * n_a]
    x_ref, g_ref, o_ref = refs[2 * n_a:]
    acc = _dot(a_refs[0][0], w_refs[0][...].astype(BF16))
    for a_ref, w_ref in zip(a_refs[1:], w_refs[1:]):
        acc = acc + _dot(a_ref[0], w_ref[...].astype(BF16))
    o_ref[0] = x_ref[0] + g_ref[0] * acc


def matmul_res(a_list, w, layer, x, gate, tm, tn):
    g, r, n = x.shape
    n_a = len(a_list)
    k = a_list[0].shape[-1]
    a_specs = [pl.BlockSpec((1, tm, k), lambda g, i, j: (g, i, 0)) for _ in a_list]
    w_specs = [pl.BlockSpec((None, k, tn), functools.partial(lambda g, i, j, q: (layer, q, j), q=q))
               for q in range(n_a)]
    if gate.shape[1] == 1:
        g_spec = pl.BlockSpec((1, 1, tn), lambda g, i, j: (g, 0, j))
    else:
        g_spec = pl.BlockSpec((1, tm, tn), lambda g, i, j: (g, i, j))
    return pl.pallas_call(
        functools.partial(_mm_res_kernel, n_a),
        grid=(g, r // tm, n // tn),
        in_specs=a_specs + w_specs + [pl.BlockSpec((1, tm, tn), lambda g, i, j: (g, i, j)), g_spec],
        out_specs=pl.BlockSpec((1, tm, tn), lambda g, i, j: (g, i, j)),
        out_shape=jax.ShapeDtypeStruct((g, r, n), F32),
        compiler_params=_cp(("arbitrary", "arbitrary", "arbitrary")),
        name="matmul_res",
    )(*a_list, *([w] * n_a), x, gate)


def _ffn_up_seq_kernel(a_ref, wg_ref, wv_ref, cwg_ref, cwv_ref, cbg_ref, cbv_ref, act_ref, lg_ref, lv_ref):
    a = a_ref[0]
    rows = a.shape[0]
    rid = _iota((rows, 1), 0)

    def conv(up, cw_ref, cb_ref):
        cw = cw_ref[...]
        s1 = jnp.where(rid >= 1, pltpu.roll(up, 1, axis=0), 0.0)
        s2 = jnp.where(rid >= 2, pltpu.roll(up, 2, axis=0), 0.0)
        return cw[0:1] * s2 + cw[1:2] * s1 + cw[2:3] * up + cb_ref[...]

    up_g = _dot(a, wg_ref[...].astype(BF16))
    lg_ref[0] = up_g[rows - 2:rows]
    gate = conv(up_g, cwg_ref, cbg_ref)
    up_v = _dot(a, wv_ref[...].astype(BF16))
    lv_ref[0] = up_v[rows - 2:rows]
    val = conv(up_v, cwv_ref, cbv_ref)
    act_ref[0] = (_silu(gate) * val).astype(BF16)


def ffn_up_seq(h, w_up, layer, conv_w, conv_b, tn):
    b, l, d = h.shape
    nt = D_FF // tn
    return pl.pallas_call(
        _ffn_up_seq_kernel,
        grid=(b, nt),
        in_specs=[pl.BlockSpec((1, l, d), lambda b, j: (b, 0, 0)),
                  pl.BlockSpec((None, d, tn), lambda b, j: (layer, 0, j)),
                  pl.BlockSpec((None, d, tn), lambda b, j: (layer, 0, j + nt)),
                  pl.BlockSpec((None, 3, tn), lambda b, j: (layer, 0, j)),
                  pl.BlockSpec((None, 3, tn), lambda b, j: (layer, 0, j + nt)),
                  pl.BlockSpec((None, 1, tn), lambda b, j: (layer, 0, j)),
                  pl.BlockSpec((None, 1, tn), lambda b, j: (layer, 0, j + nt))],
        out_specs=[pl.BlockSpec((1, l, tn), lambda b, j: (b, 0, j)),
                   pl.BlockSpec((1, 2, tn), lambda b, j: (b, 0, j)),
                   pl.BlockSpec((1, 2, tn), lambda b, j: (b, 0, j))],
        out_shape=[jax.ShapeDtypeStruct((b, l, D_FF), BF16),
                   jax.ShapeDtypeStruct((b, 2, D_FF), F32),
                   jax.ShapeDtypeStruct((b, 2, D_FF), F32)],
        compiler_params=_cp(("arbitrary", "arbitrary")),
        name="ffn_up_seq",
    )(h, w_up, w_up, conv_w, conv_w, conv_b, conv_b)


def _ffn_up_step_kernel(a_ref, wg_ref, wv_ref, cwg_ref, cwv_ref, cbg_ref, cbv_ref,
                        stg_ref, stv_ref, act_ref, ug_ref, uv_ref):
    a = a_ref[...]
    up_g = _dot(a, wg_ref[...].astype(BF16))
    up_v = _dot(a, wv_ref[...].astype(BF16))
    ug_ref[...] = up_g
    uv_ref[...] = up_v
    cwg = cwg_ref[...]
    cwv = cwv_ref[...]
    gate = cwg[0:1] * stg_ref[:, 0, :] + cwg[1:2] * stg_ref[:, 1, :] + cwg[2:3] * up_g + cbg_ref[...]
    val = cwv[0:1] * stv_ref[:, 0, :] + cwv[1:2] * stv_ref[:, 1, :] + cwv[2:3] * up_v + cbv_ref[...]
    act_ref[...] = (_silu(gate) * val).astype(BF16)


def ffn_up_step(h, w_up, layer, conv_w, conv_b, conv_state, tn):
    b, d = h.shape
    nt = D_FF // tn
    return pl.pallas_call(
        _ffn_up_step_kernel,
        grid=(nt,),
        in_specs=[pl.BlockSpec((b, d), lambda j: (0, 0)),
                  pl.BlockSpec((None, d, tn), lambda j: (layer, 0, j)),
                  pl.BlockSpec((None, d, tn), lambda j: (layer, 0, j + nt)),
                  pl.BlockSpec((None, 3, tn), lambda j: (layer, 0, j)),
                  pl.BlockSpec((None, 3, tn), lambda j: (layer, 0, j + nt)),
                  pl.BlockSpec((None, 1, tn), lambda j: (layer, 0, j)),
                  pl.BlockSpec((None, 1, tn), lambda j: (layer, 0, j + nt)),
                  pl.BlockSpec((None, b, 2, tn), lambda j: (layer, 0, 0, j)),
                  pl.BlockSpec((None, b, 2, tn), lambda j: (layer, 0, 0, j + nt))],
        out_specs=[pl.BlockSpec((b, tn), lambda j: (0, j))] * 3,
        out_shape=[jax.ShapeDtypeStruct((b, D_FF), BF16),
                   jax.ShapeDtypeStruct((b, D_FF), F32),
                   jax.ShapeDtypeStruct((b, D_FF), F32)],
        compiler_params=_cp(("arbitrary",)),
        name="ffn_up_step",
    )(h, w_up, w_up, conv_w, conv_w, conv_b, conv_b, conv_state, conv_state)


def _rwkv_pre(p, prev, prm, seg):
    mu, w0, w_up, a0, a_up, g_up, k_k, k_a = prm
    xm = p + mu * (prev - p)
    r = xm[:, 0:512]
    k = xm[:, 512:1024]
    v = xm[:, 1024:1536]
    xw = xm[:, 1536:1664]
    xa = xm[:, 1664:1792]
    xg = xm[:, 1792:2048]
    log_w = -jnp.exp(-jax.nn.softplus(-(w0 + _bdot(jnp.tanh(xw), w_up))) - 0.5)
    a = jax.nn.sigmoid(a0 + _bdot(xa, a_up))
    g = _bdot(jax.nn.sigmoid(xg), g_up)
    kx = k * k_k
    kk = kx * lax.rsqrt(_seg_dot(kx * kx, seg) + 1e-12)
    k2 = k * (1.0 + (a - 1.0) * k_a)
    return r, k2, v, log_w, a, g, kk


def _rwkv_post(y, r, k2, v, g, r_k, ln_w, ln_b, seg):
    inv = 1.0 / RWKV_HD
    mean = _seg_dot(y, seg) * inv
    dlt = y - mean
    var = _seg_dot(dlt * dlt, seg) * inv
    yn = dlt * lax.rsqrt(var + RWKV_LN_EPS) * ln_w + ln_b
    bonus = _seg_dot(r * k2 * r_k, seg) * v
    return (yn + bonus) * g


def _rwkv_seq_kernel(p_ref, mu_ref, w0_ref, wup_ref, a0_ref, aup_ref, gup_ref, kk_ref, ka_ref, rk_ref,
                     lnw_ref, lnb_ref, seg_ref, y_ref, s_out_ref, s_ref, prev_ref):
    c = CHUNK
    ns = p_ref.shape[0]
    n = ns * c
    t = pl.program_id(1)

    @pl.when(t == 0)
    def _():
        s_ref[...] = jnp.zeros_like(s_ref)
        prev_ref[...] = jnp.zeros_like(prev_ref)

    p = p_ref[...].reshape(n, RW_W)
    seg = seg_ref[...]
    seq_rows = lambda x: jnp.concatenate([jnp.broadcast_to(x[i:i + 1], (c, x.shape[-1])) for i in range(ns)], axis=0)
    tloc = _iota((n, 1), 0) & (c - 1)
    prev = jnp.where(tloc == 0, seq_rows(prev_ref[...]), pltpu.roll(p, 1, axis=0))
    last_rows = lambda x: jnp.concatenate([x[(i + 1) * c - 1:(i + 1) * c] for i in range(ns)], axis=0)
    prev_ref[...] = last_rows(p)
    prm = (mu_ref[...], w0_ref[...], wup_ref[...], a0_ref[...], aup_ref[...], gup_ref[...], kk_ref[...], ka_ref[...])
    r, k2, v, log_w, a, g, kk = _rwkv_pre(p, prev, prm, seg)

    ri = _iota((n, n), 0)
    ci = _iota((n, n), 1)
    cshift = c.bit_length() - 1
    tril = ((lax.shift_right_logical(ri, cshift) == lax.shift_right_logical(ci, cshift)) & (ci <= ri)).astype(F32)
    cum = _dot(tril, log_w, HI)
    e_in = jnp.exp(cum)
    e_out = jnp.exp(-cum)
    e_ex = jnp.exp(cum - log_w)
    tot = last_rows(cum)
    e_tail = jnp.exp(seq_rows(tot) - cum)
    e_tot = jnp.exp(tot)
    alpha = -(a * kk)
    gw = RWKV_GH * RWKV_HD
    rr = _iota((gw, gw), 0)
    cc = _iota((gw, gw), 1)
    shift = RWKV_HD.bit_length() - 1
    own = lax.shift_right_logical(rr, shift) == lax.shift_right_logical(cc, shift)
    tri_strict = (cc & (c - 1)) < (rr & (c - 1))
    tri_incl = (cc & (c - 1)) <= (rr & (c - 1))
    eye = rr == cc

    ng = RWKV_H // RWKV_GH
    be_t = (kk * e_ex).astype(BF16)
    r_t = (r * e_in).astype(BF16)
    al_t = (alpha * e_out).astype(BF16)
    k_t = (k2 * e_out).astype(BF16)
    al_p = (alpha * e_tail).astype(BF16)
    k_p = (k2 * e_tail).astype(BF16)
    vb = v.astype(BF16)

    def expand(x, i, gi):
        xg = x[i * c:(i + 1) * c, gi * gw:(gi + 1) * gw]
        return jnp.where(own, jnp.concatenate([xg] * RWKV_GH, axis=0), jnp.zeros((), BF16))

    y_rows = []
    for i in range(ns):
        ys = []
        for gi in range(ng):
            s0 = s_ref[i * ng + gi]
            lhs = jnp.concatenate([expand(be_t, i, gi), expand(r_t, i, gi)], axis=0)
            rhs = jnp.concatenate([expand(al_t, i, gi), expand(k_t, i, gi)], axis=0)
            vv = expand(vb, i, gi)
            m1 = _dot_nt(lhs, rhs)
            m2 = _dot(lhs, s0.astype(BF16))
            l_a = jnp.where(tri_strict, m1[:gw, :gw], 0.0)
            l_k = jnp.where(tri_strict, m1[:gw, gw:], 0.0)
            u = _neumann_solve(l_a, m2[:gw] + _dot(l_k.astype(BF16), vv), cshift)
            uv = jnp.concatenate([u.astype(BF16), vv], axis=0)
            a_full = jnp.concatenate([jnp.where(tri_incl, m1[gw:, :gw], 0.0),
                                      jnp.where(tri_incl, m1[gw:, gw:], 0.0)], axis=1).astype(BF16)
            y_g = m2[gw:] + _dot(a_full, uv)
            ys.append(functools.reduce(lambda p_, q_: p_ + q_, [y_g[j * c:(j + 1) * c] for j in range(RWKV_GH)]))
            scale = _row_to_col(e_tot[i:i + 1, gi * gw:(gi + 1) * gw], eye)
            tails = jnp.concatenate([expand(al_p, i, gi), expand(k_p, i, gi)], axis=0)
            s_ref[i * ng + gi] = s0 * scale + _dot_tn(tails, uv)
        y_rows.append(jnp.concatenate(ys, axis=1))

    y = jnp.concatenate(y_rows, axis=0)
    out = _rwkv_post(y, r, k2, v, g, rk_ref[...], lnw_ref[...], lnb_ref[...], seg)
    y_ref[...] = out.astype(BF16).reshape(ns, c, RWKV_D)

    @pl.when(t == pl.num_programs(1) - 1)
    def _():
        for i in range(ns):
            for h in range(RWKV_H):
                gi, hh = divmod(h, RWKV_GH)
                s_out_ref[i, h] = s_ref[i * ng + gi, hh * RWKV_HD:(hh + 1) * RWKV_HD, hh * RWKV_HD:(hh + 1) * RWKV_HD]


def rwkv_seq(proj, prm, seg, ns):
    b, l, _ = proj.shape
    c = CHUNK
    gw = RWKV_GH * RWKV_HD
    return pl.pallas_call(
        _rwkv_seq_kernel,
        grid=(b // ns, l // c),
        in_specs=[pl.BlockSpec((ns, c, RW_W), lambda b, t: (b, t, 0))] + [_const_spec(x, 2) for x in prm] + [_const_spec(seg, 2)],
        out_specs=[pl.BlockSpec((ns, c, RWKV_D), lambda b, t: (b, t, 0)),
                   pl.BlockSpec((ns, RWKV_H, RWKV_HD, RWKV_HD), lambda b, t: (b, 0, 0, 0))],
        out_shape=[jax.ShapeDtypeStruct((b, l, RWKV_D), BF16),
                   jax.ShapeDtypeStruct((b, RWKV_H, RWKV_HD, RWKV_HD), F32)],
        scratch_shapes=[pltpu.VMEM((ns * (RWKV_H // RWKV_GH), gw, gw), F32),
                        pltpu.VMEM((ns, RW_W), F32)],
        compiler_params=_cp(("arbitrary", "arbitrary")),
        name="rwkv_seq",
    )(proj, *_args(prm), seg)


def _rwkv_step_kernel(p_ref, prev_ref, s_in_ref, mu_ref, w0_ref, wup_ref, a0_ref, aup_ref, gup_ref, kk_ref,
                      ka_ref, rk_ref, lnw_ref, lnb_ref, seg_ref, y_ref, s_out_ref, nat_scr, col_scr, yt_scr):
    h = pl.program_id(0)

    @pl.when(h == 0)
    def _():
        prm = (mu_ref[...], w0_ref[...], wup_ref[...], a0_ref[...], aup_ref[...], gup_ref[...], kk_ref[...],
               ka_ref[...])
        r, k2, v, log_w, a, g, kk = _rwkv_pre(p_ref[...], prev_ref[...], prm, seg_ref[...])
        for i, x in enumerate((r, k2, v, g)):
            nat_scr[i] = x
        for i, x in enumerate((jnp.exp(log_w), kk * a, kk, k2, r, v)):
            col_scr[i] = x.T

    rows = pl.ds(pl.multiple_of(h * RWKV_HD, RWKV_HD), RWKV_HD)
    w_t, kka_t, kk_t, k_t, r_t, v_t = (col_scr[i, rows, :] for i in range(6))
    nacc = 4
    acc = [kk_t[j:j + 1] * s_in_ref[j] for j in range(nacc)]
    for k in range(nacc, RWKV_HD):
        acc[k % nacc] = acc[k % nacc] + kk_t[k:k + 1] * s_in_ref[k]
    sa = (acc[0] + acc[1]) + (acc[2] + acc[3])
    acc = [None] * nacc
    for k in range(RWKV_HD):
        s_new = s_in_ref[k] * w_t[k:k + 1] - kka_t[k:k + 1] * sa + k_t[k:k + 1] * v_t
        s_out_ref[k] = s_new
        term = r_t[k:k + 1] * s_new
        acc[k % nacc] = term if acc[k % nacc] is None else acc[k % nacc] + term
    yt_scr[rows, :] = (acc[0] + acc[1]) + (acc[2] + acc[3])

    @pl.when(h == RWKV_H - 1)
    def _():
        out = _rwkv_post(yt_scr[...].T, nat_scr[0], nat_scr[1], nat_scr[2], nat_scr[3], rk_ref[...], lnw_ref[...],
                         lnb_ref[...], seg_ref[...])
        y_ref[...] = out.astype(BF16)


def rwkv_step(proj, shift, state_t, layer, prm, seg):
    b = proj.shape[0]
    blk = (None, RWKV_HD, RWKV_HD, b)
    return pl.pallas_call(
        _rwkv_step_kernel,
        grid=(RWKV_H,),
        in_specs=[pl.BlockSpec((b, RW_W), lambda h: (0, 0)),
                  pl.BlockSpec((b, RW_W), lambda h: (0, 0)),
                  pl.BlockSpec((None,) + blk, lambda h: (layer, h, 0, 0, 0))]
                 + [_const_spec(x, 1) for x in prm] + [_const_spec(seg, 1)],
        out_specs=[pl.BlockSpec((b, RWKV_D), lambda h: (0, 0)),
                   pl.BlockSpec(blk, lambda h: (h, 0, 0, 0))],
        out_shape=[jax.ShapeDtypeStruct((b, RWKV_D), BF16),
                   jax.ShapeDtypeStruct(state_t.shape[1:], F32)],
        scratch_shapes=[pltpu.VMEM((4, b, RWKV_D), F32), pltpu.VMEM((6, RWKV_D, b), F32),
                        pltpu.VMEM((RWKV_D, b), F32)],
        compiler_params=_cp(("arbitrary",)),
        name="rwkv_step",
    )(proj, shift, state_t, *_args(prm), seg)


def _s5_glu(y, gw_ref, gb_ref):
    z = _gelu(y)
    return z * jax.nn.sigmoid(_bdot(z, gw_ref[...]) + gb_ref[...])


def _s5_out(u, h_re, h_im, cc_ref, d_ref, gw_ref, gb_ref):
    y = _dot(jnp.concatenate([h_re, h_im], axis=1).astype(BF16), cc_ref[...]) + d_ref[...] * u
    return _s5_glu(y, gw_ref, gb_ref)


def _interleave_perm(ns, tc):
    n = ns * tc
    r = _iota((n, n), 0)
    c = _iota((n, n), 1)
    t = lax.shift_right_logical(r, ns.bit_length() - 1)
    b = r & (ns - 1)
    return c == b * tc + t


def _s5_seq_kernel(u_ref, bbr_ref, bbi_ref, cc_ref, abr_ref, abi_ref, d_ref, gw_ref, gb_ref,
                   y_ref, hr_ref, hi_ref, cr_ref, ci_ref):
    ns, tc = u_ref.shape[0], u_ref.shape[1]
    n = ns * tc
    t = pl.program_id(0)

    @pl.when(t == 0)
    def _():
        cr_ref[...] = jnp.zeros_like(cr_ref)
        ci_ref[...] = jnp.zeros_like(ci_ref)

    p_fwd = _interleave_perm(ns, tc).astype(BF16)
    u = u_ref[...].reshape(n, S5_D)
    u_hi, u_lo = _split(u)
    ub = _dot(p_fwd, u_hi)
    u_t = ub + _dot(p_fwd, u_lo)
    ub = ub.astype(BF16)
    sw = (128 // S5_CH) * S5_N
    ncg = S5_D // 128
    bu_re = jnp.concatenate([_dot(ub[:, g * 128:(g + 1) * 128], bbr_ref[g * 128:(g + 1) * 128, g * sw:(g + 1) * sw])
                             for g in range(ncg)], axis=1)
    bu_im = jnp.concatenate([_dot(ub[:, g * 128:(g + 1) * 128], bbi_ref[g * 128:(g + 1) * 128, g * sw:(g + 1) * sw])
                             for g in range(ncg)], axis=1)
    ar = jnp.broadcast_to(abr_ref[...], (2 * ns, S5_STATE))
    ai = jnp.broadcast_to(abi_ref[...], (2 * ns, S5_STATE))
    first = _iota((2 * ns, 1), 0) < ns
    p_re = cr_ref[...]
    p_im = ci_ref[...]
    h_re, h_im = [], []
    for s in range(n // (2 * ns)):
        b_re = bu_re[s * 2 * ns:(s + 1) * 2 * ns]
        b_im = bu_im[s * 2 * ns:(s + 1) * 2 * ns]
        r_re = pltpu.roll(p_re, ns, axis=0)
        r_im = pltpu.roll(p_im, ns, axis=0)
        t_re = ar * r_re - ai * r_im + b_re
        t_im = ar * r_im + ai * r_re + b_im
        q_re = pltpu.roll(t_re, ns, axis=0)
        q_im = pltpu.roll(t_im, ns, axis=0)
        p_re = jnp.where(first, t_re, ar * q_re - ai * q_im + b_re)
        p_im = jnp.where(first, t_im, ar * q_im + ai * q_re + b_im)
        h_re.append(p_re)
        h_im.append(p_im)
    cr_ref[...] = p_re
    ci_ref[...] = p_im
    hb_re = jnp.concatenate(h_re, axis=0).astype(BF16)
    hb_im = jnp.concatenate(h_im, axis=0).astype(BF16)
    ch = jnp.concatenate([_dot(hb_re[:, g * sw:(g + 1) * sw], cc_ref[g * sw:(g + 1) * sw, g * 128:(g + 1) * 128])
                          + _dot(hb_im[:, g * sw:(g + 1) * sw],
                                 cc_ref[S5_STATE + g * sw:S5_STATE + (g + 1) * sw, g * 128:(g + 1) * 128])
                          for g in range(ncg)], axis=1)
    y_t = _s5_glu(ch + d_ref[...] * u_t, gw_ref, gb_ref)
    y = _dot_tn(p_fwd, y_t.astype(BF16))
    y_ref[...] = y.astype(BF16).reshape(ns, tc, S5_D)

    @pl.when(t == pl.num_programs(0) - 1)
    def _():
        hr_ref[:, 0, :] = p_re[ns:2 * ns]
        hi_ref[:, 0, :] = p_im[ns:2 * ns]


def s5_seq(proj, prm):
    b, l, _ = proj.shape
    assert 2 * b == 8, "the time-major scan packs two time steps of all sequences into one 8-row sublane tile"
    tc = CHUNK
    return pl.pallas_call(
        _s5_seq_kernel,
        grid=(l // tc,),
        in_specs=[pl.BlockSpec((b, tc, S5_D), lambda t: (0, t, S5_OFF // S5_D))] + [_const_spec(x, 1) for x in prm],
        out_specs=[pl.BlockSpec((b, tc, S5_D), lambda t: (0, t, 0)),
                   pl.BlockSpec((b, 1, S5_STATE), lambda t: (0, 0, 0)),
                   pl.BlockSpec((b, 1, S5_STATE), lambda t: (0, 0, 0))],
        out_shape=[jax.ShapeDtypeStruct((b, l, S5_D), BF16),
                   jax.ShapeDtypeStruct((b, 1, S5_STATE), F32),
                   jax.ShapeDtypeStruct((b, 1, S5_STATE), F32)],
        scratch_shapes=[pltpu.VMEM((2 * b, S5_STATE), F32), pltpu.VMEM((2 * b, S5_STATE), F32)],
        compiler_params=_cp(("arbitrary",)),
        name="s5_seq",
    )(proj, *_args(prm))


def _s5_step_kernel(u_ref, h0r_ref, h0i_ref, bbr_ref, bbi_ref, cc_ref, abr_ref, abi_ref, d_ref, gw_ref, gb_ref,
                    y_ref, hr_ref, hi_ref):
    u = u_ref[...]
    ar = abr_ref[...]
    ai = abi_ref[...]
    h0r = h0r_ref[...]
    h0i = h0i_ref[...]
    h_re = ar * h0r - ai * h0i + _dot(u, bbr_ref[...], HI)
    h_im = ar * h0i + ai * h0r + _dot(u, bbi_ref[...], HI)
    hr_ref[...] = h_re
    hi_ref[...] = h_im
    y_ref[...] = _s5_out(u, h_re, h_im, cc_ref, d_ref, gw_ref, gb_ref).astype(BF16)


def s5_step(proj, h0_re, h0_im, prm):
    b = proj.shape[0]
    return pl.pallas_call(
        _s5_step_kernel,
        grid=(1,),
        in_specs=[pl.BlockSpec((b, S5_D), lambda i: (0, S5_OFF // S5_D)),
                  _const_spec(h0_re, 1), _const_spec(h0_im, 1)] + [_const_spec(x, 1) for x in prm],
        out_specs=[pl.BlockSpec((b, S5_D), lambda i: (0, 0)),
                   pl.BlockSpec((b, S5_STATE), lambda i: (0, 0)),
                   pl.BlockSpec((b, S5_STATE), lambda i: (0, 0))],
        out_shape=[jax.ShapeDtypeStruct((b, S5_D), BF16),
                   jax.ShapeDtypeStruct((b, S5_STATE), F32),
                   jax.ShapeDtypeStruct((b, S5_STATE), F32)],
        compiler_params=_cp(("arbitrary",)),
        name="s5_step",
    )(proj, h0_re, h0_im, *_args(prm))


def _gdn_pre(qkv_c, z, ab, alog_ref, dtb_ref):
    act = _silu(qkv_c)
    qs, ks = [], []
    for h in range(GDN_H):
        qh = act[:, h * GDN_HD:(h + 1) * GDN_HD]
        kh = act[:, GDN_D + h * GDN_HD:GDN_D + (h + 1) * GDN_HD]
        qs.append(qh * lax.rsqrt(jnp.sum(qh * qh, axis=-1, keepdims=True) + 1e-6))
        ks.append(kh * lax.rsqrt(jnp.sum(kh * kh, axis=-1, keepdims=True) + 1e-6))
    v = act[:, 2 * GDN_D:3 * GDN_D]
    g = -jnp.exp(alog_ref[...]) * jax.nn.softplus(ab + dtb_ref[...])
    beta = jax.nn.sigmoid(ab)
    return qs, ks, v, g, beta


def _gdn_post(o, z_h, nw):
    on = o * lax.rsqrt(jnp.mean(o * o, axis=-1, keepdims=True) + NORM_EPS) * nw
    return on * _silu(z_h)


def _gdn_seq_kernel(p_ref, cw_ref, alog_ref, dtb_ref, nw_ref, y_ref, s_out_ref, s_ref, prev_ref):
    c = CHUNK
    ns = p_ref.shape[0]
    n = ns * c
    t = pl.program_id(1)

    @pl.when(t == 0)
    def _():
        s_ref[...] = jnp.zeros_like(s_ref)
        prev_ref[...] = jnp.zeros_like(prev_ref)

    cw = cw_ref[...]
    convs = []
    for i in range(ns):
        qkv = p_ref[i, :, 0:3 * GDN_D]
        ext = jnp.concatenate([prev_ref[i], qkv], axis=0)
        prev_ref[i] = qkv[c - 8:c]
        conv = cw[3:4] * qkv
        for j in (1, 2, 3):
            conv = conv + cw[3 - j:4 - j] * pltpu.roll(ext, j, axis=0)[8:]
        convs.append(conv)
    x = p_ref[...].reshape(n, GDN_W)
    z = x[:, 3 * GDN_D:4 * GDN_D]
    ab = x[:, 4 * GDN_D:4 * GDN_D + 128]
    qs, ks, v, g, beta = _gdn_pre(jnp.concatenate(convs, axis=0), z, ab, alog_ref, dtb_ref)

    shift = c.bit_length() - 1
    ri = _iota((n, n), 0)
    ci = _iota((n, n), 1)
    tril = ((lax.shift_right_logical(ri, shift) == lax.shift_right_logical(ci, shift)) & (ci <= ri)).astype(F32)
    gc_all = _dot(tril, g, HI)

    hc = GDN_H * c
    heads = range(GDN_H)
    stack = lambda pieces: jnp.concatenate(pieces, axis=0)
    lane = _iota((c, 128), 1)
    rr = _iota((hc, hc), 0)
    cc = _iota((hc, hc), 1)
    same = lax.shift_right_logical(rr, shift) == lax.shift_right_logical(cc, shift)
    causal = same & (cc <= rr)
    strict = same & (cc < rr)
    ones = jnp.ones((hc, 128), F32)
    nw = nw_ref[...]
    for i in range(ns):
        sq = slice(i * c, (i + 1) * c)
        gc = gc_all[sq]
        g_col = stack([gc[:, h:h + 1] for h in heads])
        b_col = stack([beta[sq, GDN_H + h:GDN_H + h + 1] for h in heads])
        g_row = _dot_nt(ones, stack([jnp.where(lane == h, gc, 0.0) for h in heads]), HI)
        decay = jnp.where(causal, jnp.exp(jnp.where(causal, g_col - g_row, 0.0)), 0.0)
        q = stack([qs[h][sq] for h in heads]) * (GDN_HD ** -0.5)
        k = stack([ks[h][sq] for h in heads])
        v_s = stack([v[sq, h * GDN_HD:(h + 1) * GDN_HD] for h in heads])
        kb = k * b_col
        m1 = _dot_nt(stack([kb, q]).astype(BF16), k.astype(BF16))
        lmat = jnp.where(strict, m1[:hc] * decay, 0.0)
        attn = m1[hc:] * decay
        eg = jnp.exp(g_col)
        sol = _neumann_solve3(-lmat, jnp.concatenate([v_s * b_col, kb * eg], axis=1), shift)
        u = sol[:, :GDN_HD]
        wk = sol[:, GDN_HD:]
        qd = q * eg
        ws = [_dot(stack([wk[h * c:(h + 1) * c], qd[h * c:(h + 1) * c]]).astype(BF16),
                   s_ref[i * GDN_H + h].astype(BF16)) for h in heads]
        v_new = u - stack([w[:c] for w in ws])
        o = stack([w[c:] for w in ws]) + _bdot(attn, v_new)
        g_last = [gc[c - 1:c, h:h + 1] for h in heads]
        k_tail = k * jnp.exp(stack([jnp.broadcast_to(gl, (c, 1)) for gl in g_last]) - g_col)
        for h in heads:
            rows = slice(h * c, (h + 1) * c)
            s_ref[i * GDN_H + h] = (s_ref[i * GDN_H + h] * jnp.exp(g_last[h])
                                    + _dot_tn(k_tail[rows].astype(BF16), v_new[rows].astype(BF16)))
            z_h = z[sq, h * GDN_HD:(h + 1) * GDN_HD]
            y_ref[i, :, h * GDN_HD:(h + 1) * GDN_HD] = _gdn_post(o[rows], z_h, nw).astype(BF16)

    @pl.when(t == pl.num_programs(1) - 1)
    def _():
        s_out_ref[...] = s_ref[...].reshape(s_out_ref.shape)


def gdn_seq(proj, prm, ns):
    b, l, _ = proj.shape
    c = CHUNK
    return pl.pallas_call(
        _gdn_seq_kernel,
        grid=(b // ns, l // c),
        in_specs=[pl.BlockSpec((ns, c, GDN_W), lambda b, t: (b, t, GDN_OFF // GDN_W))] + [_const_spec(x, 2) for x in prm],
        out_specs=[pl.BlockSpec((ns, c, GDN_D), lambda b, t: (b, t, 0)),
                   pl.BlockSpec((ns, GDN_H, GDN_HD, GDN_HD), lambda b, t: (b, 0, 0, 0))],
        out_shape=[jax.ShapeDtypeStruct((b, l, GDN_D), BF16),
                   jax.ShapeDtypeStruct((b, GDN_H, GDN_HD, GDN_HD), F32)],
        scratch_shapes=[pltpu.VMEM((ns * GDN_H, GDN_HD, GDN_HD), F32),
                        pltpu.VMEM((ns, 8, 3 * GDN_D), F32)],
        compiler_params=_cp(("arbitrary", "arbitrary")),
        name="gdn_seq",
    )(proj, *_args(prm))


def _gdn_step_kernel(nb, p_ref, cs_ref, s_in_ref, cw_ref, alog_ref, dtb_ref, nw_ref, y_ref, s_out_ref):
    x = p_ref[...]
    qkv = x[:, 0:3 * GDN_D]
    z = x[:, 3 * GDN_D:4 * GDN_D]
    ab = x[:, 4 * GDN_D:4 * GDN_D + 128]
    cs = cs_ref[...]
    cw = cw_ref[...]
    w3 = 3 * GDN_D
    conv = cw[0:1] * cs[:, 0:w3] + cw[1:2] * cs[:, w3:2 * w3] + cw[2:3] * cs[:, 2 * w3:3 * w3] + cw[3:4] * qkv
    qs, ks, v, g, beta = _gdn_pre(conv, z, ab, alog_ref, dtb_ref)
    nw = nw_ref[...]
    pad = jnp.zeros((GDN_HD - nb, GDN_HD), F32)
    ks_t = [jnp.concatenate([ks[h], pad], axis=0).T for h in range(GDN_H)]
    qs_t = [jnp.concatenate([qs[h], pad], axis=0).T for h in range(GDN_H)]
    for i in range(nb):
        for h in range(GDN_H):
            k_c = ks_t[h][:, i:i + 1]
            q_c = qs_t[h][:, i:i + 1]
            eg = jnp.exp(g[i:i + 1, h:h + 1])
            b_s = beta[i:i + 1, GDN_H + h:GDN_H + h + 1]
            s = s_in_ref[i, h]
            sk = jnp.sum(k_c * s, axis=0, keepdims=True)
            v_row = v[i:i + 1, h * GDN_HD:(h + 1) * GDN_HD]
            s_new = s * eg + (k_c * b_s) * (v_row - eg * sk)
            s_out_ref[i, h] = s_new
            o = jnp.sum(q_c * s_new, axis=0, keepdims=True) * (GDN_HD ** -0.5)
            z_h = z[i:i + 1, h * GDN_HD:(h + 1) * GDN_HD]
            y_ref[i:i + 1, h * GDN_HD:(h + 1) * GDN_HD] = _gdn_post(o, z_h, nw).astype(BF16)


def gdn_step(proj, conv_state, state, layer, prm, nb):
    b = proj.shape[0]
    blk = (nb, GDN_H, GDN_HD, GDN_HD)
    return pl.pallas_call(
        functools.partial(_gdn_step_kernel, nb),
        grid=(b // nb,),
        in_specs=[pl.BlockSpec((nb, GDN_W), lambda i: (i, GDN_OFF // GDN_W)),
                  pl.BlockSpec((nb, 9 * GDN_D), lambda i: (i, 0)),
                  pl.BlockSpec((None,) + blk, lambda i: (layer, i, 0, 0, 0))] + [_const_spec(x, 1) for x in prm],
        out_specs=[pl.BlockSpec((nb, GDN_D), lambda i: (i, 0)),
                   pl.BlockSpec(blk, lambda i: (i, 0, 0, 0))],
        out_shape=[jax.ShapeDtypeStruct((b, GDN_D), BF16),
                   jax.ShapeDtypeStruct(state.shape[1:], F32)],
        compiler_params=_cp(("arbitrary",)),
        name="gdn_step",
    )(proj, conv_state, state, *_args(prm))


def _lru_gates(xc, gate, wr_ref, br_ref, wi_ref, bi_ref, lam_ref):
    r = jax.nn.sigmoid(_bdot(xc, wr_ref[...]) + br_ref[...])
    i = jax.nn.sigmoid(_bdot(xc, wi_ref[...]) + bi_ref[...])
    log_a = -LRU_C * r * jax.nn.softplus(-lam_ref[...])
    a = jnp.exp(log_a)
    one_minus_a2 = -jnp.tanh(log_a) * (a * a + 1.0)
    b = jnp.sqrt(one_minus_a2) * (i * xc)
    return a, b, _gelu(gate)


def _lru_seq_kernel(p_ref, cw_ref, cb_ref, wr_ref, br_ref, wi_ref, bi_ref, lam_ref, y_ref, h_out_ref,
                    prev_ref, carry_ref):
    ns, tc = p_ref.shape[0], p_ref.shape[1]
    n = ns * tc
    t = pl.program_id(0)

    @pl.when(t == 0)
    def _():
        prev_ref[...] = jnp.zeros_like(prev_ref)
        carry_ref[...] = jnp.zeros_like(carry_ref)

    p_fwd = _interleave_perm(ns, tc).astype(BF16)
    x_hi, x_lo = _split(p_ref[...].reshape(n, LRU_W))
    x = _dot(p_fwd, x_hi) + _dot(p_fwd, x_lo)
    xr = x[:, :LRU_D]
    gate = x[:, LRU_D:]
    halo = prev_ref.shape[0]
    ext = jnp.concatenate([prev_ref[...], xr], axis=0)
    prev_ref[...] = xr[n - halo:n]
    cw = cw_ref[...]
    xc = cw[3:4] * xr + cb_ref[...]
    for j in (1, 2, 3):
        xc = xc + cw[3 - j:4 - j] * pltpu.roll(ext, j * ns, axis=0)[halo:]
    a, b, gg = _lru_gates(xc, gate, wr_ref, br_ref, wi_ref, bi_ref, lam_ref)
    first = _iota((2 * ns, 1), 0) < ns
    h_prev = carry_ref[...]
    hs = []
    for s in range(n // (2 * ns)):
        a_s = a[s * 2 * ns:(s + 1) * 2 * ns]
        b_s = b[s * 2 * ns:(s + 1) * 2 * ns]
        h_a = a_s * pltpu.roll(h_prev, ns, axis=0) + b_s
        h_prev = jnp.where(first, h_a, a_s * pltpu.roll(h_a, ns, axis=0) + b_s)
        hs.append(h_prev)
    carry_ref[...] = h_prev
    y_t = (jnp.concatenate(hs, axis=0) * gg).astype(BF16)
    y_ref[...] = _dot_tn(p_fwd, y_t).astype(BF16).reshape(ns, tc, LRU_D)

    @pl.when(t == pl.num_programs(0) - 1)
    def _():
        h_out_ref[:, 0, :] = h_prev[ns:2 * ns]


def lru_seq(proj, prm):
    b, l, _ = proj.shape
    assert 2 * b == 8, "the time-major scan packs two time steps of all sequences into one 8-row sublane tile"
    tc = CHUNK
    return pl.pallas_call(
        _lru_seq_kernel,
        grid=(l // tc,),
        in_specs=[pl.BlockSpec((b, tc, LRU_W), lambda t: (0, t, LRU_OFF // LRU_W))] + [_const_spec(x, 1) for x in prm],
        out_specs=[pl.BlockSpec((b, tc, LRU_D), lambda t: (0, t, 0)),
                   pl.BlockSpec((b, 1, LRU_D), lambda t: (0, 0, 0))],
        out_shape=[jax.ShapeDtypeStruct((b, l, LRU_D), BF16),
                   jax.ShapeDtypeStruct((b, 1, LRU_D), F32)],
        scratch_shapes=[pltpu.VMEM((4 * b, LRU_D), F32), pltpu.VMEM((2 * b, LRU_D), F32)],
        compiler_params=_cp(("arbitrary",)),
        name="lru_seq",
    )(proj, *_args(prm))


def _lru_step_kernel(p_ref, cs_ref, h0_ref, cw_ref, cb_ref, wr_ref, br_ref, wi_ref, bi_ref, lam_ref, y_ref, h_ref):
    x = p_ref[...]
    xr = x[:, :LRU_D]
    gate = x[:, LRU_D:]
    cs = cs_ref[...]
    cw = cw_ref[...]
    xc = (cw[0:1] * cs[:, 0:LRU_D] + cw[1:2] * cs[:, LRU_D:2 * LRU_D] + cw[2:3] * cs[:, 2 * LRU_D:3 * LRU_D]
          + cw[3:4] * xr + cb_ref[...])
    a, b, gg = _lru_gates(xc, gate, wr_ref, br_ref, wi_ref, bi_ref, lam_ref)
    h = a * h0_ref[...] + b
    h_ref[...] = h
    y_ref[...] = (h * gg).astype(BF16)


def lru_step(proj, conv_state, h0, prm):
    b = proj.shape[0]
    return pl.pallas_call(
        _lru_step_kernel,
        grid=(1,),
        in_specs=[pl.BlockSpec((b, LRU_W), lambda i: (0, LRU_OFF // LRU_W)),
                  _const_spec(conv_state, 1), _const_spec(h0, 1)] + [_const_spec(x, 1) for x in prm],
        out_specs=[pl.BlockSpec((b, LRU_D), lambda i: (0, 0)),
                   pl.BlockSpec((b, LRU_D), lambda i: (0, 0))],
        out_shape=[jax.ShapeDtypeStruct((b, LRU_D), BF16),
                   jax.ShapeDtypeStruct((b, LRU_D), F32)],
        compiler_params=_cp(("arbitrary",)),
        name="lru_step",
    )(proj, conv_state, h0, *_args(prm))


def _pad_cols(x, width):
    return jnp.pad(x, [(0, 0)] * (x.ndim - 1) + [(0, width - x.shape[-1])])


def _pack_rwkv_cols(x):
    return jnp.concatenate([x[..., :1536], _pad_cols(x[..., 1536:1632], 128), _pad_cols(x[..., 1632:1728], 128),
                            x[..., 1728:1984]], axis=-1)


def _unpack_rwkv_cols(x):
    return jnp.concatenate([x[..., :1536], x[..., 1536:1632], x[..., 1664:1760], x[..., 1792:2048]], axis=-1)


def _pack_w_in(w):
    o = RWKV_PROJ
    return jnp.concatenate([_pack_rwkv_cols(w[..., :o]), w[..., o:o + S5_D],
                            _pad_cols(w[..., o + S5_D:o + S5_D + GDN_PROJ], GDN_W), w[..., o + S5_D + GDN_PROJ:]],
                           axis=-1).astype(BF16)


def _block_diag(blocks):
    *lead, g, a, b = blocks.shape
    eye = jnp.eye(g, dtype=blocks.dtype)
    return (eye[:, None, :, None] * blocks[..., :, :, None, :]).reshape(*lead, g * a, g * b)


def _s5_params(lam_re, lam_im, log_dt, b_re, b_im, c_re, c_im, d, glu_w, glu_b):
    depth = lam_re.shape[0]
    dt = jnp.exp(log_dt)[..., None]
    mag = jnp.exp(lam_re * dt)
    ab_re, ab_im = mag * jnp.cos(lam_im * dt), mag * jnp.sin(lam_im * dt)
    den = lam_re * lam_re + lam_im * lam_im
    nr = ab_re - 1.0
    f_re = (nr * lam_re + ab_im * lam_im) / den
    f_im = (ab_im * lam_re - nr * lam_im) / den
    bb_re = f_re[..., None] * b_re - f_im[..., None] * b_im
    bb_im = f_re[..., None] * b_im + f_im[..., None] * b_re
    bbr = _block_diag(jnp.swapaxes(bb_re, -1, -2))
    bbi = _block_diag(jnp.swapaxes(bb_im, -1, -2))
    cc = jnp.concatenate([_block_diag(jnp.swapaxes(c_re, -1, -2)), -_block_diag(jnp.swapaxes(c_im, -1, -2))],
                         axis=-2).astype(BF16)
    tail = (cc, ab_re.reshape(depth, 1, S5_STATE), ab_im.reshape(depth, 1, S5_STATE), d.reshape(depth, 1, S5_D),
            glu_w, glu_b.reshape(depth, 1, S5_D))
    return (bbr.astype(BF16), bbi.astype(BF16)) + tail, (bbr, bbi) + tail


def _prepare_params(w):
    depth = w['w_in'].shape[0]
    row = lambda v: v.reshape(depth, 1, -1)
    pad_rows = lambda v: jnp.pad(v, ((0, 0), (0, 128 - v.shape[1]), (0, 0)))
    s5, s5_f32 = _s5_params(w['s5_lambda_re'], w['s5_lambda_im'], w['s5_log_dt'], w['s5_b_re'], w['s5_b_im'],
                            w['s5_c_re'], w['s5_c_im'], w['s5_d'], w['s5_glu_w'], w['s5_glu_b'])
    return {
        'norm1_g': row(w['norm1_g']), 'norm2_g': row(w['norm2_g']),
        'w_in': _pack_w_in(w['w_in']),
        'rwkv': (row(_pack_rwkv_cols(w['rwkv_mu'])), row(w['rwkv_w0']), pad_rows(w['rwkv_w_up']), row(w['rwkv_a0']),
                 pad_rows(w['rwkv_a_up']), w['rwkv_g_up'], row(w['rwkv_k_k']), row(w['rwkv_k_a']), row(w['rwkv_r_k']),
                 row(w['rwkv_ln_w']), row(w['rwkv_ln_b'])),
        's5': s5, 's5_f32': s5_f32,
        'gdn': (w['gdn_conv_w'], _pad_cols(row(w['gdn_a_log']), 128), _pad_cols(row(w['gdn_dt_bias']), 128),
                row(w['gdn_norm_w'])),
        'lru': (w['lru_conv_w'], row(w['lru_conv_b']), _block_diag(w['lru_wr']), row(w['lru_br']),
                _block_diag(w['lru_wi']), row(w['lru_bi']), row(w['lru_lambda'])),
        'ffn_conv_w': w['ffn_conv_w'], 'ffn_conv_b': row(w['ffn_conv_b']),
    }


def _layer(xp, xs, mod, st, p, seg, layer, stacked):
    bp, lp, d = xp.shape
    bs = xs.shape[1]
    sh1, sc1, g1, sh2, sc2, g2 = jnp.split(mod, 6, axis=-1)
    pm = lambda m: m[:bp, None, :]
    sm = lambda m: m[None, bp:bp + bs, :]

    lay = lambda x: Layered(x, layer)
    rw_prm, s5_prm, s5_prm_f32, gdn_prm, lru_prm = (tuple(lay(x) for x in p[k])
                                                    for k in ('rwkv', 's5', 's5_f32', 'gdn', 'lru'))

    h = norm_mod(xp, lay(p['norm1_g']), pm(sc1), pm(sh1), 512)
    proj = matmul_in(h, p['w_in'], layer, 1024, 1024)
    ya, s_wkv = rwkv_seq(proj, rw_prm, seg, bp)
    yb, s_re, s_im = s5_seq(proj, s5_prm)
    yc, s_gdn = gdn_seq(proj, gdn_prm, bp)
    yd, s_lru = lru_seq(proj, lru_prm)
    xp = matmul_res([ya, yb, yc, yd], stacked['w_out'], layer, xp, pm(g1), lp, 512)
    h = norm_mod(xp, lay(p['norm2_g']), pm(sc2), pm(sh2), 512)
    act, lg, lv = ffn_up_seq(h, stacked['ffn_w_up'], layer, p['ffn_conv_w'], p['ffn_conv_b'], 256)
    xp = matmul_res([act], stacked['ffn_w_down'], layer, xp, pm(g2), 1024, 256)
    new_p = {
        'rwkv_wkv': s_wkv,
        'rwkv_shift': _unpack_rwkv_cols(proj[:, -1, :RW_W]),
        's5_re': s_re.reshape(bp, S5_G, S5_N), 's5_im': s_im.reshape(bp, S5_G, S5_N),
        'gdn': s_gdn,
        'gdn_conv': proj[:, -3:, GDN_OFF:GDN_OFF + 3 * GDN_D],
        'lru_h': s_lru.reshape(bp, LRU_D),
        'lru_conv': proj[:, -3:, LRU_OFF:LRU_OFF + LRU_D],
        'ffn_conv': jnp.concatenate([lg, lv], axis=-1),
    }

    h = norm_mod(xs, lay(p['norm1_g']), sm(sc1), sm(sh1), bs)
    proj = matmul_in(h, p['w_in'], layer, bs, 1024)[0]
    ya, s_wkv = rwkv_step(proj, _pack_rwkv_cols(st['rwkv_shift']), stacked['rwkv_wkv'], layer, rw_prm, seg)
    yb, s_re, s_im = s5_step(proj, st['s5_re'].reshape(bs, S5_STATE), st['s5_im'].reshape(bs, S5_STATE), s5_prm_f32)
    yc, s_gdn = gdn_step(proj, st['gdn_conv'].reshape(bs, 9 * GDN_D), stacked['gdn'], layer, gdn_prm, 16)
    yd, s_lru = lru_step(proj, st['lru_conv'].reshape(bs, 3 * LRU_D), st['lru_h'], lru_prm)
    xs = matmul_res([y[None] for y in (ya, yb, yc, yd)], stacked['w_out'], layer, xs, sm(g1), bs, 512)
    h = norm_mod(xs, lay(p['norm2_g']), sm(sc2), sm(sh2), bs)
    act, ug, uv = ffn_up_step(h[0], stacked['ffn_w_up'], layer, p['ffn_conv_w'], p['ffn_conv_b'],
                              stacked['ffn_conv'], 256)
    xs = matmul_res([act[None]], stacked['ffn_w_down'], layer, xs, sm(g2), bs, 256)
    shift_rows = lambda buf, new: jnp.concatenate([buf[:, 1:], new[:, None, :]], axis=1)
    new_s = {
        'rwkv_wkv': s_wkv,
        'rwkv_shift': _unpack_rwkv_cols(proj[:, :RW_W]),
        's5_re': s_re.reshape(bs, S5_G, S5_N), 's5_im': s_im.reshape(bs, S5_G, S5_N),
        'gdn': s_gdn,
        'gdn_conv': shift_rows(st['gdn_conv'], proj[:, GDN_OFF:GDN_OFF + 3 * GDN_D]),
        'lru_h': s_lru,
        'lru_conv': shift_rows(st['lru_conv'], proj[:, LRU_OFF:LRU_OFF + LRU_D]),
        'ffn_conv': shift_rows(st['ffn_conv'], jnp.concatenate([ug, uv], axis=-1)),
    }
    return xp, xs, new_p, new_s


STATE_ORDER = ('rwkv_wkv', 'rwkv_shift', 's5_re', 's5_im', 'gdn', 'gdn_conv', 'lru_h', 'lru_conv', 'ffn_conv')


def kernel(x_prompt, x_sample, c_prompt, c_sample, state_rwkv_wkv, state_rwkv_shift, state_s5_re, state_s5_im, state_gdn, state_gdn_conv, state_lru_h, state_lru_conv, state_ffn_conv, ada_w, ada_b, norm1_g, norm2_g, final_g, w_in, w_out, rwkv_mu, rwkv_w0, rwkv_w_up, rwkv_a0, rwkv_a_up, rwkv_g_up, rwkv_k_k, rwkv_k_a, rwkv_r_k, rwkv_ln_w, rwkv_ln_b, s5_lambda_re, s5_lambda_im, s5_log_dt, s5_b_re, s5_b_im, s5_c_re, s5_c_im, s5_d, s5_glu_w, s5_glu_b, gdn_conv_w, gdn_a_log, gdn_dt_bias, gdn_norm_w, lru_conv_w, lru_conv_b, lru_wr, lru_br, lru_wi, lru_bi, lru_lambda, ffn_w_up, ffn_conv_w, ffn_conv_b, ffn_w_down):
    weights = {
        'norm1_g': norm1_g, 'norm2_g': norm2_g, 'w_in': w_in,
        'rwkv_mu': rwkv_mu, 'rwkv_w0': rwkv_w0, 'rwkv_w_up': rwkv_w_up, 'rwkv_a0': rwkv_a0,
        'rwkv_a_up': rwkv_a_up, 'rwkv_g_up': rwkv_g_up, 'rwkv_k_k': rwkv_k_k, 'rwkv_k_a': rwkv_k_a,
        'rwkv_r_k': rwkv_r_k, 'rwkv_ln_w': rwkv_ln_w, 'rwkv_ln_b': rwkv_ln_b,
        's5_lambda_re': s5_lambda_re, 's5_lambda_im': s5_lambda_im, 's5_log_dt': s5_log_dt,
        's5_b_re': s5_b_re, 's5_b_im': s5_b_im, 's5_c_re': s5_c_re, 's5_c_im': s5_c_im, 's5_d': s5_d,
        's5_glu_w': s5_glu_w, 's5_glu_b': s5_glu_b,
        'gdn_conv_w': gdn_conv_w, 'gdn_a_log': gdn_a_log, 'gdn_dt_bias': gdn_dt_bias, 'gdn_norm_w': gdn_norm_w,
        'lru_conv_w': lru_conv_w, 'lru_conv_b': lru_conv_b, 'lru_wr': lru_wr, 'lru_br': lru_br,
        'lru_wi': lru_wi, 'lru_bi': lru_bi, 'lru_lambda': lru_lambda,
        'ffn_conv_w': ffn_conv_w, 'ffn_conv_b': ffn_conv_b,
    }
    stacked = {'w_out': w_out, 'ffn_w_up': ffn_w_up, 'ffn_w_down': ffn_w_down,
               'rwkv_wkv': jnp.transpose(state_rwkv_wkv, (0, 2, 3, 4, 1)),
               'gdn': state_gdn, 'ffn_conv': state_ffn_conv}
    cache = {'rwkv_wkv': state_rwkv_wkv, 'rwkv_shift': state_rwkv_shift, 's5_re': state_s5_re,
             's5_im': state_s5_im, 'gdn': state_gdn, 'gdn_conv': state_gdn_conv, 'lru_h': state_lru_h,
             'lru_conv': state_lru_conv, 'ffn_conv': state_ffn_conv}
    depth = ada_w.shape[0]
    bp = x_prompt.shape[0]
    bs = x_sample.shape[0]
    rows = -(-(bp + bs) // 8) * 8
    c_all = jnp.pad(jnp.concatenate([c_prompt, c_sample], axis=0), ((0, rows - bp - bs), (0, 0)))
    mod = ada_mod(c_all, ada_w, ada_b)
    ids = jnp.arange(RWKV_D) // RWKV_HD
    seg = (ids[:, None] == ids[None, :]).astype(BF16)

    xp = x_prompt
    xs = jnp.swapaxes(x_sample, 0, 1)
    new_p = {n: [] for n in STATE_ORDER}
    new_s = {n: [] for n in STATE_ORDER}
    p = _prepare_params(weights)
    for l in range(depth):
        st = {n: cache[n][l] for n in STATE_ORDER}
        xp, xs, sp, ss = _layer(xp, xs, mod[l], st, p, seg, l, stacked)
        for n in STATE_ORDER:
            new_p[n].append(sp[n])
            new_s[n].append(ss[n])
    y_prompt = final_norm(xp, final_g, 512)
    y_sample = jnp.swapaxes(final_norm(xs, final_g, bs), 0, 1)
    outs_p = tuple(jnp.stack(new_p[n], axis=0) for n in STATE_ORDER)
    stacked_s = {n: jnp.stack(new_s[n], axis=0) for n in STATE_ORDER}
    stacked_s['rwkv_wkv'] = jnp.transpose(stacked_s['rwkv_wkv'], (0, 4, 1, 2, 3))
    outs_s = tuple(stacked_s[n] for n in STATE_ORDER)
    return (y_prompt, y_sample) + outs_p + outs_s
```

```python
import functools
import math
from typing import NamedTuple

import jax
import jax.numpy as jnp
from jax import lax
from jax.experimental import pallas as pl
from jax.experimental.pallas import tpu as pltpu

F32 = jnp.float32
BF16 = jnp.bfloat16
HI = lax.Precision.HIGHEST

NORM_EPS = 1e-6
RWKV_LN_EPS = 64e-5
LRU_C = 8.0

D_MODEL = 2048
RWKV_D, RWKV_HD, RWKV_H = 512, 64, 8
RWKV_R_DECAY, RWKV_R_A, RWKV_R_GATE = 96, 96, 256
RWKV_PROJ = 3 * RWKV_D + RWKV_R_DECAY + RWKV_R_A + RWKV_R_GATE
S5_D, S5_CH, S5_G, S5_N = 512, 16, 32, 64
S5_STATE = S5_G * S5_N
GDN_D, GDN_HD, GDN_H = 512, 128, 4
GDN_PROJ = 4 * GDN_D + 2 * GDN_H
LRU_D, LRU_BLOCKS = 512, 8
D_FF = 5632

RW_W = 2048
S5_OFF = 2048
GDN_OFF, GDN_W = 2560, 2560
LRU_OFF, LRU_W = 5120, 1024
NP_IN = 6144

CHUNK = 64
RWKV_GH = 4
assert CHUNK == RWKV_HD
VMEM_LIMIT = 56 * 1024 * 1024


def _cp(sem):
    return pltpu.CompilerParams(dimension_semantics=sem, vmem_limit_bytes=VMEM_LIMIT)


def _dot(a, b, prec=None):
    return jnp.dot(a, b, preferred_element_type=F32, precision=prec)


def _dot_nt(a, b, prec=None):
    return lax.dot_general(a, b, (((1,), (1,)), ((), ())), preferred_element_type=F32, precision=prec)


def _dot_tn(a, b, prec=None):
    return lax.dot_general(a, b, (((0,), (0,)), ((), ())), preferred_element_type=F32, precision=prec)


def _bdot(a, b):
    return _dot(a.astype(BF16), b.astype(BF16))


def _silu(x):
    return x * jax.nn.sigmoid(x)


def _gelu(x):
    return 0.5 * x * (1.0 + jnp.tanh(math.sqrt(2.0 / math.pi) * (x + 0.044715 * (x * x * x))))


def _seg_dot(x, seg):
    hi = x.astype(BF16)
    lo = (x - hi.astype(F32)).astype(BF16)
    n = x.shape[0]
    r = _dot(jnp.concatenate([hi, lo], axis=0), seg)
    return r[:n] + r[n:]


def _iota(shape, dim):
    return lax.broadcasted_iota(jnp.int32, shape, dim)


class Layered(NamedTuple):
    arr: jax.Array
    layer: int


def _const_spec(x, nidx):
    if isinstance(x, Layered):
        shape = (None,) + x.arr.shape[1:]
        idx = (x.layer,) + (0,) * (x.arr.ndim - 1)
    else:
        shape = x.shape
        idx = (0,) * x.ndim
    if nidx == 1:
        return pl.BlockSpec(shape, lambda i: idx)
    return pl.BlockSpec(shape, lambda b, t: idx)


def _args(params):
    return [x.arr if isinstance(x, Layered) else x for x in params]


def _neumann_solve(n, rhs, steps):
    x = rhs
    p = n
    for i in range(steps):
        x = x + _bdot(p, x)
        if i + 1 < steps:
            p = _bdot(p, p)
    return x


def _split(x):
    hi = x.astype(BF16)
    return hi, (x - hi.astype(F32)).astype(BF16)


def _dot3(a, b):
    ah, al = _split(a)
    bh, bl = _split(b)
    return _dot(ah, bh) + (_dot(ah, bl) + _dot(al, bh))


def _neumann_solve3(n, rhs, steps, fine=3):
    x = rhs
    p = n
    for i in range(steps):
        mm = _dot3 if i < fine else _bdot
        x = x + mm(p, x)
        if i + 1 < steps:
            p = mm(p, p)
    return x


def _row_to_col(row, eye):
    n = eye.shape[0]
    return jnp.sum(jnp.where(eye, jnp.broadcast_to(row, (n, n)), 0.0), axis=-1, keepdims=True)


def _ada_kernel(c_ref, w_ref, b_ref, o_ref):
    c = c_ref[...]
    o_ref[0] = _dot(_silu(c).astype(BF16), w_ref[0].astype(BF16)) + b_ref[0]


def ada_mod(c_all, ada_w, ada_b):
    depth, d, n = ada_w.shape
    r = c_all.shape[0]
    tn = 1024
    return pl.pallas_call(
        _ada_kernel,
        grid=(depth, n // tn),
        in_specs=[pl.BlockSpec((r, d), lambda l, j: (0, 0)),
                  pl.BlockSpec((1, d, tn), lambda l, j: (l, 0, j)),
                  pl.BlockSpec((1, 1, tn), lambda l, j: (l, 0, j))],
        out_specs=pl.BlockSpec((1, r, tn), lambda l, j: (l, 0, j)),
        out_shape=jax.ShapeDtypeStruct((depth, r, n), F32),
        compiler_params=_cp(("arbitrary", "arbitrary")),
        name="ada_mod",
    )(c_all, ada_w, ada_b.reshape(depth, 1, n))


def _norm_mod_kernel(x_ref, g_ref, sc_ref, sh_ref, o_ref):
    x = x_ref[0]
    h = x * lax.rsqrt(jnp.mean(x * x, axis=-1, keepdims=True) + NORM_EPS) * g_ref[...]
    o_ref[0] = (h * (1.0 + sc_ref[0]) + sh_ref[0]).astype(o_ref.dtype)


def _mod_spec(arr, tm):
    d = arr.shape[-1]
    if arr.shape[1] == 1:
        return pl.BlockSpec((1, 1, d), lambda g, i: (g, 0, 0))
    return pl.BlockSpec((1, tm, d), lambda g, i: (g, i, 0))


def norm_mod(x, gain, sc, sh, tm):
    g, r, d = x.shape
    return pl.pallas_call(
        _norm_mod_kernel,
        grid=(g, r // tm),
        in_specs=[pl.BlockSpec((1, tm, d), lambda g, i: (g, i, 0)),
                  _const_spec(gain, 2),
                  _mod_spec(sc, tm), _mod_spec(sh, tm)],
        out_specs=pl.BlockSpec((1, tm, d), lambda g, i: (g, i, 0)),
        out_shape=jax.ShapeDtypeStruct((g, r, d), BF16),
        compiler_params=_cp(("arbitrary", "arbitrary")),
        name="norm_mod",
    )(x, *_args([gain]), sc, sh)


def _rms_kernel(x_ref, g_ref, o_ref):
    x = x_ref[0]
    o_ref[0] = x * lax.rsqrt(jnp.mean(x * x, axis=-1, keepdims=True) + NORM_EPS) * g_ref[...]


def final_norm(x, gain, tm):
    g, r, d = x.shape
    return pl.pallas_call(
        _rms_kernel,
        grid=(g, r // tm),
        in_specs=[pl.BlockSpec((1, tm, d), lambda g, i: (g, i, 0)),
                  pl.BlockSpec((1, d), lambda g, i: (0, 0))],
        out_specs=pl.BlockSpec((1, tm, d), lambda g, i: (g, i, 0)),
        out_shape=jax.ShapeDtypeStruct((g, r, d), F32),
        compiler_params=_cp(("arbitrary", "arbitrary")),
        name="final_norm",
    )(x, gain.reshape(1, d))


def _mm_kernel(a_ref, w_ref, o_ref):
    o_ref[0] = _dot(a_ref[0], w_ref[...])


def matmul_in(a, w, layer, tm, tn):
    g, r, k = a.shape
    n = w.shape[2]
    return pl.pallas_call(
        _mm_kernel,
        grid=(g, r // tm, n // tn),
        in_specs=[pl.BlockSpec((1, tm, k), lambda g, i, j: (g, i, 0)),
                  pl.BlockSpec((None, k, tn), lambda g, i, j: (layer, 0, j))],
        out_specs=pl.BlockSpec((1, tm, tn), lambda g, i, j: (g, i, j)),
        out_shape=jax.ShapeDtypeStruct((g, r, n), F32),
        compiler_params=_cp(("arbitrary", "arbitrary", "arbitrary")),
        name="matmul_in",
    )(a, w)


def _mm_res_kernel(n_a, *refs):
    a_refs = refs[:n_a]
    w_refs = refs[n_a:2 * n_a]
    x_ref, g_ref, o_ref = refs[2 * n_a:]
    acc = _dot(a_refs[0][0], w_refs[0][...].astype(BF16))
    for a_ref, w_ref in zip(a_refs[1:], w_refs[1:]):
        acc = acc + _dot(a_ref[0], w_ref[...].astype(BF16))
    o_ref[0] = x_ref[0] + g_ref[0] * acc


def matmul_res(a_list, w, layer, x, gate, tm, tn):
    g, r, n = x.shape
    n_a = len(a_list)
    k = a_list[0].shape[-1]
    a_specs = [pl.BlockSpec((1, tm, k), lambda g, i, j: (g, i, 0)) for _ in a_list]
    w_specs = [pl.BlockSpec((None, k, tn), functools.partial(lambda g, i, j, q: (layer, q, j), q=q))
               for q in range(n_a)]
    if gate.shape[1] == 1:
        g_spec = pl.BlockSpec((1, 1, tn), lambda g, i, j: (g, 0, j))
    else:
        g_spec = pl.BlockSpec((1, tm, tn), lambda g, i, j: (g, i, j))
    return pl.pallas_call(
        functools.partial(_mm_res_kernel, n_a),
        grid=(g, r // tm, n // tn),
        in_specs=a_specs + w_specs + [pl.BlockSpec((1, tm, tn), lambda g, i, j: (g, i, j)), g_spec],
        out_specs=pl.BlockSpec((1, tm, tn), lambda g, i, j: (g, i, j)),
        out_shape=jax.ShapeDtypeStruct((g, r, n), F32),
        compiler_params=_cp(("arbitrary", "arbitrary", "arbitrary")),
        name="matmul_res",
    )(*a_list, *([w] * n_a), x, gate)


def _ffn_up_seq_kernel(a_ref, wg_ref, wv_ref, cwg_ref, cwv_ref, cbg_ref, cbv_ref, act_ref, lg_ref, lv_ref):
    a = a_ref[0]
    rows = a.shape[0]
    rid8 = _iota((8, 1), 0)

    def conv(up, cw_ref, cb_ref):
        cw = cw_ref[...]
        return cw[0:1] * pltpu.roll(up, 2, axis=0) + cw[1:2] * pltpu.roll(up, 1, axis=0) + cw[2:3] * up + cb_ref[...]

    def head(up, cw_ref, cb_ref):
        cw = cw_ref[...]
        u8 = up[0:8]
        s1 = jnp.where(rid8 >= 1, pltpu.roll(u8, 1, axis=0), 0.0)
        s2 = jnp.where(rid8 >= 2, pltpu.roll(u8, 2, axis=0), 0.0)
        return cw[0:1] * s2 + cw[1:2] * s1 + cw[2:3] * u8 + cb_ref[...]

    up_g = _dot(a, wg_ref[...].astype(BF16))
    lg_ref[0] = up_g[rows - 2:rows]
    gate = conv(up_g, cwg_ref, cbg_ref)
    up_v = _dot(a, wv_ref[...].astype(BF16))
    lv_ref[0] = up_v[rows - 2:rows]
    val = conv(up_v, cwv_ref, cbv_ref)
    act_ref[0] = (_silu(gate) * val).astype(BF16)
    act_ref[0, 0:8, :] = (_silu(head(up_g, cwg_ref, cbg_ref)) * head(up_v, cwv_ref, cbv_ref)).astype(BF16)


def ffn_up_seq(h, w_up, layer, conv_w, conv_b, tn):
    b, l, d = h.shape
    nt = D_FF // tn
    return pl.pallas_call(
        _ffn_up_seq_kernel,
        grid=(b, nt),
        in_specs=[pl.BlockSpec((1, l, d), lambda b, j: (b, 0, 0)),
                  pl.BlockSpec((None, d, tn), lambda b, j: (layer, 0, j)),
                  pl.BlockSpec((None, d, tn), lambda b, j: (layer, 0, j + nt)),
                  pl.BlockSpec((None, 3, tn), lambda b, j: (layer, 0, j)),
                  pl.BlockSpec((None, 3, tn), lambda b, j: (layer, 0, j + nt)),
                  pl.BlockSpec((None, 1, tn), lambda b, j: (layer, 0, j)),
                  pl.BlockSpec((None, 1, tn), lambda b, j: (layer, 0, j + nt))],
        out_specs=[pl.BlockSpec((1, l, tn), lambda b, j: (b, 0, j)),
                   pl.BlockSpec((1, 2, tn), lambda b, j: (b, 0, j)),
                   pl.BlockSpec((1, 2, tn), lambda b, j: (b, 0, j))],
        out_shape=[jax.ShapeDtypeStruct((b, l, D_FF), BF16),
                   jax.ShapeDtypeStruct((b, 2, D_FF), F32),
                   jax.ShapeDtypeStruct((b, 2, D_FF), F32)],
        compiler_params=_cp(("arbitrary", "arbitrary")),
        name="ffn_up_seq",
    )(h, w_up, w_up, conv_w, conv_w, conv_b, conv_b)


def _ffn_up_step_kernel(a_ref, wg_ref, wv_ref, cwg_ref, cwv_ref, cbg_ref, cbv_ref,
                        stg_ref, stv_ref, act_ref, ug_ref, uv_ref):
    a = a_ref[...]
    up_g = _dot(a, wg_ref[...].astype(BF16))
    up_v = _dot(a, wv_ref[...].astype(BF16))
    ug_ref[...] = up_g
    uv_ref[...] = up_v
    cwg = cwg_ref[...]
    cwv = cwv_ref[...]
    gate = cwg[0:1] * stg_ref[:, 0, :] + cwg[1:2] * stg_ref[:, 1, :] + cwg[2:3] * up_g + cbg_ref[...]
    val = cwv[0:1] * stv_ref[:, 0, :] + cwv[1:2] * stv_ref[:, 1, :] + cwv[2:3] * up_v + cbv_ref[...]
    act_ref[...] = (_silu(gate) * val).astype(BF16)


def ffn_up_step(h, w_up, layer, conv_w, conv_b, conv_state, tn):
    b, d = h.shape
    nt = D_FF // tn
    return pl.pallas_call(
        _ffn_up_step_kernel,
        grid=(nt,),
        in_specs=[pl.BlockSpec((b, d), lambda j: (0, 0)),
                  pl.BlockSpec((None, d, tn), lambda j: (layer, 0, j)),
                  pl.BlockSpec((None, d, tn), lambda j: (layer, 0, j + nt)),
                  pl.BlockSpec((None, 3, tn), lambda j: (layer, 0, j)),
                  pl.BlockSpec((None, 3, tn), lambda j: (layer, 0, j + nt)),
                  pl.BlockSpec((None, 1, tn), lambda j: (layer, 0, j)),
                  pl.BlockSpec((None, 1, tn), lambda j: (layer, 0, j + nt)),
                  pl.BlockSpec((None, b, 2, tn), lambda j: (layer, 0, 0, j)),
                  pl.BlockSpec((None, b, 2, tn), lambda j: (layer, 0, 0, j + nt))],
        out_specs=[pl.BlockSpec((b, tn), lambda j: (0, j))] * 3,
        out_shape=[jax.ShapeDtypeStruct((b, D_FF), BF16),
                   jax.ShapeDtypeStruct((b, D_FF), F32),
                   jax.ShapeDtypeStruct((b, D_FF), F32)],
        compiler_params=_cp(("arbitrary",)),
        name="ffn_up_step",
    )(h, w_up, w_up, conv_w, conv_w, conv_b, conv_b, conv_state, conv_state)


def _rwkv_pre(p, prev, prm, seg):
    mu, w0, w_up, a0, a_up, g_up, k_k, k_a = prm
    xm = p + mu * (prev - p)
    r = xm[:, 0:512]
    k = xm[:, 512:1024]
    v = xm[:, 1024:1536]
    xw = xm[:, 1536:1664]
    xa = xm[:, 1664:1792]
    xg = xm[:, 1792:2048]
    log_w = -jnp.exp(-jax.nn.softplus(-(w0 + _bdot(jnp.tanh(xw), w_up))) - 0.5)
    a = jax.nn.sigmoid(a0 + _bdot(xa, a_up))
    g = _bdot(jax.nn.sigmoid(xg), g_up)
    kx = k * k_k
    kk = kx * lax.rsqrt(_seg_dot(kx * kx, seg) + 1e-12)
    k2 = k * (1.0 + (a - 1.0) * k_a)
    return r, k2, v, log_w, a, g, kk


def _rwkv_post(y, r, k2, v, g, r_k, ln_w, ln_b, seg):
    inv = 1.0 / RWKV_HD
    mean = _seg_dot(y, seg) * inv
    dlt = y - mean
    var = _seg_dot(dlt * dlt, seg) * inv
    yn = dlt * lax.rsqrt(var + RWKV_LN_EPS) * ln_w + ln_b
    bonus = _seg_dot(r * k2 * r_k, seg) * v
    return (yn + bonus) * g


def _rwkv_seq_kernel(p_ref, mu_ref, w0_ref, wup_ref, a0_ref, aup_ref, gup_ref, kk_ref, ka_ref, rk_ref,
                     lnw_ref, lnb_ref, seg_ref, y_ref, s_out_ref, s_ref, prev_ref):
    c = CHUNK
    ns = p_ref.shape[0]
    n = ns * c
    t = pl.program_id(1)

    @pl.when(t == 0)
    def _():
        s_ref[...] = jnp.zeros_like(s_ref)
        prev_ref[...] = jnp.zeros_like(prev_ref)

    p = p_ref[...].reshape(n, RW_W)
    seg = seg_ref[...]
    seq_rows = lambda x: jnp.concatenate([jnp.broadcast_to(x[i:i + 1], (c, x.shape[-1])) for i in range(ns)], axis=0)
    tloc = _iota((n, 1), 0) & (c - 1)
    prev = jnp.where(tloc == 0, seq_rows(prev_ref[...]), pltpu.roll(p, 1, axis=0))
    last_rows = lambda x: jnp.concatenate([x[(i + 1) * c - 1:(i + 1) * c] for i in range(ns)], axis=0)
    prev_ref[...] = last_rows(p)
    prm = (mu_ref[...], w0_ref[...], wup_ref[...], a0_ref[...], aup_ref[...], gup_ref[...], kk_ref[...], ka_ref[...])
    r, k2, v, log_w, a, g, kk = _rwkv_pre(p, prev, prm, seg)

    ri = _iota((n, n), 0)
    ci = _iota((n, n), 1)
    cshift = c.bit_length() - 1
    tril = ((lax.shift_right_logical(ri, cshift) == lax.shift_right_logical(ci, cshift)) & (ci <= ri)).astype(F32)
    cum = _dot(tril, log_w, HI)
    e_in = jnp.exp(cum)
    e_out = jnp.exp(-cum)
    e_ex = jnp.exp(cum - log_w)
    tot = last_rows(cum)
    e_tail = jnp.exp(seq_rows(tot) - cum)
    e_tot = jnp.exp(tot)
    alpha = -(a * kk)
    gw = RWKV_GH * RWKV_HD
    rr = _iota((gw, gw), 0)
    cc = _iota((gw, gw), 1)
    shift = RWKV_HD.bit_length() - 1
    own = lax.shift_right_logical(rr, shift) == lax.shift_right_logical(cc, shift)
    tri_strict = (cc & (c - 1)) < (rr & (c - 1))
    tri_incl = (cc & (c - 1)) <= (rr & (c - 1))
    eye = rr == cc

    ng = RWKV_H // RWKV_GH
    be_t = (kk * e_ex).astype(BF16)
    r_t = (r * e_in).astype(BF16)
    al_t = (alpha * e_out).astype(BF16)
    k_t = (k2 * e_out).astype(BF16)
    al_p = (alpha * e_tail).astype(BF16)
    k_p = (k2 * e_tail).astype(BF16)
    vb = v.astype(BF16)

    def expand(x, i, gi):
        xg = x[i * c:(i + 1) * c, gi * gw:(gi + 1) * gw]
        return jnp.where(own, jnp.concatenate([xg] * RWKV_GH, axis=0), jnp.zeros((), BF16))

    y_rows = []
    for i in range(ns):
        ys = []
        for gi in range(ng):
            s0 = s_ref[i * ng + gi]
            lhs = jnp.concatenate([expand(be_t, i, gi), expand(r_t, i, gi)], axis=0)
            rhs = jnp.concatenate([expand(al_t, i, gi), expand(k_t, i, gi)], axis=0)
            vv = expand(vb, i, gi)
            m1 = _dot_nt(lhs, rhs)
            m2 = _dot(lhs, s0.astype(BF16))
            l_a = jnp.where(tri_strict, m1[:gw, :gw], 0.0)
            l_k = jnp.where(tri_strict, m1[:gw, gw:], 0.0)
            u = _neumann_solve(l_a, m2[:gw] + _dot(l_k.astype(BF16), vv), cshift)
            uv = jnp.concatenate([u.astype(BF16), vv], axis=0)
            a_full = jnp.concatenate([jnp.where(tri_incl, m1[gw:, :gw], 0.0),
                                      jnp.where(tri_incl, m1[gw:, gw:], 0.0)], axis=1).astype(BF16)
            y_g = m2[gw:] + _dot(a_full, uv)
            ys.append(functools.reduce(lambda p_, q_: p_ + q_, [y_g[j * c:(j + 1) * c] for j in range(RWKV_GH)]))
            scale = _row_to_col(e_tot[i:i + 1, gi * gw:(gi + 1) * gw], eye)
            tails = jnp.concatenate([expand(al_p, i, gi), expand(k_p, i, gi)], axis=0)
            s_ref[i * ng + gi] = s0 * scale + _dot_tn(tails, uv)
        y_rows.append(jnp.concatenate(ys, axis=1))

    y = jnp.concatenate(y_rows, axis=0)
    out = _rwkv_post(y, r, k2, v, g, rk_ref[...], lnw_ref[...], lnb_ref[...], seg)
    y_ref[...] = out.astype(BF16).reshape(ns, c, RWKV_D)

    @pl.when(t == pl.num_programs(1) - 1)
    def _():
        for i in range(ns):
            for h in range(RWKV_H):
                gi, hh = divmod(h, RWKV_GH)
                s_out_ref[i, h] = s_ref[i * ng + gi, hh * RWKV_HD:(hh + 1) * RWKV_HD, hh * RWKV_HD:(hh + 1) * RWKV_HD]


def rwkv_seq(proj, prm, seg, ns):
    b, l, _ = proj.shape
    c = CHUNK
    gw = RWKV_GH * RWKV_HD
    return pl.pallas_call(
        _rwkv_seq_kernel,
        grid=(b // ns, l // c),
        in_specs=[pl.BlockSpec((ns, c, RW_W), lambda b, t: (b, t, 0))] + [_const_spec(x, 2) for x in prm] + [_const_spec(seg, 2)],
        out_specs=[pl.BlockSpec((ns, c, RWKV_D), lambda b, t: (b, t, 0)),
                   pl.BlockSpec((ns, RWKV_H, RWKV_HD, RWKV_HD), lambda b, t: (b, 0, 0, 0))],
        out_shape=[jax.ShapeDtypeStruct((b, l, RWKV_D), BF16),
                   jax.ShapeDtypeStruct((b, RWKV_H, RWKV_HD, RWKV_HD), F32)],
        scratch_shapes=[pltpu.VMEM((ns * (RWKV_H // RWKV_GH), gw, gw), F32),
                        pltpu.VMEM((ns, RW_W), F32)],
        compiler_params=_cp(("arbitrary", "arbitrary")),
        name="rwkv_seq",
    )(proj, *_args(prm), seg)


def _rwkv_step_kernel(p_ref, prev_ref, s_in_ref, mu_ref, w0_ref, wup_ref, a0_ref, aup_ref, gup_ref, kk_ref,
                      ka_ref, rk_ref, lnw_ref, lnb_ref, seg_ref, y_ref, s_out_ref, nat_scr, col_scr, yt_scr):
    h = pl.program_id(0)

    @pl.when(h == 0)
    def _():
        prm = (mu_ref[...], w0_ref[...], wup_ref[...], a0_ref[...], aup_ref[...], gup_ref[...], kk_ref[...],
               ka_ref[...])
        r, k2, v, log_w, a, g, kk = _rwkv_pre(p_ref[...], prev_ref[...], prm, seg_ref[...])
        for i, x in enumerate((r, k2, v, g)):
            nat_scr[i] = x
        for i, x in enumerate((jnp.exp(log_w), kk * a, kk, k2, r, v)):
            col_scr[i] = x.T

    rows = pl.ds(pl.multiple_of(h * RWKV_HD, RWKV_HD), RWKV_HD)
    w_t, kka_t, kk_t, k_t, r_t, v_t = (col_scr[i, rows, :] for i in range(6))
    nacc = 4
    acc = [kk_t[j:j + 1] * s_in_ref[j] for j in range(nacc)]
    for k in range(nacc, RWKV_HD):
        acc[k % nacc] = acc[k % nacc] + kk_t[k:k + 1] * s_in_ref[k]
    sa = (acc[0] + acc[1]) + (acc[2] + acc[3])
    acc = [None] * nacc
    for k in range(RWKV_HD):
        s_new = s_in_ref[k] * w_t[k:k + 1] - kka_t[k:k + 1] * sa + k_t[k:k + 1] * v_t
        s_out_ref[k] = s_new
        term = r_t[k:k + 1] * s_new
        acc[k % nacc] = term if acc[k % nacc] is None else acc[k % nacc] + term
    yt_scr[rows, :] = (acc[0] + acc[1]) + (acc[2] + acc[3])

    @pl.when(h == RWKV_H - 1)
    def _():
        out = _rwkv_post(yt_scr[...].T, nat_scr[0], nat_scr[1], nat_scr[2], nat_scr[3], rk_ref[...], lnw_ref[...],
                         lnb_ref[...], seg_ref[...])
        y_ref[...] = out.astype(BF16)


def rwkv_step(proj, shift, state_t, layer, prm, seg):
    b = proj.shape[0]
    blk = (None, RWKV_HD, RWKV_HD, b)
    return pl.pallas_call(
        _rwkv_step_kernel,
        grid=(RWKV_H,),
        in_specs=[pl.BlockSpec((b, RW_W), lambda h: (0, 0)),
                  pl.BlockSpec((b, RW_W), lambda h: (0, 0)),
                  pl.BlockSpec((None,) + blk, lambda h: (layer, h, 0, 0, 0))]
                 + [_const_spec(x, 1) for x in prm] + [_const_spec(seg, 1)],
        out_specs=[pl.BlockSpec((b, RWKV_D), lambda h: (0, 0)),
                   pl.BlockSpec(blk, lambda h: (h, 0, 0, 0))],
        out_shape=[jax.ShapeDtypeStruct((b, RWKV_D), BF16),
                   jax.ShapeDtypeStruct(state_t.shape[1:], F32)],
        scratch_shapes=[pltpu.VMEM((4, b, RWKV_D), F32), pltpu.VMEM((6, RWKV_D, b), F32),
                        pltpu.VMEM((RWKV_D, b), F32)],
        compiler_params=_cp(("arbitrary",)),
        name="rwkv_step",
    )(proj, shift, state_t, *_args(prm), seg)


def _s5_glu(y, gw_ref, gb_ref):
    z = _gelu(y)
    return z * jax.nn.sigmoid(_bdot(z, gw_ref[...]) + gb_ref[...])


S5_CG = S5_D // 128
S5_SW = S5_STATE // S5_CG


def _s5_bu(u, bb_ref, prec=None):
    return jnp.concatenate([_dot(u[:, g * 128:(g + 1) * 128], bb_ref[g], prec) for g in range(S5_CG)], axis=1)


def _s5_ch(h_re, h_im, ccr_ref, cci_ref):
    return jnp.concatenate([_dot(h_re[:, g * S5_SW:(g + 1) * S5_SW], ccr_ref[g])
                            + _dot(h_im[:, g * S5_SW:(g + 1) * S5_SW], cci_ref[g]) for g in range(S5_CG)], axis=1)

def _interleave_perm(ns, tc):
    n = ns * tc
    r = _iota((n, n), 0)
    c = _iota((n, n), 1)
    t = lax.shift_right_logical(r, ns.bit_length() - 1)
    b = r & (ns - 1)
    return c == b * tc + t


def _s5_seq_kernel(u_ref, bbr_ref, bbi_ref, ccr_ref, cci_ref, abr_ref, abi_ref, d_ref, gw_ref, gb_ref,
                   y_ref, hr_ref, hi_ref, cr_ref, ci_ref):
    ns, tc = u_ref.shape[0], u_ref.shape[1]
    n = ns * tc
    t = pl.program_id(0)

    @pl.when(t == 0)
    def _():
        cr_ref[...] = jnp.zeros_like(cr_ref)
        ci_ref[...] = jnp.zeros_like(ci_ref)

    p_fwd = _interleave_perm(ns, tc).astype(BF16)
    u = u_ref[...].reshape(n, S5_D)
    u_hi, u_lo = _split(u)
    ub = _dot(p_fwd, u_hi)
    u_t = ub + _dot(p_fwd, u_lo)
    ub = ub.astype(BF16)
    bu_re = _s5_bu(ub, bbr_ref)
    bu_im = _s5_bu(ub, bbi_ref)
    ar = jnp.broadcast_to(abr_ref[...], (2 * ns, S5_STATE))
    ai = jnp.broadcast_to(abi_ref[...], (2 * ns, S5_STATE))
    first = _iota((2 * ns, 1), 0) < ns
    p_re = cr_ref[...]
    p_im = ci_ref[...]
    h_re, h_im = [], []
    for s in range(n // (2 * ns)):
        b_re = bu_re[s * 2 * ns:(s + 1) * 2 * ns]
        b_im = bu_im[s * 2 * ns:(s + 1) * 2 * ns]
        r_re = pltpu.roll(p_re, ns, axis=0)
        r_im = pltpu.roll(p_im, ns, axis=0)
        t_re = ar * r_re - ai * r_im + b_re
        t_im = ar * r_im + ai * r_re + b_im
        q_re = pltpu.roll(t_re, ns, axis=0)
        q_im = pltpu.roll(t_im, ns, axis=0)
        p_re = jnp.where(first, t_re, ar * q_re - ai * q_im + b_re)
        p_im = jnp.where(first, t_im, ar * q_im + ai * q_re + b_im)
        h_re.append(p_re)
        h_im.append(p_im)
    cr_ref[...] = p_re
    ci_ref[...] = p_im
    hb_re = jnp.concatenate(h_re, axis=0).astype(BF16)
    hb_im = jnp.concatenate(h_im, axis=0).astype(BF16)
    y_t = _s5_glu(_s5_ch(hb_re, hb_im, ccr_ref, cci_ref) + d_ref[...] * u_t, gw_ref, gb_ref)
    y = _dot_tn(p_fwd, y_t.astype(BF16))
    y_ref[...] = y.astype(BF16).reshape(ns, tc, S5_D)

    @pl.when(t == pl.num_programs(0) - 1)
    def _():
        hr_ref[:, 0, :] = p_re[ns:2 * ns]
        hi_ref[:, 0, :] = p_im[ns:2 * ns]


def s5_seq(proj, prm):
    b, l, _ = proj.shape
    assert 2 * b == 8, "the time-major scan packs two time steps of all sequences into one 8-row sublane tile"
    tc = CHUNK
    return pl.pallas_call(
        _s5_seq_kernel,
        grid=(l // tc,),
        in_specs=[pl.BlockSpec((b, tc, S5_D), lambda t: (0, t, S5_OFF // S5_D))] + [_const_spec(x, 1) for x in prm],
        out_specs=[pl.BlockSpec((b, tc, S5_D), lambda t: (0, t, 0)),
                   pl.BlockSpec((b, 1, S5_STATE), lambda t: (0, 0, 0)),
                   pl.BlockSpec((b, 1, S5_STATE), lambda t: (0, 0, 0))],
        out_shape=[jax.ShapeDtypeStruct((b, l, S5_D), BF16),
                   jax.ShapeDtypeStruct((b, 1, S5_STATE), F32),
                   jax.ShapeDtypeStruct((b, 1, S5_STATE), F32)],
        scratch_shapes=[pltpu.VMEM((2 * b, S5_STATE), F32), pltpu.VMEM((2 * b, S5_STATE), F32)],
        compiler_params=_cp(("arbitrary",)),
        name="s5_seq",
    )(proj, *_args(prm))


def _s5_step_kernel(u_ref, h0r_ref, h0i_ref, bbr_ref, bbi_ref, ccr_ref, cci_ref, abr_ref, abi_ref, d_ref, gw_ref,
                    gb_ref, y_ref, hr_ref, hi_ref):
    u = u_ref[...]
    ar = abr_ref[...]
    ai = abi_ref[...]
    h0r = h0r_ref[...]
    h0i = h0i_ref[...]
    h_re = ar * h0r - ai * h0i + _s5_bu(u, bbr_ref, HI)
    h_im = ar * h0i + ai * h0r + _s5_bu(u, bbi_ref, HI)
    hr_ref[...] = h_re
    hi_ref[...] = h_im
    ch = _s5_ch(h_re.astype(BF16), h_im.astype(BF16), ccr_ref, cci_ref)
    y_ref[...] = _s5_glu(ch + d_ref[...] * u, gw_ref, gb_ref).astype(BF16)


def s5_step(proj, h0_re, h0_im, prm):
    b = proj.shape[0]
    return pl.pallas_call(
        _s5_step_kernel,
        grid=(1,),
        in_specs=[pl.BlockSpec((b, S5_D), lambda i: (0, S5_OFF // S5_D)),
                  _const_spec(h0_re, 1), _const_spec(h0_im, 1)] + [_const_spec(x, 1) for x in prm],
        out_specs=[pl.BlockSpec((b, S5_D), lambda i: (0, 0)),
                   pl.BlockSpec((b, S5_STATE), lambda i: (0, 0)),
                   pl.BlockSpec((b, S5_STATE), lambda i: (0, 0))],
        out_shape=[jax.ShapeDtypeStruct((b, S5_D), BF16),
                   jax.ShapeDtypeStruct((b, S5_STATE), F32),
                   jax.ShapeDtypeStruct((b, S5_STATE), F32)],
        compiler_params=_cp(("arbitrary",)),
        name="s5_step",
    )(proj, h0_re, h0_im, *_args(prm))


def _gdn_pre(qkv_c, z, ab, alog_ref, dtb_ref):
    act = _silu(qkv_c)
    qs, ks = [], []
    for h in range(GDN_H):
        qh = act[:, h * GDN_HD:(h + 1) * GDN_HD]
        kh = act[:, GDN_D + h * GDN_HD:GDN_D + (h + 1) * GDN_HD]
        qs.append(qh * lax.rsqrt(jnp.sum(qh * qh, axis=-1, keepdims=True) + 1e-6))
        ks.append(kh * lax.rsqrt(jnp.sum(kh * kh, axis=-1, keepdims=True) + 1e-6))
    v = act[:, 2 * GDN_D:3 * GDN_D]
    g = -jnp.exp(alog_ref[...]) * jax.nn.softplus(ab + dtb_ref[...])
    beta = jax.nn.sigmoid(ab)
    return qs, ks, v, g, beta


def _gdn_post(o, z_h, nw):
    on = o * lax.rsqrt(jnp.mean(o * o, axis=-1, keepdims=True) + NORM_EPS) * nw
    return on * _silu(z_h)


def _gdn_seq_kernel(p_ref, cw_ref, alog_ref, dtb_ref, nw_ref, y_ref, s_out_ref, s_ref, prev_ref):
    c = CHUNK
    ns = p_ref.shape[0]
    n = ns * c
    t = pl.program_id(1)

    @pl.when(t == 0)
    def _():
        s_ref[...] = jnp.zeros_like(s_ref)
        prev_ref[...] = jnp.zeros_like(prev_ref)

    cw = cw_ref[...]
    convs = []
    for i in range(ns):
        qkv = p_ref[i, :, 0:3 * GDN_D]
        ext = jnp.concatenate([prev_ref[i], qkv], axis=0)
        prev_ref[i] = qkv[c - 8:c]
        conv = cw[3:4] * qkv
        for j in (1, 2, 3):
            conv = conv + cw[3 - j:4 - j] * pltpu.roll(ext, j, axis=0)[8:]
        convs.append(conv)
    x = p_ref[...].reshape(n, GDN_W)
    z = x[:, 3 * GDN_D:4 * GDN_D]
    ab = x[:, 4 * GDN_D:4 * GDN_D + 128]
    qs, ks, v, g, beta = _gdn_pre(jnp.concatenate(convs, axis=0), z, ab, alog_ref, dtb_ref)

    shift = c.bit_length() - 1
    ri = _iota((n, n), 0)
    ci = _iota((n, n), 1)
    tril = ((lax.shift_right_logical(ri, shift) == lax.shift_right_logical(ci, shift)) & (ci <= ri)).astype(F32)
    gc_all = _dot(tril, g, HI)

    hc = GDN_H * c
    heads = range(GDN_H)
    stack = lambda pieces: jnp.concatenate(pieces, axis=0)
    lane = _iota((c, 128), 1)
    rr = _iota((hc, hc), 0)
    cc = _iota((hc, hc), 1)
    same = lax.shift_right_logical(rr, shift) == lax.shift_right_logical(cc, shift)
    causal = same & (cc <= rr)
    strict = same & (cc < rr)
    ones = jnp.ones((hc, 128), F32)
    nw = nw_ref[...]
    for i in range(ns):
        sq = slice(i * c, (i + 1) * c)
        gc = gc_all[sq]
        g_col = stack([gc[:, h:h + 1] for h in heads])
        b_col = stack([beta[sq, GDN_H + h:GDN_H + h + 1] for h in heads])
        g_row = _dot_nt(ones, stack([jnp.where(lane == h, gc, 0.0) for h in heads]), HI)
        decay = jnp.where(causal, jnp.exp(jnp.where(causal, g_col - g_row, 0.0)), 0.0)
        q = stack([qs[h][sq] for h in heads]) * (GDN_HD ** -0.5)
        k = stack([ks[h][sq] for h in heads])
        v_s = stack([v[sq, h * GDN_HD:(h + 1) * GDN_HD] for h in heads])
        kb = k * b_col
        m1 = _dot_nt(stack([kb, q]).astype(BF16), k.astype(BF16))
        lmat = jnp.where(strict, m1[:hc] * decay, 0.0)
        attn = m1[hc:] * decay
        eg = jnp.exp(g_col)
        sol = _neumann_solve3(-lmat, jnp.concatenate([v_s * b_col, kb * eg], axis=1), shift)
        u = sol[:, :GDN_HD]
        wk = sol[:, GDN_HD:]
        qd = q * eg
        ws = [_dot(stack([wk[h * c:(h + 1) * c], qd[h * c:(h + 1) * c]]).astype(BF16),
                   s_ref[i * GDN_H + h].astype(BF16)) for h in heads]
        v_new = u - stack([w[:c] for w in ws])
        o = stack([w[c:] for w in ws]) + _bdot(attn, v_new)
        g_last = [gc[c - 1:c, h:h + 1] for h in heads]
        k_tail = k * jnp.exp(stack([jnp.broadcast_to(gl, (c, 1)) for gl in g_last]) - g_col)
        for h in heads:
            rows = slice(h * c, (h + 1) * c)
            s_ref[i * GDN_H + h] = (s_ref[i * GDN_H + h] * jnp.exp(g_last[h])
                                    + _dot_tn(k_tail[rows].astype(BF16), v_new[rows].astype(BF16)))
            z_h = z[sq, h * GDN_HD:(h + 1) * GDN_HD]
            y_ref[i, :, h * GDN_HD:(h + 1) * GDN_HD] = _gdn_post(o[rows], z_h, nw).astype(BF16)

    @pl.when(t == pl.num_programs(1) - 1)
    def _():
        s_out_ref[...] = s_ref[...].reshape(s_out_ref.shape)


def gdn_seq(proj, prm, ns):
    b, l, _ = proj.shape
    c = CHUNK
    return pl.pallas_call(
        _gdn_seq_kernel,
        grid=(b // ns, l // c),
        in_specs=[pl.BlockSpec((ns, c, GDN_W), lambda b, t: (b, t, GDN_OFF // GDN_W))] + [_const_spec(x, 2) for x in prm],
        out_specs=[pl.BlockSpec((ns, c, GDN_D), lambda b, t: (b, t, 0)),
                   pl.BlockSpec((ns, GDN_H, GDN_HD, GDN_HD), lambda b, t: (b, 0, 0, 0))],
        out_shape=[jax.ShapeDtypeStruct((b, l, GDN_D), BF16),
                   jax.ShapeDtypeStruct((b, GDN_H, GDN_HD, GDN_HD), F32)],
        scratch_shapes=[pltpu.VMEM((ns * GDN_H, GDN_HD, GDN_HD), F32),
                        pltpu.VMEM((ns, 8, 3 * GDN_D), F32)],
        compiler_params=_cp(("arbitrary", "arbitrary")),
        name="gdn_seq",
    )(proj, *_args(prm))


def _gdn_step_kernel(nb, p_ref, cs_ref, s_in_ref, cw_ref, alog_ref, dtb_ref, nw_ref, y_ref, s_out_ref):
    x = p_ref[...]
    qkv = x[:, 0:3 * GDN_D]
    z = x[:, 3 * GDN_D:4 * GDN_D]
    ab = x[:, 4 * GDN_D:4 * GDN_D + 128]
    cs = cs_ref[...]
    cw = cw_ref[...]
    w3 = 3 * GDN_D
    conv = cw[0:1] * cs[:, 0:w3] + cw[1:2] * cs[:, w3:2 * w3] + cw[2:3] * cs[:, 2 * w3:3 * w3] + cw[3:4] * qkv
    qs, ks, v, g, beta = _gdn_pre(conv, z, ab, alog_ref, dtb_ref)
    nw = nw_ref[...]
    pad = jnp.zeros((GDN_HD - nb, GDN_HD), F32)
    ks_t = [jnp.concatenate([ks[h], pad], axis=0).T for h in range(GDN_H)]
    qs_t = [jnp.concatenate([qs[h], pad], axis=0).T for h in range(GDN_H)]
    for i in range(nb):
        for h in range(GDN_H):
            k_c = ks_t[h][:, i:i + 1]
            q_c = qs_t[h][:, i:i + 1]
            eg = jnp.exp(g[i:i + 1, h:h + 1])
            b_s = beta[i:i + 1, GDN_H + h:GDN_H + h + 1]
            s = s_in_ref[i, h]
            sk = jnp.sum(k_c * s, axis=0, keepdims=True)
            v_row = v[i:i + 1, h * GDN_HD:(h + 1) * GDN_HD]
            s_new = s * eg + (k_c * b_s) * (v_row - eg * sk)
            s_out_ref[i, h] = s_new
            o = jnp.sum(q_c * s_new, axis=0, keepdims=True) * (GDN_HD ** -0.5)
            z_h = z[i:i + 1, h * GDN_HD:(h + 1) * GDN_HD]
            y_ref[i:i + 1, h * GDN_HD:(h + 1) * GDN_HD] = _gdn_post(o, z_h, nw).astype(BF16)


def gdn_step(proj, conv_state, state, layer, prm, nb):
    b = proj.shape[0]
    blk = (nb, GDN_H, GDN_HD, GDN_HD)
    return pl.pallas_call(
        functools.partial(_gdn_step_kernel, nb),
        grid=(b // nb,),
        in_specs=[pl.BlockSpec((nb, GDN_W), lambda i: (i, GDN_OFF // GDN_W)),
                  pl.BlockSpec((nb, 9 * GDN_D), lambda i: (i, 0)),
                  pl.BlockSpec((None,) + blk, lambda i: (layer, i, 0, 0, 0))] + [_const_spec(x, 1) for x in prm],
        out_specs=[pl.BlockSpec((nb, GDN_D), lambda i: (i, 0)),
                   pl.BlockSpec(blk, lambda i: (i, 0, 0, 0))],
        out_shape=[jax.ShapeDtypeStruct((b, GDN_D), BF16),
                   jax.ShapeDtypeStruct(state.shape[1:], F32)],
        compiler_params=_cp(("arbitrary",)),
        name="gdn_step",
    )(proj, conv_state, state, *_args(prm))


def _lru_gates(xc, gate, wr_ref, br_ref, wi_ref, bi_ref, lam_ref):
    r = jax.nn.sigmoid(_bdot(xc, wr_ref[...]) + br_ref[...])
    i = jax.nn.sigmoid(_bdot(xc, wi_ref[...]) + bi_ref[...])
    log_a = -LRU_C * r * jax.nn.softplus(-lam_ref[...])
    a = jnp.exp(log_a)
    one_minus_a2 = -jnp.tanh(log_a) * (a * a + 1.0)
    b = jnp.sqrt(one_minus_a2) * (i * xc)
    return a, b, _gelu(gate)


def _lru_seq_kernel(p_ref, cw_ref, cb_ref, wr_ref, br_ref, wi_ref, bi_ref, lam_ref, y_ref, h_out_ref,
                    prev_ref, carry_ref):
    ns, tc = p_ref.shape[0], p_ref.shape[1]
    n = ns * tc
    t = pl.program_id(0)

    @pl.when(t == 0)
    def _():
        prev_ref[...] = jnp.zeros_like(prev_ref)
        carry_ref[...] = jnp.zeros_like(carry_ref)

    p_fwd = _interleave_perm(ns, tc).astype(BF16)
    x_hi, x_lo = _split(p_ref[...].reshape(n, LRU_W))
    x = _dot(p_fwd, x_hi) + _dot(p_fwd, x_lo)
    xr = x[:, :LRU_D]
    gate = x[:, LRU_D:]
    halo = prev_ref.shape[0]
    ext = jnp.concatenate([prev_ref[...], xr], axis=0)
    prev_ref[...] = xr[n - halo:n]
    cw = cw_ref[...]
    xc = cw[3:4] * xr + cb_ref[...]
    for j in (1, 2, 3):
        xc = xc + cw[3 - j:4 - j] * pltpu.roll(ext, j * ns, axis=0)[halo:]
    a, b, gg = _lru_gates(xc, gate, wr_ref, br_ref, wi_ref, bi_ref, lam_ref)
    first = _iota((2 * ns, 1), 0) < ns
    h_prev = carry_ref[...]
    hs = []
    for s in range(n // (2 * ns)):
        a_s = a[s * 2 * ns:(s + 1) * 2 * ns]
        b_s = b[s * 2 * ns:(s + 1) * 2 * ns]
        h_a = a_s * pltpu.roll(h_prev, ns, axis=0) + b_s
        h_prev = jnp.where(first, h_a, a_s * pltpu.roll(h_a, ns, axis=0) + b_s)
        hs.append(h_prev)
    carry_ref[...] = h_prev
    y_t = (jnp.concatenate(hs, axis=0) * gg).astype(BF16)
    y_ref[...] = _dot_tn(p_fwd, y_t).astype(BF16).reshape(ns, tc, LRU_D)

    @pl.when(t == pl.num_programs(0) - 1)
    def _():
        h_out_ref[:, 0, :] = h_prev[ns:2 * ns]


def lru_seq(proj, prm):
    b, l, _ = proj.shape
    assert 2 * b == 8, "the time-major scan packs two time steps of all sequences into one 8-row sublane tile"
    tc = CHUNK
    return pl.pallas_call(
        _lru_seq_kernel,
        grid=(l // tc,),
        in_specs=[pl.BlockSpec((b, tc, LRU_W), lambda t: (0, t, LRU_OFF // LRU_W))] + [_const_spec(x, 1) for x in prm],
        out_specs=[pl.BlockSpec((b, tc, LRU_D), lambda t: (0, t, 0)),
                   pl.BlockSpec((b, 1, LRU_D), lambda t: (0, 0, 0))],
        out_shape=[jax.ShapeDtypeStruct((b, l, LRU_D), BF16),
                   jax.ShapeDtypeStruct((b, 1, LRU_D), F32)],
        scratch_shapes=[pltpu.VMEM((4 * b, LRU_D), F32), pltpu.VMEM((2 * b, LRU_D), F32)],
        compiler_params=_cp(("arbitrary",)),
        name="lru_seq",
    )(proj, *_args(prm))


def _lru_step_kernel(p_ref, cs_ref, h0_ref, cw_ref, cb_ref, wr_ref, br_ref, wi_ref, bi_ref, lam_ref, y_ref, h_ref):
    x = p_ref[...]
    xr = x[:, :LRU_D]
    gate = x[:, LRU_D:]
    cs = cs_ref[...]
    cw = cw_ref[...]
    xc = (cw[0:1] * cs[:, 0:LRU_D] + cw[1:2] * cs[:, LRU_D:2 * LRU_D] + cw[2:3] * cs[:, 2 * LRU_D:3 * LRU_D]
          + cw[3:4] * xr + cb_ref[...])
    a, b, gg = _lru_gates(xc, gate, wr_ref, br_ref, wi_ref, bi_ref, lam_ref)
    h = a * h0_ref[...] + b
    h_ref[...] = h
    y_ref[...] = (h * gg).astype(BF16)


def lru_step(proj, conv_state, h0, prm):
    b = proj.shape[0]
    return pl.pallas_call(
        _lru_step_kernel,
        grid=(1,),
        in_specs=[pl.BlockSpec((b, LRU_W), lambda i: (0, LRU_OFF // LRU_W)),
                  _const_spec(conv_state, 1), _const_spec(h0, 1)] + [_const_spec(x, 1) for x in prm],
        out_specs=[pl.BlockSpec((b, LRU_D), lambda i: (0, 0)),
                   pl.BlockSpec((b, LRU_D), lambda i: (0, 0))],
        out_shape=[jax.ShapeDtypeStruct((b, LRU_D), BF16),
                   jax.ShapeDtypeStruct((b, LRU_D), F32)],
        compiler_params=_cp(("arbitrary",)),
        name="lru_step",
    )(proj, conv_state, h0, *_args(prm))


def _pad_cols(x, width):
    return jnp.pad(x, [(0, 0)] * (x.ndim - 1) + [(0, width - x.shape[-1])])


def _pack_rwkv_cols(x):
    return jnp.concatenate([x[..., :1536], _pad_cols(x[..., 1536:1632], 128), _pad_cols(x[..., 1632:1728], 128),
                            x[..., 1728:1984]], axis=-1)


def _unpack_rwkv_cols(x):
    return jnp.concatenate([x[..., :1536], x[..., 1536:1632], x[..., 1664:1760], x[..., 1792:2048]], axis=-1)


def _pack_w_in_kernel(w_ref, o_ref):
    o_ref[...] = jnp.zeros(o_ref.shape, BF16)
    o = RWKV_PROJ
    for dst, a, b in ((0, 0, 1536), (1536, 1536, 1632), (1664, 1632, 1728), (1792, 1728, o), (S5_OFF, o, o + S5_D),
                      (GDN_OFF, o + S5_D, o + S5_D + GDN_PROJ), (LRU_OFF, o + S5_D + GDN_PROJ, o + S5_D + GDN_PROJ + 2 * LRU_D)):
        o_ref[0, :, dst:dst + (b - a)] = w_ref[0, :, a:b].astype(BF16)


def pack_w_in(w, tr):
    depth, d, n = w.shape
    return pl.pallas_call(
        _pack_w_in_kernel,
        grid=(depth, d // tr),
        in_specs=[pl.BlockSpec((1, tr, n), lambda l, i: (l, i, 0))],
        out_specs=pl.BlockSpec((1, tr, NP_IN), lambda l, i: (l, i, 0)),
        out_shape=jax.ShapeDtypeStruct((depth, d, NP_IN), BF16),
        compiler_params=_cp(("arbitrary", "arbitrary")),
        name="pack_w_in",
    )(w)


def _block_diag(blocks):
    *lead, g, a, b = blocks.shape
    eye = jnp.eye(g, dtype=blocks.dtype)
    return (eye[:, None, :, None] * blocks[..., :, :, None, :]).reshape(*lead, g * a, g * b)


def _s5_params(lam_re, lam_im, log_dt, b_re, b_im, c_re, c_im, d, glu_w, glu_b):
    depth = lam_re.shape[0]
    dt = jnp.exp(log_dt)[..., None]
    mag = jnp.exp(lam_re * dt)
    ab_re, ab_im = mag * jnp.cos(lam_im * dt), mag * jnp.sin(lam_im * dt)
    den = lam_re * lam_re + lam_im * lam_im
    nr = ab_re - 1.0
    f_re = (nr * lam_re + ab_im * lam_im) / den
    f_im = (ab_im * lam_re - nr * lam_im) / den
    bb_re = f_re[..., None] * b_re - f_im[..., None] * b_im
    bb_im = f_re[..., None] * b_im + f_im[..., None] * b_re
    macro = lambda x: _block_diag(jnp.swapaxes(x, -1, -2).reshape(depth, S5_CG, S5_G // S5_CG, x.shape[-1], x.shape[-2]))
    bbr = macro(bb_re)
    bbi = macro(bb_im)
    tail = (macro(c_re).astype(BF16), macro(-c_im).astype(BF16),
            ab_re.reshape(depth, 1, S5_STATE), ab_im.reshape(depth, 1, S5_STATE), d.reshape(depth, 1, S5_D),
            glu_w, glu_b.reshape(depth, 1, S5_D))
    return (bbr.astype(BF16), bbi.astype(BF16)) + tail, (bbr, bbi) + tail


def _prepare_params(w):
    depth = w['w_in'].shape[0]
    row = lambda v: v.reshape(depth, 1, -1)
    pad_rows = lambda v: jnp.pad(v, ((0, 0), (0, 128 - v.shape[1]), (0, 0)))
    s5, s5_f32 = _s5_params(w['s5_lambda_re'], w['s5_lambda_im'], w['s5_log_dt'], w['s5_b_re'], w['s5_b_im'],
                            w['s5_c_re'], w['s5_c_im'], w['s5_d'], w['s5_glu_w'], w['s5_glu_b'])
    return {
        'norm1_g': row(w['norm1_g']), 'norm2_g': row(w['norm2_g']),
        'w_in': pack_w_in(w['w_in'], 512),
        'rwkv': (row(_pack_rwkv_cols(w['rwkv_mu'])), row(w['rwkv_w0']), pad_rows(w['rwkv_w_up']), row(w['rwkv_a0']),
                 pad_rows(w['rwkv_a_up']), w['rwkv_g_up'], row(w['rwkv_k_k']), row(w['rwkv_k_a']), row(w['rwkv_r_k']),
                 row(w['rwkv_ln_w']), row(w['rwkv_ln_b'])),
        's5': s5, 's5_f32': s5_f32,
        'gdn': (w['gdn_conv_w'], _pad_cols(row(w['gdn_a_log']), 128), _pad_cols(row(w['gdn_dt_bias']), 128),
                row(w['gdn_norm_w'])),
        'lru': (w['lru_conv_w'], row(w['lru_conv_b']), _block_diag(w['lru_wr']), row(w['lru_br']),
                _block_diag(w['lru_wi']), row(w['lru_bi']), row(w['lru_lambda'])),
        'ffn_conv_w': w['ffn_conv_w'], 'ffn_conv_b': row(w['ffn_conv_b']),
    }


def _layer(xp, xs, mod, st, p, seg, layer, stacked):
    bp, lp, d = xp.shape
    bs = xs.shape[1]
    sh1, sc1, g1, sh2, sc2, g2 = jnp.split(mod, 6, axis=-1)
    pm = lambda m: m[:bp, None, :]
    sm = lambda m: m[None, bp:bp + bs, :]

    lay = lambda x: Layered(x, layer)
    rw_prm, s5_prm, s5_prm_f32, gdn_prm, lru_prm = (tuple(lay(x) for x in p[k])
                                                    for k in ('rwkv', 's5', 's5_f32', 'gdn', 'lru'))

    h = norm_mod(xp, lay(p['norm1_g']), pm(sc1), pm(sh1), 512)
    proj = matmul_in(h, p['w_in'], layer, 1024, 1024)
    ya, s_wkv = rwkv_seq(proj, rw_prm, seg, bp)
    yb, s_re, s_im = s5_seq(proj, s5_prm)
    yc, s_gdn = gdn_seq(proj, gdn_prm, bp)
    yd, s_lru = lru_seq(proj, lru_prm)
    xp = matmul_res([ya, yb, yc, yd], stacked['w_out'], layer, xp, pm(g1), lp, 512)
    h = norm_mod(xp, lay(p['norm2_g']), pm(sc2), pm(sh2), 512)
    act, lg, lv = ffn_up_seq(h, stacked['ffn_w_up'], layer, p['ffn_conv_w'], p['ffn_conv_b'], 256)
    xp = matmul_res([act], stacked['ffn_w_down'], layer, xp, pm(g2), 1024, 256)
    new_p = {
        'rwkv_wkv': s_wkv,
        'rwkv_shift': _unpack_rwkv_cols(proj[:, -1, :RW_W]),
        's5_re': s_re.reshape(bp, S5_G, S5_N), 's5_im': s_im.reshape(bp, S5_G, S5_N),
        'gdn': s_gdn,
        'gdn_conv': proj[:, -3:, GDN_OFF:GDN_OFF + 3 * GDN_D],
        'lru_h': s_lru.reshape(bp, LRU_D),
        'lru_conv': proj[:, -3:, LRU_OFF:LRU_OFF + LRU_D],
        'ffn_conv': jnp.concatenate([lg, lv], axis=-1),
    }

    h = norm_mod(xs, lay(p['norm1_g']), sm(sc1), sm(sh1), bs)
    proj = matmul_in(h, p['w_in'], layer, bs, 1024)[0]
    ya, s_wkv = rwkv_step(proj, _pack_rwkv_cols(st['rwkv_shift']), stacked['rwkv_wkv'], layer, rw_prm, seg)
    yb, s_re, s_im = s5_step(proj, st['s5_re'].reshape(bs, S5_STATE), st['s5_im'].reshape(bs, S5_STATE), s5_prm_f32)
    yc, s_gdn = gdn_step(proj, st['gdn_conv'].reshape(bs, 9 * GDN_D), stacked['gdn'], layer, gdn_prm, 16)
    yd, s_lru = lru_step(proj, st['lru_conv'].reshape(bs, 3 * LRU_D), st['lru_h'], lru_prm)
    xs = matmul_res([y[None] for y in (ya, yb, yc, yd)], stacked['w_out'], layer, xs, sm(g1), bs, 512)
    h = norm_mod(xs, lay(p['norm2_g']), sm(sc2), sm(sh2), bs)
    act, ug, uv = ffn_up_step(h[0], stacked['ffn_w_up'], layer, p['ffn_conv_w'], p['ffn_conv_b'],
                              stacked['ffn_conv'], 256)
    xs = matmul_res([act[None]], stacked['ffn_w_down'], layer, xs, sm(g2), bs, 256)
    shift_rows = lambda buf, new: jnp.concatenate([buf[:, 1:], new[:, None, :]], axis=1)
    new_s = {
        'rwkv_wkv': s_wkv,
        'rwkv_shift': _unpack_rwkv_cols(proj[:, :RW_W]),
        's5_re': s_re.reshape(bs, S5_G, S5_N), 's5_im': s_im.reshape(bs, S5_G, S5_N),
        'gdn': s_gdn,
        'gdn_conv': shift_rows(st['gdn_conv'], proj[:, GDN_OFF:GDN_OFF + 3 * GDN_D]),
        'lru_h': s_lru,
        'lru_conv': shift_rows(st['lru_conv'], proj[:, LRU_OFF:LRU_OFF + LRU_D]),
        'ffn_conv': shift_rows(st['ffn_conv'], jnp.concatenate([ug, uv], axis=-1)),
    }
    return xp, xs, new_p, new_s


STATE_ORDER = ('rwkv_wkv', 'rwkv_shift', 's5_re', 's5_im', 'gdn', 'gdn_conv', 'lru_h', 'lru_conv', 'ffn_conv')


def kernel(x_prompt, x_sample, c_prompt, c_sample, state_rwkv_wkv, state_rwkv_shift, state_s5_re, state_s5_im, state_gdn, state_gdn_conv, state_lru_h, state_lru_conv, state_ffn_conv, ada_w, ada_b, norm1_g, norm2_g, final_g, w_in, w_out, rwkv_mu, rwkv_w0, rwkv_w_up, rwkv_a0, rwkv_a_up, rwkv_g_up, rwkv_k_k, rwkv_k_a, rwkv_r_k, rwkv_ln_w, rwkv_ln_b, s5_lambda_re, s5_lambda_im, s5_log_dt, s5_b_re, s5_b_im, s5_c_re, s5_c_im, s5_d, s5_glu_w, s5_glu_b, gdn_conv_w, gdn_a_log, gdn_dt_bias, gdn_norm_w, lru_conv_w, lru_conv_b, lru_wr, lru_br, lru_wi, lru_bi, lru_lambda, ffn_w_up, ffn_conv_w, ffn_conv_b, ffn_w_down):
    weights = {
        'norm1_g': norm1_g, 'norm2_g': norm2_g, 'w_in': w_in,
        'rwkv_mu': rwkv_mu, 'rwkv_w0': rwkv_w0, 'rwkv_w_up': rwkv_w_up, 'rwkv_a0': rwkv_a0,
        'rwkv_a_up': rwkv_a_up, 'rwkv_g_up': rwkv_g_up, 'rwkv_k_k': rwkv_k_k, 'rwkv_k_a': rwkv_k_a,
        'rwkv_r_k': rwkv_r_k, 'rwkv_ln_w': rwkv_ln_w, 'rwkv_ln_b': rwkv_ln_b,
        's5_lambda_re': s5_lambda_re, 's5_lambda_im': s5_lambda_im, 's5_log_dt': s5_log_dt,
        's5_b_re': s5_b_re, 's5_b_im': s5_b_im, 's5_c_re': s5_c_re, 's5_c_im': s5_c_im, 's5_d': s5_d,
        's5_glu_w': s5_glu_w, 's5_glu_b': s5_glu_b,
        'gdn_conv_w': gdn_conv_w, 'gdn_a_log': gdn_a_log, 'gdn_dt_bias': gdn_dt_bias, 'gdn_norm_w': gdn_norm_w,
        'lru_conv_w': lru_conv_w, 'lru_conv_b': lru_conv_b, 'lru_wr': lru_wr, 'lru_br': lru_br,
        'lru_wi': lru_wi, 'lru_bi': lru_bi, 'lru_lambda': lru_lambda,
        'ffn_conv_w': ffn_conv_w, 'ffn_conv_b': ffn_conv_b,
    }
    stacked = {'w_out': w_out, 'ffn_w_up': ffn_w_up, 'ffn_w_down': ffn_w_down,
               'rwkv_wkv': jnp.transpose(state_rwkv_wkv, (0, 2, 3, 4, 1)),
               'gdn': state_gdn, 'ffn_conv': state_ffn_conv}
    cache = {'rwkv_wkv': state_rwkv_wkv, 'rwkv_shift': state_rwkv_shift, 's5_re': state_s5_re,
             's5_im': state_s5_im, 'gdn': state_gdn, 'gdn_conv': state_gdn_conv, 'lru_h': state_lru_h,
             'lru_conv': state_lru_conv, 'ffn_conv': state_ffn_conv}
    depth = ada_w.shape[0]
    bp = x_prompt.shape[0]
    bs = x_sample.shape[0]
    rows = -(-(bp + bs) // 8) * 8
    c_all = jnp.pad(jnp.concatenate([c_prompt, c_sample], axis=0), ((0, rows - bp - bs), (0, 0)))
    mod = ada_mod(c_all, ada_w, ada_b)
    ids = jnp.arange(RWKV_D) // RWKV_HD
    seg = (ids[:, None] == ids[None, :]).astype(BF16)

    xp = x_prompt
    xs = jnp.swapaxes(x_sample, 0, 1)
    new_p = {n: [] for n in STATE_ORDER}
    new_s = {n: [] for n in STATE_ORDER}
    p = _prepare_params(weights)
    for l in range(depth):
        st = {n: cache[n][l] for n in STATE_ORDER}
        xp, xs, sp, ss = _layer(xp, xs, mod[l], st, p, seg, l, stacked)
        for n in STATE_ORDER:
            new_p[n].append(sp[n])
            new_s[n].append(ss[n])
    y_prompt = final_norm(xp, final_g, 512)
    y_sample = jnp.swapaxes(final_norm(xs, final_g, bs), 0, 1)
    outs_p = tuple(jnp.stack(new_p[n], axis=0) for n in STATE_ORDER)
    stacked_s = {n: jnp.stack(new_s[n], axis=0) for n in STATE_ORDER}
    stacked_s['rwkv_wkv'] = jnp.transpose(stacked_s['rwkv_wkv'], (0, 4, 1, 2, 3))
    outs_s = tuple(stacked_s[n] for n in STATE_ORDER)
    return (y_prompt, y_sample) + outs_p + outs_s
```

```python
import functools
import math
from typing import NamedTuple

import jax
import jax.numpy as jnp
from jax import lax
from jax.experimental import pallas as pl
from jax.experimental.pallas import tpu as pltpu

F32 = jnp.float32
BF16 = jnp.bfloat16
HI = lax.Precision.HIGHEST

NORM_EPS = 1e-6
RWKV_LN_EPS = 64e-5
LRU_C = 8.0

D_MODEL = 2048
RWKV_D, RWKV_HD, RWKV_H = 512, 64, 8
RWKV_R_DECAY, RWKV_R_A, RWKV_R_GATE = 96, 96, 256
RWKV_PROJ = 3 * RWKV_D + RWKV_R_DECAY + RWKV_R_A + RWKV_R_GATE
S5_D, S5_CH, S5_G, S5_N = 512, 16, 32, 64
S5_STATE = S5_G * S5_N
GDN_D, GDN_HD, GDN_H = 512, 128, 4
GDN_PROJ = 4 * GDN_D + 2 * GDN_H
LRU_D, LRU_BLOCKS = 512, 8
D_FF = 5632

RW_W = 2048
S5_OFF = 2048
GDN_OFF, GDN_W = 2560, 2560
LRU_OFF, LRU_W = 5120, 1024
NP_IN = 6144

CHUNK = 64
RWKV_GH = 4
assert CHUNK == RWKV_HD
VMEM_LIMIT = 56 * 1024 * 1024


def _cp(sem):
    return pltpu.CompilerParams(dimension_semantics=sem, vmem_limit_bytes=VMEM_LIMIT)


def _dot(a, b, prec=None):
    return jnp.dot(a, b, preferred_element_type=F32, precision=prec)


def _dot_nt(a, b, prec=None):
    return lax.dot_general(a, b, (((1,), (1,)), ((), ())), preferred_element_type=F32, precision=prec)


def _dot_tn(a, b, prec=None):
    return lax.dot_general(a, b, (((0,), (0,)), ((), ())), preferred_element_type=F32, precision=prec)


def _bdot(a, b):
    return _dot(a.astype(BF16), b.astype(BF16))


def _silu(x):
    return x * jax.nn.sigmoid(x)


def _gelu(x):
    return 0.5 * x * (1.0 + jnp.tanh(math.sqrt(2.0 / math.pi) * (x + 0.044715 * (x * x * x))))


def _seg_dot(x, seg):
    hi = x.astype(BF16)
    lo = (x - hi.astype(F32)).astype(BF16)
    n = x.shape[0]
    r = _dot(jnp.concatenate([hi, lo], axis=0), seg)
    return r[:n] + r[n:]


def _iota(shape, dim):
    return lax.broadcasted_iota(jnp.int32, shape, dim)


class Layered(NamedTuple):
    arr: jax.Array
    layer: int


def _const_spec(x, nidx):
    if isinstance(x, Layered):
        shape = (None,) + x.arr.shape[1:]
        idx = (x.layer,) + (0,) * (x.arr.ndim - 1)
    else:
        shape = x.shape
        idx = (0,) * x.ndim
    if nidx == 1:
        return pl.BlockSpec(shape, lambda i: idx)
    return pl.BlockSpec(shape, lambda b, t: idx)


def _args(params):
    return [x.arr if isinstance(x, Layered) else x for x in params]


def _neumann_solve(n, rhs, steps):
    x = rhs
    p = n
    for i in range(steps):
        x = x + _bdot(p, x)
        if i + 1 < steps:
            p = _bdot(p, p)
    return x


def _split(x):
    hi = x.astype(BF16)
    return hi, (x - hi.astype(F32)).astype(BF16)


def _dot3(a, b):
    ah, al = _split(a)
    bh, bl = _split(b)
    return _dot(ah, bh) + (_dot(ah, bl) + _dot(al, bh))


def _neumann_solve3(n, rhs, steps):
    x = rhs
    p = n
    for i in range(steps):
        x = x + _dot3(p, x)
        if i + 1 < steps:
            p = _dot3(p, p)
    return x


def _row_to_col(row, eye):
    n = eye.shape[0]
    return jnp.sum(jnp.where(eye, jnp.broadcast_to(row, (n, n)), 0.0), axis=-1, keepdims=True)


def _ada_kernel(c_ref, w_ref, b_ref, o_ref):
    c = c_ref[...]
    o_ref[0] = _dot(_silu(c).astype(BF16), w_ref[0].astype(BF16)) + b_ref[0]


def ada_mod(c_all, ada_w, ada_b):
    depth, d, n = ada_w.shape
    r = c_all.shape[0]
    tn = 1024
    return pl.pallas_call(
        _ada_kernel,
        grid=(depth, n // tn),
        in_specs=[pl.BlockSpec((r, d), lambda l, j: (0, 0)),
                  pl.BlockSpec((1, d, tn), lambda l, j: (l, 0, j)),
                  pl.BlockSpec((1, 1, tn), lambda l, j: (l, 0, j))],
        out_specs=pl.BlockSpec((1, r, tn), lambda l, j: (l, 0, j)),
        out_shape=jax.ShapeDtypeStruct((depth, r, n), F32),
        compiler_params=_cp(("arbitrary", "arbitrary")),
        name="ada_mod",
    )(c_all, ada_w, ada_b.reshape(depth, 1, n))


def _norm_mod_kernel(x_ref, g_ref, sc_ref, sh_ref, o_ref):
    x = x_ref[0]
    h = x * lax.rsqrt(jnp.mean(x * x, axis=-1, keepdims=True) + NORM_EPS) * g_ref[...]
    o_ref[0] = (h * (1.0 + sc_ref[0]) + sh_ref[0]).astype(o_ref.dtype)


def _mod_spec(arr, tm):
    d = arr.shape[-1]
    if arr.shape[1] == 1:
        return pl.BlockSpec((1, 1, d), lambda g, i: (g, 0, 0))
    return pl.BlockSpec((1, tm, d), lambda g, i: (g, i, 0))


def norm_mod(x, gain, sc, sh, tm):
    g, r, d = x.shape
    return pl.pallas_call(
        _norm_mod_kernel,
        grid=(g, r // tm),
        in_specs=[pl.BlockSpec((1, tm, d), lambda g, i: (g, i, 0)),
                  _const_spec(gain, 2),
                  _mod_spec(sc, tm), _mod_spec(sh, tm)],
        out_specs=pl.BlockSpec((1, tm, d), lambda g, i: (g, i, 0)),
        out_shape=jax.ShapeDtypeStruct((g, r, d), BF16),
        compiler_params=_cp(("arbitrary", "arbitrary")),
        name="norm_mod",
    )(x, *_args([gain]), sc, sh)


def _rms_kernel(x_ref, g_ref, o_ref):
    x = x_ref[0]
    o_ref[0] = x * lax.rsqrt(jnp.mean(x * x, axis=-1, keepdims=True) + NORM_EPS) * g_ref[...]


def final_norm(x, gain, tm):
    g, r, d = x.shape
    return pl.pallas_call(
        _rms_kernel,
        grid=(g, r // tm),
        in_specs=[pl.BlockSpec((1, tm, d), lambda g, i: (g, i, 0)),
                  pl.BlockSpec((1, d), lambda g, i: (0, 0))],
        out_specs=pl.BlockSpec((1, tm, d), lambda g, i: (g, i, 0)),
        out_shape=jax.ShapeDtypeStruct((g, r, d), F32),
        compiler_params=_cp(("arbitrary", "arbitrary")),
        name="final_norm",
    )(x, gain.reshape(1, d))


def _mm_kernel(a_ref, w_ref, o_ref):
    o_ref[0] = _dot_nt(a_ref[0], w_ref[...])


def matmul_in(a, w_t, layer, tm, tn):
    g, r, k = a.shape
    n = w_t.shape[1]
    return pl.pallas_call(
        _mm_kernel,
        grid=(g, r // tm, n // tn),
        in_specs=[pl.BlockSpec((1, tm, k), lambda g, i, j: (g, i, 0)),
                  pl.BlockSpec((None, tn, k), lambda g, i, j: (layer, j, 0))],
        out_specs=pl.BlockSpec((1, tm, tn), lambda g, i, j: (g, i, j)),
        out_shape=jax.ShapeDtypeStruct((g, r, n), F32),
        compiler_params=_cp(("arbitrary", "arbitrary", "arbitrary")),
        name="matmul_in",
    )(a, w_t)


def _mm_res_kernel(n_a, *refs):
    a_refs = refs[:n_a]
    w_refs = refs[n_a:2 * n_a]
    x_ref, g_ref, o_ref = refs[2 * n_a:]
    acc = _dot(a_refs[0][0], w_refs[0][...].astype(BF16))
    for a_ref, w_ref in zip(a_refs[1:], w_refs[1:]):
        acc = acc + _dot(a_ref[0], w_ref[...].astype(BF16))
    o_ref[0] = x_ref[0] + g_ref[0] * acc


def matmul_res(a_list, w, layer, x, gate, tm, tn):
    g, r, n = x.shape
    n_a = len(a_list)
    k = a_list[0].shape[-1]
    a_specs = [pl.BlockSpec((1, tm, k), lambda g, i, j: (g, i, 0)) for _ in a_list]
    w_specs = [pl.BlockSpec((None, k, tn), functools.partial(lambda g, i, j, q: (layer, q, j), q=q))
               for q in range(n_a)]
    if gate.shape[1] == 1:
        g_spec = pl.BlockSpec((1, 1, tn), lambda g, i, j: (g, 0, j))
    else:
        g_spec = pl.BlockSpec((1, tm, tn), lambda g, i, j: (g, i, j))
    return pl.pallas_call(
        functools.partial(_mm_res_kernel, n_a),
        grid=(g, r // tm, n // tn),
        in_specs=a_specs + w_specs + [pl.BlockSpec((1, tm, tn), lambda g, i, j: (g, i, j)), g_spec],
        out_specs=pl.BlockSpec((1, tm, tn), lambda g, i, j: (g, i, j)),
        out_shape=jax.ShapeDtypeStruct((g, r, n), F32),
        compiler_params=_cp(("arbitrary", "arbitrary", "arbitrary")),
        name="matmul_res",
    )(*a_list, *([w] * n_a), x, gate)


def _ffn_up_seq_kernel(a_ref, wg_ref, wv_ref, cwg_ref, cwv_ref, cbg_ref, cbv_ref, act_ref, lg_ref, lv_ref):
    a = a_ref[0]
    rows = a.shape[0]
    rid8 = _iota((8, 1), 0)

    def conv(up, cw_ref, cb_ref):
        cw = cw_ref[...]
        return cw[0:1] * pltpu.roll(up, 2, axis=0) + cw[1:2] * pltpu.roll(up, 1, axis=0) + cw[2:3] * up + cb_ref[...]

    def head(up, cw_ref, cb_ref):
        cw = cw_ref[...]
        u8 = up[0:8]
        s1 = jnp.where(rid8 >= 1, pltpu.roll(u8, 1, axis=0), 0.0)
        s2 = jnp.where(rid8 >= 2, pltpu.roll(u8, 2, axis=0), 0.0)
        return cw[0:1] * s2 + cw[1:2] * s1 + cw[2:3] * u8 + cb_ref[...]

    up_g = _dot(a, wg_ref[...].astype(BF16))
    lg_ref[0] = up_g[rows - 2:rows]
    gate = conv(up_g, cwg_ref, cbg_ref)
    up_v = _dot(a, wv_ref[...].astype(BF16))
    lv_ref[0] = up_v[rows - 2:rows]
    val = conv(up_v, cwv_ref, cbv_ref)
    act_ref[0] = (_silu(gate) * val).astype(BF16)
    act_ref[0, 0:8, :] = (_silu(head(up_g, cwg_ref, cbg_ref)) * head(up_v, cwv_ref, cbv_ref)).astype(BF16)


def ffn_up_seq(h, w_up, layer, conv_w, conv_b, tn):
    b, l, d = h.shape
    nt = D_FF // tn
    return pl.pallas_call(
        _ffn_up_seq_kernel,
        grid=(b, nt),
        in_specs=[pl.BlockSpec((1, l, d), lambda b, j: (b, 0, 0)),
                  pl.BlockSpec((None, d, tn), lambda b, j: (layer, 0, j)),
                  pl.BlockSpec((None, d, tn), lambda b, j: (layer, 0, j + nt)),
                  pl.BlockSpec((None, 3, tn), lambda b, j: (layer, 0, j)),
                  pl.BlockSpec((None, 3, tn), lambda b, j: (layer, 0, j + nt)),
                  pl.BlockSpec((None, 1, tn), lambda b, j: (layer, 0, j)),
                  pl.BlockSpec((None, 1, tn), lambda b, j: (layer, 0, j + nt))],
        out_specs=[pl.BlockSpec((1, l, tn), lambda b, j: (b, 0, j)),
                   pl.BlockSpec((1, 2, tn), lambda b, j: (b, 0, j)),
                   pl.BlockSpec((1, 2, tn), lambda b, j: (b, 0, j))],
        out_shape=[jax.ShapeDtypeStruct((b, l, D_FF), BF16),
                   jax.ShapeDtypeStruct((b, 2, D_FF), F32),
                   jax.ShapeDtypeStruct((b, 2, D_FF), F32)],
        compiler_params=_cp(("arbitrary", "arbitrary")),
        name="ffn_up_seq",
    )(h, w_up, w_up, conv_w, conv_w, conv_b, conv_b)


def _ffn_up_step_kernel(a_ref, wg_ref, wv_ref, cwg_ref, cwv_ref, cbg_ref, cbv_ref,
                        stg_ref, stv_ref, act_ref, ug_ref, uv_ref):
    a = a_ref[...]
    up_g = _dot(a, wg_ref[...].astype(BF16))
    up_v = _dot(a, wv_ref[...].astype(BF16))
    ug_ref[...] = up_g
    uv_ref[...] = up_v
    cwg = cwg_ref[...]
    cwv = cwv_ref[...]
    gate = cwg[0:1] * stg_ref[:, 0, :] + cwg[1:2] * stg_ref[:, 1, :] + cwg[2:3] * up_g + cbg_ref[...]
    val = cwv[0:1] * stv_ref[:, 0, :] + cwv[1:2] * stv_ref[:, 1, :] + cwv[2:3] * up_v + cbv_ref[...]
    act_ref[...] = (_silu(gate) * val).astype(BF16)


def ffn_up_step(h, w_up, layer, conv_w, conv_b, conv_state, tn):
    b, d = h.shape
    nt = D_FF // tn
    return pl.pallas_call(
        _ffn_up_step_kernel,
        grid=(nt,),
        in_specs=[pl.BlockSpec((b, d), lambda j: (0, 0)),
                  pl.BlockSpec((None, d, tn), lambda j: (layer, 0, j)),
                  pl.BlockSpec((None, d, tn), lambda j: (layer, 0, j + nt)),
                  pl.BlockSpec((None, 3, tn), lambda j: (layer, 0, j)),
                  pl.BlockSpec((None, 3, tn), lambda j: (layer, 0, j + nt)),
                  pl.BlockSpec((None, 1, tn), lambda j: (layer, 0, j)),
                  pl.BlockSpec((None, 1, tn), lambda j: (layer, 0, j + nt)),
                  pl.BlockSpec((None, b, 2, tn), lambda j: (layer, 0, 0, j)),
                  pl.BlockSpec((None, b, 2, tn), lambda j: (layer, 0, 0, j + nt))],
        out_specs=[pl.BlockSpec((b, tn), lambda j: (0, j))] * 3,
        out_shape=[jax.ShapeDtypeStruct((b, D_FF), BF16),
                   jax.ShapeDtypeStruct((b, D_FF), F32),
                   jax.ShapeDtypeStruct((b, D_FF), F32)],
        compiler_params=_cp(("arbitrary",)),
        name="ffn_up_step",
    )(h, w_up, w_up, conv_w, conv_w, conv_b, conv_b, conv_state, conv_state)


def _rwkv_pre(p, prev, prm, seg):
    mu, w0, w_up, a0, a_up, g_up, k_k, k_a = prm
    xm = p + mu * (prev - p)
    r = xm[:, 0:512]
    k = xm[:, 512:1024]
    v = xm[:, 1024:1536]
    xw = xm[:, 1536:1664]
    xa = xm[:, 1664:1792]
    xg = xm[:, 1792:2048]
    log_w = -jnp.exp(-jax.nn.softplus(-(w0 + _bdot(jnp.tanh(xw), w_up))) - 0.5)
    a = jax.nn.sigmoid(a0 + _bdot(xa, a_up))
    g = _bdot(jax.nn.sigmoid(xg), g_up)
    kx = k * k_k
    kk = kx * lax.rsqrt(_seg_dot(kx * kx, seg) + 1e-12)
    k2 = k * (1.0 + (a - 1.0) * k_a)
    return r, k2, v, log_w, a, g, kk


def _rwkv_post(y, r, k2, v, g, r_k, ln_w, ln_b, seg):
    inv = 1.0 / RWKV_HD
    mean = _seg_dot(y, seg) * inv
    dlt = y - mean
    var = _seg_dot(dlt * dlt, seg) * inv
    yn = dlt * lax.rsqrt(var + RWKV_LN_EPS) * ln_w + ln_b
    bonus = _seg_dot(r * k2 * r_k, seg) * v
    return (yn + bonus) * g


def _rwkv_seq_kernel(p_ref, mu_ref, w0_ref, wup_ref, a0_ref, aup_ref, gup_ref, kk_ref, ka_ref, rk_ref,
                     lnw_ref, lnb_ref, seg_ref, y_ref, s_out_ref, s_ref, prev_ref):
    c = CHUNK
    ns = p_ref.shape[0]
    n = ns * c
    t = pl.program_id(1)

    @pl.when(t == 0)
    def _():
        s_ref[...] = jnp.zeros_like(s_ref)
        prev_ref[...] = jnp.zeros_like(prev_ref)

    p = p_ref[...].reshape(n, RW_W)
    seg = seg_ref[...]
    seq_rows = lambda x: jnp.concatenate([jnp.broadcast_to(x[i:i + 1], (c, x.shape[-1])) for i in range(ns)], axis=0)
    tloc = _iota((n, 1), 0) & (c - 1)
    prev = jnp.where(tloc == 0, seq_rows(prev_ref[...]), pltpu.roll(p, 1, axis=0))
    last_rows = lambda x: jnp.concatenate([x[(i + 1) * c - 1:(i + 1) * c] for i in range(ns)], axis=0)
    prev_ref[...] = last_rows(p)
    prm = (mu_ref[...], w0_ref[...], wup_ref[...], a0_ref[...], aup_ref[...], gup_ref[...], kk_ref[...], ka_ref[...])
    r, k2, v, log_w, a, g, kk = _rwkv_pre(p, prev, prm, seg)

    ri = _iota((n, n), 0)
    ci = _iota((n, n), 1)
    cshift = c.bit_length() - 1
    tril = ((lax.shift_right_logical(ri, cshift) == lax.shift_right_logical(ci, cshift)) & (ci <= ri)).astype(F32)
    cum = _dot(tril, log_w, HI)
    e_in = jnp.exp(cum)
    e_out = jnp.exp(-cum)
    e_ex = jnp.exp(cum - log_w)
    tot = last_rows(cum)
    e_tail = jnp.exp(seq_rows(tot) - cum)
    e_tot = jnp.exp(tot)
    alpha = -(a * kk)
    gw = RWKV_GH * RWKV_HD
    rr = _iota((gw, gw), 0)
    cc = _iota((gw, gw), 1)
    shift = RWKV_HD.bit_length() - 1
    own = lax.shift_right_logical(rr, shift) == lax.shift_right_logical(cc, shift)
    tri_strict = (cc & (c - 1)) < (rr & (c - 1))
    tri_incl = (cc & (c - 1)) <= (rr & (c - 1))
    eye = rr == cc

    ng = RWKV_H // RWKV_GH
    be_t = (kk * e_ex).astype(BF16)
    r_t = (r * e_in).astype(BF16)
    al_t = (alpha * e_out).astype(BF16)
    k_t = (k2 * e_out).astype(BF16)
    al_p = (alpha * e_tail).astype(BF16)
    k_p = (k2 * e_tail).astype(BF16)
    vb = v.astype(BF16)

    def expand(x, i, gi):
        xg = x[i * c:(i + 1) * c, gi * gw:(gi + 1) * gw]
        return jnp.where(own, jnp.concatenate([xg] * RWKV_GH, axis=0), jnp.zeros((), BF16))

    y_rows = []
    for i in range(ns):
        ys = []
        for gi in range(ng):
            s0 = s_ref[i * ng + gi]
            lhs = jnp.concatenate([expand(be_t, i, gi), expand(r_t, i, gi)], axis=0)
            rhs = jnp.concatenate([expand(al_t, i, gi), expand(k_t, i, gi)], axis=0)
            vv = expand(vb, i, gi)
            m1 = _dot_nt(lhs, rhs)
            m2 = _dot(lhs, s0.astype(BF16))
            l_a = jnp.where(tri_strict, m1[:gw, :gw], 0.0)
            l_k = jnp.where(tri_strict, m1[:gw, gw:], 0.0)
            u = _neumann_solve(l_a, m2[:gw] + _dot(l_k.astype(BF16), vv), cshift)
            uv = jnp.concatenate([u.astype(BF16), vv], axis=0)
            a_full = jnp.concatenate([jnp.where(tri_incl, m1[gw:, :gw], 0.0),
                                      jnp.where(tri_incl, m1[gw:, gw:], 0.0)], axis=1).astype(BF16)
            y_g = m2[gw:] + _dot(a_full, uv)
            ys.append(functools.reduce(lambda p_, q_: p_ + q_, [y_g[j * c:(j + 1) * c] for j in range(RWKV_GH)]))
            scale = _row_to_col(e_tot[i:i + 1, gi * gw:(gi + 1) * gw], eye)
            tails = jnp.concatenate([expand(al_p, i, gi), expand(k_p, i, gi)], axis=0)
            s_ref[i * ng + gi] = s0 * scale + _dot_tn(tails, uv)
        y_rows.append(jnp.concatenate(ys, axis=1))

    y = jnp.concatenate(y_rows, axis=0)
    out = _rwkv_post(y, r, k2, v, g, rk_ref[...], lnw_ref[...], lnb_ref[...], seg)
    y_ref[...] = out.astype(BF16).reshape(ns, c, RWKV_D)

    @pl.when(t == pl.num_programs(1) - 1)
    def _():
        for i in range(ns):
            for h in range(RWKV_H):
                gi, hh = divmod(h, RWKV_GH)
                s_out_ref[i, h] = s_ref[i * ng + gi, hh * RWKV_HD:(hh + 1) * RWKV_HD, hh * RWKV_HD:(hh + 1) * RWKV_HD]


def rwkv_seq(proj, prm, seg, ns):
    b, l, _ = proj.shape
    c = CHUNK
    gw = RWKV_GH * RWKV_HD
    return pl.pallas_call(
        _rwkv_seq_kernel,
        grid=(b // ns, l // c),
        in_specs=[pl.BlockSpec((ns, c, RW_W), lambda b, t: (b, t, 0))] + [_const_spec(x, 2) for x in prm] + [_const_spec(seg, 2)],
        out_specs=[pl.BlockSpec((ns, c, RWKV_D), lambda b, t: (b, t, 0)),
                   pl.BlockSpec((ns, RWKV_H, RWKV_HD, RWKV_HD), lambda b, t: (b, 0, 0, 0))],
        out_shape=[jax.ShapeDtypeStruct((b, l, RWKV_D), BF16),
                   jax.ShapeDtypeStruct((b, RWKV_H, RWKV_HD, RWKV_HD), F32)],
        scratch_shapes=[pltpu.VMEM((ns * (RWKV_H // RWKV_GH), gw, gw), F32),
                        pltpu.VMEM((ns, RW_W), F32)],
        compiler_params=_cp(("arbitrary", "arbitrary")),
        name="rwkv_seq",
    )(proj, *_args(prm), seg)


def _rwkv_step_kernel(p_ref, prev_ref, s_in_ref, mu_ref, w0_ref, wup_ref, a0_ref, aup_ref, gup_ref, kk_ref,
                      ka_ref, rk_ref, lnw_ref, lnb_ref, seg_ref, y_ref, s_out_ref, nat_scr, col_scr, yt_scr):
    h = pl.program_id(0)

    @pl.when(h == 0)
    def _():
        prm = (mu_ref[...], w0_ref[...], wup_ref[...], a0_ref[...], aup_ref[...], gup_ref[...], kk_ref[...],
               ka_ref[...])
        r, k2, v, log_w, a, g, kk = _rwkv_pre(p_ref[...], prev_ref[...], prm, seg_ref[...])
        for i, x in enumerate((r, k2, v, g)):
            nat_scr[i] = x
        for i, x in enumerate((jnp.exp(log_w), kk * a, kk, k2, r, v)):
            col_scr[i] = x.T

    rows = pl.ds(pl.multiple_of(h * RWKV_HD, RWKV_HD), RWKV_HD)
    w_t, kka_t, kk_t, k_t, r_t, v_t = (col_scr[i, rows, :] for i in range(6))
    nacc = 4
    acc = [kk_t[j:j + 1] * s_in_ref[j] for j in range(nacc)]
    for k in range(nacc, RWKV_HD):
        acc[k % nacc] = acc[k % nacc] + kk_t[k:k + 1] * s_in_ref[k]
    sa = (acc[0] + acc[1]) + (acc[2] + acc[3])
    acc = [None] * nacc
    for k in range(RWKV_HD):
        s_new = s_in_ref[k] * w_t[k:k + 1] - kka_t[k:k + 1] * sa + k_t[k:k + 1] * v_t
        s_out_ref[k] = s_new
        term = r_t[k:k + 1] * s_new
        acc[k % nacc] = term if acc[k % nacc] is None else acc[k % nacc] + term
    yt_scr[rows, :] = (acc[0] + acc[1]) + (acc[2] + acc[3])

    @pl.when(h == RWKV_H - 1)
    def _():
        out = _rwkv_post(yt_scr[...].T, nat_scr[0], nat_scr[1], nat_scr[2], nat_scr[3], rk_ref[...], lnw_ref[...],
                         lnb_ref[...], seg_ref[...])
        y_ref[...] = out.astype(BF16)


def rwkv_step(proj, shift, state_t, layer, prm, seg):
    b = proj.shape[0]
    blk = (None, RWKV_HD, RWKV_HD, b)
    return pl.pallas_call(
        _rwkv_step_kernel,
        grid=(RWKV_H,),
        in_specs=[pl.BlockSpec((b, RW_W), lambda h: (0, 0)),
                  pl.BlockSpec((b, RW_W), lambda h: (0, 0)),
                  pl.BlockSpec((None,) + blk, lambda h: (layer, h, 0, 0, 0))]
                 + [_const_spec(x, 1) for x in prm] + [_const_spec(seg, 1)],
        out_specs=[pl.BlockSpec((b, RWKV_D), lambda h: (0, 0)),
                   pl.BlockSpec(blk, lambda h: (h, 0, 0, 0))],
        out_shape=[jax.ShapeDtypeStruct((b, RWKV_D), BF16),
                   jax.ShapeDtypeStruct(state_t.shape[1:], F32)],
        scratch_shapes=[pltpu.VMEM((4, b, RWKV_D), F32), pltpu.VMEM((6, RWKV_D, b), F32),
                        pltpu.VMEM((RWKV_D, b), F32)],
        compiler_params=_cp(("arbitrary",)),
        name="rwkv_step",
    )(proj, shift, state_t, *_args(prm), seg)


def _s5_glu(y, gw_ref, gb_ref):
    z = _gelu(y)
    return z * jax.nn.sigmoid(_bdot(z, gw_ref[...]) + gb_ref[...])


S5_CG = S5_D // 128
S5_SW = S5_STATE // S5_CG


def _s5_bu(u, bb_ref, prec=None):
    return jnp.concatenate([_dot(u[:, g * 128:(g + 1) * 128], bb_ref[g], prec) for g in range(S5_CG)], axis=1)


def _s5_ch(h_re, h_im, ccr_ref, cci_ref):
    return jnp.concatenate([_dot(h_re[:, g * S5_SW:(g + 1) * S5_SW], ccr_ref[g])
                            + _dot(h_im[:, g * S5_SW:(g + 1) * S5_SW], cci_ref[g]) for g in range(S5_CG)], axis=1)

def _interleave_perm(ns, tc):
    n = ns * tc
    r = _iota((n, n), 0)
    c = _iota((n, n), 1)
    t = lax.shift_right_logical(r, ns.bit_length() - 1)
    b = r & (ns - 1)
    return c == b * tc + t


def _s5_seq_kernel(u_ref, bbr_ref, bbi_ref, ccr_ref, cci_ref, abr_ref, abi_ref, d_ref, gw_ref, gb_ref,
                   y_ref, hr_ref, hi_ref, cr_ref, ci_ref):
    ns, tc = u_ref.shape[0], u_ref.shape[1]
    n = ns * tc
    t = pl.program_id(0)

    @pl.when(t == 0)
    def _():
        cr_ref[...] = jnp.zeros_like(cr_ref)
        ci_ref[...] = jnp.zeros_like(ci_ref)

    p_fwd = _interleave_perm(ns, tc).astype(BF16)
    u = u_ref[...].reshape(n, S5_D)
    u_hi, u_lo = _split(u)
    ub = _dot(p_fwd, u_hi)
    u_t = ub + _dot(p_fwd, u_lo)
    ub = ub.astype(BF16)
    bu_re = _s5_bu(ub, bbr_ref)
    bu_im = _s5_bu(ub, bbi_ref)
    ar = jnp.broadcast_to(abr_ref[...], (2 * ns, S5_STATE))
    ai = jnp.broadcast_to(abi_ref[...], (2 * ns, S5_STATE))
    first = _iota((2 * ns, 1), 0) < ns
    p_re = cr_ref[...]
    p_im = ci_ref[...]
    h_re, h_im = [], []
    for s in range(n // (2 * ns)):
        b_re = bu_re[s * 2 * ns:(s + 1) * 2 * ns]
        b_im = bu_im[s * 2 * ns:(s + 1) * 2 * ns]
        r_re = pltpu.roll(p_re, ns, axis=0)
        r_im = pltpu.roll(p_im, ns, axis=0)
        t_re = ar * r_re - ai * r_im + b_re
        t_im = ar * r_im + ai * r_re + b_im
        q_re = pltpu.roll(t_re, ns, axis=0)
        q_im = pltpu.roll(t_im, ns, axis=0)
        p_re = jnp.where(first, t_re, ar * q_re - ai * q_im + b_re)
        p_im = jnp.where(first, t_im, ar * q_im + ai * q_re + b_im)
        h_re.append(p_re)
        h_im.append(p_im)
    cr_ref[...] = p_re
    ci_ref[...] = p_im
    hb_re = jnp.concatenate(h_re, axis=0).astype(BF16)
    hb_im = jnp.concatenate(h_im, axis=0).astype(BF16)
    y_t = _s5_glu(_s5_ch(hb_re, hb_im, ccr_ref, cci_ref) + d_ref[...] * u_t, gw_ref, gb_ref)
    y = _dot_tn(p_fwd, y_t.astype(BF16))
    y_ref[...] = y.astype(BF16).reshape(ns, tc, S5_D)

    @pl.when(t == pl.num_programs(0) - 1)
    def _():
        hr_ref[:, 0, :] = p_re[ns:2 * ns]
        hi_ref[:, 0, :] = p_im[ns:2 * ns]


def s5_seq(proj, prm):
    b, l, _ = proj.shape
    assert 2 * b == 8, "the time-major scan packs two time steps of all sequences into one 8-row sublane tile"
    tc = CHUNK
    return pl.pallas_call(
        _s5_seq_kernel,
        grid=(l // tc,),
        in_specs=[pl.BlockSpec((b, tc, S5_D), lambda t: (0, t, S5_OFF // S5_D))] + [_const_spec(x, 1) for x in prm],
        out_specs=[pl.BlockSpec((b, tc, S5_D), lambda t: (0, t, 0)),
                   pl.BlockSpec((b, 1, S5_STATE), lambda t: (0, 0, 0)),
                   pl.BlockSpec((b, 1, S5_STATE), lambda t: (0, 0, 0))],
        out_shape=[jax.ShapeDtypeStruct((b, l, S5_D), BF16),
                   jax.ShapeDtypeStruct((b, 1, S5_STATE), F32),
                   jax.ShapeDtypeStruct((b, 1, S5_STATE), F32)],
        scratch_shapes=[pltpu.VMEM((2 * b, S5_STATE), F32), pltpu.VMEM((2 * b, S5_STATE), F32)],
        compiler_params=_cp(("arbitrary",)),
        name="s5_seq",
    )(proj, *_args(prm))


def _s5_step_kernel(u_ref, h0r_ref, h0i_ref, bbr_ref, bbi_ref, ccr_ref, cci_ref, abr_ref, abi_ref, d_ref, gw_ref,
                    gb_ref, y_ref, hr_ref, hi_ref):
    u = u_ref[...]
    ar = abr_ref[...]
    ai = abi_ref[...]
    h0r = h0r_ref[...]
    h0i = h0i_ref[...]
    h_re = ar * h0r - ai * h0i + _s5_bu(u, bbr_ref, HI)
    h_im = ar * h0i + ai * h0r + _s5_bu(u, bbi_ref, HI)
    hr_ref[...] = h_re
    hi_ref[...] = h_im
    ch = _s5_ch(h_re.astype(BF16), h_im.astype(BF16), ccr_ref, cci_ref)
    y_ref[...] = _s5_glu(ch + d_ref[...] * u, gw_ref, gb_ref).astype(BF16)


def s5_step(proj, h0_re, h0_im, prm):
    b = proj.shape[0]
    return pl.pallas_call(
        _s5_step_kernel,
        grid=(1,),
        in_specs=[pl.BlockSpec((b, S5_D), lambda i: (0, S5_OFF // S5_D)),
                  _const_spec(h0_re, 1), _const_spec(h0_im, 1)] + [_const_spec(x, 1) for x in prm],
        out_specs=[pl.BlockSpec((b, S5_D), lambda i: (0, 0)),
                   pl.BlockSpec((b, S5_STATE), lambda i: (0, 0)),
                   pl.BlockSpec((b, S5_STATE), lambda i: (0, 0))],
        out_shape=[jax.ShapeDtypeStruct((b, S5_D), BF16),
                   jax.ShapeDtypeStruct((b, S5_STATE), F32),
                   jax.ShapeDtypeStruct((b, S5_STATE), F32)],
        compiler_params=_cp(("arbitrary",)),
        name="s5_step",
    )(proj, h0_re, h0_im, *_args(prm))


def _gdn_pre(qkv_c, z, ab, alog_ref, dtb_ref):
    act = _silu(qkv_c)
    qs, ks = [], []
    for h in range(GDN_H):
        qh = act[:, h * GDN_HD:(h + 1) * GDN_HD]
        kh = act[:, GDN_D + h * GDN_HD:GDN_D + (h + 1) * GDN_HD]
        qs.append(qh * lax.rsqrt(jnp.sum(qh * qh, axis=-1, keepdims=True) + 1e-6))
        ks.append(kh * lax.rsqrt(jnp.sum(kh * kh, axis=-1, keepdims=True) + 1e-6))
    v = act[:, 2 * GDN_D:3 * GDN_D]
    g = -jnp.exp(alog_ref[...]) * jax.nn.softplus(ab + dtb_ref[...])
    beta = jax.nn.sigmoid(ab)
    return qs, ks, v, g, beta


def _gdn_post(o, z_h, nw):
    on = o * lax.rsqrt(jnp.mean(o * o, axis=-1, keepdims=True) + NORM_EPS) * nw
    return on * _silu(z_h)


def _gdn_seq_kernel(p_ref, cw_ref, alog_ref, dtb_ref, nw_ref, y_ref, s_out_ref, s_ref, prev_ref):
    c = CHUNK
    ns = p_ref.shape[0]
    n = ns * c
    t = pl.program_id(1)

    @pl.when(t == 0)
    def _():
        s_ref[...] = jnp.zeros_like(s_ref)
        prev_ref[...] = jnp.zeros_like(prev_ref)

    cw = cw_ref[...]
    convs = []
    for i in range(ns):
        qkv = p_ref[i, :, 0:3 * GDN_D]
        ext = jnp.concatenate([prev_ref[i], qkv], axis=0)
        prev_ref[i] = qkv[c - 8:c]
        conv = cw[3:4] * qkv
        for j in (1, 2, 3):
            conv = conv + cw[3 - j:4 - j] * pltpu.roll(ext, j, axis=0)[8:]
        convs.append(conv)
    x = p_ref[...].reshape(n, GDN_W)
    z = x[:, 3 * GDN_D:4 * GDN_D]
    ab = x[:, 4 * GDN_D:4 * GDN_D + 128]
    qs, ks, v, g, beta = _gdn_pre(jnp.concatenate(convs, axis=0), z, ab, alog_ref, dtb_ref)

    shift = c.bit_length() - 1
    ri = _iota((n, n), 0)
    ci = _iota((n, n), 1)
    tril = ((lax.shift_right_logical(ri, shift) == lax.shift_right_logical(ci, shift)) & (ci <= ri)).astype(F32)
    gc_all = _dot(tril, g, HI)

    hc = GDN_H * c
    heads = range(GDN_H)
    stack = lambda pieces: jnp.concatenate(pieces, axis=0)
    lane = _iota((c, 128), 1)
    rr = _iota((hc, hc), 0)
    cc = _iota((hc, hc), 1)
    same = lax.shift_right_logical(rr, shift) == lax.shift_right_logical(cc, shift)
    causal = same & (cc <= rr)
    strict = same & (cc < rr)
    ones = jnp.ones((hc, 128), F32)
    nw = nw_ref[...]
    for i in range(ns):
        sq = slice(i * c, (i + 1) * c)
        gc = gc_all[sq]
        g_col = stack([gc[:, h:h + 1] for h in heads])
        b_col = stack([beta[sq, GDN_H + h:GDN_H + h + 1] for h in heads])
        g_row = _dot_nt(ones, stack([jnp.where(lane == h, gc, 0.0) for h in heads]), HI)
        decay = jnp.where(causal, jnp.exp(jnp.where(causal, g_col - g_row, 0.0)), 0.0)
        q = stack([qs[h][sq] for h in heads]) * (GDN_HD ** -0.5)
        k = stack([ks[h][sq] for h in heads])
        v_s = stack([v[sq, h * GDN_HD:(h + 1) * GDN_HD] for h in heads])
        kb = k * b_col
        m1 = _dot_nt(stack([kb, q]).astype(BF16), k.astype(BF16))
        lmat = jnp.where(strict, m1[:hc] * decay, 0.0)
        attn = m1[hc:] * decay
        eg = jnp.exp(g_col)
        sol = _neumann_solve3(-lmat, jnp.concatenate([v_s * b_col, kb * eg], axis=1), shift)
        u = sol[:, :GDN_HD]
        wk = sol[:, GDN_HD:]
        qd = q * eg
        ws = [_dot(stack([wk[h * c:(h + 1) * c], qd[h * c:(h + 1) * c]]).astype(BF16),
                   s_ref[i * GDN_H + h].astype(BF16)) for h in heads]
        v_new = u - stack([w[:c] for w in ws])
        o = stack([w[c:] for w in ws]) + _bdot(attn, v_new)
        g_last = [gc[c - 1:c, h:h + 1] for h in heads]
        k_tail = k * jnp.exp(stack([jnp.broadcast_to(gl, (c, 1)) for gl in g_last]) - g_col)
        for h in heads:
            rows = slice(h * c, (h + 1) * c)
            s_ref[i * GDN_H + h] = (s_ref[i * GDN_H + h] * jnp.exp(g_last[h])
                                    + _dot_tn(k_tail[rows].astype(BF16), v_new[rows].astype(BF16)))
            z_h = z[sq, h * GDN_HD:(h + 1) * GDN_HD]
            y_ref[i, :, h * GDN_HD:(h + 1) * GDN_HD] = _gdn_post(o[rows], z_h, nw).astype(BF16)

    @pl.when(t == pl.num_programs(1) - 1)
    def _():
        s_out_ref[...] = s_ref[...].reshape(s_out_ref.shape)


def gdn_seq(proj, prm, ns):
    b, l, _ = proj.shape
    c = CHUNK
    return pl.pallas_call(
        _gdn_seq_kernel,
        grid=(b // ns, l // c),
        in_specs=[pl.BlockSpec((ns, c, GDN_W), lambda b, t: (b, t, GDN_OFF // GDN_W))] + [_const_spec(x, 2) for x in prm],
        out_specs=[pl.BlockSpec((ns, c, GDN_D), lambda b, t: (b, t, 0)),
                   pl.BlockSpec((ns, GDN_H, GDN_HD, GDN_HD), lambda b, t: (b, 0, 0, 0))],
        out_shape=[jax.ShapeDtypeStruct((b, l, GDN_D), BF16),
                   jax.ShapeDtypeStruct((b, GDN_H, GDN_HD, GDN_HD), F32)],
        scratch_shapes=[pltpu.VMEM((ns * GDN_H, GDN_HD, GDN_HD), F32),
                        pltpu.VMEM((ns, 8, 3 * GDN_D), F32)],
        compiler_params=_cp(("arbitrary", "arbitrary")),
        name="gdn_seq",
    )(proj, *_args(prm))


def _gdn_step_kernel(nb, p_ref, cs_ref, s_in_ref, cw_ref, alog_ref, dtb_ref, nw_ref, y_ref, s_out_ref):
    x = p_ref[...]
    qkv = x[:, 0:3 * GDN_D]
    z = x[:, 3 * GDN_D:4 * GDN_D]
    ab = x[:, 4 * GDN_D:4 * GDN_D + 128]
    cs = cs_ref[...]
    cw = cw_ref[...]
    w3 = 3 * GDN_D
    conv = cw[0:1] * cs[:, 0:w3] + cw[1:2] * cs[:, w3:2 * w3] + cw[2:3] * cs[:, 2 * w3:3 * w3] + cw[3:4] * qkv
    qs, ks, v, g, beta = _gdn_pre(conv, z, ab, alog_ref, dtb_ref)
    nw = nw_ref[...]
    pad = jnp.zeros((GDN_HD - nb, GDN_HD), F32)
    ks_t = [jnp.concatenate([ks[h], pad], axis=0).T for h in range(GDN_H)]
    qs_t = [jnp.concatenate([qs[h], pad], axis=0).T for h in range(GDN_H)]
    for i in range(nb):
        for h in range(GDN_H):
            k_c = ks_t[h][:, i:i + 1]
            q_c = qs_t[h][:, i:i + 1]
            eg = jnp.exp(g[i:i + 1, h:h + 1])
            b_s = beta[i:i + 1, GDN_H + h:GDN_H + h + 1]
            s = s_in_ref[i, h]
            sk = jnp.sum(k_c * s, axis=0, keepdims=True)
            v_row = v[i:i + 1, h * GDN_HD:(h + 1) * GDN_HD]
            s_new = s * eg + (k_c * b_s) * (v_row - eg * sk)
            s_out_ref[i, h] = s_new
            o = jnp.sum(q_c * s_new, axis=0, keepdims=True) * (GDN_HD ** -0.5)
            z_h = z[i:i + 1, h * GDN_HD:(h + 1) * GDN_HD]
            y_ref[i:i + 1, h * GDN_HD:(h + 1) * GDN_HD] = _gdn_post(o, z_h, nw).astype(BF16)


def gdn_step(proj, conv_state, state, layer, prm, nb):
    b = proj.shape[0]
    blk = (nb, GDN_H, GDN_HD, GDN_HD)
    return pl.pallas_call(
        functools.partial(_gdn_step_kernel, nb),
        grid=(b // nb,),
        in_specs=[pl.BlockSpec((nb, GDN_W), lambda i: (i, GDN_OFF // GDN_W)),
                  pl.BlockSpec((nb, 9 * GDN_D), lambda i: (i, 0)),
                  pl.BlockSpec((None,) + blk, lambda i: (layer, i, 0, 0, 0))] + [_const_spec(x, 1) for x in prm],
        out_specs=[pl.BlockSpec((nb, GDN_D), lambda i: (i, 0)),
                   pl.BlockSpec(blk, lambda i: (i, 0, 0, 0))],
        out_shape=[jax.ShapeDtypeStruct((b, GDN_D), BF16),
                   jax.ShapeDtypeStruct(state.shape[1:], F32)],
        compiler_params=_cp(("arbitrary",)),
        name="gdn_step",
    )(proj, conv_state, state, *_args(prm))


def _lru_gates(xc, gate, wr_ref, br_ref, wi_ref, bi_ref, lam_ref):
    r = jax.nn.sigmoid(_bdot(xc, wr_ref[...]) + br_ref[...])
    i = jax.nn.sigmoid(_bdot(xc, wi_ref[...]) + bi_ref[...])
    log_a = -LRU_C * r * jax.nn.softplus(-lam_ref[...])
    a = jnp.exp(log_a)
    one_minus_a2 = -jnp.tanh(log_a) * (a * a + 1.0)
    b = jnp.sqrt(one_minus_a2) * (i * xc)
    return a, b, _gelu(gate)


def _lru_seq_kernel(p_ref, cw_ref, cb_ref, wr_ref, br_ref, wi_ref, bi_ref, lam_ref, y_ref, h_out_ref,
                    prev_ref, carry_ref):
    ns, tc = p_ref.shape[0], p_ref.shape[1]
    n = ns * tc
    t = pl.program_id(0)

    @pl.when(t == 0)
    def _():
        prev_ref[...] = jnp.zeros_like(prev_ref)
        carry_ref[...] = jnp.zeros_like(carry_ref)

    p_fwd = _interleave_perm(ns, tc).astype(BF16)
    x_hi, x_lo = _split(p_ref[...].reshape(n, LRU_W))
    x = _dot(p_fwd, x_hi) + _dot(p_fwd, x_lo)
    xr = x[:, :LRU_D]
    gate = x[:, LRU_D:]
    halo = prev_ref.shape[0]
    ext = jnp.concatenate([prev_ref[...], xr], axis=0)
    prev_ref[...] = xr[n - halo:n]
    cw = cw_ref[...]
    xc = cw[3:4] * xr + cb_ref[...]
    for j in (1, 2, 3):
        xc = xc + cw[3 - j:4 - j] * pltpu.roll(ext, j * ns, axis=0)[halo:]
    a, b, gg = _lru_gates(xc, gate, wr_ref, br_ref, wi_ref, bi_ref, lam_ref)
    first = _iota((2 * ns, 1), 0) < ns
    h_prev = carry_ref[...]
    hs = []
    for s in range(n // (2 * ns)):
        a_s = a[s * 2 * ns:(s + 1) * 2 * ns]
        b_s = b[s * 2 * ns:(s + 1) * 2 * ns]
        h_a = a_s * pltpu.roll(h_prev, ns, axis=0) + b_s
        h_prev = jnp.where(first, h_a, a_s * pltpu.roll(h_a, ns, axis=0) + b_s)
        hs.append(h_prev)
    carry_ref[...] = h_prev
    y_t = (jnp.concatenate(hs, axis=0) * gg).astype(BF16)
    y_ref[...] = _dot_tn(p_fwd, y_t).astype(BF16).reshape(ns, tc, LRU_D)

    @pl.when(t == pl.num_programs(0) - 1)
    def _():
        h_out_ref[:, 0, :] = h_prev[ns:2 * ns]


def lru_seq(proj, prm):
    b, l, _ = proj.shape
    assert 2 * b == 8, "the time-major scan packs two time steps of all sequences into one 8-row sublane tile"
    tc = CHUNK
    return pl.pallas_call(
        _lru_seq_kernel,
        grid=(l // tc,),
        in_specs=[pl.BlockSpec((b, tc, LRU_W), lambda t: (0, t, LRU_OFF // LRU_W))] + [_const_spec(x, 1) for x in prm],
        out_specs=[pl.BlockSpec((b, tc, LRU_D), lambda t: (0, t, 0)),
                   pl.BlockSpec((b, 1, LRU_D), lambda t: (0, 0, 0))],
        out_shape=[jax.ShapeDtypeStruct((b, l, LRU_D), BF16),
                   jax.ShapeDtypeStruct((b, 1, LRU_D), F32)],
        scratch_shapes=[pltpu.VMEM((4 * b, LRU_D), F32), pltpu.VMEM((2 * b, LRU_D), F32)],
        compiler_params=_cp(("arbitrary",)),
        name="lru_seq",
    )(proj, *_args(prm))


def _lru_step_kernel(p_ref, cs_ref, h0_ref, cw_ref, cb_ref, wr_ref, br_ref, wi_ref, bi_ref, lam_ref, y_ref, h_ref):
    x = p_ref[...]
    xr = x[:, :LRU_D]
    gate = x[:, LRU_D:]
    cs = cs_ref[...]
    cw = cw_ref[...]
    xc = (cw[0:1] * cs[:, 0:LRU_D] + cw[1:2] * cs[:, LRU_D:2 * LRU_D] + cw[2:3] * cs[:, 2 * LRU_D:3 * LRU_D]
          + cw[3:4] * xr + cb_ref[...])
    a, b, gg = _lru_gates(xc, gate, wr_ref, br_ref, wi_ref, bi_ref, lam_ref)
    h = a * h0_ref[...] + b
    h_ref[...] = h
    y_ref[...] = (h * gg).astype(BF16)


def lru_step(proj, conv_state, h0, prm):
    b = proj.shape[0]
    return pl.pallas_call(
        _lru_step_kernel,
        grid=(1,),
        in_specs=[pl.BlockSpec((b, LRU_W), lambda i: (0, LRU_OFF // LRU_W)),
                  _const_spec(conv_state, 1), _const_spec(h0, 1)] + [_const_spec(x, 1) for x in prm],
        out_specs=[pl.BlockSpec((b, LRU_D), lambda i: (0, 0)),
                   pl.BlockSpec((b, LRU_D), lambda i: (0, 0))],
        out_shape=[jax.ShapeDtypeStruct((b, LRU_D), BF16),
                   jax.ShapeDtypeStruct((b, LRU_D), F32)],
        compiler_params=_cp(("arbitrary",)),
        name="lru_step",
    )(proj, conv_state, h0, *_args(prm))


def _pad_cols(x, width):
    return jnp.pad(x, [(0, 0)] * (x.ndim - 1) + [(0, width - x.shape[-1])])


def _pack_rwkv_cols(x):
    return jnp.concatenate([x[..., :1536], _pad_cols(x[..., 1536:1632], 128), _pad_cols(x[..., 1632:1728], 128),
                            x[..., 1728:1984]], axis=-1)


def _unpack_rwkv_cols(x):
    return jnp.concatenate([x[..., :1536], x[..., 1536:1632], x[..., 1664:1760], x[..., 1792:2048]], axis=-1)


def _pack_w_in_kernel(w_ref, o_ref):
    o_ref[...] = jnp.zeros(o_ref.shape, BF16)
    o = RWKV_PROJ
    for dst, a, b in ((0, 0, 1536), (1536, 1536, 1632), (1664, 1632, 1728), (1792, 1728, o), (S5_OFF, o, o + S5_D),
                      (GDN_OFF, o + S5_D, o + S5_D + GDN_PROJ), (LRU_OFF, o + S5_D + GDN_PROJ, o + S5_D + GDN_PROJ + 2 * LRU_D)):
        o_ref[0, dst:dst + (b - a), :] = w_ref[0, a:b, :].astype(BF16)


def pack_w_in(w_t, tk):
    depth, n, d = w_t.shape
    return pl.pallas_call(
        _pack_w_in_kernel,
        grid=(depth, d // tk),
        in_specs=[pl.BlockSpec((1, n, tk), lambda l, i: (l, 0, i))],
        out_specs=pl.BlockSpec((1, NP_IN, tk), lambda l, i: (l, 0, i)),
        out_shape=jax.ShapeDtypeStruct((depth, NP_IN, d), BF16),
        compiler_params=_cp(("arbitrary", "arbitrary")),
        name="pack_w_in",
    )(w_t)


def _block_diag(blocks):
    *lead, g, a, b = blocks.shape
    eye = jnp.eye(g, dtype=blocks.dtype)
    return (eye[:, None, :, None] * blocks[..., :, :, None, :]).reshape(*lead, g * a, g * b)


def _s5_params(lam_re, lam_im, log_dt, b_re, b_im, c_re, c_im, d, glu_w, glu_b):
    depth = lam_re.shape[0]
    dt = jnp.exp(log_dt)[..., None]
    mag = jnp.exp(lam_re * dt)
    ab_re, ab_im = mag * jnp.cos(lam_im * dt), mag * jnp.sin(lam_im * dt)
    den = lam_re * lam_re + lam_im * lam_im
    nr = ab_re - 1.0
    f_re = (nr * lam_re + ab_im * lam_im) / den
    f_im = (ab_im * lam_re - nr * lam_im) / den
    bb_re = f_re[..., None] * b_re - f_im[..., None] * b_im
    bb_im = f_re[..., None] * b_im + f_im[..., None] * b_re
    macro = lambda x: _block_diag(jnp.swapaxes(x, -1, -2).reshape(depth, S5_CG, S5_G // S5_CG, x.shape[-1], x.shape[-2]))
    bbr = macro(bb_re)
    bbi = macro(bb_im)
    tail = (macro(c_re).astype(BF16), macro(-c_im).astype(BF16),
            ab_re.reshape(depth, 1, S5_STATE), ab_im.reshape(depth, 1, S5_STATE), d.reshape(depth, 1, S5_D),
            glu_w, glu_b.reshape(depth, 1, S5_D))
    return (bbr.astype(BF16), bbi.astype(BF16)) + tail, (bbr, bbi) + tail


def _prepare_params(w):
    depth = w['w_in'].shape[0]
    row = lambda v: v.reshape(depth, 1, -1)
    pad_rows = lambda v: jnp.pad(v, ((0, 0), (0, 128 - v.shape[1]), (0, 0)))
    s5, s5_f32 = _s5_params(w['s5_lambda_re'], w['s5_lambda_im'], w['s5_log_dt'], w['s5_b_re'], w['s5_b_im'],
                            w['s5_c_re'], w['s5_c_im'], w['s5_d'], w['s5_glu_w'], w['s5_glu_b'])
    return {
        'norm1_g': row(w['norm1_g']), 'norm2_g': row(w['norm2_g']),
        'w_in': pack_w_in(jnp.swapaxes(w['w_in'], 1, 2), 256),
        'rwkv': (row(_pack_rwkv_cols(w['rwkv_mu'])), row(w['rwkv_w0']), pad_rows(w['rwkv_w_up']), row(w['rwkv_a0']),
                 pad_rows(w['rwkv_a_up']), w['rwkv_g_up'], row(w['rwkv_k_k']), row(w['rwkv_k_a']), row(w['rwkv_r_k']),
                 row(w['rwkv_ln_w']), row(w['rwkv_ln_b'])),
        's5': s5, 's5_f32': s5_f32,
        'gdn': (w['gdn_conv_w'], _pad_cols(row(w['gdn_a_log']), 128), _pad_cols(row(w['gdn_dt_bias']), 128),
                row(w['gdn_norm_w'])),
        'lru': (w['lru_conv_w'], row(w['lru_conv_b']), _block_diag(w['lru_wr']), row(w['lru_br']),
                _block_diag(w['lru_wi']), row(w['lru_bi']), row(w['lru_lambda'])),
        'ffn_conv_w': w['ffn_conv_w'], 'ffn_conv_b': row(w['ffn_conv_b']),
    }


def _layer(xp, xs, mod, st, p, seg, layer, stacked):
    bp, lp, d = xp.shape
    bs = xs.shape[1]
    sh1, sc1, g1, sh2, sc2, g2 = jnp.split(mod, 6, axis=-1)
    pm = lambda m: m[:bp, None, :]
    sm = lambda m: m[None, bp:bp + bs, :]

    lay = lambda x: Layered(x, layer)
    rw_prm, s5_prm, s5_prm_f32, gdn_prm, lru_prm = (tuple(lay(x) for x in p[k])
                                                    for k in ('rwkv', 's5', 's5_f32', 'gdn', 'lru'))

    h = norm_mod(xp, lay(p['norm1_g']), pm(sc1), pm(sh1), 1024)
    proj = matmul_in(h, p['w_in'], layer, 1024, 1024)
    ya, s_wkv = rwkv_seq(proj, rw_prm, seg, bp)
    yb, s_re, s_im = s5_seq(proj, s5_prm)
    yc, s_gdn = gdn_seq(proj, gdn_prm, bp)
    yd, s_lru = lru_seq(proj, lru_prm)
    xp = matmul_res([ya, yb, yc, yd], stacked['w_out'], layer, xp, pm(g1), lp, 512)
    h = norm_mod(xp, lay(p['norm2_g']), pm(sc2), pm(sh2), 1024)
    act, lg, lv = ffn_up_seq(h, stacked['ffn_w_up'], layer, p['ffn_conv_w'], p['ffn_conv_b'], 256)
    xp = matmul_res([act], stacked['ffn_w_down'], layer, xp, pm(g2), 1024, 256)
    new_p = {
        'rwkv_wkv': s_wkv,
        'rwkv_shift': _unpack_rwkv_cols(proj[:, -1, :RW_W]),
        's5_re': s_re.reshape(bp, S5_G, S5_N), 's5_im': s_im.reshape(bp, S5_G, S5_N),
        'gdn': s_gdn,
        'gdn_conv': proj[:, -3:, GDN_OFF:GDN_OFF + 3 * GDN_D],
        'lru_h': s_lru.reshape(bp, LRU_D),
        'lru_conv': proj[:, -3:, LRU_OFF:LRU_OFF + LRU_D],
        'ffn_conv': jnp.concatenate([lg, lv], axis=-1),
    }

    h = norm_mod(xs, lay(p['norm1_g']), sm(sc1), sm(sh1), bs)
    proj = matmul_in(h, p['w_in'], layer, bs, 1024)[0]
    ya, s_wkv = rwkv_step(proj, _pack_rwkv_cols(st['rwkv_shift']), stacked['rwkv_wkv'], layer, rw_prm, seg)
    yb, s_re, s_im = s5_step(proj, st['s5_re'].reshape(bs, S5_STATE), st['s5_im'].reshape(bs, S5_STATE), s5_prm_f32)
    yc, s_gdn = gdn_step(proj, st['gdn_conv'].reshape(bs, 9 * GDN_D), stacked['gdn'], layer, gdn_prm, 16)
    yd, s_lru = lru_step(proj, st['lru_conv'].reshape(bs, 3 * LRU_D), st['lru_h'], lru_prm)
    xs = matmul_res([y[None] for y in (ya, yb, yc, yd)], stacked['w_out'], layer, xs, sm(g1), bs, 512)
    h = norm_mod(xs, lay(p['norm2_g']), sm(sc2), sm(sh2), bs)
    act, ug, uv = ffn_up_step(h[0], stacked['ffn_w_up'], layer, p['ffn_conv_w'], p['ffn_conv_b'],
                              stacked['ffn_conv'], 256)
    xs = matmul_res([act[None]], stacked['ffn_w_down'], layer, xs, sm(g2), bs, 256)
    shift_rows = lambda buf, new: jnp.concatenate([buf[:, 1:], new[:, None, :]], axis=1)
    new_s = {
        'rwkv_wkv': s_wkv,
        'rwkv_shift': _unpack_rwkv_cols(proj[:, :RW_W]),
        's5_re': s_re.reshape(bs, S5_G, S5_N), 's5_im': s_im.reshape(bs, S5_G, S5_N),
        'gdn': s_gdn,
        'gdn_conv': shift_rows(st['gdn_conv'], proj[:, GDN_OFF:GDN_OFF + 3 * GDN_D]),
        'lru_h': s_lru,
        'lru_conv': shift_rows(st['lru_conv'], proj[:, LRU_OFF:LRU_OFF + LRU_D]),
        'ffn_conv': shift_rows(st['ffn_conv'], jnp.concatenate([ug, uv], axis=-1)),
    }
    return xp, xs, new_p, new_s


STATE_ORDER = ('rwkv_wkv', 'rwkv_shift', 's5_re', 's5_im', 'gdn', 'gdn_conv', 'lru_h', 'lru_conv', 'ffn_conv')


def kernel(x_prompt, x_sample, c_prompt, c_sample, state_rwkv_wkv, state_rwkv_shift, state_s5_re, state_s5_im, state_gdn, state_gdn_conv, state_lru_h, state_lru_conv, state_ffn_conv, ada_w, ada_b, norm1_g, norm2_g, final_g, w_in, w_out, rwkv_mu, rwkv_w0, rwkv_w_up, rwkv_a0, rwkv_a_up, rwkv_g_up, rwkv_k_k, rwkv_k_a, rwkv_r_k, rwkv_ln_w, rwkv_ln_b, s5_lambda_re, s5_lambda_im, s5_log_dt, s5_b_re, s5_b_im, s5_c_re, s5_c_im, s5_d, s5_glu_w, s5_glu_b, gdn_conv_w, gdn_a_log, gdn_dt_bias, gdn_norm_w, lru_conv_w, lru_conv_b, lru_wr, lru_br, lru_wi, lru_bi, lru_lambda, ffn_w_up, ffn_conv_w, ffn_conv_b, ffn_w_down):
    weights = {
        'norm1_g': norm1_g, 'norm2_g': norm2_g, 'w_in': w_in,
        'rwkv_mu': rwkv_mu, 'rwkv_w0': rwkv_w0, 'rwkv_w_up': rwkv_w_up, 'rwkv_a0': rwkv_a0,
        'rwkv_a_up': rwkv_a_up, 'rwkv_g_up': rwkv_g_up, 'rwkv_k_k': rwkv_k_k, 'rwkv_k_a': rwkv_k_a,
        'rwkv_r_k': rwkv_r_k, 'rwkv_ln_w': rwkv_ln_w, 'rwkv_ln_b': rwkv_ln_b,
        's5_lambda_re': s5_lambda_re, 's5_lambda_im': s5_lambda_im, 's5_log_dt': s5_log_dt,
        's5_b_re': s5_b_re, 's5_b_im': s5_b_im, 's5_c_re': s5_c_re, 's5_c_im': s5_c_im, 's5_d': s5_d,
        's5_glu_w': s5_glu_w, 's5_glu_b': s5_glu_b,
        'gdn_conv_w': gdn_conv_w, 'gdn_a_log': gdn_a_log, 'gdn_dt_bias': gdn_dt_bias, 'gdn_norm_w': gdn_norm_w,
        'lru_conv_w': lru_conv_w, 'lru_conv_b': lru_conv_b, 'lru_wr': lru_wr, 'lru_br': lru_br,
        'lru_wi': lru_wi, 'lru_bi': lru_bi, 'lru_lambda': lru_lambda,
        'ffn_conv_w': ffn_conv_w, 'ffn_conv_b': ffn_conv_b,
    }
    stacked = {'w_out': w_out, 'ffn_w_up': ffn_w_up, 'ffn_w_down': ffn_w_down,
               'rwkv_wkv': jnp.transpose(state_rwkv_wkv, (0, 2, 3, 4, 1)),
               'gdn': state_gdn, 'ffn_conv': state_ffn_conv}
    cache = {'rwkv_wkv': state_rwkv_wkv, 'rwkv_shift': state_rwkv_shift, 's5_re': state_s5_re,
             's5_im': state_s5_im, 'gdn': state_gdn, 'gdn_conv': state_gdn_conv, 'lru_h': state_lru_h,
             'lru_conv': state_lru_conv, 'ffn_conv': state_ffn_conv}
    depth = ada_w.shape[0]
    bp = x_prompt.shape[0]
    bs = x_sample.shape[0]
    rows = -(-(bp + bs) // 8) * 8
    c_all = jnp.pad(jnp.concatenate([c_prompt, c_sample], axis=0), ((0, rows - bp - bs), (0, 0)))
    mod = ada_mod(c_all, ada_w, ada_b)
    ids = jnp.arange(RWKV_D) // RWKV_HD
    seg = (ids[:, None] == ids[None, :]).astype(BF16)

    xp = x_prompt
    xs = jnp.swapaxes(x_sample, 0, 1)
    new_p = {n: [] for n in STATE_ORDER}
    new_s = {n: [] for n in STATE_ORDER}
    p = _prepare_params(weights)
    for l in range(depth):
        st = {n: cache[n][l] for n in STATE_ORDER}
        xp, xs, sp, ss = _layer(xp, xs, mod[l], st, p, seg, l, stacked)
        for n in STATE_ORDER:
            new_p[n].append(sp[n])
            new_s[n].append(ss[n])
    y_prompt = final_norm(xp, final_g, 1024)
    y_sample = jnp.swapaxes(final_norm(xs, final_g, bs), 0, 1)
    outs_p = tuple(jnp.stack(new_p[n], axis=0) for n in STATE_ORDER)
    stacked_s = {n: jnp.stack(new_s[n], axis=0) for n in STATE_ORDER}
    stacked_s['rwkv_wkv'] = jnp.transpose(stacked_s['rwkv_wkv'], (0, 4, 1, 2, 3))
    outs_s = tuple(stacked_s[n] for n in STATE_ORDER)
    return (y_prompt, y_sample) + outs_p + outs_s
```

```python
import functools
import math
from typing import NamedTuple

import jax
import jax.numpy as jnp
from jax import lax
from jax.experimental import pallas as pl
from jax.experimental.pallas import tpu as pltpu

F32 = jnp.float32
BF16 = jnp.bfloat16
HI = lax.Precision.HIGHEST

NORM_EPS = 1e-6
RWKV_LN_EPS = 64e-5
LRU_C = 8.0

D_MODEL = 2048
RWKV_D, RWKV_HD, RWKV_H = 512, 64, 8
RWKV_R_DECAY, RWKV_R_A, RWKV_R_GATE = 96, 96, 256
RWKV_PROJ = 3 * RWKV_D + RWKV_R_DECAY + RWKV_R_A + RWKV_R_GATE
S5_D, S5_CH, S5_G, S5_N = 512, 16, 32, 64
S5_STATE = S5_G * S5_N
GDN_D, GDN_HD, GDN_H = 512, 128, 4
GDN_PROJ = 4 * GDN_D + 2 * GDN_H
LRU_D, LRU_BLOCKS = 512, 8
D_FF = 5632

RW_W = 2048
S5_OFF = 2048
GDN_OFF, GDN_W = 2560, 2560
LRU_OFF, LRU_W = 5120, 1024
NP_IN = 6144

CHUNK = 64
RWKV_GH = 4
assert CHUNK == RWKV_HD

V7X_VMEM_BYTES = 64 * 1024 * 1024
VMEM_LIMIT = V7X_VMEM_BYTES - 8 * 1024 * 1024

TILE = {
    'ada_cols': 1024,
    'norm_rows': 1024,
    'in_rows': 1024, 'in_cols': 1024,
    'out_cols': 512,
    'ffn_cols': 256,
    'down_rows': 1024, 'down_cols': 256,
    'pack_cols': 256,
    'gdn_step_rows': 16,
}


def _cp(sem):
    return pltpu.CompilerParams(dimension_semantics=sem, vmem_limit_bytes=VMEM_LIMIT)


def _dot(a, b, prec=None):
    return jnp.dot(a, b, preferred_element_type=F32, precision=prec)


def _dot_nt(a, b, prec=None):
    return lax.dot_general(a, b, (((1,), (1,)), ((), ())), preferred_element_type=F32, precision=prec)


def _dot_tn(a, b, prec=None):
    return lax.dot_general(a, b, (((0,), (0,)), ((), ())), preferred_element_type=F32, precision=prec)


def _bdot(a, b):
    return _dot(a.astype(BF16), b.astype(BF16))


def _silu(x):
    return x * jax.nn.sigmoid(x)


def _gelu(x):
    return 0.5 * x * (1.0 + jnp.tanh(math.sqrt(2.0 / math.pi) * (x + 0.044715 * (x * x * x))))


def _seg_dot(x, seg):
    hi = x.astype(BF16)
    lo = (x - hi.astype(F32)).astype(BF16)
    n = x.shape[0]
    r = _dot(jnp.concatenate([hi, lo], axis=0), seg)
    return r[:n] + r[n:]


def _iota(shape, dim):
    return lax.broadcasted_iota(jnp.int32, shape, dim)


class Layered(NamedTuple):
    arr: jax.Array
    layer: int


def _const_spec(x, nidx):
    if isinstance(x, Layered):
        shape = (None,) + x.arr.shape[1:]
        idx = (x.layer,) + (0,) * (x.arr.ndim - 1)
    else:
        shape = x.shape
        idx = (0,) * x.ndim
    if nidx == 1:
        return pl.BlockSpec(shape, lambda i: idx)
    return pl.BlockSpec(shape, lambda b, t: idx)


def _args(params):
    return [x.arr if isinstance(x, Layered) else x for x in params]


def _neumann_solve(n, rhs, steps):
    x = rhs
    p = n
    for i in range(steps):
        x = x + _bdot(p, x)
        if i + 1 < steps:
            p = _bdot(p, p)
    return x


def _split(x):
    hi = x.astype(BF16)
    return hi, (x - hi.astype(F32)).astype(BF16)


def _dot3(a, b):
    ah, al = _split(a)
    bh, bl = _split(b)
    return _dot(ah, bh) + (_dot(ah, bl) + _dot(al, bh))


def _neumann_solve3(n, rhs, steps):
    x = rhs
    p = n
    for i in range(steps):
        x = x + _dot3(p, x)
        if i + 1 < steps:
            p = _dot3(p, p)
    return x


def _row_to_col(row, eye):
    n = eye.shape[0]
    return jnp.sum(jnp.where(eye, jnp.broadcast_to(row, (n, n)), 0.0), axis=-1, keepdims=True)


def _ada_kernel(c_ref, w_ref, b_ref, o_ref):
    c = c_ref[...]
    o_ref[0] = _dot(_silu(c).astype(BF16), w_ref[0].astype(BF16)) + b_ref[0]


def ada_mod(c_all, ada_w, ada_b):
    depth, d, n = ada_w.shape
    r = c_all.shape[0]
    tn = TILE['ada_cols']
    return pl.pallas_call(
        _ada_kernel,
        grid=(depth, n // tn),
        in_specs=[pl.BlockSpec((r, d), lambda l, j: (0, 0)),
                  pl.BlockSpec((1, d, tn), lambda l, j: (l, 0, j)),
                  pl.BlockSpec((1, 1, tn), lambda l, j: (l, 0, j))],
        out_specs=pl.BlockSpec((1, r, tn), lambda l, j: (l, 0, j)),
        out_shape=jax.ShapeDtypeStruct((depth, r, n), F32),
        compiler_params=_cp(("arbitrary", "arbitrary")),
        name="ada_mod",
    )(c_all, ada_w, ada_b.reshape(depth, 1, n))


def _norm_mod_kernel(x_ref, g_ref, sc_ref, sh_ref, o_ref):
    x = x_ref[0]
    h = x * lax.rsqrt(jnp.mean(x * x, axis=-1, keepdims=True) + NORM_EPS) * g_ref[...]
    o_ref[0] = (h * (1.0 + sc_ref[0]) + sh_ref[0]).astype(o_ref.dtype)


def _mod_spec(arr, tm):
    d = arr.shape[-1]
    if arr.shape[1] == 1:
        return pl.BlockSpec((1, 1, d), lambda g, i: (g, 0, 0))
    return pl.BlockSpec((1, tm, d), lambda g, i: (g, i, 0))


def norm_mod(x, gain, sc, sh, tm):
    g, r, d = x.shape
    return pl.pallas_call(
        _norm_mod_kernel,
        grid=(g, r // tm),
        in_specs=[pl.BlockSpec((1, tm, d), lambda g, i: (g, i, 0)),
                  _const_spec(gain, 2),
                  _mod_spec(sc, tm), _mod_spec(sh, tm)],
        out_specs=pl.BlockSpec((1, tm, d), lambda g, i: (g, i, 0)),
        out_shape=jax.ShapeDtypeStruct((g, r, d), BF16),
        compiler_params=_cp(("arbitrary", "arbitrary")),
        name="norm_mod",
    )(x, *_args([gain]), sc, sh)


def _rms_kernel(x_ref, g_ref, o_ref):
    x = x_ref[0]
    o_ref[0] = x * lax.rsqrt(jnp.mean(x * x, axis=-1, keepdims=True) + NORM_EPS) * g_ref[...]


def final_norm(x, gain, tm):
    g, r, d = x.shape
    return pl.pallas_call(
        _rms_kernel,
        grid=(g, r // tm),
        in_specs=[pl.BlockSpec((1, tm, d), lambda g, i: (g, i, 0)),
                  pl.BlockSpec((1, d), lambda g, i: (0, 0))],
        out_specs=pl.BlockSpec((1, tm, d), lambda g, i: (g, i, 0)),
        out_shape=jax.ShapeDtypeStruct((g, r, d), F32),
        compiler_params=_cp(("arbitrary", "arbitrary")),
        name="final_norm",
    )(x, gain.reshape(1, d))


def _norm_mm_kernel(x_ref, g_ref, sc_ref, sh_ref, w_ref, o_ref, h_ref):
    @pl.when(pl.program_id(2) == 0)
    def _():
        x = x_ref[0]
        h = x * lax.rsqrt(jnp.mean(x * x, axis=-1, keepdims=True) + NORM_EPS) * g_ref[...]
        h_ref[...] = (h * (1.0 + sc_ref[0]) + sh_ref[0]).astype(BF16)

    o_ref[0] = _dot_nt(h_ref[...], w_ref[...])


def norm_matmul_in(x, gain, sc, sh, w_t, layer, tm, tn):
    g, r, k = x.shape
    n = w_t.shape[1]
    mod_spec = lambda m: (pl.BlockSpec((1, 1, k), lambda g, i, j: (g, 0, 0)) if m.shape[1] == 1
                          else pl.BlockSpec((1, tm, k), lambda g, i, j: (g, i, 0)))
    gain_spec = pl.BlockSpec((None, 1, k), lambda g, i, j: (gain.layer, 0, 0))
    return pl.pallas_call(
        _norm_mm_kernel,
        grid=(g, r // tm, n // tn),
        in_specs=[pl.BlockSpec((1, tm, k), lambda g, i, j: (g, i, 0)), gain_spec, mod_spec(sc), mod_spec(sh),
                  pl.BlockSpec((None, tn, k), lambda g, i, j: (layer, j, 0))],
        out_specs=pl.BlockSpec((1, tm, tn), lambda g, i, j: (g, i, j)),
        out_shape=jax.ShapeDtypeStruct((g, r, n), F32),
        scratch_shapes=[pltpu.VMEM((tm, k), BF16)],
        compiler_params=_cp(("arbitrary", "arbitrary", "arbitrary")),
        name="norm_matmul_in",
    )(x, gain.arr, sc, sh, w_t)


def _mm_res_kernel(n_a, *refs):
    a_refs = refs[:n_a]
    w_refs = refs[n_a:2 * n_a]
    x_ref, g_ref, o_ref = refs[2 * n_a:]
    acc = _dot(a_refs[0][0], w_refs[0][...].astype(BF16))
    for a_ref, w_ref in zip(a_refs[1:], w_refs[1:]):
        acc = acc + _dot(a_ref[0], w_ref[...].astype(BF16))
    o_ref[0] = x_ref[0] + g_ref[0] * acc


def matmul_res(a_list, w, layer, x, gate, tm, tn):
    g, r, n = x.shape
    n_a = len(a_list)
    k = a_list[0].shape[-1]
    a_specs = [pl.BlockSpec((1, tm, k), lambda g, i, j: (g, i, 0)) for _ in a_list]
    w_specs = [pl.BlockSpec((None, k, tn), functools.partial(lambda g, i, j, q: (layer, q, j), q=q))
               for q in range(n_a)]
    if gate.shape[1] == 1:
        g_spec = pl.BlockSpec((1, 1, tn), lambda g, i, j: (g, 0, j))
    else:
        g_spec = pl.BlockSpec((1, tm, tn), lambda g, i, j: (g, i, j))
    return pl.pallas_call(
        functools.partial(_mm_res_kernel, n_a),
        grid=(g, r // tm, n // tn),
        in_specs=a_specs + w_specs + [pl.BlockSpec((1, tm, tn), lambda g, i, j: (g, i, j)), g_spec],
        out_specs=pl.BlockSpec((1, tm, tn), lambda g, i, j: (g, i, j)),
        out_shape=jax.ShapeDtypeStruct((g, r, n), F32),
        compiler_params=_cp(("arbitrary", "arbitrary", "arbitrary")),
        name="matmul_res",
    )(*a_list, *([w] * n_a), x, gate)


def _ffn_up_seq_kernel(a_ref, wg_ref, wv_ref, cwg_ref, cwv_ref, cbg_ref, cbv_ref, act_ref, lg_ref, lv_ref):
    a = a_ref[0]
    rows = a.shape[0]
    rid8 = _iota((8, 1), 0)

    def conv(up, cw_ref, cb_ref):
        cw = cw_ref[...]
        return cw[0:1] * pltpu.roll(up, 2, axis=0) + cw[1:2] * pltpu.roll(up, 1, axis=0) + cw[2:3] * up + cb_ref[...]

    def head(up, cw_ref, cb_ref):
        cw = cw_ref[...]
        u8 = up[0:8]
        s1 = jnp.where(rid8 >= 1, pltpu.roll(u8, 1, axis=0), 0.0)
        s2 = jnp.where(rid8 >= 2, pltpu.roll(u8, 2, axis=0), 0.0)
        return cw[0:1] * s2 + cw[1:2] * s1 + cw[2:3] * u8 + cb_ref[...]

    up_g = _dot(a, wg_ref[...].astype(BF16))
    lg_ref[0] = up_g[rows - 2:rows]
    gate = conv(up_g, cwg_ref, cbg_ref)
    up_v = _dot(a, wv_ref[...].astype(BF16))
    lv_ref[0] = up_v[rows - 2:rows]
    val = conv(up_v, cwv_ref, cbv_ref)
    act_ref[0] = (_silu(gate) * val).astype(BF16)
    act_ref[0, 0:8, :] = (_silu(head(up_g, cwg_ref, cbg_ref)) * head(up_v, cwv_ref, cbv_ref)).astype(BF16)


def ffn_up_seq(h, w_up, layer, conv_w, conv_b, tn):
    b, l, d = h.shape
    nt = D_FF // tn
    return pl.pallas_call(
        _ffn_up_seq_kernel,
        grid=(b, nt),
        in_specs=[pl.BlockSpec((1, l, d), lambda b, j: (b, 0, 0)),
                  pl.BlockSpec((None, d, tn), lambda b, j: (layer, 0, j)),
                  pl.BlockSpec((None, d, tn), lambda b, j: (layer, 0, j + nt)),
                  pl.BlockSpec((None, 3, tn), lambda b, j: (layer, 0, j)),
                  pl.BlockSpec((None, 3, tn), lambda b, j: (layer, 0, j + nt)),
                  pl.BlockSpec((None, 1, tn), lambda b, j: (layer, 0, j)),
                  pl.BlockSpec((None, 1, tn), lambda b, j: (layer, 0, j + nt))],
        out_specs=[pl.BlockSpec((1, l, tn), lambda b, j: (b, 0, j)),
                   pl.BlockSpec((1, 2, tn), lambda b, j: (b, 0, j)),
                   pl.BlockSpec((1, 2, tn), lambda b, j: (b, 0, j))],
        out_shape=[jax.ShapeDtypeStruct((b, l, D_FF), BF16),
                   jax.ShapeDtypeStruct((b, 2, D_FF), F32),
                   jax.ShapeDtypeStruct((b, 2, D_FF), F32)],
        compiler_params=_cp(("arbitrary", "arbitrary")),
        name="ffn_up_seq",
    )(h, w_up, w_up, conv_w, conv_w, conv_b, conv_b)


def _ffn_up_step_kernel(a_ref, wg_ref, wv_ref, cwg_ref, cwv_ref, cbg_ref, cbv_ref,
                        stg_ref, stv_ref, act_ref, ug_ref, uv_ref):
    a = a_ref[...]
    up_g = _dot(a, wg_ref[...].astype(BF16))
    up_v = _dot(a, wv_ref[...].astype(BF16))
    ug_ref[...] = up_g
    uv_ref[...] = up_v
    cwg = cwg_ref[...]
    cwv = cwv_ref[...]
    gate = cwg[0:1] * stg_ref[:, 0, :] + cwg[1:2] * stg_ref[:, 1, :] + cwg[2:3] * up_g + cbg_ref[...]
    val = cwv[0:1] * stv_ref[:, 0, :] + cwv[1:2] * stv_ref[:, 1, :] + cwv[2:3] * up_v + cbv_ref[...]
    act_ref[...] = (_silu(gate) * val).astype(BF16)


def ffn_up_step(h, w_up, layer, conv_w, conv_b, conv_state, tn):
    b, d = h.shape
    nt = D_FF // tn
    return pl.pallas_call(
        _ffn_up_step_kernel,
        grid=(nt,),
        in_specs=[pl.BlockSpec((b, d), lambda j: (0, 0)),
                  pl.BlockSpec((None, d, tn), lambda j: (layer, 0, j)),
                  pl.BlockSpec((None, d, tn), lambda j: (layer, 0, j + nt)),
                  pl.BlockSpec((None, 3, tn), lambda j: (layer, 0, j)),
                  pl.BlockSpec((None, 3, tn), lambda j: (layer, 0, j + nt)),
                  pl.BlockSpec((None, 1, tn), lambda j: (layer, 0, j)),
                  pl.BlockSpec((None, 1, tn), lambda j: (layer, 0, j + nt)),
                  pl.BlockSpec((None, b, 2, tn), lambda j: (layer, 0, 0, j)),
                  pl.BlockSpec((None, b, 2, tn), lambda j: (layer, 0, 0, j + nt))],
        out_specs=[pl.BlockSpec((b, tn), lambda j: (0, j))] * 3,
        out_shape=[jax.ShapeDtypeStruct((b, D_FF), BF16),
                   jax.ShapeDtypeStruct((b, D_FF), F32),
                   jax.ShapeDtypeStruct((b, D_FF), F32)],
        compiler_params=_cp(("arbitrary",)),
        name="ffn_up_step",
    )(h, w_up, w_up, conv_w, conv_w, conv_b, conv_b, conv_state, conv_state)


def _rwkv_pre(p, prev, prm, seg):
    mu, w0, w_up, a0, a_up, g_up, k_k, k_a = prm
    xm = p + mu * (prev - p)
    r = xm[:, 0:512]
    k = xm[:, 512:1024]
    v = xm[:, 1024:1536]
    xw = xm[:, 1536:1664]
    xa = xm[:, 1664:1792]
    xg = xm[:, 1792:2048]
    log_w = -jnp.exp(-jax.nn.softplus(-(w0 + _bdot(jnp.tanh(xw), w_up))) - 0.5)
    a = jax.nn.sigmoid(a0 + _bdot(xa, a_up))
    g = _bdot(jax.nn.sigmoid(xg), g_up)
    kx = k * k_k
    kk = kx * lax.rsqrt(_seg_dot(kx * kx, seg) + 1e-12)
    k2 = k * (1.0 + (a - 1.0) * k_a)
    return r, k2, v, log_w, a, g, kk


def _rwkv_post(y, r, k2, v, g, r_k, ln_w, ln_b, seg):
    inv = 1.0 / RWKV_HD
    mean = _seg_dot(y, seg) * inv
    dlt = y - mean
    var = _seg_dot(dlt * dlt, seg) * inv
    yn = dlt * lax.rsqrt(var + RWKV_LN_EPS) * ln_w + ln_b
    bonus = _seg_dot(r * k2 * r_k, seg) * v
    return (yn + bonus) * g


def _rwkv_seq_kernel(p_ref, mu_ref, w0_ref, wup_ref, a0_ref, aup_ref, gup_ref, kk_ref, ka_ref, rk_ref,
                     lnw_ref, lnb_ref, seg_ref, y_ref, s_out_ref, s_ref, prev_ref):
    c = CHUNK
    ns = p_ref.shape[0]
    n = ns * c
    t = pl.program_id(1)

    @pl.when(t == 0)
    def _():
        s_ref[...] = jnp.zeros_like(s_ref)
        prev_ref[...] = jnp.zeros_like(prev_ref)

    p = p_ref[...].reshape(n, RW_W)
    seg = seg_ref[...]
    seq_rows = lambda x: jnp.concatenate([jnp.broadcast_to(x[i:i + 1], (c, x.shape[-1])) for i in range(ns)], axis=0)
    tloc = _iota((n, 1), 0) & (c - 1)
    prev = jnp.where(tloc == 0, seq_rows(prev_ref[...]), pltpu.roll(p, 1, axis=0))
    last_rows = lambda x: jnp.concatenate([x[(i + 1) * c - 1:(i + 1) * c] for i in range(ns)], axis=0)
    prev_ref[...] = last_rows(p)
    prm = (mu_ref[...], w0_ref[...], wup_ref[...], a0_ref[...], aup_ref[...], gup_ref[...], kk_ref[...], ka_ref[...])
    r, k2, v, log_w, a, g, kk = _rwkv_pre(p, prev, prm, seg)

    ri = _iota((n, n), 0)
    ci = _iota((n, n), 1)
    cshift = c.bit_length() - 1
    tril = ((lax.shift_right_logical(ri, cshift) == lax.shift_right_logical(ci, cshift)) & (ci <= ri)).astype(F32)
    cum = _dot(tril, log_w, HI)
    e_in = jnp.exp(cum)
    e_out = jnp.exp(-cum)
    e_ex = jnp.exp(cum - log_w)
    tot = last_rows(cum)
    e_tail = jnp.exp(seq_rows(tot) - cum)
    e_tot = jnp.exp(tot)
    alpha = -(a * kk)
    gw = RWKV_GH * RWKV_HD
    rr = _iota((gw, gw), 0)
    cc = _iota((gw, gw), 1)
    shift = RWKV_HD.bit_length() - 1
    own = lax.shift_right_logical(rr, shift) == lax.shift_right_logical(cc, shift)
    tri_strict = (cc & (c - 1)) < (rr & (c - 1))
    tri_incl = (cc & (c - 1)) <= (rr & (c - 1))
    eye = rr == cc

    ng = RWKV_H // RWKV_GH
    be_t = (kk * e_ex).astype(BF16)
    r_t = (r * e_in).astype(BF16)
    al_t = (alpha * e_out).astype(BF16)
    k_t = (k2 * e_out).astype(BF16)
    al_p = (alpha * e_tail).astype(BF16)
    k_p = (k2 * e_tail).astype(BF16)
    vb = v.astype(BF16)

    def expand(x, i, gi):
        xg = x[i * c:(i + 1) * c, gi * gw:(gi + 1) * gw]
        return jnp.where(own, jnp.concatenate([xg] * RWKV_GH, axis=0), jnp.zeros((), BF16))

    y_rows = []
    for i in range(ns):
        ys = []
        for gi in range(ng):
            s0 = s_ref[i * ng + gi]
            lhs = jnp.concatenate([expand(be_t, i, gi), expand(r_t, i, gi)], axis=0)
            rhs = jnp.concatenate([expand(al_t, i, gi), expand(k_t, i, gi)], axis=0)
            vv = expand(vb, i, gi)
            m1 = _dot_nt(lhs, rhs)
            m2 = _dot(lhs, s0.astype(BF16))
            l_a = jnp.where(tri_strict, m1[:gw, :gw], 0.0)
            l_k = jnp.where(tri_strict, m1[:gw, gw:], 0.0)
            u = _neumann_solve(l_a, m2[:gw] + _dot(l_k.astype(BF16), vv), cshift)
            uv = jnp.concatenate([u.astype(BF16), vv], axis=0)
            a_full = jnp.concatenate([jnp.where(tri_incl, m1[gw:, :gw], 0.0),
                                      jnp.where(tri_incl, m1[gw:, gw:], 0.0)], axis=1).astype(BF16)
            y_g = m2[gw:] + _dot(a_full, uv)
            ys.append(functools.reduce(lambda p_, q_: p_ + q_, [y_g[j * c:(j + 1) * c] for j in range(RWKV_GH)]))
            scale = _row_to_col(e_tot[i:i + 1, gi * gw:(gi + 1) * gw], eye)
            tails = jnp.concatenate([expand(al_p, i, gi), expand(k_p, i, gi)], axis=0)
            s_ref[i * ng + gi] = s0 * scale + _dot_tn(tails, uv)
        y_rows.append(jnp.concatenate(ys, axis=1))

    y = jnp.concatenate(y_rows, axis=0)
    out = _rwkv_post(y, r, k2, v, g, rk_ref[...], lnw_ref[...], lnb_ref[...], seg)
    y_ref[...] = out.astype(BF16).reshape(ns, c, RWKV_D)

    @pl.when(t == pl.num_programs(1) - 1)
    def _():
        for i in range(ns):
            for h in range(RWKV_H):
                gi, hh = divmod(h, RWKV_GH)
                s_out_ref[i, h] = s_ref[i * ng + gi, hh * RWKV_HD:(hh + 1) * RWKV_HD, hh * RWKV_HD:(hh + 1) * RWKV_HD]


def rwkv_seq(proj, prm, seg, ns):
    b, l, _ = proj.shape
    c = CHUNK
    gw = RWKV_GH * RWKV_HD
    return pl.pallas_call(
        _rwkv_seq_kernel,
        grid=(b // ns, l // c),
        in_specs=[pl.BlockSpec((ns, c, RW_W), lambda b, t: (b, t, 0))] + [_const_spec(x, 2) for x in prm] + [_const_spec(seg, 2)],
        out_specs=[pl.BlockSpec((ns, c, RWKV_D), lambda b, t: (b, t, 0)),
                   pl.BlockSpec((ns, RWKV_H, RWKV_HD, RWKV_HD), lambda b, t: (b, 0, 0, 0))],
        out_shape=[jax.ShapeDtypeStruct((b, l, RWKV_D), BF16),
                   jax.ShapeDtypeStruct((b, RWKV_H, RWKV_HD, RWKV_HD), F32)],
        scratch_shapes=[pltpu.VMEM((ns * (RWKV_H // RWKV_GH), gw, gw), F32),
                        pltpu.VMEM((ns, RW_W), F32)],
        compiler_params=_cp(("arbitrary", "arbitrary")),
        name="rwkv_seq",
    )(proj, *_args(prm), seg)


def _rwkv_step_kernel(p_ref, prev_ref, s_in_ref, mu_ref, w0_ref, wup_ref, a0_ref, aup_ref, gup_ref, kk_ref,
                      ka_ref, rk_ref, lnw_ref, lnb_ref, seg_ref, y_ref, s_out_ref, nat_scr, col_scr, yt_scr):
    h = pl.program_id(0)

    @pl.when(h == 0)
    def _():
        prm = (mu_ref[...], w0_ref[...], wup_ref[...], a0_ref[...], aup_ref[...], gup_ref[...], kk_ref[...],
               ka_ref[...])
        r, k2, v, log_w, a, g, kk = _rwkv_pre(p_ref[...], prev_ref[...], prm, seg_ref[...])
        for i, x in enumerate((r, k2, v, g)):
            nat_scr[i] = x
        for i, x in enumerate((jnp.exp(log_w), kk * a, kk, k2, r, v)):
            col_scr[i] = x.T

    rows = pl.ds(pl.multiple_of(h * RWKV_HD, RWKV_HD), RWKV_HD)
    w_t, kka_t, kk_t, k_t, r_t, v_t = (col_scr[i, rows, :] for i in range(6))
    nacc = 4
    acc = [kk_t[j:j + 1] * s_in_ref[j] for j in range(nacc)]
    for k in range(nacc, RWKV_HD):
        acc[k % nacc] = acc[k % nacc] + kk_t[k:k + 1] * s_in_ref[k]
    sa = (acc[0] + acc[1]) + (acc[2] + acc[3])
    acc = [None] * nacc
    for k in range(RWKV_HD):
        s_new = s_in_ref[k] * w_t[k:k + 1] - kka_t[k:k + 1] * sa + k_t[k:k + 1] * v_t
        s_out_ref[k] = s_new
        term = r_t[k:k + 1] * s_new
        acc[k % nacc] = term if acc[k % nacc] is None else acc[k % nacc] + term
    yt_scr[rows, :] = (acc[0] + acc[1]) + (acc[2] + acc[3])

    @pl.when(h == RWKV_H - 1)
    def _():
        out = _rwkv_post(yt_scr[...].T, nat_scr[0], nat_scr[1], nat_scr[2], nat_scr[3], rk_ref[...], lnw_ref[...],
                         lnb_ref[...], seg_ref[...])
        y_ref[...] = out.astype(BF16)


def rwkv_step(proj, shift, state_t, layer, prm, seg):
    b = proj.shape[0]
    blk = (None, RWKV_HD, RWKV_HD, b)
    return pl.pallas_call(
        _rwkv_step_kernel,
        grid=(RWKV_H,),
        in_specs=[pl.BlockSpec((b, RW_W), lambda h: (0, 0)),
                  pl.BlockSpec((b, RW_W), lambda h: (0, 0)),
                  pl.BlockSpec((None,) + blk, lambda h: (layer, h, 0, 0, 0))]
                 + [_const_spec(x, 1) for x in prm] + [_const_spec(seg, 1)],
        out_specs=[pl.BlockSpec((b, RWKV_D), lambda h: (0, 0)),
                   pl.BlockSpec(blk, lambda h: (h, 0, 0, 0))],
        out_shape=[jax.ShapeDtypeStruct((b, RWKV_D), BF16),
                   jax.ShapeDtypeStruct(state_t.shape[1:], F32)],
        scratch_shapes=[pltpu.VMEM((4, b, RWKV_D), F32), pltpu.VMEM((6, RWKV_D, b), F32),
                        pltpu.VMEM((RWKV_D, b), F32)],
        compiler_params=_cp(("arbitrary",)),
        name="rwkv_step",
    )(proj, shift, state_t, *_args(prm), seg)


def _s5_glu(y, gw_ref, gb_ref):
    z = _gelu(y)
    return z * jax.nn.sigmoid(_bdot(z, gw_ref[...]) + gb_ref[...])


S5_CG = S5_D // 128
S5_SW = S5_STATE // S5_CG


def _s5_bu(u, bb_ref, prec=None):
    return jnp.concatenate([_dot(u[:, g * 128:(g + 1) * 128], bb_ref[g], prec) for g in range(S5_CG)], axis=1)


def _s5_ch(h_re, h_im, ccr_ref, cci_ref):
    return jnp.concatenate([_dot(h_re[:, g * S5_SW:(g + 1) * S5_SW], ccr_ref[g])
                            + _dot(h_im[:, g * S5_SW:(g + 1) * S5_SW], cci_ref[g]) for g in range(S5_CG)], axis=1)

def _interleave_perm(ns, tc):
    n = ns * tc
    r = _iota((n, n), 0)
    c = _iota((n, n), 1)
    t = lax.shift_right_logical(r, ns.bit_length() - 1)
    b = r & (ns - 1)
    return c == b * tc + t


def _s5_seq_kernel(u_ref, bbr_ref, bbi_ref, ccr_ref, cci_ref, abr_ref, abi_ref, d_ref, gw_ref, gb_ref,
                   y_ref, hr_ref, hi_ref, cr_ref, ci_ref):
    ns, tc = u_ref.shape[0], u_ref.shape[1]
    n = ns * tc
    t = pl.program_id(0)

    @pl.when(t == 0)
    def _():
        cr_ref[...] = jnp.zeros_like(cr_ref)
        ci_ref[...] = jnp.zeros_like(ci_ref)

    p_fwd = _interleave_perm(ns, tc).astype(BF16)
    u = u_ref[...].reshape(n, S5_D)
    u_hi, u_lo = _split(u)
    ub = _dot(p_fwd, u_hi)
    u_t = ub + _dot(p_fwd, u_lo)
    ub = ub.astype(BF16)
    bu_re = _s5_bu(ub, bbr_ref)
    bu_im = _s5_bu(ub, bbi_ref)
    ar = jnp.broadcast_to(abr_ref[...], (2 * ns, S5_STATE))
    ai = jnp.broadcast_to(abi_ref[...], (2 * ns, S5_STATE))
    first = _iota((2 * ns, 1), 0) < ns
    p_re = cr_ref[...]
    p_im = ci_ref[...]
    h_re, h_im = [], []
    for s in range(n // (2 * ns)):
        b_re = bu_re[s * 2 * ns:(s + 1) * 2 * ns]
        b_im = bu_im[s * 2 * ns:(s + 1) * 2 * ns]
        r_re = pltpu.roll(p_re, ns, axis=0)
        r_im = pltpu.roll(p_im, ns, axis=0)
        t_re = ar * r_re - ai * r_im + b_re
        t_im = ar * r_im + ai * r_re + b_im
        q_re = pltpu.roll(t_re, ns, axis=0)
        q_im = pltpu.roll(t_im, ns, axis=0)
        p_re = jnp.where(first, t_re, ar * q_re - ai * q_im + b_re)
        p_im = jnp.where(first, t_im, ar * q_im + ai * q_re + b_im)
        h_re.append(p_re)
        h_im.append(p_im)
    cr_ref[...] = p_re
    ci_ref[...] = p_im
    hb_re = jnp.concatenate(h_re, axis=0).astype(BF16)
    hb_im = jnp.concatenate(h_im, axis=0).astype(BF16)
    y_t = _s5_glu(_s5_ch(hb_re, hb_im, ccr_ref, cci_ref) + d_ref[...] * u_t, gw_ref, gb_ref)
    y = _dot_tn(p_fwd, y_t.astype(BF16))
    y_ref[...] = y.astype(BF16).reshape(ns, tc, S5_D)

    @pl.when(t == pl.num_programs(0) - 1)
    def _():
        hr_ref[:, 0, :] = p_re[ns:2 * ns]
        hi_ref[:, 0, :] = p_im[ns:2 * ns]


def s5_seq(proj, prm):
    b, l, _ = proj.shape
    assert 2 * b == 8, "the time-major scan packs two time steps of all sequences into one 8-row sublane tile"
    tc = CHUNK
    return pl.pallas_call(
        _s5_seq_kernel,
        grid=(l // tc,),
        in_specs=[pl.BlockSpec((b, tc, S5_D), lambda t: (0, t, S5_OFF // S5_D))] + [_const_spec(x, 1) for x in prm],
        out_specs=[pl.BlockSpec((b, tc, S5_D), lambda t: (0, t, 0)),
                   pl.BlockSpec((b, 1, S5_STATE), lambda t: (0, 0, 0)),
                   pl.BlockSpec((b, 1, S5_STATE), lambda t: (0, 0, 0))],
        out_shape=[jax.ShapeDtypeStruct((b, l, S5_D), BF16),
                   jax.ShapeDtypeStruct((b, 1, S5_STATE), F32),
                   jax.ShapeDtypeStruct((b, 1, S5_STATE), F32)],
        scratch_shapes=[pltpu.VMEM((2 * b, S5_STATE), F32), pltpu.VMEM((2 * b, S5_STATE), F32)],
        compiler_params=_cp(("arbitrary",)),
        name="s5_seq",
    )(proj, *_args(prm))


def _s5_step_kernel(u_ref, h0r_ref, h0i_ref, bbr_ref, bbi_ref, ccr_ref, cci_ref, abr_ref, abi_ref, d_ref, gw_ref,
                    gb_ref, y_ref, hr_ref, hi_ref):
    u = u_ref[...]
    ar = abr_ref[...]
    ai = abi_ref[...]
    h0r = h0r_ref[...]
    h0i = h0i_ref[...]
    h_re = ar * h0r - ai * h0i + _s5_bu(u, bbr_ref, HI)
    h_im = ar * h0i + ai * h0r + _s5_bu(u, bbi_ref, HI)
    hr_ref[...] = h_re
    hi_ref[...] = h_im
    ch = _s5_ch(h_re.astype(BF16), h_im.astype(BF16), ccr_ref, cci_ref)
    y_ref[...] = _s5_glu(ch + d_ref[...] * u, gw_ref, gb_ref).astype(BF16)


def s5_step(proj, h0_re, h0_im, prm):
    b = proj.shape[0]
    return pl.pallas_call(
        _s5_step_kernel,
        grid=(1,),
        in_specs=[pl.BlockSpec((b, S5_D), lambda i: (0, S5_OFF // S5_D)),
                  _const_spec(h0_re, 1), _const_spec(h0_im, 1)] + [_const_spec(x, 1) for x in prm],
        out_specs=[pl.BlockSpec((b, S5_D), lambda i: (0, 0)),
                   pl.BlockSpec((b, S5_STATE), lambda i: (0, 0)),
                   pl.BlockSpec((b, S5_STATE), lambda i: (0, 0))],
        out_shape=[jax.ShapeDtypeStruct((b, S5_D), BF16),
                   jax.ShapeDtypeStruct((b, S5_STATE), F32),
                   jax.ShapeDtypeStruct((b, S5_STATE), F32)],
        compiler_params=_cp(("arbitrary",)),
        name="s5_step",
    )(proj, h0_re, h0_im, *_args(prm))


def _gdn_pre(qkv_c, z, ab, alog_ref, dtb_ref):
    act = _silu(qkv_c)
    qs, ks = [], []
    for h in range(GDN_H):
        qh = act[:, h * GDN_HD:(h + 1) * GDN_HD]
        kh = act[:, GDN_D + h * GDN_HD:GDN_D + (h + 1) * GDN_HD]
        qs.append(qh * lax.rsqrt(jnp.sum(qh * qh, axis=-1, keepdims=True) + 1e-6))
        ks.append(kh * lax.rsqrt(jnp.sum(kh * kh, axis=-1, keepdims=True) + 1e-6))
    v = act[:, 2 * GDN_D:3 * GDN_D]
    g = -jnp.exp(alog_ref[...]) * jax.nn.softplus(ab + dtb_ref[...])
    beta = jax.nn.sigmoid(ab)
    return qs, ks, v, g, beta


def _gdn_post(o, z_h, nw):
    on = o * lax.rsqrt(jnp.mean(o * o, axis=-1, keepdims=True) + NORM_EPS) * nw
    return on * _silu(z_h)


def _gdn_seq_kernel(p_ref, cw_ref, alog_ref, dtb_ref, nw_ref, y_ref, s_out_ref, s_ref, prev_ref):
    c = CHUNK
    ns = p_ref.shape[0]
    n = ns * c
    t = pl.program_id(1)

    @pl.when(t == 0)
    def _():
        s_ref[...] = jnp.zeros_like(s_ref)
        prev_ref[...] = jnp.zeros_like(prev_ref)

    cw = cw_ref[...]
    convs = []
    for i in range(ns):
        qkv = p_ref[i, :, 0:3 * GDN_D]
        ext = jnp.concatenate([prev_ref[i], qkv], axis=0)
        prev_ref[i] = qkv[c - 8:c]
        conv = cw[3:4] * qkv
        for j in (1, 2, 3):
            conv = conv + cw[3 - j:4 - j] * pltpu.roll(ext, j, axis=0)[8:]
        convs.append(conv)
    x = p_ref[...].reshape(n, GDN_W)
    z = x[:, 3 * GDN_D:4 * GDN_D]
    ab = x[:, 4 * GDN_D:4 * GDN_D + 128]
    qs, ks, v, g, beta = _gdn_pre(jnp.concatenate(convs, axis=0), z, ab, alog_ref, dtb_ref)

    shift = c.bit_length() - 1
    ri = _iota((n, n), 0)
    ci = _iota((n, n), 1)
    tril = ((lax.shift_right_logical(ri, shift) == lax.shift_right_logical(ci, shift)) & (ci <= ri)).astype(F32)
    gc_all = _dot(tril, g, HI)

    hc = GDN_H * c
    heads = range(GDN_H)
    stack = lambda pieces: jnp.concatenate(pieces, axis=0)
    lane = _iota((c, 128), 1)
    rr = _iota((hc, hc), 0)
    cc = _iota((hc, hc), 1)
    same = lax.shift_right_logical(rr, shift) == lax.shift_right_logical(cc, shift)
    causal = same & (cc <= rr)
    strict = same & (cc < rr)
    ones = jnp.ones((hc, 128), F32)
    nw = nw_ref[...]
    for i in range(ns):
        sq = slice(i * c, (i + 1) * c)
        gc = gc_all[sq]
        g_col = stack([gc[:, h:h + 1] for h in heads])
        b_col = stack([beta[sq, GDN_H + h:GDN_H + h + 1] for h in heads])
        g_row = _dot_nt(ones, stack([jnp.where(lane == h, gc, 0.0) for h in heads]), HI)
        decay = jnp.where(causal, jnp.exp(jnp.where(causal, g_col - g_row, 0.0)), 0.0)
        q = stack([qs[h][sq] for h in heads]) * (GDN_HD ** -0.5)
        k = stack([ks[h][sq] for h in heads])
        v_s = stack([v[sq, h * GDN_HD:(h + 1) * GDN_HD] for h in heads])
        kb = k * b_col
        m1 = _dot_nt(stack([kb, q]).astype(BF16), k.astype(BF16))
        lmat = jnp.where(strict, m1[:hc] * decay, 0.0)
        attn = m1[hc:] * decay
        eg = jnp.exp(g_col)
        sol = _neumann_solve3(-lmat, jnp.concatenate([v_s * b_col, kb * eg], axis=1), shift)
        u = sol[:, :GDN_HD]
        wk = sol[:, GDN_HD:]
        qd = q * eg
        ws = [_dot(stack([wk[h * c:(h + 1) * c], qd[h * c:(h + 1) * c]]).astype(BF16),
                   s_ref[i * GDN_H + h].astype(BF16)) for h in heads]
        v_new = u - stack([w[:c] for w in ws])
        o = stack([w[c:] for w in ws]) + _bdot(attn, v_new)
        g_last = [gc[c - 1:c, h:h + 1] for h in heads]
        k_tail = k * jnp.exp(stack([jnp.broadcast_to(gl, (c, 1)) for gl in g_last]) - g_col)
        for h in heads:
            rows = slice(h * c, (h + 1) * c)
            s_ref[i * GDN_H + h] = (s_ref[i * GDN_H + h] * jnp.exp(g_last[h])
                                    + _dot_tn(k_tail[rows].astype(BF16), v_new[rows].astype(BF16)))
            z_h = z[sq, h * GDN_HD:(h + 1) * GDN_HD]
            y_ref[i, :, h * GDN_HD:(h + 1) * GDN_HD] = _gdn_post(o[rows], z_h, nw).astype(BF16)

    @pl.when(t == pl.num_programs(1) - 1)
    def _():
        s_out_ref[...] = s_ref[...].reshape(s_out_ref.shape)


def gdn_seq(proj, prm, ns):
    b, l, _ = proj.shape
    c = CHUNK
    return pl.pallas_call(
        _gdn_seq_kernel,
        grid=(b // ns, l // c),
        in_specs=[pl.BlockSpec((ns, c, GDN_W), lambda b, t: (b, t, GDN_OFF // GDN_W))] + [_const_spec(x, 2) for x in prm],
        out_specs=[pl.BlockSpec((ns, c, GDN_D), lambda b, t: (b, t, 0)),
                   pl.BlockSpec((ns, GDN_H, GDN_HD, GDN_HD), lambda b, t: (b, 0, 0, 0))],
        out_shape=[jax.ShapeDtypeStruct((b, l, GDN_D), BF16),
                   jax.ShapeDtypeStruct((b, GDN_H, GDN_HD, GDN_HD), F32)],
        scratch_shapes=[pltpu.VMEM((ns * GDN_H, GDN_HD, GDN_HD), F32),
                        pltpu.VMEM((ns, 8, 3 * GDN_D), F32)],
        compiler_params=_cp(("arbitrary", "arbitrary")),
        name="gdn_seq",
    )(proj, *_args(prm))


def _gdn_step_kernel(nb, p_ref, cs_ref, s_in_ref, cw_ref, alog_ref, dtb_ref, nw_ref, y_ref, s_out_ref):
    x = p_ref[...]
    qkv = x[:, 0:3 * GDN_D]
    z = x[:, 3 * GDN_D:4 * GDN_D]
    ab = x[:, 4 * GDN_D:4 * GDN_D + 128]
    cs = cs_ref[...]
    cw = cw_ref[...]
    w3 = 3 * GDN_D
    conv = cw[0:1] * cs[:, 0:w3] + cw[1:2] * cs[:, w3:2 * w3] + cw[2:3] * cs[:, 2 * w3:3 * w3] + cw[3:4] * qkv
    qs, ks, v, g, beta = _gdn_pre(conv, z, ab, alog_ref, dtb_ref)
    nw = nw_ref[...]
    pad = jnp.zeros((GDN_HD - nb, GDN_HD), F32)
    ks_t = [jnp.concatenate([ks[h], pad], axis=0).T for h in range(GDN_H)]
    qs_t = [jnp.concatenate([qs[h], pad], axis=0).T for h in range(GDN_H)]
    for i in range(nb):
        for h in range(GDN_H):
            k_c = ks_t[h][:, i:i + 1]
            q_c = qs_t[h][:, i:i + 1]
            eg = jnp.exp(g[i:i + 1, h:h + 1])
            b_s = beta[i:i + 1, GDN_H + h:GDN_H + h + 1]
            s = s_in_ref[i, h]
            sk = jnp.sum(k_c * s, axis=0, keepdims=True)
            v_row = v[i:i + 1, h * GDN_HD:(h + 1) * GDN_HD]
            s_new = s * eg + (k_c * b_s) * (v_row - eg * sk)
            s_out_ref[i, h] = s_new
            o = jnp.sum(q_c * s_new, axis=0, keepdims=True) * (GDN_HD ** -0.5)
            z_h = z[i:i + 1, h * GDN_HD:(h + 1) * GDN_HD]
            y_ref[i:i + 1, h * GDN_HD:(h + 1) * GDN_HD] = _gdn_post(o, z_h, nw).astype(BF16)


def gdn_step(proj, conv_state, state, layer, prm, nb):
    b = proj.shape[0]
    blk = (nb, GDN_H, GDN_HD, GDN_HD)
    return pl.pallas_call(
        functools.partial(_gdn_step_kernel, nb),
        grid=(b // nb,),
        in_specs=[pl.BlockSpec((nb, GDN_W), lambda i: (i, GDN_OFF // GDN_W)),
                  pl.BlockSpec((nb, 9 * GDN_D), lambda i: (i, 0)),
                  pl.BlockSpec((None,) + blk, lambda i: (layer, i, 0, 0, 0))] + [_const_spec(x, 1) for x in prm],
        out_specs=[pl.BlockSpec((nb, GDN_D), lambda i: (i, 0)),
                   pl.BlockSpec(blk, lambda i: (i, 0, 0, 0))],
        out_shape=[jax.ShapeDtypeStruct((b, GDN_D), BF16),
                   jax.ShapeDtypeStruct(state.shape[1:], F32)],
        compiler_params=_cp(("arbitrary",)),
        name="gdn_step",
    )(proj, conv_state, state, *_args(prm))


def _lru_gates(xc, gate, wr_ref, br_ref, wi_ref, bi_ref, lam_ref):
    r = jax.nn.sigmoid(_bdot(xc, wr_ref[...]) + br_ref[...])
    i = jax.nn.sigmoid(_bdot(xc, wi_ref[...]) + bi_ref[...])
    log_a = -LRU_C * r * jax.nn.softplus(-lam_ref[...])
    a = jnp.exp(log_a)
    one_minus_a2 = -jnp.tanh(log_a) * (a * a + 1.0)
    b = jnp.sqrt(one_minus_a2) * (i * xc)
    return a, b, _gelu(gate)


def _lru_seq_kernel(p_ref, cw_ref, cb_ref, wr_ref, br_ref, wi_ref, bi_ref, lam_ref, y_ref, h_out_ref,
                    prev_ref, carry_ref):
    ns, tc = p_ref.shape[0], p_ref.shape[1]
    n = ns * tc
    t = pl.program_id(0)

    @pl.when(t == 0)
    def _():
        prev_ref[...] = jnp.zeros_like(prev_ref)
        carry_ref[...] = jnp.zeros_like(carry_ref)

    p_fwd = _interleave_perm(ns, tc).astype(BF16)
    x_hi, x_lo = _split(p_ref[...].reshape(n, LRU_W))
    x = _dot(p_fwd, x_hi) + _dot(p_fwd, x_lo)
    xr = x[:, :LRU_D]
    gate = x[:, LRU_D:]
    halo = prev_ref.shape[0]
    ext = jnp.concatenate([prev_ref[...], xr], axis=0)
    prev_ref[...] = xr[n - halo:n]
    cw = cw_ref[...]
    xc = cw[3:4] * xr + cb_ref[...]
    for j in (1, 2, 3):
        xc = xc + cw[3 - j:4 - j] * pltpu.roll(ext, j * ns, axis=0)[halo:]
    a, b, gg = _lru_gates(xc, gate, wr_ref, br_ref, wi_ref, bi_ref, lam_ref)
    first = _iota((2 * ns, 1), 0) < ns
    h_prev = carry_ref[...]
    hs = []
    for s in range(n // (2 * ns)):
        a_s = a[s * 2 * ns:(s + 1) * 2 * ns]
        b_s = b[s * 2 * ns:(s + 1) * 2 * ns]
        h_a = a_s * pltpu.roll(h_prev, ns, axis=0) + b_s
        h_prev = jnp.where(first, h_a, a_s * pltpu.roll(h_a, ns, axis=0) + b_s)
        hs.append(h_prev)
    carry_ref[...] = h_prev
    y_t = (jnp.concatenate(hs, axis=0) * gg).astype(BF16)
    y_ref[...] = _dot_tn(p_fwd, y_t).astype(BF16).reshape(ns, tc, LRU_D)

    @pl.when(t == pl.num_programs(0) - 1)
    def _():
        h_out_ref[:, 0, :] = h_prev[ns:2 * ns]


def lru_seq(proj, prm):
    b, l, _ = proj.shape
    assert 2 * b == 8, "the time-major scan packs two time steps of all sequences into one 8-row sublane tile"
    tc = CHUNK
    return pl.pallas_call(
        _lru_seq_kernel,
        grid=(l // tc,),
        in_specs=[pl.BlockSpec((b, tc, LRU_W), lambda t: (0, t, LRU_OFF // LRU_W))] + [_const_spec(x, 1) for x in prm],
        out_specs=[pl.BlockSpec((b, tc, LRU_D), lambda t: (0, t, 0)),
                   pl.BlockSpec((b, 1, LRU_D), lambda t: (0, 0, 0))],
        out_shape=[jax.ShapeDtypeStruct((b, l, LRU_D), BF16),
                   jax.ShapeDtypeStruct((b, 1, LRU_D), F32)],
        scratch_shapes=[pltpu.VMEM((4 * b, LRU_D), F32), pltpu.VMEM((2 * b, LRU_D), F32)],
        compiler_params=_cp(("arbitrary",)),
        name="lru_seq",
    )(proj, *_args(prm))


def _lru_step_kernel(p_ref, cs_ref, h0_ref, cw_ref, cb_ref, wr_ref, br_ref, wi_ref, bi_ref, lam_ref, y_ref, h_ref):
    x = p_ref[...]
    xr = x[:, :LRU_D]
    gate = x[:, LRU_D:]
    cs = cs_ref[...]
    cw = cw_ref[...]
    xc = (cw[0:1] * cs[:, 0:LRU_D] + cw[1:2] * cs[:, LRU_D:2 * LRU_D] + cw[2:3] * cs[:, 2 * LRU_D:3 * LRU_D]
          + cw[3:4] * xr + cb_ref[...])
    a, b, gg = _lru_gates(xc, gate, wr_ref, br_ref, wi_ref, bi_ref, lam_ref)
    h = a * h0_ref[...] + b
    h_ref[...] = h
    y_ref[...] = (h * gg).astype(BF16)


def lru_step(proj, conv_state, h0, prm):
    b = proj.shape[0]
    return pl.pallas_call(
        _lru_step_kernel,
        grid=(1,),
        in_specs=[pl.BlockSpec((b, LRU_W), lambda i: (0, LRU_OFF // LRU_W)),
                  _const_spec(conv_state, 1), _const_spec(h0, 1)] + [_const_spec(x, 1) for x in prm],
        out_specs=[pl.BlockSpec((b, LRU_D), lambda i: (0, 0)),
                   pl.BlockSpec((b, LRU_D), lambda i: (0, 0))],
        out_shape=[jax.ShapeDtypeStruct((b, LRU_D), BF16),
                   jax.ShapeDtypeStruct((b, LRU_D), F32)],
        compiler_params=_cp(("arbitrary",)),
        name="lru_step",
    )(proj, conv_state, h0, *_args(prm))


def _pad_cols(x, width):
    return jnp.pad(x, [(0, 0)] * (x.ndim - 1) + [(0, width - x.shape[-1])])


def _pack_rwkv_cols(x):
    return jnp.concatenate([x[..., :1536], _pad_cols(x[..., 1536:1632], 128), _pad_cols(x[..., 1632:1728], 128),
                            x[..., 1728:1984]], axis=-1)


def _unpack_rwkv_cols(x):
    return jnp.concatenate([x[..., :1536], x[..., 1536:1632], x[..., 1664:1760], x[..., 1792:2048]], axis=-1)


def _pack_w_in_kernel(w_ref, o_ref):
    o_ref[...] = jnp.zeros(o_ref.shape, BF16)
    o = RWKV_PROJ
    for dst, a, b in ((0, 0, 1536), (1536, 1536, 1632), (1664, 1632, 1728), (1792, 1728, o), (S5_OFF, o, o + S5_D),
                      (GDN_OFF, o + S5_D, o + S5_D + GDN_PROJ), (LRU_OFF, o + S5_D + GDN_PROJ, o + S5_D + GDN_PROJ + 2 * LRU_D)):
        o_ref[0, dst:dst + (b - a), :] = w_ref[0, a:b, :].astype(BF16)


def pack_w_in(w_t, tk):
    depth, n, d = w_t.shape
    return pl.pallas_call(
        _pack_w_in_kernel,
        grid=(depth, d // tk),
        in_specs=[pl.BlockSpec((1, n, tk), lambda l, i: (l, 0, i))],
        out_specs=pl.BlockSpec((1, NP_IN, tk), lambda l, i: (l, 0, i)),
        out_shape=jax.ShapeDtypeStruct((depth, NP_IN, d), BF16),
        compiler_params=_cp(("arbitrary", "arbitrary")),
        name="pack_w_in",
    )(w_t)


def _block_diag(blocks):
    *lead, g, a, b = blocks.shape
    eye = jnp.eye(g, dtype=blocks.dtype)
    return (eye[:, None, :, None] * blocks[..., :, :, None, :]).reshape(*lead, g * a, g * b)


def _s5_params(lam_re, lam_im, log_dt, b_re, b_im, c_re, c_im, d, glu_w, glu_b):
    depth = lam_re.shape[0]
    dt = jnp.exp(log_dt)[..., None]
    mag = jnp.exp(lam_re * dt)
    ab_re, ab_im = mag * jnp.cos(lam_im * dt), mag * jnp.sin(lam_im * dt)
    den = lam_re * lam_re + lam_im * lam_im
    nr = ab_re - 1.0
    f_re = (nr * lam_re + ab_im * lam_im) / den
    f_im = (ab_im * lam_re - nr * lam_im) / den
    bb_re = f_re[..., None] * b_re - f_im[..., None] * b_im
    bb_im = f_re[..., None] * b_im + f_im[..., None] * b_re
    macro = lambda x: _block_diag(jnp.swapaxes(x, -1, -2).reshape(depth, S5_CG, S5_G // S5_CG, x.shape[-1], x.shape[-2]))
    bbr = macro(bb_re)
    bbi = macro(bb_im)
    tail = (macro(c_re).astype(BF16), macro(-c_im).astype(BF16),
            ab_re.reshape(depth, 1, S5_STATE), ab_im.reshape(depth, 1, S5_STATE), d.reshape(depth, 1, S5_D),
            glu_w, glu_b.reshape(depth, 1, S5_D))
    return (bbr.astype(BF16), bbi.astype(BF16)) + tail, (bbr, bbi) + tail


def _prepare_params(w):
    depth = w['w_in'].shape[0]
    row = lambda v: v.reshape(depth, 1, -1)
    pad_rows = lambda v: jnp.pad(v, ((0, 0), (0, 128 - v.shape[1]), (0, 0)))
    s5, s5_f32 = _s5_params(w['s5_lambda_re'], w['s5_lambda_im'], w['s5_log_dt'], w['s5_b_re'], w['s5_b_im'],
                            w['s5_c_re'], w['s5_c_im'], w['s5_d'], w['s5_glu_w'], w['s5_glu_b'])
    return {
        'norm1_g': row(w['norm1_g']), 'norm2_g': row(w['norm2_g']),
        'w_in': pack_w_in(jnp.swapaxes(w['w_in'], 1, 2), TILE['pack_cols']),
        'rwkv': (row(_pack_rwkv_cols(w['rwkv_mu'])), row(w['rwkv_w0']), pad_rows(w['rwkv_w_up']), row(w['rwkv_a0']),
                 pad_rows(w['rwkv_a_up']), w['rwkv_g_up'], row(w['rwkv_k_k']), row(w['rwkv_k_a']), row(w['rwkv_r_k']),
                 row(w['rwkv_ln_w']), row(w['rwkv_ln_b'])),
        's5': s5, 's5_f32': s5_f32,
        'gdn': (w['gdn_conv_w'], _pad_cols(row(w['gdn_a_log']), 128), _pad_cols(row(w['gdn_dt_bias']), 128),
                row(w['gdn_norm_w'])),
        'lru': (w['lru_conv_w'], row(w['lru_conv_b']), _block_diag(w['lru_wr']), row(w['lru_br']),
                _block_diag(w['lru_wi']), row(w['lru_bi']), row(w['lru_lambda'])),
        'ffn_conv_w': w['ffn_conv_w'], 'ffn_conv_b': row(w['ffn_conv_b']),
    }


def _layer(xp, xs, mod, st, p, seg, layer, stacked):
    bp, lp, d = xp.shape
    bs = xs.shape[1]
    sh1, sc1, g1, sh2, sc2, g2 = jnp.split(mod, 6, axis=-1)
    pm = lambda m: m[:bp, None, :]
    sm = lambda m: m[None, bp:bp + bs, :]

    lay = lambda x: Layered(x, layer)
    rw_prm, s5_prm, s5_prm_f32, gdn_prm, lru_prm = (tuple(lay(x) for x in p[k])
                                                    for k in ('rwkv', 's5', 's5_f32', 'gdn', 'lru'))

    proj = norm_matmul_in(xp, lay(p['norm1_g']), pm(sc1), pm(sh1), p['w_in'], layer, TILE['in_rows'], TILE['in_cols'])
    ya, s_wkv = rwkv_seq(proj, rw_prm, seg, bp)
    yb, s_re, s_im = s5_seq(proj, s5_prm)
    yc, s_gdn = gdn_seq(proj, gdn_prm, bp)
    yd, s_lru = lru_seq(proj, lru_prm)
    xp = matmul_res([ya, yb, yc, yd], stacked['w_out'], layer, xp, pm(g1), lp, TILE['out_cols'])
    h = norm_mod(xp, lay(p['norm2_g']), pm(sc2), pm(sh2), TILE['norm_rows'])
    act, lg, lv = ffn_up_seq(h, stacked['ffn_w_up'], layer, p['ffn_conv_w'], p['ffn_conv_b'], TILE['ffn_cols'])
    xp = matmul_res([act], stacked['ffn_w_down'], layer, xp, pm(g2), TILE['down_rows'], TILE['down_cols'])
    new_p = {
        'rwkv_wkv': s_wkv,
        'rwkv_shift': _unpack_rwkv_cols(proj[:, -1, :RW_W]),
        's5_re': s_re.reshape(bp, S5_G, S5_N), 's5_im': s_im.reshape(bp, S5_G, S5_N),
        'gdn': s_gdn,
        'gdn_conv': proj[:, -3:, GDN_OFF:GDN_OFF + 3 * GDN_D],
        'lru_h': s_lru.reshape(bp, LRU_D),
        'lru_conv': proj[:, -3:, LRU_OFF:LRU_OFF + LRU_D],
        'ffn_conv': jnp.concatenate([lg, lv], axis=-1),
    }

    proj = norm_matmul_in(xs, lay(p['norm1_g']), sm(sc1), sm(sh1), p['w_in'], layer, bs, TILE['in_cols'])[0]
    ya, s_wkv = rwkv_step(proj, _pack_rwkv_cols(st['rwkv_shift']), stacked['rwkv_wkv'], layer, rw_prm, seg)
    yb, s_re, s_im = s5_step(proj, st['s5_re'].reshape(bs, S5_STATE), st['s5_im'].reshape(bs, S5_STATE), s5_prm_f32)
    yc, s_gdn = gdn_step(proj, st['gdn_conv'].reshape(bs, 9 * GDN_D), stacked['gdn'], layer, gdn_prm, TILE['gdn_step_rows'])
    yd, s_lru = lru_step(proj, st['lru_conv'].reshape(bs, 3 * LRU_D), st['lru_h'], lru_prm)
    xs = matmul_res([y[None] for y in (ya, yb, yc, yd)], stacked['w_out'], layer, xs, sm(g1), bs, TILE['out_cols'])
    h = norm_mod(xs, lay(p['norm2_g']), sm(sc2), sm(sh2), bs)
    act, ug, uv = ffn_up_step(h[0], stacked['ffn_w_up'], layer, p['ffn_conv_w'], p['ffn_conv_b'],
                              stacked['ffn_conv'], TILE['ffn_cols'])
    xs = matmul_res([act[None]], stacked['ffn_w_down'], layer, xs, sm(g2), bs, TILE['down_cols'])
    shift_rows = lambda buf, new: jnp.concatenate([buf[:, 1:], new[:, None, :]], axis=1)
    new_s = {
        'rwkv_wkv': s_wkv,
        'rwkv_shift': _unpack_rwkv_cols(proj[:, :RW_W]),
        's5_re': s_re.reshape(bs, S5_G, S5_N), 's5_im': s_im.reshape(bs, S5_G, S5_N),
        'gdn': s_gdn,
        'gdn_conv': shift_rows(st['gdn_conv'], proj[:, GDN_OFF:GDN_OFF + 3 * GDN_D]),
        'lru_h': s_lru,
        'lru_conv': shift_rows(st['lru_conv'], proj[:, LRU_OFF:LRU_OFF + LRU_D]),
        'ffn_conv': shift_rows(st['ffn_conv'], jnp.concatenate([ug, uv], axis=-1)),
    }
    return xp, xs, new_p, new_s


STATE_ORDER = ('rwkv_wkv', 'rwkv_shift', 's5_re', 's5_im', 'gdn', 'gdn_conv', 'lru_h', 'lru_conv', 'ffn_conv')


def kernel(x_prompt, x_sample, c_prompt, c_sample, state_rwkv_wkv, state_rwkv_shift, state_s5_re, state_s5_im, state_gdn, state_gdn_conv, state_lru_h, state_lru_conv, state_ffn_conv, ada_w, ada_b, norm1_g, norm2_g, final_g, w_in, w_out, rwkv_mu, rwkv_w0, rwkv_w_up, rwkv_a0, rwkv_a_up, rwkv_g_up, rwkv_k_k, rwkv_k_a, rwkv_r_k, rwkv_ln_w, rwkv_ln_b, s5_lambda_re, s5_lambda_im, s5_log_dt, s5_b_re, s5_b_im, s5_c_re, s5_c_im, s5_d, s5_glu_w, s5_glu_b, gdn_conv_w, gdn_a_log, gdn_dt_bias, gdn_norm_w, lru_conv_w, lru_conv_b, lru_wr, lru_br, lru_wi, lru_bi, lru_lambda, ffn_w_up, ffn_conv_w, ffn_conv_b, ffn_w_down):
    weights = {
        'norm1_g': norm1_g, 'norm2_g': norm2_g, 'w_in': w_in,
        'rwkv_mu': rwkv_mu, 'rwkv_w0': rwkv_w0, 'rwkv_w_up': rwkv_w_up, 'rwkv_a0': rwkv_a0,
        'rwkv_a_up': rwkv_a_up, 'rwkv_g_up': rwkv_g_up, 'rwkv_k_k': rwkv_k_k, 'rwkv_k_a': rwkv_k_a,
        'rwkv_r_k': rwkv_r_k, 'rwkv_ln_w': rwkv_ln_w, 'rwkv_ln_b': rwkv_ln_b,
        's5_lambda_re': s5_lambda_re, 's5_lambda_im': s5_lambda_im, 's5_log_dt': s5_log_dt,
        's5_b_re': s5_b_re, 's5_b_im': s5_b_im, 's5_c_re': s5_c_re, 's5_c_im': s5_c_im, 's5_d': s5_d,
        's5_glu_w': s5_glu_w, 's5_glu_b': s5_glu_b,
        'gdn_conv_w': gdn_conv_w, 'gdn_a_log': gdn_a_log, 'gdn_dt_bias': gdn_dt_bias, 'gdn_norm_w': gdn_norm_w,
        'lru_conv_w': lru_conv_w, 'lru_conv_b': lru_conv_b, 'lru_wr': lru_wr, 'lru_br': lru_br,
        'lru_wi': lru_wi, 'lru_bi': lru_bi, 'lru_lambda': lru_lambda,
        'ffn_conv_w': ffn_conv_w, 'ffn_conv_b': ffn_conv_b,
    }
    stacked = {'w_out': w_out, 'ffn_w_up': ffn_w_up, 'ffn_w_down': ffn_w_down,
               'rwkv_wkv': jnp.transpose(state_rwkv_wkv, (0, 2, 3, 4, 1)),
               'gdn': state_gdn, 'ffn_conv': state_ffn_conv}
    cache = {'rwkv_wkv': state_rwkv_wkv, 'rwkv_shift': state_rwkv_shift, 's5_re': state_s5_re,
             's5_im': state_s5_im, 'gdn': state_gdn, 'gdn_conv': state_gdn_conv, 'lru_h': state_lru_h,
             'lru_conv': state_lru_conv, 'ffn_conv': state_ffn_conv}
    depth = ada_w.shape[0]
    bp = x_prompt.shape[0]
    bs = x_sample.shape[0]
    rows = -(-(bp + bs) // 8) * 8
    c_all = jnp.pad(jnp.concatenate([c_prompt, c_sample], axis=0), ((0, rows - bp - bs), (0, 0)))
    mod = ada_mod(c_all, ada_w, ada_b)
    ids = jnp.arange(RWKV_D) // RWKV_HD
    seg = (ids[:, None] == ids[None, :]).astype(BF16)

    xp = x_prompt
    xs = jnp.swapaxes(x_sample, 0, 1)
    new_p = {n: [] for n in STATE_ORDER}
    new_s = {n: [] for n in STATE_ORDER}
    p = _prepare_params(weights)
    for l in range(depth):
        st = {n: cache[n][l] for n in STATE_ORDER}
        xp, xs, sp, ss = _layer(xp, xs, mod[l], st, p, seg, l, stacked)
        for n in STATE_ORDER:
            new_p[n].append(sp[n])
            new_s[n].append(ss[n])
    y_prompt = final_norm(xp, final_g, TILE['norm_rows'])
    y_sample = jnp.swapaxes(final_norm(xs, final_g, bs), 0, 1)
    outs_p = tuple(jnp.stack(new_p[n], axis=0) for n in STATE_ORDER)
    stacked_s = {n: jnp.stack(new_s[n], axis=0) for n in STATE_ORDER}
    stacked_s['rwkv_wkv'] = jnp.transpose(stacked_s['rwkv_wkv'], (0, 4, 1, 2, 3))
    outs_s = tuple(stacked_s[n] for n in STATE_ORDER)
    return (y_prompt, y_sample) + outs_p + outs_s
```

```python
import functools
import math
from typing import NamedTuple

import jax
import jax.numpy as jnp
from jax import lax
from jax.experimental import pallas as pl
from jax.experimental.pallas import tpu as pltpu

F32 = jnp.float32
BF16 = jnp.bfloat16
HI = lax.Precision.HIGHEST

NORM_EPS = 1e-6
RWKV_LN_EPS = 64e-5
LRU_C = 8.0

D_MODEL = 2048
RWKV_D, RWKV_HD, RWKV_H = 512, 64, 8
RWKV_R_DECAY, RWKV_R_A, RWKV_R_GATE = 96, 96, 256
RWKV_PROJ = 3 * RWKV_D + RWKV_R_DECAY + RWKV_R_A + RWKV_R_GATE
S5_D, S5_CH, S5_G, S5_N = 512, 16, 32, 64
S5_STATE = S5_G * S5_N
GDN_D, GDN_HD, GDN_H = 512, 128, 4
GDN_PROJ = 4 * GDN_D + 2 * GDN_H
LRU_D, LRU_BLOCKS = 512, 8
D_FF = 5632

RW_W = 2048
S5_OFF = 2048
GDN_OFF, GDN_W = 2560, 2560
LRU_OFF, LRU_W = 5120, 1024
NP_IN = 6144

CHUNK = 64
RWKV_GH = 4
assert CHUNK == RWKV_HD

V7X_VMEM_BYTES = 64 * 1024 * 1024
VMEM_LIMIT = V7X_VMEM_BYTES - 8 * 1024 * 1024

TILE = {
    'ada_cols': 1024,
    'norm_rows': 1024,
    'in_rows': 1024, 'in_cols': 1024,
    'out_cols': 512,
    'ffn_cols': 256,
    'down_rows': 1024, 'down_cols': 256,
    'pack_cols': 256,
    'gdn_step_rows': 16,
}


def _cp(sem):
    return pltpu.CompilerParams(dimension_semantics=sem, vmem_limit_bytes=VMEM_LIMIT)


def _dot(a, b, prec=None):
    return jnp.dot(a, b, preferred_element_type=F32, precision=prec)


def _dot_nt(a, b, prec=None):
    return lax.dot_general(a, b, (((1,), (1,)), ((), ())), preferred_element_type=F32, precision=prec)


def _dot_tn(a, b, prec=None):
    return lax.dot_general(a, b, (((0,), (0,)), ((), ())), preferred_element_type=F32, precision=prec)


def _bdot(a, b):
    return _dot(a.astype(BF16), b.astype(BF16))


def _silu(x):
    return x * jax.nn.sigmoid(x)


def _gelu(x):
    return 0.5 * x * (1.0 + jnp.tanh(math.sqrt(2.0 / math.pi) * (x + 0.044715 * (x * x * x))))


def _seg_dot(x, seg):
    hi = x.astype(BF16)
    lo = (x - hi.astype(F32)).astype(BF16)
    n = x.shape[0]
    r = _dot(jnp.concatenate([hi, lo], axis=0), seg)
    return r[:n] + r[n:]


def _iota(shape, dim):
    return lax.broadcasted_iota(jnp.int32, shape, dim)


class Layered(NamedTuple):
    arr: jax.Array
    layer: int


def _const_spec(x, nidx):
    if isinstance(x, Layered):
        shape = (None,) + x.arr.shape[1:]
        idx = (x.layer,) + (0,) * (x.arr.ndim - 1)
    else:
        shape = x.shape
        idx = (0,) * x.ndim
    if nidx == 1:
        return pl.BlockSpec(shape, lambda i: idx)
    return pl.BlockSpec(shape, lambda b, t: idx)


def _args(params):
    return [x.arr if isinstance(x, Layered) else x for x in params]


def _neumann_solve(n, rhs, steps):
    x = rhs
    p = n
    for i in range(steps):
        x = x + _bdot(p, x)
        if i + 1 < steps:
            p = _bdot(p, p)
    return x


def _split(x):
    hi = x.astype(BF16)
    return hi, (x - hi.astype(F32)).astype(BF16)


def _dot3(a, b):
    ah, al = _split(a)
    bh, bl = _split(b)
    return _dot(ah, bh) + (_dot(ah, bl) + _dot(al, bh))


def _neumann_solve3(n, rhs, steps):
    x = rhs
    p = n
    for i in range(steps):
        x = x + _dot3(p, x)
        if i + 1 < steps:
            p = _dot3(p, p)
    return x


def _row_to_col(row, eye):
    n = eye.shape[0]
    return jnp.sum(jnp.where(eye, jnp.broadcast_to(row, (n, n)), 0.0), axis=-1, keepdims=True)


def _ada_kernel(c_ref, w_ref, b_ref, o_ref):
    c = c_ref[...]
    o_ref[0] = _dot(_silu(c).astype(BF16), w_ref[0].astype(BF16)) + b_ref[0]


def ada_mod(c_all, ada_w, ada_b):
    depth, d, n = ada_w.shape
    r = c_all.shape[0]
    tn = TILE['ada_cols']
    return pl.pallas_call(
        _ada_kernel,
        grid=(depth, n // tn),
        in_specs=[pl.BlockSpec((r, d), lambda l, j: (0, 0)),
                  pl.BlockSpec((1, d, tn), lambda l, j: (l, 0, j)),
                  pl.BlockSpec((1, 1, tn), lambda l, j: (l, 0, j))],
        out_specs=pl.BlockSpec((1, r, tn), lambda l, j: (l, 0, j)),
        out_shape=jax.ShapeDtypeStruct((depth, r, n), F32),
        compiler_params=_cp(("arbitrary", "arbitrary")),
        name="ada_mod",
    )(c_all, ada_w, ada_b.reshape(depth, 1, n))


def _norm_mod_kernel(x_ref, g_ref, sc_ref, sh_ref, o_ref):
    x = x_ref[0]
    h = x * lax.rsqrt(jnp.mean(x * x, axis=-1, keepdims=True) + NORM_EPS) * g_ref[...]
    o_ref[0] = (h * (1.0 + sc_ref[0]) + sh_ref[0]).astype(o_ref.dtype)


def _mod_spec(arr, tm):
    d = arr.shape[-1]
    if arr.shape[1] == 1:
        return pl.BlockSpec((1, 1, d), lambda g, i: (g, 0, 0))
    return pl.BlockSpec((1, tm, d), lambda g, i: (g, i, 0))


def norm_mod(x, gain, sc, sh, tm):
    g, r, d = x.shape
    return pl.pallas_call(
        _norm_mod_kernel,
        grid=(g, r // tm),
        in_specs=[pl.BlockSpec((1, tm, d), lambda g, i: (g, i, 0)),
                  _const_spec(gain, 2),
                  _mod_spec(sc, tm), _mod_spec(sh, tm)],
        out_specs=pl.BlockSpec((1, tm, d), lambda g, i: (g, i, 0)),
        out_shape=jax.ShapeDtypeStruct((g, r, d), BF16),
        compiler_params=_cp(("arbitrary", "arbitrary")),
        name="norm_mod",
    )(x, *_args([gain]), sc, sh)


def _rms_kernel(x_ref, g_ref, o_ref):
    x = x_ref[0]
    o_ref[0] = x * lax.rsqrt(jnp.mean(x * x, axis=-1, keepdims=True) + NORM_EPS) * g_ref[...]


def final_norm(x, gain, tm):
    g, r, d = x.shape
    return pl.pallas_call(
        _rms_kernel,
        grid=(g, r // tm),
        in_specs=[pl.BlockSpec((1, tm, d), lambda g, i: (g, i, 0)),
                  pl.BlockSpec((1, d), lambda g, i: (0, 0))],
        out_specs=pl.BlockSpec((1, tm, d), lambda g, i: (g, i, 0)),
        out_shape=jax.ShapeDtypeStruct((g, r, d), F32),
        compiler_params=_cp(("arbitrary", "arbitrary")),
        name="final_norm",
    )(x, gain.reshape(1, d))


def _norm_mm_kernel(x_ref, g_ref, sc_ref, sh_ref, w_ref, o_ref, h_ref):
    @pl.when(pl.program_id(2) == 0)
    def _():
        x = x_ref[0]
        h = x * lax.rsqrt(jnp.mean(x * x, axis=-1, keepdims=True) + NORM_EPS) * g_ref[...]
        h_ref[...] = (h * (1.0 + sc_ref[0]) + sh_ref[0]).astype(BF16)

    o_ref[0] = _dot_nt(h_ref[...], w_ref[...])


def norm_matmul_in(x, gain, sc, sh, w_t, layer, tm, tn):
    g, r, k = x.shape
    n = w_t.shape[1]
    mod_spec = lambda m: (pl.BlockSpec((1, 1, k), lambda g, i, j: (g, 0, 0)) if m.shape[1] == 1
                          else pl.BlockSpec((1, tm, k), lambda g, i, j: (g, i, 0)))
    gain_spec = pl.BlockSpec((None, 1, k), lambda g, i, j: (gain.layer, 0, 0))
    return pl.pallas_call(
        _norm_mm_kernel,
        grid=(g, r // tm, n // tn),
        in_specs=[pl.BlockSpec((1, tm, k), lambda g, i, j: (g, i, 0)), gain_spec, mod_spec(sc), mod_spec(sh),
                  pl.BlockSpec((None, tn, k), lambda g, i, j: (layer, j, 0))],
        out_specs=pl.BlockSpec((1, tm, tn), lambda g, i, j: (g, i, j)),
        out_shape=jax.ShapeDtypeStruct((g, r, n), F32),
        scratch_shapes=[pltpu.VMEM((tm, k), BF16)],
        compiler_params=_cp(("arbitrary", "arbitrary", "arbitrary")),
        name="norm_matmul_in",
    )(x, gain.arr, sc, sh, w_t)


def _mm_res_kernel(n_a, *refs):
    a_refs = refs[:n_a]
    w_refs = refs[n_a:2 * n_a]
    x_ref, g_ref, o_ref = refs[2 * n_a:]
    acc = _dot(a_refs[0][0], w_refs[0][...].astype(BF16))
    for a_ref, w_ref in zip(a_refs[1:], w_refs[1:]):
        acc = acc + _dot(a_ref[0], w_ref[...].astype(BF16))
    o_ref[0] = x_ref[0] + g_ref[0] * acc


def matmul_res(a_list, w, layer, x, gate, tm, tn):
    g, r, n = x.shape
    n_a = len(a_list)
    k = a_list[0].shape[-1]
    a_specs = [pl.BlockSpec((1, tm, k), lambda g, i, j: (g, i, 0)) for _ in a_list]
    w_specs = [pl.BlockSpec((None, k, tn), functools.partial(lambda g, i, j, q: (layer, q, j), q=q))
               for q in range(n_a)]
    if gate.shape[1] == 1:
        g_spec = pl.BlockSpec((1, 1, tn), lambda g, i, j: (g, 0, j))
    else:
        g_spec = pl.BlockSpec((1, tm, tn), lambda g, i, j: (g, i, j))
    return pl.pallas_call(
        functools.partial(_mm_res_kernel, n_a),
        grid=(g, r // tm, n // tn),
        in_specs=a_specs + w_specs + [pl.BlockSpec((1, tm, tn), lambda g, i, j: (g, i, j)), g_spec],
        out_specs=pl.BlockSpec((1, tm, tn), lambda g, i, j: (g, i, j)),
        out_shape=jax.ShapeDtypeStruct((g, r, n), F32),
        compiler_params=_cp(("arbitrary", "arbitrary", "arbitrary")),
        name="matmul_res",
    )(*a_list, *([w] * n_a), x, gate)


def _ffn_up_seq_kernel(a_ref, wg_ref, wv_ref, cwg_ref, cwv_ref, cbg_ref, cbv_ref, act_ref, lg_ref, lv_ref):
    a = a_ref[0]
    rows = a.shape[0]
    rid8 = _iota((8, 1), 0)

    def conv(up, cw_ref, cb_ref):
        cw = cw_ref[...]
        return cw[0:1] * pltpu.roll(up, 2, axis=0) + cw[1:2] * pltpu.roll(up, 1, axis=0) + cw[2:3] * up + cb_ref[...]

    def head(up, cw_ref, cb_ref):
        cw = cw_ref[...]
        u8 = up[0:8]
        s1 = jnp.where(rid8 >= 1, pltpu.roll(u8, 1, axis=0), 0.0)
        s2 = jnp.where(rid8 >= 2, pltpu.roll(u8, 2, axis=0), 0.0)
        return cw[0:1] * s2 + cw[1:2] * s1 + cw[2:3] * u8 + cb_ref[...]

    up_g = _dot(a, wg_ref[...].astype(BF16))
    lg_ref[0] = up_g[rows - 2:rows]
    gate = conv(up_g, cwg_ref, cbg_ref)
    up_v = _dot(a, wv_ref[...].astype(BF16))
    lv_ref[0] = up_v[rows - 2:rows]
    val = conv(up_v, cwv_ref, cbv_ref)
    act_ref[0] = (_silu(gate) * val).astype(BF16)
    act_ref[0, 0:8, :] = (_silu(head(up_g, cwg_ref, cbg_ref)) * head(up_v, cwv_ref, cbv_ref)).astype(BF16)


def ffn_up_seq(h, w_up, layer, conv_w, conv_b, tn):
    b, l, d = h.shape
    nt = D_FF // tn
    return pl.pallas_call(
        _ffn_up_seq_kernel,
        grid=(b, nt),
        in_specs=[pl.BlockSpec((1, l, d), lambda b, j: (b, 0, 0)),
                  pl.BlockSpec((None, d, tn), lambda b, j: (layer, 0, j)),
                  pl.BlockSpec((None, d, tn), lambda b, j: (layer, 0, j + nt)),
                  pl.BlockSpec((None, 3, tn), lambda b, j: (layer, 0, j)),
                  pl.BlockSpec((None, 3, tn), lambda b, j: (layer, 0, j + nt)),
                  pl.BlockSpec((None, 1, tn), lambda b, j: (layer, 0, j)),
                  pl.BlockSpec((None, 1, tn), lambda b, j: (layer, 0, j + nt))],
        out_specs=[pl.BlockSpec((1, l, tn), lambda b, j: (b, 0, j)),
                   pl.BlockSpec((1, 2, tn), lambda b, j: (b, 0, j)),
                   pl.BlockSpec((1, 2, tn), lambda b, j: (b, 0, j))],
        out_shape=[jax.ShapeDtypeStruct((b, l, D_FF), BF16),
                   jax.ShapeDtypeStruct((b, 2, D_FF), F32),
                   jax.ShapeDtypeStruct((b, 2, D_FF), F32)],
        compiler_params=_cp(("arbitrary", "arbitrary")),
        name="ffn_up_seq",
    )(h, w_up, w_up, conv_w, conv_w, conv_b, conv_b)


def _ffn_up_step_kernel(a_ref, wg_ref, wv_ref, cwg_ref, cwv_ref, cbg_ref, cbv_ref,
                        stg_ref, stv_ref, act_ref, ug_ref, uv_ref):
    a = a_ref[...]
    up_g = _dot(a, wg_ref[...].astype(BF16))
    up_v = _dot(a, wv_ref[...].astype(BF16))
    ug_ref[...] = up_g
    uv_ref[...] = up_v
    cwg = cwg_ref[...]
    cwv = cwv_ref[...]
    gate = cwg[0:1] * stg_ref[:, 0, :] + cwg[1:2] * stg_ref[:, 1, :] + cwg[2:3] * up_g + cbg_ref[...]
    val = cwv[0:1] * stv_ref[:, 0, :] + cwv[1:2] * stv_ref[:, 1, :] + cwv[2:3] * up_v + cbv_ref[...]
    act_ref[...] = (_silu(gate) * val).astype(BF16)


def ffn_up_step(h, w_up, layer, conv_w, conv_b, conv_state, tn):
    b, d = h.shape
    nt = D_FF // tn
    return pl.pallas_call(
        _ffn_up_step_kernel,
        grid=(nt,),
        in_specs=[pl.BlockSpec((b, d), lambda j: (0, 0)),
                  pl.BlockSpec((None, d, tn), lambda j: (layer, 0, j)),
                  pl.BlockSpec((None, d, tn), lambda j: (layer, 0, j + nt)),
                  pl.BlockSpec((None, 3, tn), lambda j: (layer, 0, j)),
                  pl.BlockSpec((None, 3, tn), lambda j: (layer, 0, j + nt)),
                  pl.BlockSpec((None, 1, tn), lambda j: (layer, 0, j)),
                  pl.BlockSpec((None, 1, tn), lambda j: (layer, 0, j + nt)),
                  pl.BlockSpec((None, b, 2, tn), lambda j: (layer, 0, 0, j)),
                  pl.BlockSpec((None, b, 2, tn), lambda j: (layer, 0, 0, j + nt))],
        out_specs=[pl.BlockSpec((b, tn), lambda j: (0, j))] * 3,
        out_shape=[jax.ShapeDtypeStruct((b, D_FF), BF16),
                   jax.ShapeDtypeStruct((b, D_FF), F32),
                   jax.ShapeDtypeStruct((b, D_FF), F32)],
        compiler_params=_cp(("arbitrary",)),
        name="ffn_up_step",
    )(h, w_up, w_up, conv_w, conv_w, conv_b, conv_b, conv_state, conv_state)


def _rwkv_pre(p, prev, prm, seg):
    mu, w0, w_up, a0, a_up, g_up, k_k, k_a = prm
    xm = p + mu * (prev - p)
    r = xm[:, 0:512]
    k = xm[:, 512:1024]
    v = xm[:, 1024:1536]
    xw = xm[:, 1536:1664]
    xa = xm[:, 1664:1792]
    xg = xm[:, 1792:2048]
    log_w = -jnp.exp(-jax.nn.softplus(-(w0 + _bdot(jnp.tanh(xw), w_up))) - 0.5)
    a = jax.nn.sigmoid(a0 + _bdot(xa, a_up))
    g = _bdot(jax.nn.sigmoid(xg), g_up)
    kx = k * k_k
    kk = kx * lax.rsqrt(_seg_dot(kx * kx, seg) + 1e-12)
    k2 = k * (1.0 + (a - 1.0) * k_a)
    return r, k2, v, log_w, a, g, kk


def _rwkv_post(y, r, k2, v, g, r_k, ln_w, ln_b, seg):
    inv = 1.0 / RWKV_HD
    mean = _seg_dot(y, seg) * inv
    dlt = y - mean
    var = _seg_dot(dlt * dlt, seg) * inv
    yn = dlt * lax.rsqrt(var + RWKV_LN_EPS) * ln_w + ln_b
    bonus = _seg_dot(r * k2 * r_k, seg) * v
    return (yn + bonus) * g


def _rwkv_seq_kernel(p_ref, mu_ref, w0_ref, wup_ref, a0_ref, aup_ref, gup_ref, kk_ref, ka_ref, rk_ref,
                     lnw_ref, lnb_ref, seg_ref, y_ref, s_out_ref, s_ref, prev_ref):
    c = CHUNK
    ns = p_ref.shape[0]
    n = ns * c
    t = pl.program_id(1)

    @pl.when(t == 0)
    def _():
        s_ref[...] = jnp.zeros_like(s_ref)
        prev_ref[...] = jnp.zeros_like(prev_ref)

    p = p_ref[...].reshape(n, RW_W)
    seg = seg_ref[...]
    seq_rows = lambda x: jnp.concatenate([jnp.broadcast_to(x[i:i + 1], (c, x.shape[-1])) for i in range(ns)], axis=0)
    tloc = _iota((n, 1), 0) & (c - 1)
    prev = jnp.where(tloc == 0, seq_rows(prev_ref[...]), pltpu.roll(p, 1, axis=0))
    last_rows = lambda x: jnp.concatenate([x[(i + 1) * c - 1:(i + 1) * c] for i in range(ns)], axis=0)
    prev_ref[...] = last_rows(p)
    prm = (mu_ref[...], w0_ref[...], wup_ref[...], a0_ref[...], aup_ref[...], gup_ref[...], kk_ref[...], ka_ref[...])
    r, k2, v, log_w, a, g, kk = _rwkv_pre(p, prev, prm, seg)

    ri = _iota((n, n), 0)
    ci = _iota((n, n), 1)
    cshift = c.bit_length() - 1
    tril = ((lax.shift_right_logical(ri, cshift) == lax.shift_right_logical(ci, cshift)) & (ci <= ri)).astype(F32)
    cum = _dot(tril, log_w, HI)
    e_in = jnp.exp(cum)
    e_out = jnp.exp(-cum)
    e_ex = jnp.exp(cum - log_w)
    tot = last_rows(cum)
    e_tail = jnp.exp(seq_rows(tot) - cum)
    e_tot = jnp.exp(tot)
    alpha = -(a * kk)
    gw = RWKV_GH * RWKV_HD
    rr = _iota((gw, gw), 0)
    cc = _iota((gw, gw), 1)
    shift = RWKV_HD.bit_length() - 1
    own = lax.shift_right_logical(rr, shift) == lax.shift_right_logical(cc, shift)
    tri_strict = (cc & (c - 1)) < (rr & (c - 1))
    tri_incl = (cc & (c - 1)) <= (rr & (c - 1))
    eye = rr == cc

    ng = RWKV_H // RWKV_GH
    be_t = (kk * e_ex).astype(BF16)
    r_t = (r * e_in).astype(BF16)
    al_t = (alpha * e_out).astype(BF16)
    k_t = (k2 * e_out).astype(BF16)
    al_p = (alpha * e_tail).astype(BF16)
    k_p = (k2 * e_tail).astype(BF16)
    vb = v.astype(BF16)

    def expand(x, i, gi):
        xg = x[i * c:(i + 1) * c, gi * gw:(gi + 1) * gw]
        return jnp.where(own, jnp.concatenate([xg] * RWKV_GH, axis=0), jnp.zeros((), BF16))

    y_rows = []
    for i in range(ns):
        ys = []
        for gi in range(ng):
            s0 = s_ref[i * ng + gi]
            lhs = jnp.concatenate([expand(be_t, i, gi), expand(r_t, i, gi)], axis=0)
            rhs = jnp.concatenate([expand(al_t, i, gi), expand(k_t, i, gi)], axis=0)
            vv = expand(vb, i, gi)
            m1 = _dot_nt(lhs, rhs)
            m2 = _dot(lhs, s0.astype(BF16))
            l_a = jnp.where(tri_strict, m1[:gw, :gw], 0.0)
            l_k = jnp.where(tri_strict, m1[:gw, gw:], 0.0)
            u = _neumann_solve(l_a, m2[:gw] + _dot(l_k.astype(BF16), vv), cshift)
            uv = jnp.concatenate([u.astype(BF16), vv], axis=0)
            a_full = jnp.concatenate([jnp.where(tri_incl, m1[gw:, :gw], 0.0),
                                      jnp.where(tri_incl, m1[gw:, gw:], 0.0)], axis=1).astype(BF16)
            y_g = m2[gw:] + _dot(a_full, uv)
            ys.append(functools.reduce(lambda p_, q_: p_ + q_, [y_g[j * c:(j + 1) * c] for j in range(RWKV_GH)]))
            scale = _row_to_col(e_tot[i:i + 1, gi * gw:(gi + 1) * gw], eye)
            tails = jnp.concatenate([expand(al_p, i, gi), expand(k_p, i, gi)], axis=0)
            s_ref[i * ng + gi] = s0 * scale + _dot_tn(tails, uv)
        y_rows.append(jnp.concatenate(ys, axis=1))

    y = jnp.concatenate(y_rows, axis=0)
    out = _rwkv_post(y, r, k2, v, g, rk_ref[...], lnw_ref[...], lnb_ref[...], seg)
    y_ref[...] = out.astype(BF16).reshape(ns, c, RWKV_D)

    @pl.when(t == pl.num_programs(1) - 1)
    def _():
        for i in range(ns):
            for h in range(RWKV_H):
                gi, hh = divmod(h, RWKV_GH)
                s_out_ref[i, h] = s_ref[i * ng + gi, hh * RWKV_HD:(hh + 1) * RWKV_HD, hh * RWKV_HD:(hh + 1) * RWKV_HD]


def rwkv_seq(proj, prm, seg, ns):
    b, l, _ = proj.shape
    c = CHUNK
    gw = RWKV_GH * RWKV_HD
    return pl.pallas_call(
        _rwkv_seq_kernel,
        grid=(b // ns, l // c),
        in_specs=[pl.BlockSpec((ns, c, RW_W), lambda b, t: (b, t, 0))] + [_const_spec(x, 2) for x in prm] + [_const_spec(seg, 2)],
        out_specs=[pl.BlockSpec((ns, c, RWKV_D), lambda b, t: (b, t, 0)),
                   pl.BlockSpec((ns, RWKV_H, RWKV_HD, RWKV_HD), lambda b, t: (b, 0, 0, 0))],
        out_shape=[jax.ShapeDtypeStruct((b, l, RWKV_D), BF16),
                   jax.ShapeDtypeStruct((b, RWKV_H, RWKV_HD, RWKV_HD), F32)],
        scratch_shapes=[pltpu.VMEM((ns * (RWKV_H // RWKV_GH), gw, gw), F32),
                        pltpu.VMEM((ns, RW_W), F32)],
        compiler_params=_cp(("arbitrary", "arbitrary")),
        name="rwkv_seq",
    )(proj, *_args(prm), seg)


def _rwkv_step_kernel(p_ref, prev_ref, s_in_ref, mu_ref, w0_ref, wup_ref, a0_ref, aup_ref, gup_ref, kk_ref,
                      ka_ref, rk_ref, lnw_ref, lnb_ref, seg_ref, y_ref, s_out_ref, nat_scr, col_scr, yt_scr):
    h = pl.program_id(0)

    @pl.when(h == 0)
    def _():
        prm = (mu_ref[...], w0_ref[...], wup_ref[...], a0_ref[...], aup_ref[...], gup_ref[...], kk_ref[...],
               ka_ref[...])
        r, k2, v, log_w, a, g, kk = _rwkv_pre(p_ref[...], prev_ref[...], prm, seg_ref[...])
        for i, x in enumerate((r, k2, v, g)):
            nat_scr[i] = x
        for i, x in enumerate((jnp.exp(log_w), kk * a, kk, k2, r, v)):
            col_scr[i] = x.T

    rows = pl.ds(pl.multiple_of(h * RWKV_HD, RWKV_HD), RWKV_HD)
    w_t, kka_t, kk_t, k_t, r_t, v_t = (col_scr[i, rows, :] for i in range(6))
    nacc = 4
    acc = [kk_t[j:j + 1] * s_in_ref[j] for j in range(nacc)]
    for k in range(nacc, RWKV_HD):
        acc[k % nacc] = acc[k % nacc] + kk_t[k:k + 1] * s_in_ref[k]
    sa = (acc[0] + acc[1]) + (acc[2] + acc[3])
    acc = [None] * nacc
    for k in range(RWKV_HD):
        s_new = s_in_ref[k] * w_t[k:k + 1] - kka_t[k:k + 1] * sa + k_t[k:k + 1] * v_t
        s_out_ref[k] = s_new
        term = r_t[k:k + 1] * s_new
        acc[k % nacc] = term if acc[k % nacc] is None else acc[k % nacc] + term
    yt_scr[rows, :] = (acc[0] + acc[1]) + (acc[2] + acc[3])

    @pl.when(h == RWKV_H - 1)
    def _():
        out = _rwkv_post(yt_scr[...].T, nat_scr[0], nat_scr[1], nat_scr[2], nat_scr[3], rk_ref[...], lnw_ref[...],
                         lnb_ref[...], seg_ref[...])
        y_ref[...] = out.astype(BF16)


def rwkv_step(proj, shift, state_t, layer, prm, seg):
    b = proj.shape[0]
    blk = (None, RWKV_HD, RWKV_HD, b)
    return pl.pallas_call(
        _rwkv_step_kernel,
        grid=(RWKV_H,),
        in_specs=[pl.BlockSpec((b, RW_W), lambda h: (0, 0)),
                  pl.BlockSpec((b, RW_W), lambda h: (0, 0)),
                  pl.BlockSpec((None,) + blk, lambda h: (layer, h, 0, 0, 0))]
                 + [_const_spec(x, 1) for x in prm] + [_const_spec(seg, 1)],
        out_specs=[pl.BlockSpec((b, RWKV_D), lambda h: (0, 0)),
                   pl.BlockSpec(blk, lambda h: (h, 0, 0, 0))],
        out_shape=[jax.ShapeDtypeStruct((b, RWKV_D), BF16),
                   jax.ShapeDtypeStruct(state_t.shape[1:], F32)],
        scratch_shapes=[pltpu.VMEM((4, b, RWKV_D), F32), pltpu.VMEM((6, RWKV_D, b), F32),
                        pltpu.VMEM((RWKV_D, b), F32)],
        compiler_params=_cp(("arbitrary",)),
        name="rwkv_step",
    )(proj, shift, state_t, *_args(prm), seg)


def _s5_glu(y, gw_ref, gb_ref):
    z = _gelu(y)
    return z * jax.nn.sigmoid(_bdot(z, gw_ref[...]) + gb_ref[...])


S5_CG = S5_D // 128
S5_SW = S5_STATE // S5_CG


def _s5_bu(u, bb_ref, prec=None):
    return jnp.concatenate([_dot(u[:, g * 128:(g + 1) * 128], bb_ref[g], prec) for g in range(S5_CG)], axis=1)


def _s5_ch(h_re, h_im, ccr_ref, cci_ref):
    return jnp.concatenate([_dot(h_re[:, g * S5_SW:(g + 1) * S5_SW], ccr_ref[g])
                            + _dot(h_im[:, g * S5_SW:(g + 1) * S5_SW], cci_ref[g]) for g in range(S5_CG)], axis=1)

def _interleave_perm(ns, tc):
    n = ns * tc
    r = _iota((n, n), 0)
    c = _iota((n, n), 1)
    t = lax.shift_right_logical(r, ns.bit_length() - 1)
    b = r & (ns - 1)
    return c == b * tc + t


def _s5_seq_kernel(u_ref, bbr_ref, bbi_ref, ccr_ref, cci_ref, abr_ref, abi_ref, d_ref, gw_ref, gb_ref,
                   y_ref, hr_ref, hi_ref, cr_ref, ci_ref):
    ns, tc = u_ref.shape[0], u_ref.shape[1]
    n = ns * tc
    t = pl.program_id(0)

    @pl.when(t == 0)
    def _():
        cr_ref[...] = jnp.zeros_like(cr_ref)
        ci_ref[...] = jnp.zeros_like(ci_ref)

    p_fwd = _interleave_perm(ns, tc).astype(BF16)
    u = u_ref[...].reshape(n, S5_D)
    u_hi, u_lo = _split(u)
    ub = _dot(p_fwd, u_hi)
    u_t = ub + _dot(p_fwd, u_lo)
    ub = ub.astype(BF16)
    bu_re = _s5_bu(ub, bbr_ref)
    bu_im = _s5_bu(ub, bbi_ref)
    ar = jnp.broadcast_to(abr_ref[...], (2 * ns, S5_STATE))
    ai = jnp.broadcast_to(abi_ref[...], (2 * ns, S5_STATE))
    first = _iota((2 * ns, 1), 0) < ns
    p_re = cr_ref[...]
    p_im = ci_ref[...]
    h_re, h_im = [], []
    for s in range(n // (2 * ns)):
        b_re = bu_re[s * 2 * ns:(s + 1) * 2 * ns]
        b_im = bu_im[s * 2 * ns:(s + 1) * 2 * ns]
        r_re = pltpu.roll(p_re, ns, axis=0)
        r_im = pltpu.roll(p_im, ns, axis=0)
        t_re = ar * r_re - ai * r_im + b_re
        t_im = ar * r_im + ai * r_re + b_im
        q_re = pltpu.roll(t_re, ns, axis=0)
        q_im = pltpu.roll(t_im, ns, axis=0)
        p_re = jnp.where(first, t_re, ar * q_re - ai * q_im + b_re)
        p_im = jnp.where(first, t_im, ar * q_im + ai * q_re + b_im)
        h_re.append(p_re)
        h_im.append(p_im)
    cr_ref[...] = p_re
    ci_ref[...] = p_im
    hb_re = jnp.concatenate(h_re, axis=0).astype(BF16)
    hb_im = jnp.concatenate(h_im, axis=0).astype(BF16)
    y_t = _s5_glu(_s5_ch(hb_re, hb_im, ccr_ref, cci_ref) + d_ref[...] * u_t, gw_ref, gb_ref)
    y = _dot_tn(p_fwd, y_t.astype(BF16))
    y_ref[...] = y.astype(BF16).reshape(ns, tc, S5_D)

    @pl.when(t == pl.num_programs(0) - 1)
    def _():
        hr_ref[:, 0, :] = p_re[ns:2 * ns]
        hi_ref[:, 0, :] = p_im[ns:2 * ns]


def s5_seq(proj, prm):
    b, l, _ = proj.shape
    assert 2 * b == 8, "the time-major scan packs two time steps of all sequences into one 8-row sublane tile"
    tc = CHUNK
    return pl.pallas_call(
        _s5_seq_kernel,
        grid=(l // tc,),
        in_specs=[pl.BlockSpec((b, tc, S5_D), lambda t: (0, t, S5_OFF // S5_D))] + [_const_spec(x, 1) for x in prm],
        out_specs=[pl.BlockSpec((b, tc, S5_D), lambda t: (0, t, 0)),
                   pl.BlockSpec((b, 1, S5_STATE), lambda t: (0, 0, 0)),
                   pl.BlockSpec((b, 1, S5_STATE), lambda t: (0, 0, 0))],
        out_shape=[jax.ShapeDtypeStruct((b, l, S5_D), BF16),
                   jax.ShapeDtypeStruct((b, 1, S5_STATE), F32),
                   jax.ShapeDtypeStruct((b, 1, S5_STATE), F32)],
        scratch_shapes=[pltpu.VMEM((2 * b, S5_STATE), F32), pltpu.VMEM((2 * b, S5_STATE), F32)],
        compiler_params=_cp(("arbitrary",)),
        name="s5_seq",
    )(proj, *_args(prm))


def _s5_step_kernel(u_ref, h0r_ref, h0i_ref, bbr_ref, bbi_ref, ccr_ref, cci_ref, abr_ref, abi_ref, d_ref, gw_ref,
                    gb_ref, y_ref, hr_ref, hi_ref):
    u = u_ref[...]
    ar = abr_ref[...]
    ai = abi_ref[...]
    h0r = h0r_ref[...]
    h0i = h0i_ref[...]
    h_re = ar * h0r - ai * h0i + _s5_bu(u, bbr_ref, HI)
    h_im = ar * h0i + ai * h0r + _s5_bu(u, bbi_ref, HI)
    hr_ref[...] = h_re
    hi_ref[...] = h_im
    ch = _s5_ch(h_re.astype(BF16), h_im.astype(BF16), ccr_ref, cci_ref)
    y_ref[...] = _s5_glu(ch + d_ref[...] * u, gw_ref, gb_ref).astype(BF16)


def s5_step(proj, h0_re, h0_im, prm):
    b = proj.shape[0]
    return pl.pallas_call(
        _s5_step_kernel,
        grid=(1,),
        in_specs=[pl.BlockSpec((b, S5_D), lambda i: (0, S5_OFF // S5_D)),
                  _const_spec(h0_re, 1), _const_spec(h0_im, 1)] + [_const_spec(x, 1) for x in prm],
        out_specs=[pl.BlockSpec((b, S5_D), lambda i: (0, 0)),
                   pl.BlockSpec((b, S5_STATE), lambda i: (0, 0)),
                   pl.BlockSpec((b, S5_STATE), lambda i: (0, 0))],
        out_shape=[jax.ShapeDtypeStruct((b, S5_D), BF16),
                   jax.ShapeDtypeStruct((b, S5_STATE), F32),
                   jax.ShapeDtypeStruct((b, S5_STATE), F32)],
        compiler_params=_cp(("arbitrary",)),
        name="s5_step",
    )(proj, h0_re, h0_im, *_args(prm))


def _gdn_pre(qkv_c, z, ab, alog_ref, dtb_ref):
    act = _silu(qkv_c)
    qs, ks = [], []
    for h in range(GDN_H):
        qh = act[:, h * GDN_HD:(h + 1) * GDN_HD]
        kh = act[:, GDN_D + h * GDN_HD:GDN_D + (h + 1) * GDN_HD]
        qs.append(qh * lax.rsqrt(jnp.sum(qh * qh, axis=-1, keepdims=True) + 1e-6))
        ks.append(kh * lax.rsqrt(jnp.sum(kh * kh, axis=-1, keepdims=True) + 1e-6))
    v = act[:, 2 * GDN_D:3 * GDN_D]
    g = -jnp.exp(alog_ref[...]) * jax.nn.softplus(ab + dtb_ref[...])
    beta = jax.nn.sigmoid(ab)
    return qs, ks, v, g, beta


def _gdn_post(o, z_h, nw):
    on = o * lax.rsqrt(jnp.mean(o * o, axis=-1, keepdims=True) + NORM_EPS) * nw
    return on * _silu(z_h)


def _gdn_seq_kernel(p_ref, cw_ref, alog_ref, dtb_ref, nw_ref, y_ref, s_out_ref, s_ref, prev_ref):
    c = CHUNK
    ns = p_ref.shape[0]
    n = ns * c
    t = pl.program_id(1)

    @pl.when(t == 0)
    def _():
        s_ref[...] = jnp.zeros_like(s_ref)
        prev_ref[...] = jnp.zeros_like(prev_ref)

    cw = cw_ref[...]
    convs = []
    for i in range(ns):
        qkv = p_ref[i, :, 0:3 * GDN_D]
        ext = jnp.concatenate([prev_ref[i], qkv], axis=0)
        prev_ref[i] = qkv[c - 8:c]
        conv = cw[3:4] * qkv
        for j in (1, 2, 3):
            conv = conv + cw[3 - j:4 - j] * pltpu.roll(ext, j, axis=0)[8:]
        convs.append(conv)
    x = p_ref[...].reshape(n, GDN_W)
    z = x[:, 3 * GDN_D:4 * GDN_D]
    ab = x[:, 4 * GDN_D:4 * GDN_D + 128]
    qs, ks, v, g, beta = _gdn_pre(jnp.concatenate(convs, axis=0), z, ab, alog_ref, dtb_ref)

    shift = c.bit_length() - 1
    ri = _iota((n, n), 0)
    ci = _iota((n, n), 1)
    tril = ((lax.shift_right_logical(ri, shift) == lax.shift_right_logical(ci, shift)) & (ci <= ri)).astype(F32)
    gc_all = _dot(tril, g, HI)

    hc = GDN_H * c
    heads = range(GDN_H)
    stack = lambda pieces: jnp.concatenate(pieces, axis=0)
    lane = _iota((c, 128), 1)
    rr = _iota((hc, hc), 0)
    cc = _iota((hc, hc), 1)
    same = lax.shift_right_logical(rr, shift) == lax.shift_right_logical(cc, shift)
    causal = same & (cc <= rr)
    strict = same & (cc < rr)
    ones = jnp.ones((hc, 128), F32)
    nw = nw_ref[...]
    for i in range(ns):
        sq = slice(i * c, (i + 1) * c)
        gc = gc_all[sq]
        g_col = stack([gc[:, h:h + 1] for h in heads])
        b_col = stack([beta[sq, GDN_H + h:GDN_H + h + 1] for h in heads])
        g_row = _dot_nt(ones, stack([jnp.where(lane == h, gc, 0.0) for h in heads]), HI)
        decay = jnp.where(causal, jnp.exp(jnp.where(causal, g_col - g_row, 0.0)), 0.0)
        q = stack([qs[h][sq] for h in heads]) * (GDN_HD ** -0.5)
        k = stack([ks[h][sq] for h in heads])
        v_s = stack([v[sq, h * GDN_HD:(h + 1) * GDN_HD] for h in heads])
        kb = k * b_col
        m1 = _dot_nt(stack([kb, q]).astype(BF16), k.astype(BF16))
        lmat = jnp.where(strict, m1[:hc] * decay, 0.0)
        attn = m1[hc:] * decay
        eg = jnp.exp(g_col)
        sol = _neumann_solve3(-lmat, jnp.concatenate([v_s * b_col, kb * eg], axis=1), shift)
        u = sol[:, :GDN_HD]
        wk = sol[:, GDN_HD:]
        qd = q * eg
        ws = [_dot(stack([wk[h * c:(h + 1) * c], qd[h * c:(h + 1) * c]]).astype(BF16),
                   s_ref[i * GDN_H + h].astype(BF16)) for h in heads]
        v_new = u - stack([w[:c] for w in ws])
        o = stack([w[c:] for w in ws]) + _bdot(attn, v_new)
        g_last = [gc[c - 1:c, h:h + 1] for h in heads]
        k_tail = k * jnp.exp(stack([jnp.broadcast_to(gl, (c, 1)) for gl in g_last]) - g_col)
        for h in heads:
            rows = slice(h * c, (h + 1) * c)
            s_ref[i * GDN_H + h] = (s_ref[i * GDN_H + h] * jnp.exp(g_last[h])
                                    + _dot_tn(k_tail[rows].astype(BF16), v_new[rows].astype(BF16)))
            z_h = z[sq, h * GDN_HD:(h + 1) * GDN_HD]
            y_ref[i, :, h * GDN_HD:(h + 1) * GDN_HD] = _gdn_post(o[rows], z_h, nw).astype(BF16)

    @pl.when(t == pl.num_programs(1) - 1)
    def _():
        s_out_ref[...] = s_ref[...].reshape(s_out_ref.shape)


def gdn_seq(proj, prm, ns):
    b, l, _ = proj.shape
    c = CHUNK
    return pl.pallas_call(
        _gdn_seq_kernel,
        grid=(b // ns, l // c),
        in_specs=[pl.BlockSpec((ns, c, GDN_W), lambda b, t: (b, t, GDN_OFF // GDN_W))] + [_const_spec(x, 2) for x in prm],
        out_specs=[pl.BlockSpec((ns, c, GDN_D), lambda b, t: (b, t, 0)),
                   pl.BlockSpec((ns, GDN_H, GDN_HD, GDN_HD), lambda b, t: (b, 0, 0, 0))],
        out_shape=[jax.ShapeDtypeStruct((b, l, GDN_D), BF16),
                   jax.ShapeDtypeStruct((b, GDN_H, GDN_HD, GDN_HD), F32)],
        scratch_shapes=[pltpu.VMEM((ns * GDN_H, GDN_HD, GDN_HD), F32),
                        pltpu.VMEM((ns, 8, 3 * GDN_D), F32)],
        compiler_params=_cp(("arbitrary", "arbitrary")),
        name="gdn_seq",
    )(proj, *_args(prm))


def _gdn_step_kernel(nb, p_ref, cs_ref, s_in_ref, cw_ref, alog_ref, dtb_ref, nw_ref, y_ref, s_out_ref, o_scr):
    x = p_ref[...]
    qkv = x[:, 0:3 * GDN_D]
    z = x[:, 3 * GDN_D:4 * GDN_D]
    ab = x[:, 4 * GDN_D:4 * GDN_D + 128]
    cs = cs_ref[...]
    cw = cw_ref[...]
    w3 = 3 * GDN_D
    conv = cw[0:1] * cs[:, 0:w3] + cw[1:2] * cs[:, w3:2 * w3] + cw[2:3] * cs[:, 2 * w3:3 * w3] + cw[3:4] * qkv
    qs, ks, v, g, beta = _gdn_pre(conv, z, ab, alog_ref, dtb_ref)
    nw = nw_ref[...]
    pad = jnp.zeros((GDN_HD - nb, GDN_HD), F32)
    ks_t = [jnp.concatenate([ks[h], pad], axis=0).T for h in range(GDN_H)]
    qs_t = [jnp.concatenate([qs[h], pad], axis=0).T for h in range(GDN_H)]
    for i in range(nb):
        for h in range(GDN_H):
            k_c = ks_t[h][:, i:i + 1]
            q_c = qs_t[h][:, i:i + 1]
            eg = jnp.exp(g[i:i + 1, h:h + 1])
            b_s = beta[i:i + 1, GDN_H + h:GDN_H + h + 1]
            s = s_in_ref[i, h]
            sk = jnp.sum(k_c * s, axis=0, keepdims=True)
            v_row = v[i:i + 1, h * GDN_HD:(h + 1) * GDN_HD]
            s_new = s * eg + (k_c * b_s) * (v_row - eg * sk)
            s_out_ref[i, h] = s_new
            o_scr[i:i + 1, h * GDN_HD:(h + 1) * GDN_HD] = jnp.sum(q_c * s_new, axis=0, keepdims=True)
    for h in range(GDN_H):
        cols = slice(h * GDN_HD, (h + 1) * GDN_HD)
        y_ref[:, cols] = _gdn_post(o_scr[:, cols] * (GDN_HD ** -0.5), z[:, cols], nw).astype(BF16)


def gdn_step(proj, conv_state, state, layer, prm, nb):
    b = proj.shape[0]
    blk = (nb, GDN_H, GDN_HD, GDN_HD)
    return pl.pallas_call(
        functools.partial(_gdn_step_kernel, nb),
        grid=(b // nb,),
        in_specs=[pl.BlockSpec((nb, GDN_W), lambda i: (i, GDN_OFF // GDN_W)),
                  pl.BlockSpec((nb, 9 * GDN_D), lambda i: (i, 0)),
                  pl.BlockSpec((None,) + blk, lambda i: (layer, i, 0, 0, 0))] + [_const_spec(x, 1) for x in prm],
        out_specs=[pl.BlockSpec((nb, GDN_D), lambda i: (i, 0)),
                   pl.BlockSpec(blk, lambda i: (i, 0, 0, 0))],
        out_shape=[jax.ShapeDtypeStruct((b, GDN_D), BF16),
                   jax.ShapeDtypeStruct(state.shape[1:], F32)],
        scratch_shapes=[pltpu.VMEM((nb, GDN_D), F32)],
        compiler_params=_cp(("arbitrary",)),
        name="gdn_step",
    )(proj, conv_state, state, *_args(prm))


def _lru_gates(xc, gate, wr_ref, br_ref, wi_ref, bi_ref, lam_ref):
    r = jax.nn.sigmoid(_bdot(xc, wr_ref[...]) + br_ref[...])
    i = jax.nn.sigmoid(_bdot(xc, wi_ref[...]) + bi_ref[...])
    log_a = -LRU_C * r * jax.nn.softplus(-lam_ref[...])
    a = jnp.exp(log_a)
    one_minus_a2 = -jnp.tanh(log_a) * (a * a + 1.0)
    b = jnp.sqrt(one_minus_a2) * (i * xc)
    return a, b, _gelu(gate)


def _lru_seq_kernel(p_ref, cw_ref, cb_ref, wr_ref, br_ref, wi_ref, bi_ref, lam_ref, y_ref, h_out_ref,
                    prev_ref, carry_ref):
    ns, tc = p_ref.shape[0], p_ref.shape[1]
    n = ns * tc
    t = pl.program_id(0)

    @pl.when(t == 0)
    def _():
        prev_ref[...] = jnp.zeros_like(prev_ref)
        carry_ref[...] = jnp.zeros_like(carry_ref)

    p_fwd = _interleave_perm(ns, tc).astype(BF16)
    x_hi, x_lo = _split(p_ref[...].reshape(n, LRU_W))
    x = _dot(p_fwd, x_hi) + _dot(p_fwd, x_lo)
    xr = x[:, :LRU_D]
    gate = x[:, LRU_D:]
    halo = prev_ref.shape[0]
    ext = jnp.concatenate([prev_ref[...], xr], axis=0)
    prev_ref[...] = xr[n - halo:n]
    cw = cw_ref[...]
    xc = cw[3:4] * xr + cb_ref[...]
    for j in (1, 2, 3):
        xc = xc + cw[3 - j:4 - j] * pltpu.roll(ext, j * ns, axis=0)[halo:]
    a, b, gg = _lru_gates(xc, gate, wr_ref, br_ref, wi_ref, bi_ref, lam_ref)
    first = _iota((2 * ns, 1), 0) < ns
    h_prev = carry_ref[...]
    hs = []
    for s in range(n // (2 * ns)):
        a_s = a[s * 2 * ns:(s + 1) * 2 * ns]
        b_s = b[s * 2 * ns:(s + 1) * 2 * ns]
        h_a = a_s * pltpu.roll(h_prev, ns, axis=0) + b_s
        h_prev = jnp.where(first, h_a, a_s * pltpu.roll(h_a, ns, axis=0) + b_s)
        hs.append(h_prev)
    carry_ref[...] = h_prev
    y_t = (jnp.concatenate(hs, axis=0) * gg).astype(BF16)
    y_ref[...] = _dot_tn(p_fwd, y_t).astype(BF16).reshape(ns, tc, LRU_D)

    @pl.when(t == pl.num_programs(0) - 1)
    def _():
        h_out_ref[:, 0, :] = h_prev[ns:2 * ns]


def lru_seq(proj, prm):
    b, l, _ = proj.shape
    assert 2 * b == 8, "the time-major scan packs two time steps of all sequences into one 8-row sublane tile"
    tc = CHUNK
    return pl.pallas_call(
        _lru_seq_kernel,
        grid=(l // tc,),
        in_specs=[pl.BlockSpec((b, tc, LRU_W), lambda t: (0, t, LRU_OFF // LRU_W))] + [_const_spec(x, 1) for x in prm],
        out_specs=[pl.BlockSpec((b, tc, LRU_D), lambda t: (0, t, 0)),
                   pl.BlockSpec((b, 1, LRU_D), lambda t: (0, 0, 0))],
        out_shape=[jax.ShapeDtypeStruct((b, l, LRU_D), BF16),
                   jax.ShapeDtypeStruct((b, 1, LRU_D), F32)],
        scratch_shapes=[pltpu.VMEM((4 * b, LRU_D), F32), pltpu.VMEM((2 * b, LRU_D), F32)],
        compiler_params=_cp(("arbitrary",)),
        name="lru_seq",
    )(proj, *_args(prm))


def _lru_step_kernel(p_ref, cs_ref, h0_ref, cw_ref, cb_ref, wr_ref, br_ref, wi_ref, bi_ref, lam_ref, y_ref, h_ref):
    x = p_ref[...]
    xr = x[:, :LRU_D]
    gate = x[:, LRU_D:]
    cs = cs_ref[...]
    cw = cw_ref[...]
    xc = (cw[0:1] * cs[:, 0:LRU_D] + cw[1:2] * cs[:, LRU_D:2 * LRU_D] + cw[2:3] * cs[:, 2 * LRU_D:3 * LRU_D]
          + cw[3:4] * xr + cb_ref[...])
    a, b, gg = _lru_gates(xc, gate, wr_ref, br_ref, wi_ref, bi_ref, lam_ref)
    h = a * h0_ref[...] + b
    h_ref[...] = h
    y_ref[...] = (h * gg).astype(BF16)


def lru_step(proj, conv_state, h0, prm):
    b = proj.shape[0]
    return pl.pallas_call(
        _lru_step_kernel,
        grid=(1,),
        in_specs=[pl.BlockSpec((b, LRU_W), lambda i: (0, LRU_OFF // LRU_W)),
                  _const_spec(conv_state, 1), _const_spec(h0, 1)] + [_const_spec(x, 1) for x in prm],
        out_specs=[pl.BlockSpec((b, LRU_D), lambda i: (0, 0)),
                   pl.BlockSpec((b, LRU_D), lambda i: (0, 0))],
        out_shape=[jax.ShapeDtypeStruct((b, LRU_D), BF16),
                   jax.ShapeDtypeStruct((b, LRU_D), F32)],
        compiler_params=_cp(("arbitrary",)),
        name="lru_step",
    )(proj, conv_state, h0, *_args(prm))


def _pad_cols(x, width):
    return jnp.pad(x, [(0, 0)] * (x.ndim - 1) + [(0, width - x.shape[-1])])


def _pack_rwkv_cols(x):
    return jnp.concatenate([x[..., :1536], _pad_cols(x[..., 1536:1632], 128), _pad_cols(x[..., 1632:1728], 128),
                            x[..., 1728:1984]], axis=-1)


def _unpack_rwkv_cols(x):
    return jnp.concatenate([x[..., :1536], x[..., 1536:1632], x[..., 1664:1760], x[..., 1792:2048]], axis=-1)


def _pack_w_in_kernel(w_ref, o_ref):
    o_ref[...] = jnp.zeros(o_ref.shape, BF16)
    o = RWKV_PROJ
    for dst, a, b in ((0, 0, 1536), (1536, 1536, 1632), (1664, 1632, 1728), (1792, 1728, o), (S5_OFF, o, o + S5_D),
                      (GDN_OFF, o + S5_D, o + S5_D + GDN_PROJ), (LRU_OFF, o + S5_D + GDN_PROJ, o + S5_D + GDN_PROJ + 2 * LRU_D)):
        o_ref[0, dst:dst + (b - a), :] = w_ref[0, a:b, :].astype(BF16)


def pack_w_in(w_t, tk):
    depth, n, d = w_t.shape
    return pl.pallas_call(
        _pack_w_in_kernel,
        grid=(depth, d // tk),
        in_specs=[pl.BlockSpec((1, n, tk), lambda l, i: (l, 0, i))],
        out_specs=pl.BlockSpec((1, NP_IN, tk), lambda l, i: (l, 0, i)),
        out_shape=jax.ShapeDtypeStruct((depth, NP_IN, d), BF16),
        compiler_params=_cp(("arbitrary", "arbitrary")),
        name="pack_w_in",
    )(w_t)


def _block_diag(blocks):
    *lead, g, a, b = blocks.shape
    eye = jnp.eye(g, dtype=blocks.dtype)
    return (eye[:, None, :, None] * blocks[..., :, :, None, :]).reshape(*lead, g * a, g * b)


def _s5_params(lam_re, lam_im, log_dt, b_re, b_im, c_re, c_im, d, glu_w, glu_b):
    depth = lam_re.shape[0]
    dt = jnp.exp(log_dt)[..., None]
    mag = jnp.exp(lam_re * dt)
    ab_re, ab_im = mag * jnp.cos(lam_im * dt), mag * jnp.sin(lam_im * dt)
    den = lam_re * lam_re + lam_im * lam_im
    nr = ab_re - 1.0
    f_re = (nr * lam_re + ab_im * lam_im) / den
    f_im = (ab_im * lam_re - nr * lam_im) / den
    bb_re = f_re[..., None] * b_re - f_im[..., None] * b_im
    bb_im = f_re[..., None] * b_im + f_im[..., None] * b_re
    macro = lambda x: _block_diag(jnp.swapaxes(x, -1, -2).reshape(depth, S5_CG, S5_G // S5_CG, x.shape[-1], x.shape[-2]))
    bbr = macro(bb_re)
    bbi = macro(bb_im)
    tail = (macro(c_re).astype(BF16), macro(-c_im).astype(BF16),
            ab_re.reshape(depth, 1, S5_STATE), ab_im.reshape(depth, 1, S5_STATE), d.reshape(depth, 1, S5_D),
            glu_w, glu_b.reshape(depth, 1, S5_D))
    return (bbr.astype(BF16), bbi.astype(BF16)) + tail, (bbr, bbi) + tail


def _prepare_params(w):
    depth = w['w_in'].shape[0]
    row = lambda v: v.reshape(depth, 1, -1)
    pad_rows = lambda v: jnp.pad(v, ((0, 0), (0, 128 - v.shape[1]), (0, 0)))
    s5, s5_f32 = _s5_params(w['s5_lambda_re'], w['s5_lambda_im'], w['s5_log_dt'], w['s5_b_re'], w['s5_b_im'],
                            w['s5_c_re'], w['s5_c_im'], w['s5_d'], w['s5_glu_w'], w['s5_glu_b'])
    return {
        'norm1_g': row(w['norm1_g']), 'norm2_g': row(w['norm2_g']),
        'w_in': pack_w_in(jnp.swapaxes(w['w_in'], 1, 2), TILE['pack_cols']),
        'rwkv': (row(_pack_rwkv_cols(w['rwkv_mu'])), row(w['rwkv_w0']), pad_rows(w['rwkv_w_up']), row(w['rwkv_a0']),
                 pad_rows(w['rwkv_a_up']), w['rwkv_g_up'], row(w['rwkv_k_k']), row(w['rwkv_k_a']), row(w['rwkv_r_k']),
                 row(w['rwkv_ln_w']), row(w['rwkv_ln_b'])),
        's5': s5, 's5_f32': s5_f32,
        'gdn': (w['gdn_conv_w'], _pad_cols(row(w['gdn_a_log']), 128), _pad_cols(row(w['gdn_dt_bias']), 128),
                row(w['gdn_norm_w'])),
        'lru': (w['lru_conv_w'], row(w['lru_conv_b']), _block_diag(w['lru_wr']), row(w['lru_br']),
                _block_diag(w['lru_wi']), row(w['lru_bi']), row(w['lru_lambda'])),
        'ffn_conv_w': w['ffn_conv_w'], 'ffn_conv_b': row(w['ffn_conv_b']),
    }


def _layer(xp, xs, mod, st, p, seg, layer, stacked):
    bp, lp, d = xp.shape
    bs = xs.shape[1]
    sh1, sc1, g1, sh2, sc2, g2 = jnp.split(mod, 6, axis=-1)
    pm = lambda m: m[:bp, None, :]
    sm = lambda m: m[None, bp:bp + bs, :]

    lay = lambda x: Layered(x, layer)
    rw_prm, s5_prm, s5_prm_f32, gdn_prm, lru_prm = (tuple(lay(x) for x in p[k])
                                                    for k in ('rwkv', 's5', 's5_f32', 'gdn', 'lru'))

    proj = norm_matmul_in(xp, lay(p['norm1_g']), pm(sc1), pm(sh1), p['w_in'], layer, TILE['in_rows'], TILE['in_cols'])
    ya, s_wkv = rwkv_seq(proj, rw_prm, seg, bp)
    yb, s_re, s_im = s5_seq(proj, s5_prm)
    yc, s_gdn = gdn_seq(proj, gdn_prm, bp)
    yd, s_lru = lru_seq(proj, lru_prm)
    xp = matmul_res([ya, yb, yc, yd], stacked['w_out'], layer, xp, pm(g1), lp, TILE['out_cols'])
    h = norm_mod(xp, lay(p['norm2_g']), pm(sc2), pm(sh2), TILE['norm_rows'])
    act, lg, lv = ffn_up_seq(h, stacked['ffn_w_up'], layer, p['ffn_conv_w'], p['ffn_conv_b'], TILE['ffn_cols'])
    xp = matmul_res([act], stacked['ffn_w_down'], layer, xp, pm(g2), TILE['down_rows'], TILE['down_cols'])
    new_p = {
        'rwkv_wkv': s_wkv,
        'rwkv_shift': _unpack_rwkv_cols(proj[:, -1, :RW_W]),
        's5_re': s_re.reshape(bp, S5_G, S5_N), 's5_im': s_im.reshape(bp, S5_G, S5_N),
        'gdn': s_gdn,
        'gdn_conv': proj[:, -3:, GDN_OFF:GDN_OFF + 3 * GDN_D],
        'lru_h': s_lru.reshape(bp, LRU_D),
        'lru_conv': proj[:, -3:, LRU_OFF:LRU_OFF + LRU_D],
        'ffn_conv': jnp.concatenate([lg, lv], axis=-1),
    }

    proj = norm_matmul_in(xs, lay(p['norm1_g']), sm(sc1), sm(sh1), p['w_in'], layer, bs, TILE['in_cols'])[0]
    ya, s_wkv = rwkv_step(proj, _pack_rwkv_cols(st['rwkv_shift']), stacked['rwkv_wkv'], layer, rw_prm, seg)
    yb, s_re, s_im = s5_step(proj, st['s5_re'].reshape(bs, S5_STATE), st['s5_im'].reshape(bs, S5_STATE), s5_prm_f32)
    yc, s_gdn = gdn_step(proj, st['gdn_conv'].reshape(bs, 9 * GDN_D), stacked['gdn'], layer, gdn_prm, TILE['gdn_step_rows'])
    yd, s_lru = lru_step(proj, st['lru_conv'].reshape(bs, 3 * LRU_D), st['lru_h'], lru_prm)
    xs = matmul_res([y[None] for y in (ya, yb, yc, yd)], stacked['w_out'], layer, xs, sm(g1), bs, TILE['out_cols'])
    h = norm_mod(xs, lay(p['norm2_g']), sm(sc2), sm(sh2), bs)
    act, ug, uv = ffn_up_step(h[0], stacked['ffn_w_up'], layer, p['ffn_conv_w'], p['ffn_conv_b'],
                              stacked['ffn_conv'], TILE['ffn_cols'])
    xs = matmul_res([act[None]], stacked['ffn_w_down'], layer, xs, sm(g2), bs, TILE['down_cols'])
    shift_rows = lambda buf, new: jnp.concatenate([buf[:, 1:], new[:, None, :]], axis=1)
    new_s = {
        'rwkv_wkv': s_wkv,
        'rwkv_shift': _unpack_rwkv_cols(proj[:, :RW_W]),
        's5_re': s_re.reshape(bs, S5_G, S5_N), 's5_im': s_im.reshape(bs, S5_G, S5_N),
        'gdn': s_gdn,
        'gdn_conv': shift_rows(st['gdn_conv'], proj[:, GDN_OFF:GDN_OFF + 3 * GDN_D]),
        'lru_h': s_lru,
        'lru_conv': shift_rows(st['lru_conv'], proj[:, LRU_OFF:LRU_OFF + LRU_D]),
        'ffn_conv': shift_rows(st['ffn_conv'], jnp.concatenate([ug, uv], axis=-1)),
    }
    return xp, xs, new_p, new_s


STATE_ORDER = ('rwkv_wkv', 'rwkv_shift', 's5_re', 's5_im', 'gdn', 'gdn_conv', 'lru_h', 'lru_conv', 'ffn_conv')


def kernel(x_prompt, x_sample, c_prompt, c_sample, state_rwkv_wkv, state_rwkv_shift, state_s5_re, state_s5_im, state_gdn, state_gdn_conv, state_lru_h, state_lru_conv, state_ffn_conv, ada_w, ada_b, norm1_g, norm2_g, final_g, w_in, w_out, rwkv_mu, rwkv_w0, rwkv_w_up, rwkv_a0, rwkv_a_up, rwkv_g_up, rwkv_k_k, rwkv_k_a, rwkv_r_k, rwkv_ln_w, rwkv_ln_b, s5_lambda_re, s5_lambda_im, s5_log_dt, s5_b_re, s5_b_im, s5_c_re, s5_c_im, s5_d, s5_glu_w, s5_glu_b, gdn_conv_w, gdn_a_log, gdn_dt_bias, gdn_norm_w, lru_conv_w, lru_conv_b, lru_wr, lru_br, lru_wi, lru_bi, lru_lambda, ffn_w_up, ffn_conv_w, ffn_conv_b, ffn_w_down):
    weights = {
        'norm1_g': norm1_g, 'norm2_g': norm2_g, 'w_in': w_in,
        'rwkv_mu': rwkv_mu, 'rwkv_w0': rwkv_w0, 'rwkv_w_up': rwkv_w_up, 'rwkv_a0': rwkv_a0,
        'rwkv_a_up': rwkv_a_up, 'rwkv_g_up': rwkv_g_up, 'rwkv_k_k': rwkv_k_k, 'rwkv_k_a': rwkv_k_a,
        'rwkv_r_k': rwkv_r_k, 'rwkv_ln_w': rwkv_ln_w, 'rwkv_ln_b': rwkv_ln_b,
        's5_lambda_re': s5_lambda_re, 's5_lambda_im': s5_lambda_im, 's5_log_dt': s5_log_dt,
        's5_b_re': s5_b_re, 's5_b_im': s5_b_im, 's5_c_re': s5_c_re, 's5_c_im': s5_c_im, 's5_d': s5_d,
        's5_glu_w': s5_glu_w, 's5_glu_b': s5_glu_b,
        'gdn_conv_w': gdn_conv_w, 'gdn_a_log': gdn_a_log, 'gdn_dt_bias': gdn_dt_bias, 'gdn_norm_w': gdn_norm_w,
        'lru_conv_w': lru_conv_w, 'lru_conv_b': lru_conv_b, 'lru_wr': lru_wr, 'lru_br': lru_br,
        'lru_wi': lru_wi, 'lru_bi': lru_bi, 'lru_lambda': lru_lambda,
        'ffn_conv_w': ffn_conv_w, 'ffn_conv_b': ffn_conv_b,
    }
    stacked = {'w_out': w_out, 'ffn_w_up': ffn_w_up, 'ffn_w_down': ffn_w_down,
               'rwkv_wkv': jnp.transpose(state_rwkv_wkv, (0, 2, 3, 4, 1)),
               'gdn': state_gdn, 'ffn_conv': state_ffn_conv}
    cache = {'rwkv_wkv': state_rwkv_wkv, 'rwkv_shift': state_rwkv_shift, 's5_re': state_s5_re,
             's5_im': state_s5_im, 'gdn': state_gdn, 'gdn_conv': state_gdn_conv, 'lru_h': state_lru_h,
             'lru_conv': state_lru_conv, 'ffn_conv': state_ffn_conv}
    depth = ada_w.shape[0]
    bp = x_prompt.shape[0]
    bs = x_sample.shape[0]
    rows = -(-(bp + bs) // 8) * 8
    c_all = jnp.pad(jnp.concatenate([c_prompt, c_sample], axis=0), ((0, rows - bp - bs), (0, 0)))
    mod = ada_mod(c_all, ada_w, ada_b)
    ids = jnp.arange(RWKV_D) // RWKV_HD
    seg = (ids[:, None] == ids[None, :]).astype(BF16)

    xp = x_prompt
    xs = jnp.swapaxes(x_sample, 0, 1)
    new_p = {n: [] for n in STATE_ORDER}
    new_s = {n: [] for n in STATE_ORDER}
    p = _prepare_params(weights)
    for l in range(depth):
        st = {n: cache[n][l] for n in STATE_ORDER}
        xp, xs, sp, ss = _layer(xp, xs, mod[l], st, p, seg, l, stacked)
        for n in STATE_ORDER:
            new_p[n].append(sp[n])
            new_s[n].append(ss[n])
    y_prompt = final_norm(xp, final_g, TILE['norm_rows'])
    y_sample = jnp.swapaxes(final_norm(xs, final_g, bs), 0, 1)
    outs_p = tuple(jnp.stack(new_p[n], axis=0) for n in STATE_ORDER)
    stacked_s = {n: jnp.stack(new_s[n], axis=0) for n in STATE_ORDER}
    stacked_s['rwkv_wkv'] = jnp.transpose(stacked_s['rwkv_wkv'], (0, 4, 1, 2, 3))
    outs_s = tuple(stacked_s[n] for n in STATE_ORDER)
    return (y_prompt, y_sample) + outs_p + outs_s
```

```python
import functools
import math
from typing import NamedTuple

import jax
import jax.numpy as jnp
from jax import lax
from jax.experimental import pallas as pl
from jax.experimental.pallas import tpu as pltpu

F32 = jnp.float32
BF16 = jnp.bfloat16
HI = lax.Precision.HIGHEST

NORM_EPS = 1e-6
RWKV_LN_EPS = 64e-5
LRU_C = 8.0

D_MODEL = 2048
RWKV_D, RWKV_HD, RWKV_H = 512, 64, 8
RWKV_R_DECAY, RWKV_R_A, RWKV_R_GATE = 96, 96, 256
RWKV_PROJ = 3 * RWKV_D + RWKV_R_DECAY + RWKV_R_A + RWKV_R_GATE
S5_D, S5_CH, S5_G, S5_N = 512, 16, 32, 64
S5_STATE = S5_G * S5_N
GDN_D, GDN_HD, GDN_H = 512, 128, 4
GDN_PROJ = 4 * GDN_D + 2 * GDN_H
LRU_D, LRU_BLOCKS = 512, 8
D_FF = 5632

RW_W = 2048
S5_OFF = 2048
GDN_OFF, GDN_W = 2560, 2560
LRU_OFF, LRU_W = 5120, 1024
NP_IN = 6144

CHUNK = 64
RWKV_GH = 4
assert CHUNK == RWKV_HD

V7X_VMEM_BYTES = 64 * 1024 * 1024
VMEM_LIMIT = V7X_VMEM_BYTES - 8 * 1024 * 1024

TILE = {
    'ada_cols': 1024,
    'norm_rows': 1024,
    'in_rows': 1024, 'in_cols': 1024,
    'out_cols': 512,
    'ffn_cols': 512,
    'down_rows': 1024, 'down_cols': 256,
    'pack_cols': 256,
    'gdn_step_rows': 16,
}


def _cp(sem):
    return pltpu.CompilerParams(dimension_semantics=sem, vmem_limit_bytes=VMEM_LIMIT)


def _dot(a, b, prec=None):
    return jnp.dot(a, b, preferred_element_type=F32, precision=prec)


def _dot_nt(a, b, prec=None):
    return lax.dot_general(a, b, (((1,), (1,)), ((), ())), preferred_element_type=F32, precision=prec)


def _dot_tn(a, b, prec=None):
    return lax.dot_general(a, b, (((0,), (0,)), ((), ())), preferred_element_type=F32, precision=prec)


def _bdot(a, b):
    return _dot(a.astype(BF16), b.astype(BF16))


def _silu(x):
    return x * jax.nn.sigmoid(x)


def _gelu(x):
    return 0.5 * x * (1.0 + jnp.tanh(math.sqrt(2.0 / math.pi) * (x + 0.044715 * (x * x * x))))


def _seg_dot(x, seg):
    hi = x.astype(BF16)
    lo = (x - hi.astype(F32)).astype(BF16)
    n = x.shape[0]
    r = _dot(jnp.concatenate([hi, lo], axis=0), seg)
    return r[:n] + r[n:]


def _iota(shape, dim):
    return lax.broadcasted_iota(jnp.int32, shape, dim)


class Layered(NamedTuple):
    arr: jax.Array
    layer: int


def _const_spec(x, nidx):
    if isinstance(x, Layered):
        shape = (None,) + x.arr.shape[1:]
        idx = (x.layer,) + (0,) * (x.arr.ndim - 1)
    else:
        shape = x.shape
        idx = (0,) * x.ndim
    if nidx == 1:
        return pl.BlockSpec(shape, lambda i: idx)
    return pl.BlockSpec(shape, lambda b, t: idx)


def _args(params):
    return [x.arr if isinstance(x, Layered) else x for x in params]


def _neumann_solve(n, rhs, steps):
    x = rhs
    p = n
    for i in range(steps):
        x = x + _bdot(p, x)
        if i + 1 < steps:
            p = _bdot(p, p)
    return x


def _split(x):
    hi = x.astype(BF16)
    return hi, (x - hi.astype(F32)).astype(BF16)


def _dot3(a, b):
    ah, al = _split(a)
    bh, bl = _split(b)
    return _dot(ah, bh) + (_dot(ah, bl) + _dot(al, bh))


def _neumann_solve3(n, rhs, steps):
    x = rhs
    p = n
    for i in range(steps):
        x = x + _dot3(p, x)
        if i + 1 < steps:
            p = _dot3(p, p)
    return x


def _row_to_col(row, eye):
    n = eye.shape[0]
    return jnp.sum(jnp.where(eye, jnp.broadcast_to(row, (n, n)), 0.0), axis=-1, keepdims=True)


def _ada_kernel(c_ref, w_ref, b_ref, o_ref):
    c = c_ref[...]
    o_ref[0] = _dot(_silu(c).astype(BF16), w_ref[0].astype(BF16)) + b_ref[0]


def ada_mod(c_all, ada_w, ada_b):
    depth, d, n = ada_w.shape
    r = c_all.shape[0]
    tn = TILE['ada_cols']
    return pl.pallas_call(
        _ada_kernel,
        grid=(depth, n // tn),
        in_specs=[pl.BlockSpec((r, d), lambda l, j: (0, 0)),
                  pl.BlockSpec((1, d, tn), lambda l, j: (l, 0, j)),
                  pl.BlockSpec((1, 1, tn), lambda l, j: (l, 0, j))],
        out_specs=pl.BlockSpec((1, r, tn), lambda l, j: (l, 0, j)),
        out_shape=jax.ShapeDtypeStruct((depth, r, n), F32),
        compiler_params=_cp(("arbitrary", "arbitrary")),
        name="ada_mod",
    )(c_all, ada_w, ada_b.reshape(depth, 1, n))


def _norm_mod_kernel(x_ref, g_ref, sc_ref, sh_ref, o_ref):
    x = x_ref[0]
    h = x * lax.rsqrt(jnp.mean(x * x, axis=-1, keepdims=True) + NORM_EPS) * g_ref[...]
    o_ref[0] = (h * (1.0 + sc_ref[0]) + sh_ref[0]).astype(o_ref.dtype)


def _mod_spec(arr, tm):
    d = arr.shape[-1]
    if arr.shape[1] == 1:
        return pl.BlockSpec((1, 1, d), lambda g, i: (g, 0, 0))
    return pl.BlockSpec((1, tm, d), lambda g, i: (g, i, 0))


def norm_mod(x, gain, sc, sh, tm):
    g, r, d = x.shape
    return pl.pallas_call(
        _norm_mod_kernel,
        grid=(g, r // tm),
        in_specs=[pl.BlockSpec((1, tm, d), lambda g, i: (g, i, 0)),
                  _const_spec(gain, 2),
                  _mod_spec(sc, tm), _mod_spec(sh, tm)],
        out_specs=pl.BlockSpec((1, tm, d), lambda g, i: (g, i, 0)),
        out_shape=jax.ShapeDtypeStruct((g, r, d), BF16),
        compiler_params=_cp(("arbitrary", "arbitrary")),
        name="norm_mod",
    )(x, *_args([gain]), sc, sh)


def _rms_kernel(x_ref, g_ref, o_ref):
    x = x_ref[0]
    o_ref[0] = x * lax.rsqrt(jnp.mean(x * x, axis=-1, keepdims=True) + NORM_EPS) * g_ref[...]


def final_norm(x, gain, tm):
    g, r, d = x.shape
    return pl.pallas_call(
        _rms_kernel,
        grid=(g, r // tm),
        in_specs=[pl.BlockSpec((1, tm, d), lambda g, i: (g, i, 0)),
                  pl.BlockSpec((1, d), lambda g, i: (0, 0))],
        out_specs=pl.BlockSpec((1, tm, d), lambda g, i: (g, i, 0)),
        out_shape=jax.ShapeDtypeStruct((g, r, d), F32),
        compiler_params=_cp(("arbitrary", "arbitrary")),
        name="final_norm",
    )(x, gain.reshape(1, d))


def _norm_mm_kernel(x_ref, g_ref, sc_ref, sh_ref, w_ref, o_ref, h_ref):
    @pl.when(pl.program_id(2) == 0)
    def _():
        x = x_ref[0]
        h = x * lax.rsqrt(jnp.mean(x * x, axis=-1, keepdims=True) + NORM_EPS) * g_ref[...]
        h_ref[...] = (h * (1.0 + sc_ref[0]) + sh_ref[0]).astype(BF16)

    o_ref[0] = _dot_nt(h_ref[...], w_ref[...])


def norm_matmul_in(x, gain, sc, sh, w_t, layer, tm, tn):
    g, r, k = x.shape
    n = w_t.shape[1]
    mod_spec = lambda m: (pl.BlockSpec((1, 1, k), lambda g, i, j: (g, 0, 0)) if m.shape[1] == 1
                          else pl.BlockSpec((1, tm, k), lambda g, i, j: (g, i, 0)))
    gain_spec = pl.BlockSpec((None, 1, k), lambda g, i, j: (gain.layer, 0, 0))
    return pl.pallas_call(
        _norm_mm_kernel,
        grid=(g, r // tm, n // tn),
        in_specs=[pl.BlockSpec((1, tm, k), lambda g, i, j: (g, i, 0)), gain_spec, mod_spec(sc), mod_spec(sh),
                  pl.BlockSpec((None, tn, k), lambda g, i, j: (layer, j, 0))],
        out_specs=pl.BlockSpec((1, tm, tn), lambda g, i, j: (g, i, j)),
        out_shape=jax.ShapeDtypeStruct((g, r, n), F32),
        scratch_shapes=[pltpu.VMEM((tm, k), BF16)],
        compiler_params=_cp(("arbitrary", "arbitrary", "arbitrary")),
        name="norm_matmul_in",
    )(x, gain.arr, sc, sh, w_t)


def _mm_res_kernel(n_a, *refs):
    a_refs = refs[:n_a]
    w_refs = refs[n_a:2 * n_a]
    x_ref, g_ref, o_ref = refs[2 * n_a:]
    acc = _dot(a_refs[0][0], w_refs[0][...].astype(BF16))
    for a_ref, w_ref in zip(a_refs[1:], w_refs[1:]):
        acc = acc + _dot(a_ref[0], w_ref[...].astype(BF16))
    o_ref[0] = x_ref[0] + g_ref[0] * acc


def matmul_res(a_list, w, layer, x, gate, tm, tn):
    g, r, n = x.shape
    n_a = len(a_list)
    k = a_list[0].shape[-1]
    a_specs = [pl.BlockSpec((1, tm, k), lambda g, i, j: (g, i, 0)) for _ in a_list]
    w_specs = [pl.BlockSpec((None, k, tn), functools.partial(lambda g, i, j, q: (layer, q, j), q=q))
               for q in range(n_a)]
    if gate.shape[1] == 1:
        g_spec = pl.BlockSpec((1, 1, tn), lambda g, i, j: (g, 0, j))
    else:
        g_spec = pl.BlockSpec((1, tm, tn), lambda g, i, j: (g, i, j))
    return pl.pallas_call(
        functools.partial(_mm_res_kernel, n_a),
        grid=(g, r // tm, n // tn),
        in_specs=a_specs + w_specs + [pl.BlockSpec((1, tm, tn), lambda g, i, j: (g, i, j)), g_spec],
        out_specs=pl.BlockSpec((1, tm, tn), lambda g, i, j: (g, i, j)),
        out_shape=jax.ShapeDtypeStruct((g, r, n), F32),
        compiler_params=_cp(("arbitrary", "arbitrary", "arbitrary")),
        name="matmul_res",
    )(*a_list, *([w] * n_a), x, gate)


def _ffn_up_seq_kernel(a_ref, wg_ref, wv_ref, cwg_ref, cwv_ref, cbg_ref, cbv_ref, act_ref, lg_ref, lv_ref):
    a = a_ref[0]
    rows = a.shape[0]
    rid8 = _iota((8, 1), 0)

    def conv(up, cw_ref, cb_ref):
        cw = cw_ref[...]
        return cw[0:1] * pltpu.roll(up, 2, axis=0) + cw[1:2] * pltpu.roll(up, 1, axis=0) + cw[2:3] * up + cb_ref[...]

    def head(up, cw_ref, cb_ref):
        cw = cw_ref[...]
        u8 = up[0:8]
        s1 = jnp.where(rid8 >= 1, pltpu.roll(u8, 1, axis=0), 0.0)
        s2 = jnp.where(rid8 >= 2, pltpu.roll(u8, 2, axis=0), 0.0)
        return cw[0:1] * s2 + cw[1:2] * s1 + cw[2:3] * u8 + cb_ref[...]

    up_g = _dot(a, wg_ref[...].astype(BF16))
    lg_ref[0] = up_g[rows - 2:rows]
    gate = conv(up_g, cwg_ref, cbg_ref)
    up_v = _dot(a, wv_ref[...].astype(BF16))
    lv_ref[0] = up_v[rows - 2:rows]
    val = conv(up_v, cwv_ref, cbv_ref)
    act_ref[0] = (_silu(gate) * val).astype(BF16)
    act_ref[0, 0:8, :] = (_silu(head(up_g, cwg_ref, cbg_ref)) * head(up_v, cwv_ref, cbv_ref)).astype(BF16)


def ffn_up_seq(h, w_up, layer, conv_w, conv_b, tn):
    b, l, d = h.shape
    nt = D_FF // tn
    return pl.pallas_call(
        _ffn_up_seq_kernel,
        grid=(b, nt),
        in_specs=[pl.BlockSpec((1, l, d), lambda b, j: (b, 0, 0)),
                  pl.BlockSpec((None, d, tn), lambda b, j: (layer, 0, j)),
                  pl.BlockSpec((None, d, tn), lambda b, j: (layer, 0, j + nt)),
                  pl.BlockSpec((None, 3, tn), lambda b, j: (layer, 0, j)),
                  pl.BlockSpec((None, 3, tn), lambda b, j: (layer, 0, j + nt)),
                  pl.BlockSpec((None, 1, tn), lambda b, j: (layer, 0, j)),
                  pl.BlockSpec((None, 1, tn), lambda b, j: (layer, 0, j + nt))],
        out_specs=[pl.BlockSpec((1, l, tn), lambda b, j: (b, 0, j)),
                   pl.BlockSpec((1, 2, tn), lambda b, j: (b, 0, j)),
                   pl.BlockSpec((1, 2, tn), lambda b, j: (b, 0, j))],
        out_shape=[jax.ShapeDtypeStruct((b, l, D_FF), BF16),
                   jax.ShapeDtypeStruct((b, 2, D_FF), F32),
                   jax.ShapeDtypeStruct((b, 2, D_FF), F32)],
        compiler_params=_cp(("arbitrary", "arbitrary")),
        name="ffn_up_seq",
    )(h, w_up, w_up, conv_w, conv_w, conv_b, conv_b)


def _ffn_up_step_kernel(a_ref, wg_ref, wv_ref, cwg_ref, cwv_ref, cbg_ref, cbv_ref,
                        stg_ref, stv_ref, act_ref, ug_ref, uv_ref):
    a = a_ref[...]
    up_g = _dot(a, wg_ref[...].astype(BF16))
    up_v = _dot(a, wv_ref[...].astype(BF16))
    ug_ref[...] = up_g
    uv_ref[...] = up_v
    cwg = cwg_ref[...]
    cwv = cwv_ref[...]
    gate = cwg[0:1] * stg_ref[:, 0, :] + cwg[1:2] * stg_ref[:, 1, :] + cwg[2:3] * up_g + cbg_ref[...]
    val = cwv[0:1] * stv_ref[:, 0, :] + cwv[1:2] * stv_ref[:, 1, :] + cwv[2:3] * up_v + cbv_ref[...]
    act_ref[...] = (_silu(gate) * val).astype(BF16)


def ffn_up_step(h, w_up, layer, conv_w, conv_b, conv_state, tn):
    b, d = h.shape
    nt = D_FF // tn
    return pl.pallas_call(
        _ffn_up_step_kernel,
        grid=(nt,),
        in_specs=[pl.BlockSpec((b, d), lambda j: (0, 0)),
                  pl.BlockSpec((None, d, tn), lambda j: (layer, 0, j)),
                  pl.BlockSpec((None, d, tn), lambda j: (layer, 0, j + nt)),
                  pl.BlockSpec((None, 3, tn), lambda j: (layer, 0, j)),
                  pl.BlockSpec((None, 3, tn), lambda j: (layer, 0, j + nt)),
                  pl.BlockSpec((None, 1, tn), lambda j: (layer, 0, j)),
                  pl.BlockSpec((None, 1, tn), lambda j: (layer, 0, j + nt)),
                  pl.BlockSpec((None, b, 2, tn), lambda j: (layer, 0, 0, j)),
                  pl.BlockSpec((None, b, 2, tn), lambda j: (layer, 0, 0, j + nt))],
        out_specs=[pl.BlockSpec((b, tn), lambda j: (0, j))] * 3,
        out_shape=[jax.ShapeDtypeStruct((b, D_FF), BF16),
                   jax.ShapeDtypeStruct((b, D_FF), F32),
                   jax.ShapeDtypeStruct((b, D_FF), F32)],
        compiler_params=_cp(("arbitrary",)),
        name="ffn_up_step",
    )(h, w_up, w_up, conv_w, conv_w, conv_b, conv_b, conv_state, conv_state)


def _rwkv_pre(p, prev, prm, seg):
    mu, w0, w_up, a0, a_up, g_up, k_k, k_a = prm
    xm = p + mu * (prev - p)
    r = xm[:, 0:512]
    k = xm[:, 512:1024]
    v = xm[:, 1024:1536]
    xw = xm[:, 1536:1664]
    xa = xm[:, 1664:1792]
    xg = xm[:, 1792:2048]
    log_w = -jnp.exp(-jax.nn.softplus(-(w0 + _bdot(jnp.tanh(xw), w_up))) - 0.5)
    a = jax.nn.sigmoid(a0 + _bdot(xa, a_up))
    g = _bdot(jax.nn.sigmoid(xg), g_up)
    kx = k * k_k
    kk = kx * lax.rsqrt(_seg_dot(kx * kx, seg) + 1e-12)
    k2 = k * (1.0 + (a - 1.0) * k_a)
    return r, k2, v, log_w, a, g, kk


def _rwkv_post(y, r, k2, v, g, r_k, ln_w, ln_b, seg):
    inv = 1.0 / RWKV_HD
    mean = _seg_dot(y, seg) * inv
    dlt = y - mean
    var = _seg_dot(dlt * dlt, seg) * inv
    yn = dlt * lax.rsqrt(var + RWKV_LN_EPS) * ln_w + ln_b
    bonus = _seg_dot(r * k2 * r_k, seg) * v
    return (yn + bonus) * g


def _rwkv_seq_kernel(p_ref, mu_ref, w0_ref, wup_ref, a0_ref, aup_ref, gup_ref, kk_ref, ka_ref, rk_ref,
                     lnw_ref, lnb_ref, seg_ref, y_ref, s_out_ref, s_ref, prev_ref):
    c = CHUNK
    ns = p_ref.shape[0]
    n = ns * c
    t = pl.program_id(1)

    @pl.when(t == 0)
    def _():
        s_ref[...] = jnp.zeros_like(s_ref)
        prev_ref[...] = jnp.zeros_like(prev_ref)

    p = p_ref[...].reshape(n, RW_W)
    seg = seg_ref[...]
    seq_rows = lambda x: jnp.concatenate([jnp.broadcast_to(x[i:i + 1], (c, x.shape[-1])) for i in range(ns)], axis=0)
    tloc = _iota((n, 1), 0) & (c - 1)
    prev = jnp.where(tloc == 0, seq_rows(prev_ref[...]), pltpu.roll(p, 1, axis=0))
    last_rows = lambda x: jnp.concatenate([x[(i + 1) * c - 1:(i + 1) * c] for i in range(ns)], axis=0)
    prev_ref[...] = last_rows(p)
    prm = (mu_ref[...], w0_ref[...], wup_ref[...], a0_ref[...], aup_ref[...], gup_ref[...], kk_ref[...], ka_ref[...])
    r, k2, v, log_w, a, g, kk = _rwkv_pre(p, prev, prm, seg)

    ri = _iota((n, n), 0)
    ci = _iota((n, n), 1)
    cshift = c.bit_length() - 1
    tril = ((lax.shift_right_logical(ri, cshift) == lax.shift_right_logical(ci, cshift)) & (ci <= ri)).astype(F32)
    cum = _dot(tril, log_w, HI)
    e_in = jnp.exp(cum)
    e_out = jnp.exp(-cum)
    e_ex = jnp.exp(cum - log_w)
    tot = last_rows(cum)
    e_tail = jnp.exp(seq_rows(tot) - cum)
    e_tot = jnp.exp(tot)
    alpha = -(a * kk)
    gw = RWKV_GH * RWKV_HD
    rr = _iota((gw, gw), 0)
    cc = _iota((gw, gw), 1)
    shift = RWKV_HD.bit_length() - 1
    own = lax.shift_right_logical(rr, shift) == lax.shift_right_logical(cc, shift)
    tri_strict = (cc & (c - 1)) < (rr & (c - 1))
    tri_incl = (cc & (c - 1)) <= (rr & (c - 1))
    eye = rr == cc

    ng = RWKV_H // RWKV_GH
    be_t = (kk * e_ex).astype(BF16)
    r_t = (r * e_in).astype(BF16)
    al_t = (alpha * e_out).astype(BF16)
    k_t = (k2 * e_out).astype(BF16)
    al_p = (alpha * e_tail).astype(BF16)
    k_p = (k2 * e_tail).astype(BF16)
    vb = v.astype(BF16)

    def expand(x, i, gi):
        xg = x[i * c:(i + 1) * c, gi * gw:(gi + 1) * gw]
        return jnp.where(own, jnp.concatenate([xg] * RWKV_GH, axis=0), jnp.zeros((), BF16))

    y_rows = []
    for i in range(ns):
        ys = []
        for gi in range(ng):
            s0 = s_ref[i * ng + gi]
            lhs = jnp.concatenate([expand(be_t, i, gi), expand(r_t, i, gi)], axis=0)
            rhs = jnp.concatenate([expand(al_t, i, gi), expand(k_t, i, gi)], axis=0)
            vv = expand(vb, i, gi)
            m1 = _dot_nt(lhs, rhs)
            m2 = _dot(lhs, s0.astype(BF16))
            l_a = jnp.where(tri_strict, m1[:gw, :gw], 0.0)
            l_k = jnp.where(tri_strict, m1[:gw, gw:], 0.0)
            u = _neumann_solve(l_a, m2[:gw] + _dot(l_k.astype(BF16), vv), cshift)
            uv = jnp.concatenate([u.astype(BF16), vv], axis=0)
            a_full = jnp.concatenate([jnp.where(tri_incl, m1[gw:, :gw], 0.0),
                                      jnp.where(tri_incl, m1[gw:, gw:], 0.0)], axis=1).astype(BF16)
            y_g = m2[gw:] + _dot(a_full, uv)
            ys.append(functools.reduce(lambda p_, q_: p_ + q_, [y_g[j * c:(j + 1) * c] for j in range(RWKV_GH)]))
            scale = _row_to_col(e_tot[i:i + 1, gi * gw:(gi + 1) * gw], eye)
            tails = jnp.concatenate([expand(al_p, i, gi), expand(k_p, i, gi)], axis=0)
            s_ref[i * ng + gi] = s0 * scale + _dot_tn(tails, uv)
        y_rows.append(jnp.concatenate(ys, axis=1))

    y = jnp.concatenate(y_rows, axis=0)
    out = _rwkv_post(y, r, k2, v, g, rk_ref[...], lnw_ref[...], lnb_ref[...], seg)
    y_ref[...] = out.astype(BF16).reshape(ns, c, RWKV_D)

    @pl.when(t == pl.num_programs(1) - 1)
    def _():
        for i in range(ns):
            for h in range(RWKV_H):
                gi, hh = divmod(h, RWKV_GH)
                s_out_ref[i, h] = s_ref[i * ng + gi, hh * RWKV_HD:(hh + 1) * RWKV_HD, hh * RWKV_HD:(hh + 1) * RWKV_HD]


def rwkv_seq(proj, prm, seg, ns):
    b, l, _ = proj.shape
    c = CHUNK
    gw = RWKV_GH * RWKV_HD
    return pl.pallas_call(
        _rwkv_seq_kernel,
        grid=(b // ns, l // c),
        in_specs=[pl.BlockSpec((ns, c, RW_W), lambda b, t: (b, t, 0))] + [_const_spec(x, 2) for x in prm] + [_const_spec(seg, 2)],
        out_specs=[pl.BlockSpec((ns, c, RWKV_D), lambda b, t: (b, t, 0)),
                   pl.BlockSpec((ns, RWKV_H, RWKV_HD, RWKV_HD), lambda b, t: (b, 0, 0, 0))],
        out_shape=[jax.ShapeDtypeStruct((b, l, RWKV_D), BF16),
                   jax.ShapeDtypeStruct((b, RWKV_H, RWKV_HD, RWKV_HD), F32)],
        scratch_shapes=[pltpu.VMEM((ns * (RWKV_H // RWKV_GH), gw, gw), F32),
                        pltpu.VMEM((ns, RW_W), F32)],
        compiler_params=_cp(("arbitrary", "arbitrary")),
        name="rwkv_seq",
    )(proj, *_args(prm), seg)


def _rwkv_step_kernel(p_ref, prev_ref, s_in_ref, mu_ref, w0_ref, wup_ref, a0_ref, aup_ref, gup_ref, kk_ref,
                      ka_ref, rk_ref, lnw_ref, lnb_ref, seg_ref, y_ref, s_out_ref, nat_scr, col_scr, yt_scr):
    h = pl.program_id(0)

    @pl.when(h == 0)
    def _():
        prm = (mu_ref[...], w0_ref[...], wup_ref[...], a0_ref[...], aup_ref[...], gup_ref[...], kk_ref[...],
               ka_ref[...])
        r, k2, v, log_w, a, g, kk = _rwkv_pre(p_ref[...], prev_ref[...], prm, seg_ref[...])
        for i, x in enumerate((r, k2, v, g)):
            nat_scr[i] = x
        for i, x in enumerate((jnp.exp(log_w), kk * a, kk, k2, r, v)):
            col_scr[i] = x.T

    rows = pl.ds(pl.multiple_of(h * RWKV_HD, RWKV_HD), RWKV_HD)
    w_t, kka_t, kk_t, k_t, r_t, v_t = (col_scr[i, rows, :] for i in range(6))
    nacc = 4
    acc = [kk_t[j:j + 1] * s_in_ref[j] for j in range(nacc)]
    for k in range(nacc, RWKV_HD):
        acc[k % nacc] = acc[k % nacc] + kk_t[k:k + 1] * s_in_ref[k]
    sa = (acc[0] + acc[1]) + (acc[2] + acc[3])
    acc = [None] * nacc
    for k in range(RWKV_HD):
        s_new = s_in_ref[k] * w_t[k:k + 1] - kka_t[k:k + 1] * sa + k_t[k:k + 1] * v_t
        s_out_ref[k] = s_new
        term = r_t[k:k + 1] * s_new
        acc[k % nacc] = term if acc[k % nacc] is None else acc[k % nacc] + term
    yt_scr[rows, :] = (acc[0] + acc[1]) + (acc[2] + acc[3])

    @pl.when(h == RWKV_H - 1)
    def _():
        out = _rwkv_post(yt_scr[...].T, nat_scr[0], nat_scr[1], nat_scr[2], nat_scr[3], rk_ref[...], lnw_ref[...],
                         lnb_ref[...], seg_ref[...])
        y_ref[...] = out.astype(BF16)


def rwkv_step(proj, shift, state_t, layer, prm, seg):
    b = proj.shape[0]
    blk = (None, RWKV_HD, RWKV_HD, b)
    return pl.pallas_call(
        _rwkv_step_kernel,
        grid=(RWKV_H,),
        in_specs=[pl.BlockSpec((b, RW_W), lambda h: (0, 0)),
                  pl.BlockSpec((b, RW_W), lambda h: (0, 0)),
                  pl.BlockSpec((None,) + blk, lambda h: (layer, h, 0, 0, 0))]
                 + [_const_spec(x, 1) for x in prm] + [_const_spec(seg, 1)],
        out_specs=[pl.BlockSpec((b, RWKV_D), lambda h: (0, 0)),
                   pl.BlockSpec(blk, lambda h: (h, 0, 0, 0))],
        out_shape=[jax.ShapeDtypeStruct((b, RWKV_D), BF16),
                   jax.ShapeDtypeStruct(state_t.shape[1:], F32)],
        scratch_shapes=[pltpu.VMEM((4, b, RWKV_D), F32), pltpu.VMEM((6, RWKV_D, b), F32),
                        pltpu.VMEM((RWKV_D, b), F32)],
        compiler_params=_cp(("arbitrary",)),
        name="rwkv_step",
    )(proj, shift, state_t, *_args(prm), seg)


def _s5_glu(y, gw_ref, gb_ref):
    z = _gelu(y)
    return z * jax.nn.sigmoid(_bdot(z, gw_ref[...]) + gb_ref[...])


S5_CG = S5_D // 128
S5_SW = S5_STATE // S5_CG


def _s5_bu(u, bb_ref, prec=None):
    return jnp.concatenate([_dot(u[:, g * 128:(g + 1) * 128], bb_ref[g], prec) for g in range(S5_CG)], axis=1)


def _s5_ch(h_re, h_im, ccr_ref, cci_ref):
    return jnp.concatenate([_dot(h_re[:, g * S5_SW:(g + 1) * S5_SW], ccr_ref[g])
                            + _dot(h_im[:, g * S5_SW:(g + 1) * S5_SW], cci_ref[g]) for g in range(S5_CG)], axis=1)

def _interleave_perm(ns, tc):
    n = ns * tc
    r = _iota((n, n), 0)
    c = _iota((n, n), 1)
    t = lax.shift_right_logical(r, ns.bit_length() - 1)
    b = r & (ns - 1)
    return c == b * tc + t


def _s5_seq_kernel(u_ref, bbr_ref, bbi_ref, ccr_ref, cci_ref, abr_ref, abi_ref, d_ref, gw_ref, gb_ref,
                   y_ref, hr_ref, hi_ref, cr_ref, ci_ref):
    ns, tc = u_ref.shape[0], u_ref.shape[1]
    n = ns * tc
    t = pl.program_id(0)

    @pl.when(t == 0)
    def _():
        cr_ref[...] = jnp.zeros_like(cr_ref)
        ci_ref[...] = jnp.zeros_like(ci_ref)

    p_fwd = _interleave_perm(ns, tc).astype(BF16)
    u = u_ref[...].reshape(n, S5_D)
    u_hi, u_lo = _split(u)
    ub = _dot(p_fwd, u_hi)
    u_t = ub + _dot(p_fwd, u_lo)
    ub = ub.astype(BF16)
    bu_re = _s5_bu(ub, bbr_ref)
    bu_im = _s5_bu(ub, bbi_ref)
    ar = jnp.broadcast_to(abr_ref[...], (2 * ns, S5_STATE))
    ai = jnp.broadcast_to(abi_ref[...], (2 * ns, S5_STATE))
    first = _iota((2 * ns, 1), 0) < ns
    p_re = cr_ref[...]
    p_im = ci_ref[...]
    h_re, h_im = [], []
    for s in range(n // (2 * ns)):
        b_re = bu_re[s * 2 * ns:(s + 1) * 2 * ns]
        b_im = bu_im[s * 2 * ns:(s + 1) * 2 * ns]
        r_re = pltpu.roll(p_re, ns, axis=0)
        r_im = pltpu.roll(p_im, ns, axis=0)
        t_re = ar * r_re - ai * r_im + b_re
        t_im = ar * r_im + ai * r_re + b_im
        q_re = pltpu.roll(t_re, ns, axis=0)
        q_im = pltpu.roll(t_im, ns, axis=0)
        p_re = jnp.where(first, t_re, ar * q_re - ai * q_im + b_re)
        p_im = jnp.where(first, t_im, ar * q_im + ai * q_re + b_im)
        h_re.append(p_re)
        h_im.append(p_im)
    cr_ref[...] = p_re
    ci_ref[...] = p_im
    hb_re = jnp.concatenate(h_re, axis=0).astype(BF16)
    hb_im = jnp.concatenate(h_im, axis=0).astype(BF16)
    y_t = _s5_glu(_s5_ch(hb_re, hb_im, ccr_ref, cci_ref) + d_ref[...] * u_t, gw_ref, gb_ref)
    y = _dot_tn(p_fwd, y_t.astype(BF16))
    y_ref[...] = y.astype(BF16).reshape(ns, tc, S5_D)

    @pl.when(t == pl.num_programs(0) - 1)
    def _():
        hr_ref[:, 0, :] = p_re[ns:2 * ns]
        hi_ref[:, 0, :] = p_im[ns:2 * ns]


def s5_seq(proj, prm):
    b, l, _ = proj.shape
    assert 2 * b == 8, "the time-major scan packs two time steps of all sequences into one 8-row sublane tile"
    tc = CHUNK
    return pl.pallas_call(
        _s5_seq_kernel,
        grid=(l // tc,),
        in_specs=[pl.BlockSpec((b, tc, S5_D), lambda t: (0, t, S5_OFF // S5_D))] + [_const_spec(x, 1) for x in prm],
        out_specs=[pl.BlockSpec((b, tc, S5_D), lambda t: (0, t, 0)),
                   pl.BlockSpec((b, 1, S5_STATE), lambda t: (0, 0, 0)),
                   pl.BlockSpec((b, 1, S5_STATE), lambda t: (0, 0, 0))],
        out_shape=[jax.ShapeDtypeStruct((b, l, S5_D), BF16),
                   jax.ShapeDtypeStruct((b, 1, S5_STATE), F32),
                   jax.ShapeDtypeStruct((b, 1, S5_STATE), F32)],
        scratch_shapes=[pltpu.VMEM((2 * b, S5_STATE), F32), pltpu.VMEM((2 * b, S5_STATE), F32)],
        compiler_params=_cp(("arbitrary",)),
        name="s5_seq",
    )(proj, *_args(prm))


def _s5_step_kernel(u_ref, h0r_ref, h0i_ref, bbr_ref, bbi_ref, ccr_ref, cci_ref, abr_ref, abi_ref, d_ref, gw_ref,
                    gb_ref, y_ref, hr_ref, hi_ref):
    u = u_ref[...]
    ar = abr_ref[...]
    ai = abi_ref[...]
    h0r = h0r_ref[...]
    h0i = h0i_ref[...]
    h_re = ar * h0r - ai * h0i + _s5_bu(u, bbr_ref, HI)
    h_im = ar * h0i + ai * h0r + _s5_bu(u, bbi_ref, HI)
    hr_ref[...] = h_re
    hi_ref[...] = h_im
    ch = _s5_ch(h_re.astype(BF16), h_im.astype(BF16), ccr_ref, cci_ref)
    y_ref[...] = _s5_glu(ch + d_ref[...] * u, gw_ref, gb_ref).astype(BF16)


def s5_step(proj, h0_re, h0_im, prm):
    b = proj.shape[0]
    return pl.pallas_call(
        _s5_step_kernel,
        grid=(1,),
        in_specs=[pl.BlockSpec((b, S5_D), lambda i: (0, S5_OFF // S5_D)),
                  _const_spec(h0_re, 1), _const_spec(h0_im, 1)] + [_const_spec(x, 1) for x in prm],
        out_specs=[pl.BlockSpec((b, S5_D), lambda i: (0, 0)),
                   pl.BlockSpec((b, S5_STATE), lambda i: (0, 0)),
                   pl.BlockSpec((b, S5_STATE), lambda i: (0, 0))],
        out_shape=[jax.ShapeDtypeStruct((b, S5_D), BF16),
                   jax.ShapeDtypeStruct((b, S5_STATE), F32),
                   jax.ShapeDtypeStruct((b, S5_STATE), F32)],
        compiler_params=_cp(("arbitrary",)),
        name="s5_step",
    )(proj, h0_re, h0_im, *_args(prm))


def _gdn_pre(qkv_c, z, ab, alog_ref, dtb_ref):
    act = _silu(qkv_c)
    qs, ks = [], []
    for h in range(GDN_H):
        qh = act[:, h * GDN_HD:(h + 1) * GDN_HD]
        kh = act[:, GDN_D + h * GDN_HD:GDN_D + (h + 1) * GDN_HD]
        qs.append(qh * lax.rsqrt(jnp.sum(qh * qh, axis=-1, keepdims=True) + 1e-6))
        ks.append(kh * lax.rsqrt(jnp.sum(kh * kh, axis=-1, keepdims=True) + 1e-6))
    v = act[:, 2 * GDN_D:3 * GDN_D]
    g = -jnp.exp(alog_ref[...]) * jax.nn.softplus(ab + dtb_ref[...])
    beta = jax.nn.sigmoid(ab)
    return qs, ks, v, g, beta


def _gdn_post(o, z_h, nw):
    on = o * lax.rsqrt(jnp.mean(o * o, axis=-1, keepdims=True) + NORM_EPS) * nw
    return on * _silu(z_h)


def _gdn_seq_kernel(p_ref, cw_ref, alog_ref, dtb_ref, nw_ref, y_ref, s_out_ref, s_ref, prev_ref):
    c = CHUNK
    ns = p_ref.shape[0]
    n = ns * c
    t = pl.program_id(1)

    @pl.when(t == 0)
    def _():
        s_ref[...] = jnp.zeros_like(s_ref)
        prev_ref[...] = jnp.zeros_like(prev_ref)

    cw = cw_ref[...]
    convs = []
    for i in range(ns):
        qkv = p_ref[i, :, 0:3 * GDN_D]
        ext = jnp.concatenate([prev_ref[i], qkv], axis=0)
        prev_ref[i] = qkv[c - 8:c]
        conv = cw[3:4] * qkv
        for j in (1, 2, 3):
            conv = conv + cw[3 - j:4 - j] * pltpu.roll(ext, j, axis=0)[8:]
        convs.append(conv)
    x = p_ref[...].reshape(n, GDN_W)
    z = x[:, 3 * GDN_D:4 * GDN_D]
    ab = x[:, 4 * GDN_D:4 * GDN_D + 128]
    qs, ks, v, g, beta = _gdn_pre(jnp.concatenate(convs, axis=0), z, ab, alog_ref, dtb_ref)

    shift = c.bit_length() - 1
    ri = _iota((n, n), 0)
    ci = _iota((n, n), 1)
    tril = ((lax.shift_right_logical(ri, shift) == lax.shift_right_logical(ci, shift)) & (ci <= ri)).astype(F32)
    gc_all = _dot(tril, g, HI)

    hc = GDN_H * c
    heads = range(GDN_H)
    stack = lambda pieces: jnp.concatenate(pieces, axis=0)
    lane = _iota((c, 128), 1)
    rr = _iota((hc, hc), 0)
    cc = _iota((hc, hc), 1)
    same = lax.shift_right_logical(rr, shift) == lax.shift_right_logical(cc, shift)
    causal = same & (cc <= rr)
    strict = same & (cc < rr)
    ones = jnp.ones((hc, 128), F32)
    nw = nw_ref[...]
    for i in range(ns):
        sq = slice(i * c, (i + 1) * c)
        gc = gc_all[sq]
        g_col = stack([gc[:, h:h + 1] for h in heads])
        b_col = stack([beta[sq, GDN_H + h:GDN_H + h + 1] for h in heads])
        g_row = _dot_nt(ones, stack([jnp.where(lane == h, gc, 0.0) for h in heads]), HI)
        decay = jnp.where(causal, jnp.exp(jnp.where(causal, g_col - g_row, 0.0)), 0.0)
        q = stack([qs[h][sq] for h in heads]) * (GDN_HD ** -0.5)
        k = stack([ks[h][sq] for h in heads])
        v_s = stack([v[sq, h * GDN_HD:(h + 1) * GDN_HD] for h in heads])
        kb = k * b_col
        m1 = _dot_nt(stack([kb, q]).astype(BF16), k.astype(BF16))
        lmat = jnp.where(strict, m1[:hc] * decay, 0.0)
        attn = m1[hc:] * decay
        eg = jnp.exp(g_col)
        sol = _neumann_solve3(-lmat, jnp.concatenate([v_s * b_col, kb * eg], axis=1), shift)
        u = sol[:, :GDN_HD]
        wk = sol[:, GDN_HD:]
        qd = q * eg
        ws = [_dot(stack([wk[h * c:(h + 1) * c], qd[h * c:(h + 1) * c]]).astype(BF16),
                   s_ref[i * GDN_H + h].astype(BF16)) for h in heads]
        v_new = u - stack([w[:c] for w in ws])
        o = stack([w[c:] for w in ws]) + _bdot(attn, v_new)
        g_last = [gc[c - 1:c, h:h + 1] for h in heads]
        k_tail = k * jnp.exp(stack([jnp.broadcast_to(gl, (c, 1)) for gl in g_last]) - g_col)
        for h in heads:
            rows = slice(h * c, (h + 1) * c)
            s_ref[i * GDN_H + h] = (s_ref[i * GDN_H + h] * jnp.exp(g_last[h])
                                    + _dot_tn(k_tail[rows].astype(BF16), v_new[rows].astype(BF16)))
            z_h = z[sq, h * GDN_HD:(h + 1) * GDN_HD]
            y_ref[i, :, h * GDN_HD:(h + 1) * GDN_HD] = _gdn_post(o[rows], z_h, nw).astype(BF16)

    @pl.when(t == pl.num_programs(1) - 1)
    def _():
        s_out_ref[...] = s_ref[...].reshape(s_out_ref.shape)


def gdn_seq(proj, prm, ns):
    b, l, _ = proj.shape
    c = CHUNK
    return pl.pallas_call(
        _gdn_seq_kernel,
        grid=(b // ns, l // c),
        in_specs=[pl.BlockSpec((ns, c, GDN_W), lambda b, t: (b, t, GDN_OFF // GDN_W))] + [_const_spec(x, 2) for x in prm],
        out_specs=[pl.BlockSpec((ns, c, GDN_D), lambda b, t: (b, t, 0)),
                   pl.BlockSpec((ns, GDN_H, GDN_HD, GDN_HD), lambda b, t: (b, 0, 0, 0))],
        out_shape=[jax.ShapeDtypeStruct((b, l, GDN_D), BF16),
                   jax.ShapeDtypeStruct((b, GDN_H, GDN_HD, GDN_HD), F32)],
        scratch_shapes=[pltpu.VMEM((ns * GDN_H, GDN_HD, GDN_HD), F32),
                        pltpu.VMEM((ns, 8, 3 * GDN_D), F32)],
        compiler_params=_cp(("arbitrary", "arbitrary")),
        name="gdn_seq",
    )(proj, *_args(prm))


def _gdn_step_kernel(nb, p_ref, cs_ref, s_in_ref, cw_ref, alog_ref, dtb_ref, nw_ref, y_ref, s_out_ref, o_scr):
    x = p_ref[...]
    qkv = x[:, 0:3 * GDN_D]
    z = x[:, 3 * GDN_D:4 * GDN_D]
    ab = x[:, 4 * GDN_D:4 * GDN_D + 128]
    cs = cs_ref[...]
    cw = cw_ref[...]
    w3 = 3 * GDN_D
    conv = cw[0:1] * cs[:, 0:w3] + cw[1:2] * cs[:, w3:2 * w3] + cw[2:3] * cs[:, 2 * w3:3 * w3] + cw[3:4] * qkv
    qs, ks, v, g, beta = _gdn_pre(conv, z, ab, alog_ref, dtb_ref)
    nw = nw_ref[...]
    pad = jnp.zeros((GDN_HD - nb, GDN_HD), F32)
    ks_t = [jnp.concatenate([ks[h], pad], axis=0).T for h in range(GDN_H)]
    qs_t = [jnp.concatenate([qs[h], pad], axis=0).T for h in range(GDN_H)]
    for i in range(nb):
        for h in range(GDN_H):
            k_c = ks_t[h][:, i:i + 1]
            q_c = qs_t[h][:, i:i + 1]
            eg = jnp.exp(g[i:i + 1, h:h + 1])
            b_s = beta[i:i + 1, GDN_H + h:GDN_H + h + 1]
            s = s_in_ref[i, h]
            sk = jnp.sum(k_c * s, axis=0, keepdims=True)
            v_row = v[i:i + 1, h * GDN_HD:(h + 1) * GDN_HD]
            s_new = s * eg + (k_c * b_s) * (v_row - eg * sk)
            s_out_ref[i, h] = s_new
            o_scr[i:i + 1, h * GDN_HD:(h + 1) * GDN_HD] = jnp.sum(q_c * s_new, axis=0, keepdims=True)
    for h in range(GDN_H):
        cols = slice(h * GDN_HD, (h + 1) * GDN_HD)
        y_ref[:, cols] = _gdn_post(o_scr[:, cols] * (GDN_HD ** -0.5), z[:, cols], nw).astype(BF16)


def gdn_step(proj, conv_state, state, layer, prm, nb):
    b = proj.shape[0]
    blk = (nb, GDN_H, GDN_HD, GDN_HD)
    return pl.pallas_call(
        functools.partial(_gdn_step_kernel, nb),
        grid=(b // nb,),
        in_specs=[pl.BlockSpec((nb, GDN_W), lambda i: (i, GDN_OFF // GDN_W)),
                  pl.BlockSpec((nb, 9 * GDN_D), lambda i: (i, 0)),
                  pl.BlockSpec((None,) + blk, lambda i: (layer, i, 0, 0, 0))] + [_const_spec(x, 1) for x in prm],
        out_specs=[pl.BlockSpec((nb, GDN_D), lambda i: (i, 0)),
                   pl.BlockSpec(blk, lambda i: (i, 0, 0, 0))],
        out_shape=[jax.ShapeDtypeStruct((b, GDN_D), BF16),
                   jax.ShapeDtypeStruct(state.shape[1:], F32)],
        scratch_shapes=[pltpu.VMEM((nb, GDN_D), F32)],
        compiler_params=_cp(("arbitrary",)),
        name="gdn_step",
    )(proj, conv_state, state, *_args(prm))


def _lru_gates(xc, gate, wr_ref, br_ref, wi_ref, bi_ref, lam_ref):
    r = jax.nn.sigmoid(_bdot(xc, wr_ref[...]) + br_ref[...])
    i = jax.nn.sigmoid(_bdot(xc, wi_ref[...]) + bi_ref[...])
    log_a = -LRU_C * r * jax.nn.softplus(-lam_ref[...])
    a = jnp.exp(log_a)
    one_minus_a2 = -jnp.tanh(log_a) * (a * a + 1.0)
    b = jnp.sqrt(one_minus_a2) * (i * xc)
    return a, b, _gelu(gate)


def _lru_seq_kernel(p_ref, cw_ref, cb_ref, wr_ref, br_ref, wi_ref, bi_ref, lam_ref, y_ref, h_out_ref,
                    prev_ref, carry_ref):
    ns, tc = p_ref.shape[0], p_ref.shape[1]
    n = ns * tc
    t = pl.program_id(0)

    @pl.when(t == 0)
    def _():
        prev_ref[...] = jnp.zeros_like(prev_ref)
        carry_ref[...] = jnp.zeros_like(carry_ref)

    p_fwd = _interleave_perm(ns, tc).astype(BF16)
    x_hi, x_lo = _split(p_ref[...].reshape(n, LRU_W))
    x = _dot(p_fwd, x_hi) + _dot(p_fwd, x_lo)
    xr = x[:, :LRU_D]
    gate = x[:, LRU_D:]
    halo = prev_ref.shape[0]
    ext = jnp.concatenate([prev_ref[...], xr], axis=0)
    prev_ref[...] = xr[n - halo:n]
    cw = cw_ref[...]
    xc = cw[3:4] * xr + cb_ref[...]
    for j in (1, 2, 3):
        xc = xc + cw[3 - j:4 - j] * pltpu.roll(ext, j * ns, axis=0)[halo:]
    a, b, gg = _lru_gates(xc, gate, wr_ref, br_ref, wi_ref, bi_ref, lam_ref)
    first = _iota((2 * ns, 1), 0) < ns
    h_prev = carry_ref[...]
    hs = []
    for s in range(n // (2 * ns)):
        a_s = a[s * 2 * ns:(s + 1) * 2 * ns]
        b_s = b[s * 2 * ns:(s + 1) * 2 * ns]
        h_a = a_s * pltpu.roll(h_prev, ns, axis=0) + b_s
        h_prev = jnp.where(first, h_a, a_s * pltpu.roll(h_a, ns, axis=0) + b_s)
        hs.append(h_prev)
    carry_ref[...] = h_prev
    y_t = (jnp.concatenate(hs, axis=0) * gg).astype(BF16)
    y_ref[...] = _dot_tn(p_fwd, y_t).astype(BF16).reshape(ns, tc, LRU_D)

    @pl.when(t == pl.num_programs(0) - 1)
    def _():
        h_out_ref[:, 0, :] = h_prev[ns:2 * ns]


def lru_seq(proj, prm):
    b, l, _ = proj.shape
    assert 2 * b == 8, "the time-major scan packs two time steps of all sequences into one 8-row sublane tile"
    tc = CHUNK
    return pl.pallas_call(
        _lru_seq_kernel,
        grid=(l // tc,),
        in_specs=[pl.BlockSpec((b, tc, LRU_W), lambda t: (0, t, LRU_OFF // LRU_W))] + [_const_spec(x, 1) for x in prm],
        out_specs=[pl.BlockSpec((b, tc, LRU_D), lambda t: (0, t, 0)),
                   pl.BlockSpec((b, 1, LRU_D), lambda t: (0, 0, 0))],
        out_shape=[jax.ShapeDtypeStruct((b, l, LRU_D), BF16),
                   jax.ShapeDtypeStruct((b, 1, LRU_D), F32)],
        scratch_shapes=[pltpu.VMEM((4 * b, LRU_D), F32), pltpu.VMEM((2 * b, LRU_D), F32)],
        compiler_params=_cp(("arbitrary",)),
        name="lru_seq",
    )(proj, *_args(prm))


def _lru_step_kernel(p_ref, cs_ref, h0_ref, cw_ref, cb_ref, wr_ref, br_ref, wi_ref, bi_ref, lam_ref, y_ref, h_ref):
    x = p_ref[...]
    xr = x[:, :LRU_D]
    gate = x[:, LRU_D:]
    cs = cs_ref[...]
    cw = cw_ref[...]
    xc = (cw[0:1] * cs[:, 0:LRU_D] + cw[1:2] * cs[:, LRU_D:2 * LRU_D] + cw[2:3] * cs[:, 2 * LRU_D:3 * LRU_D]
          + cw[3:4] * xr + cb_ref[...])
    a, b, gg = _lru_gates(xc, gate, wr_ref, br_ref, wi_ref, bi_ref, lam_ref)
    h = a * h0_ref[...] + b
    h_ref[...] = h
    y_ref[...] = (h * gg).astype(BF16)


def lru_step(proj, conv_state, h0, prm):
    b = proj.shape[0]
    return pl.pallas_call(
        _lru_step_kernel,
        grid=(1,),
        in_specs=[pl.BlockSpec((b, LRU_W), lambda i: (0, LRU_OFF // LRU_W)),
                  _const_spec(conv_state, 1), _const_spec(h0, 1)] + [_const_spec(x, 1) for x in prm],
        out_specs=[pl.BlockSpec((b, LRU_D), lambda i: (0, 0)),
                   pl.BlockSpec((b, LRU_D), lambda i: (0, 0))],
        out_shape=[jax.ShapeDtypeStruct((b, LRU_D), BF16),
                   jax.ShapeDtypeStruct((b, LRU_D), F32)],
        compiler_params=_cp(("arbitrary",)),
        name="lru_step",
    )(proj, conv_state, h0, *_args(prm))


def _pad_cols(x, width):
    return jnp.pad(x, [(0, 0)] * (x.ndim - 1) + [(0, width - x.shape[-1])])


def _pack_rwkv_cols(x):
    return jnp.concatenate([x[..., :1536], _pad_cols(x[..., 1536:1632], 128), _pad_cols(x[..., 1632:1728], 128),
                            x[..., 1728:1984]], axis=-1)


def _unpack_rwkv_cols(x):
    return jnp.concatenate([x[..., :1536], x[..., 1536:1632], x[..., 1664:1760], x[..., 1792:2048]], axis=-1)


def _pack_w_in_kernel(w_ref, o_ref):
    o_ref[...] = jnp.zeros(o_ref.shape, BF16)
    o = RWKV_PROJ
    for dst, a, b in ((0, 0, 1536), (1536, 1536, 1632), (1664, 1632, 1728), (1792, 1728, o), (S5_OFF, o, o + S5_D),
                      (GDN_OFF, o + S5_D, o + S5_D + GDN_PROJ), (LRU_OFF, o + S5_D + GDN_PROJ, o + S5_D + GDN_PROJ + 2 * LRU_D)):
        o_ref[0, dst:dst + (b - a), :] = w_ref[0, a:b, :].astype(BF16)


def pack_w_in(w_t, tk):
    depth, n, d = w_t.shape
    return pl.pallas_call(
        _pack_w_in_kernel,
        grid=(depth, d // tk),
        in_specs=[pl.BlockSpec((1, n, tk), lambda l, i: (l, 0, i))],
        out_specs=pl.BlockSpec((1, NP_IN, tk), lambda l, i: (l, 0, i)),
        out_shape=jax.ShapeDtypeStruct((depth, NP_IN, d), BF16),
        compiler_params=_cp(("arbitrary", "arbitrary")),
        name="pack_w_in",
    )(w_t)


def _block_diag(blocks):
    *lead, g, a, b = blocks.shape
    eye = jnp.eye(g, dtype=blocks.dtype)
    return (eye[:, None, :, None] * blocks[..., :, :, None, :]).reshape(*lead, g * a, g * b)


def _s5_params(lam_re, lam_im, log_dt, b_re, b_im, c_re, c_im, d, glu_w, glu_b):
    depth = lam_re.shape[0]
    dt = jnp.exp(log_dt)[..., None]
    mag = jnp.exp(lam_re * dt)
    ab_re, ab_im = mag * jnp.cos(lam_im * dt), mag * jnp.sin(lam_im * dt)
    den = lam_re * lam_re + lam_im * lam_im
    nr = ab_re - 1.0
    f_re = (nr * lam_re + ab_im * lam_im) / den
    f_im = (ab_im * lam_re - nr * lam_im) / den
    bb_re = f_re[..., None] * b_re - f_im[..., None] * b_im
    bb_im = f_re[..., None] * b_im + f_im[..., None] * b_re
    macro = lambda x: _block_diag(jnp.swapaxes(x, -1, -2).reshape(depth, S5_CG, S5_G // S5_CG, x.shape[-1], x.shape[-2]))
    bbr = macro(bb_re)
    bbi = macro(bb_im)
    tail = (macro(c_re).astype(BF16), macro(-c_im).astype(BF16),
            ab_re.reshape(depth, 1, S5_STATE), ab_im.reshape(depth, 1, S5_STATE), d.reshape(depth, 1, S5_D),
            glu_w, glu_b.reshape(depth, 1, S5_D))
    return (bbr.astype(BF16), bbi.astype(BF16)) + tail, (bbr, bbi) + tail


def _prepare_params(w):
    depth = w['w_in'].shape[0]
    row = lambda v: v.reshape(depth, 1, -1)
    pad_rows = lambda v: jnp.pad(v, ((0, 0), (0, 128 - v.shape[1]), (0, 0)))
    s5, s5_f32 = _s5_params(w['s5_lambda_re'], w['s5_lambda_im'], w['s5_log_dt'], w['s5_b_re'], w['s5_b_im'],
                            w['s5_c_re'], w['s5_c_im'], w['s5_d'], w['s5_glu_w'], w['s5_glu_b'])
    return {
        'norm1_g': row(w['norm1_g']), 'norm2_g': row(w['norm2_g']),
        'w_in': pack_w_in(jnp.swapaxes(w['w_in'], 1, 2), TILE['pack_cols']),
        'rwkv': (row(_pack_rwkv_cols(w['rwkv_mu'])), row(w['rwkv_w0']), pad_rows(w['rwkv_w_up']), row(w['rwkv_a0']),
                 pad_rows(w['rwkv_a_up']), w['rwkv_g_up'], row(w['rwkv_k_k']), row(w['rwkv_k_a']), row(w['rwkv_r_k']),
                 row(w['rwkv_ln_w']), row(w['rwkv_ln_b'])),
        's5': s5, 's5_f32': s5_f32,
        'gdn': (w['gdn_conv_w'], _pad_cols(row(w['gdn_a_log']), 128), _pad_cols(row(w['gdn_dt_bias']), 128),
                row(w['gdn_norm_w'])),
        'lru': (w['lru_conv_w'], row(w['lru_conv_b']), _block_diag(w['lru_wr']), row(w['lru_br']),
                _block_diag(w['lru_wi']), row(w['lru_bi']), row(w['lru_lambda'])),
        'ffn_conv_w': w['ffn_conv_w'], 'ffn_conv_b': row(w['ffn_conv_b']),
    }


def _layer(xp, xs, mod, st, p, seg, layer, stacked):
    bp, lp, d = xp.shape
    bs = xs.shape[1]
    sh1, sc1, g1, sh2, sc2, g2 = jnp.split(mod, 6, axis=-1)
    pm = lambda m: m[:bp, None, :]
    sm = lambda m: m[None, bp:bp + bs, :]

    lay = lambda x: Layered(x, layer)
    rw_prm, s5_prm, s5_prm_f32, gdn_prm, lru_prm = (tuple(lay(x) for x in p[k])
                                                    for k in ('rwkv', 's5', 's5_f32', 'gdn', 'lru'))

    proj = norm_matmul_in(xp, lay(p['norm1_g']), pm(sc1), pm(sh1), p['w_in'], layer, TILE['in_rows'], TILE['in_cols'])
    ya, s_wkv = rwkv_seq(proj, rw_prm, seg, bp)
    yb, s_re, s_im = s5_seq(proj, s5_prm)
    yc, s_gdn = gdn_seq(proj, gdn_prm, bp)
    yd, s_lru = lru_seq(proj, lru_prm)
    xp = matmul_res([ya, yb, yc, yd], stacked['w_out'], layer, xp, pm(g1), lp, TILE['out_cols'])
    h = norm_mod(xp, lay(p['norm2_g']), pm(sc2), pm(sh2), TILE['norm_rows'])
    act, lg, lv = ffn_up_seq(h, stacked['ffn_w_up'], layer, p['ffn_conv_w'], p['ffn_conv_b'], TILE['ffn_cols'])
    xp = matmul_res([act], stacked['ffn_w_down'], layer, xp, pm(g2), TILE['down_rows'], TILE['down_cols'])
    new_p = {
        'rwkv_wkv': s_wkv,
        'rwkv_shift': _unpack_rwkv_cols(proj[:, -1, :RW_W]),
        's5_re': s_re.reshape(bp, S5_G, S5_N), 's5_im': s_im.reshape(bp, S5_G, S5_N),
        'gdn': s_gdn,
        'gdn_conv': proj[:, -3:, GDN_OFF:GDN_OFF + 3 * GDN_D],
        'lru_h': s_lru.reshape(bp, LRU_D),
        'lru_conv': proj[:, -3:, LRU_OFF:LRU_OFF + LRU_D],
        'ffn_conv': jnp.concatenate([lg, lv], axis=-1),
    }

    proj = norm_matmul_in(xs, lay(p['norm1_g']), sm(sc1), sm(sh1), p['w_in'], layer, bs, TILE['in_cols'])[0]
    ya, s_wkv = rwkv_step(proj, _pack_rwkv_cols(st['rwkv_shift']), stacked['rwkv_wkv'], layer, rw_prm, seg)
    yb, s_re, s_im = s5_step(proj, st['s5_re'].reshape(bs, S5_STATE), st['s5_im'].reshape(bs, S5_STATE), s5_prm_f32)
    yc, s_gdn = gdn_step(proj, st['gdn_conv'].reshape(bs, 9 * GDN_D), stacked['gdn'], layer, gdn_prm, TILE['gdn_step_rows'])
    yd, s_lru = lru_step(proj, st['lru_conv'].reshape(bs, 3 * LRU_D), st['lru_h'], lru_prm)
    xs = matmul_res([y[None] for y in (ya, yb, yc, yd)], stacked['w_out'], layer, xs, sm(g1), bs, TILE['out_cols'])
    h = norm_mod(xs, lay(p['norm2_g']), sm(sc2), sm(sh2), bs)
    act, ug, uv = ffn_up_step(h[0], stacked['ffn_w_up'], layer, p['ffn_conv_w'], p['ffn_conv_b'],
                              stacked['ffn_conv'], TILE['ffn_cols'])
    xs = matmul_res([act[None]], stacked['ffn_w_down'], layer, xs, sm(g2), bs, TILE['down_cols'])
    shift_rows = lambda buf, new: jnp.concatenate([buf[:, 1:], new[:, None, :]], axis=1)
    new_s = {
        'rwkv_wkv': s_wkv,
        'rwkv_shift': _unpack_rwkv_cols(proj[:, :RW_W]),
        's5_re': s_re.reshape(bs, S5_G, S5_N), 's5_im': s_im.reshape(bs, S5_G, S5_N),
        'gdn': s_gdn,
        'gdn_conv': shift_rows(st['gdn_conv'], proj[:, GDN_OFF:GDN_OFF + 3 * GDN_D]),
        'lru_h': s_lru,
        'lru_conv': shift_rows(st['lru_conv'], proj[:, LRU_OFF:LRU_OFF + LRU_D]),
        'ffn_conv': shift_rows(st['ffn_conv'], jnp.concatenate([ug, uv], axis=-1)),
    }
    return xp, xs, new_p, new_s


STATE_ORDER = ('rwkv_wkv', 'rwkv_shift', 's5_re', 's5_im', 'gdn', 'gdn_conv', 'lru_h', 'lru_conv', 'ffn_conv')


def kernel(x_prompt, x_sample, c_prompt, c_sample, state_rwkv_wkv, state_rwkv_shift, state_s5_re, state_s5_im, state_gdn, state_gdn_conv, state_lru_h, state_lru_conv, state_ffn_conv, ada_w, ada_b, norm1_g, norm2_g, final_g, w_in, w_out, rwkv_mu, rwkv_w0, rwkv_w_up, rwkv_a0, rwkv_a_up, rwkv_g_up, rwkv_k_k, rwkv_k_a, rwkv_r_k, rwkv_ln_w, rwkv_ln_b, s5_lambda_re, s5_lambda_im, s5_log_dt, s5_b_re, s5_b_im, s5_c_re, s5_c_im, s5_d, s5_glu_w, s5_glu_b, gdn_conv_w, gdn_a_log, gdn_dt_bias, gdn_norm_w, lru_conv_w, lru_conv_b, lru_wr, lru_br, lru_wi, lru_bi, lru_lambda, ffn_w_up, ffn_conv_w, ffn_conv_b, ffn_w_down):
    weights = {
        'norm1_g': norm1_g, 'norm2_g': norm2_g, 'w_in': w_in,
        'rwkv_mu': rwkv_mu, 'rwkv_w0': rwkv_w0, 'rwkv_w_up': rwkv_w_up, 'rwkv_a0': rwkv_a0,
        'rwkv_a_up': rwkv_a_up, 'rwkv_g_up': rwkv_g_up, 'rwkv_k_k': rwkv_k_k, 'rwkv_k_a': rwkv_k_a,
        'rwkv_r_k': rwkv_r_k, 'rwkv_ln_w': rwkv_ln_w, 'rwkv_ln_b': rwkv_ln_b,
        's5_lambda_re': s5_lambda_re, 's5_lambda_im': s5_lambda_im, 's5_log_dt': s5_log_dt,
        's5_b_re': s5_b_re, 's5_b_im': s5_b_im, 's5_c_re': s5_c_re, 's5_c_im': s5_c_im, 's5_d': s5_d,
        's5_glu_w': s5_glu_w, 's5_glu_b': s5_glu_b,
        'gdn_conv_w': gdn_conv_w, 'gdn_a_log': gdn_a_log, 'gdn_dt_bias': gdn_dt_bias, 'gdn_norm_w': gdn_norm_w,
        'lru_conv_w': lru_conv_w, 'lru_conv_b': lru_conv_b, 'lru_wr': lru_wr, 'lru_br': lru_br,
        'lru_wi': lru_wi, 'lru_bi': lru_bi, 'lru_lambda': lru_lambda,
        'ffn_conv_w': ffn_conv_w, 'ffn_conv_b': ffn_conv_b,
    }
    stacked = {'w_out': w_out, 'ffn_w_up': ffn_w_up, 'ffn_w_down': ffn_w_down,
               'rwkv_wkv': jnp.transpose(state_rwkv_wkv, (0, 2, 3, 4, 1)),
               'gdn': state_gdn, 'ffn_conv': state_ffn_conv}
    cache = {'rwkv_wkv': state_rwkv_wkv, 'rwkv_shift': state_rwkv_shift, 's5_re': state_s5_re,
             's5_im': state_s5_im, 'gdn': state_gdn, 'gdn_conv': state_gdn_conv, 'lru_h': state_lru_h,
             'lru_conv': state_lru_conv, 'ffn_conv': state_ffn_conv}
    depth = ada_w.shape[0]
    bp = x_prompt.shape[0]
    bs = x_sample.shape[0]
    rows = -(-(bp + bs) // 8) * 8
    c_all = jnp.pad(jnp.concatenate([c_prompt, c_sample], axis=0), ((0, rows - bp - bs), (0, 0)))
    mod = ada_mod(c_all, ada_w, ada_b)
    ids = jnp.arange(RWKV_D) // RWKV_HD
    seg = (ids[:, None] == ids[None, :]).astype(BF16)

    xp = x_prompt
    xs = jnp.swapaxes(x_sample, 0, 1)
    new_p = {n: [] for n in STATE_ORDER}
    new_s = {n: [] for n in STATE_ORDER}
    p = _prepare_params(weights)
    for l in range(depth):
        st = {n: cache[n][l] for n in STATE_ORDER}
        xp, xs, sp, ss = _layer(xp, xs, mod[l], st, p, seg, l, stacked)
        for n in STATE_ORDER:
            new_p[n].append(sp[n])
            new_s[n].append(ss[n])
    y_prompt = final_norm(xp, final_g, TILE['norm_rows'])
    y_sample = jnp.swapaxes(final_norm(xs, final_g, bs), 0, 1)
    outs_p = tuple(jnp.stack(new_p[n], axis=0) for n in STATE_ORDER)
    stacked_s = {n: jnp.stack(new_s[n], axis=0) for n in STATE_ORDER}
    stacked_s['rwkv_wkv'] = jnp.transpose(stacked_s['rwkv_wkv'], (0, 4, 1, 2, 3))
    outs_s = tuple(stacked_s[n] for n in STATE_ORDER)
    return (y_prompt, y_sample) + outs_p + outs_s
```
